```python
import jax, jax.numpy as jnp
from jax import lax
import numpy as np

D_MODEL = 2048
BATCH = 8
SEQ = 2048
DEPTH = 1

EPS = 1e-6
MLA_HEADS = 16
MLA_Q_RANK = 512
MLA_KV_RANK = 512
MLA_NOPE = 128
MLA_ROPE = 64
MLA_V = D_MODEL // MLA_HEADS
ROPE_THETA = 10000.0
Q_BLOCK = 128
GLA_HEADS = 4
GLA_DK = D_MODEL // 2 // GLA_HEADS
GLA_DV = D_MODEL // GLA_HEADS
GLA_GATE_RANK = 16
GLA_TAU = 16.0
GLA_CHUNK = 64
PEER_HEADS = 8
PEER_NKEYS = 128
PEER_N = PEER_NKEYS * PEER_NKEYS
PEER_DQ = 256
PEER_HALF = PEER_DQ // 2
PEER_TOPK = 16
PEER_TOKEN_BLOCK = 128
IN_SPLITS = (MLA_Q_RANK, MLA_KV_RANK, MLA_ROPE,
             GLA_HEADS * GLA_DK, GLA_HEADS * GLA_DK, GLA_HEADS * GLA_DV,
             GLA_GATE_RANK, GLA_HEADS * GLA_DV, D_MODEL, D_MODEL)
D_IN = 11344

kernel_name = 'hybrid_mla_gla_peer_block'


def _rmsnorm(x, g):
    x32 = x.astype(jnp.float32)
    y = x32 * lax.rsqrt(jnp.mean(x32 * x32, axis=-1, keepdims=True) + EPS)
    return (y * g.astype(jnp.float32)).astype(x.dtype)


def _rope_tables(positions, dtype):
    inv_freq = ROPE_THETA ** (-jnp.arange(0, MLA_ROPE, 2, dtype=jnp.float32) / MLA_ROPE)
    ang = positions.astype(jnp.float32)[..., None] * inv_freq
    return jnp.cos(ang).astype(dtype), jnp.sin(ang).astype(dtype)


def _apply_rope(x, cos, sin):
    x1, x2 = jnp.split(x, 2, axis=-1)
    return jnp.concatenate([x1 * cos - x2 * sin, x2 * cos + x1 * sin], axis=-1)


def _mla(c_q, c_kv, k_pe, cos, sin, g_cq, w_uq, g_ckv, w_ukv, g_qn, g_qr, g_kn, g_kr):
    B, S, _ = c_q.shape
    H = MLA_HEADS
    q = (_rmsnorm(c_q, g_cq) @ w_uq).reshape(B, S, H, MLA_NOPE + MLA_ROPE)
    kv = (_rmsnorm(c_kv, g_ckv) @ w_ukv).reshape(B, S, H, MLA_NOPE + MLA_V)
    q_nope, q_pe = q[..., :MLA_NOPE], q[..., MLA_NOPE:]
    k_nope, v = kv[..., :MLA_NOPE], kv[..., MLA_NOPE:]
    q_nope = _rmsnorm(q_nope, g_qn)
    q_pe = _apply_rope(_rmsnorm(q_pe, g_qr), cos[:, :, None, :], sin[:, :, None, :])
    k_nope = _rmsnorm(k_nope, g_kn)
    k_pe = _apply_rope(_rmsnorm(k_pe, g_kr), cos, sin)
    scale = (MLA_NOPE + MLA_ROPE) ** -0.5
    nb = S // Q_BLOCK
    qn_b = q_nope.reshape(B, nb, Q_BLOCK, H, MLA_NOPE).transpose(1, 0, 2, 3, 4)
    qr_b = q_pe.reshape(B, nb, Q_BLOCK, H, MLA_ROPE).transpose(1, 0, 2, 3, 4)
    key_pos = jnp.arange(S)

    def attend(args):
        qn, qr, blk = args
        s = (jnp.einsum('bqhd,bkhd->bhqk', qn, k_nope).astype(jnp.float32)
             + jnp.einsum('bqhr,bkr->bhqk', qr, k_pe).astype(jnp.float32)) * scale
        q_pos = blk * Q_BLOCK + jnp.arange(Q_BLOCK)
        mask = key_pos[None, :] <= q_pos[:, None]
        s = jnp.where(mask[None, None], s, -jnp.inf)
        prob = jax.nn.softmax(s, axis=-1).astype(v.dtype)
        return jnp.einsum('bhqk,bkhd->bqhd', prob, v)

    o = lax.map(attend, (qn_b, qr_b, jnp.arange(nb)))
    return o.transpose(1, 0, 2, 3, 4).reshape(B, S, H * MLA_V)


def _gla(q, k, v, a_lr, og, w_a2, b_a, g_on):
    B, S, _ = q.shape
    H, DK, DV, C = GLA_HEADS, GLA_DK, GLA_DV, GLA_CHUNK
    nc = S // C
    log_a = jax.nn.log_sigmoid((a_lr @ w_a2 + b_a).astype(jnp.float32)) / GLA_TAU

    def to_chunks(t, d):
        return t.reshape(B, nc, C, H, d).transpose(1, 0, 3, 2, 4)

    qc = to_chunks(q * (DK ** -0.5), DK)
    kc = to_chunks(k, DK)
    vc = to_chunks(v, DV)
    bc = jnp.cumsum(to_chunks(log_a, DK), axis=-2)
    causal = jnp.tril(jnp.ones((C, C), dtype=bool))

    def step(state, inp):
        qi, ki, vi, bi = inp
        inter = jnp.einsum('bhcd,bhde->bhce', qi * jnp.exp(bi), state)
        diff = bi[:, :, :, None, :] - bi[:, :, None, :, :]
        decay = jnp.exp(jnp.where(causal[None, None, :, :, None], diff, -jnp.inf))
        att = jnp.einsum('bhid,bhjd,bhijd->bhij', qi, ki, decay)
        intra = jnp.einsum('bhij,bhje->bhie', att, vi)
        b_last = bi[:, :, -1:, :]
        new_state = (jnp.exp(b_last[:, :, 0, :])[..., None] * state
                     + jnp.einsum('bhjd,bhje->bhde', ki * jnp.exp(b_last - bi), vi))
        return new_state, inter + intra

    state0 = jnp.zeros((B, H, DK, DV), jnp.float32)
    _, o = lax.scan(step, state0, (qc, kc, vc, bc))
    o = o.transpose(1, 0, 3, 2, 4).reshape(B, S, H, DV)
    o = _rmsnorm(o, g_on) * jax.nn.silu(og.reshape(B, S, H, DV)).astype(jnp.float32)
    return o.reshape(B, S, H * DV).astype(q.dtype)


def _peer(h, w_pq, sub_keys, u_emb, v_emb):
    B, S, D = h.shape
    TB, K = PEER_TOKEN_BLOCK, PEER_TOPK
    hf = h.reshape((B * S) // TB, TB, D)

    def block(hb):
        q = (hb @ w_pq).reshape(TB, PEER_HEADS, 2, PEER_HALF)
        s = jnp.einsum('thpd,hpnd->thpn', q, sub_keys).astype(jnp.float32)
        top_s, top_i = lax.top_k(s, K)
        cand_s = (top_s[:, :, 0, :, None] + top_s[:, :, 1, None, :]).reshape(TB, PEER_HEADS, K * K)
        cand_id = (top_i[:, :, 0, :, None] * PEER_NKEYS + top_i[:, :, 1, None, :]).reshape(TB, PEER_HEADS, K * K)
        best_s, best_j = lax.top_k(cand_s, K)
        ids = jnp.take_along_axis(cand_id, best_j, axis=-1)
        gate = jax.nn.softmax(best_s, axis=-1)
        u_sel = jnp.take(u_emb, ids, axis=0)
        v_sel = jnp.take(v_emb, ids, axis=0)
        act = jax.nn.gelu(jnp.einsum('td,thkd->thk', hb, u_sel), approximate=False)
        return jnp.einsum('thk,thkd->td', (gate * act).astype(hb.dtype), v_sel)

    return lax.map(block, hf).reshape(B, S, D)


def setup_inputs(seed: int = 0) -> dict:
    key = jax.random.key(seed)
    ks = jax.random.split(key, 24)
    f32 = jnp.float32
    L, D = DEPTH, D_MODEL

    def nrm(k, shape, scale):
        return jax.random.normal(k, shape, f32) * scale

    def gain(k, shape):
        return 1.0 + 0.01 * jax.random.normal(k, shape, f32)

    x = jax.random.normal(ks[0], (BATCH, SEQ, D), f32)
    offset = jax.random.randint(ks[1], (BATCH, 1), 0, 4096, dtype=jnp.int32)
    positions = offset + jnp.arange(SEQ, dtype=jnp.int32)[None, :]
    return {
        'x': x,
        'positions': positions,
        'g_norm_mix': gain(ks[2], (L, D)),
        'w_in': nrm(ks[3], (L, D, D_IN), D ** -0.5),
        'b_gate': nrm(ks[4], (L, 2, D), 0.1),
        'g_cq': gain(ks[5], (L, MLA_Q_RANK)),
        'w_uq': nrm(ks[6], (L, MLA_Q_RANK, MLA_HEADS * (MLA_NOPE + MLA_ROPE)), MLA_Q_RANK ** -0.5),
        'g_ckv': gain(ks[7], (L, MLA_KV_RANK)),
        'w_ukv': nrm(ks[8], (L, MLA_KV_RANK, MLA_HEADS * (MLA_NOPE + MLA_V)), MLA_KV_RANK ** -0.5),
        'g_qn': gain(ks[9], (L, MLA_NOPE)),
        'g_qr': gain(ks[10], (L, MLA_ROPE)),
        'g_kn': gain(ks[11], (L, MLA_NOPE)),
        'g_kr': gain(ks[12], (L, MLA_ROPE)),
        'w_a2': nrm(ks[13], (L, GLA_GATE_RANK, GLA_HEADS * GLA_DK), GLA_GATE_RANK ** -0.5),
        'b_a': nrm(ks[14], (L, GLA_HEADS * GLA_DK), 0.1),
        'g_gla_out': gain(ks[15], (L, GLA_DV)),
        'w_out': nrm(ks[16], (L, D, D), D ** -0.5),
        'g_norm_ffn': gain(ks[17], (L, D)),
        'w_pq': nrm(ks[18], (L, D, PEER_HEADS * PEER_DQ), D ** -0.5),
        'sub_keys': nrm(ks[19], (L, PEER_HEADS, 2, PEER_NKEYS, PEER_HALF), PEER_HALF ** -0.5),
        'u_emb': nrm(ks[20], (L, PEER_N, D), D ** -0.5),
        'v_emb': nrm(ks[21], (L, PEER_N, D), 0.5),
    }


def reference(x, positions, g_norm_mix, w_in, b_gate, g_cq, w_uq, g_ckv, w_ukv, g_qn, g_qr, g_kn, g_kr,
              w_a2, b_a, g_gla_out, w_out, g_norm_ffn, w_pq, sub_keys, u_emb, v_emb):
    cos, sin = _rope_tables(positions, x.dtype)
    offsets = [int(o) for o in np.cumsum(IN_SPLITS)[:-1]]
    for l in range(DEPTH):
        h = _rmsnorm(x, g_norm_mix[l])
        p = h @ w_in[l]
        c_q, c_kv, k_pe, q_g, k_g, v_g, a_lr, og, gate_a, gate_b = jnp.split(p, offsets, axis=-1)
        o_mla = _mla(c_q, c_kv, k_pe, cos, sin, g_cq[l], w_uq[l], g_ckv[l], w_ukv[l],
                     g_qn[l], g_qr[l], g_kn[l], g_kr[l])
        o_gla = _gla(q_g, k_g, v_g, a_lr, og, w_a2[l], b_a[l], g_gla_out[l])
        merged = (jax.nn.sigmoid(gate_a + b_gate[l, 0]) * o_mla
                  + jax.nn.sigmoid(gate_b + b_gate[l, 1]) * o_gla)
        x = x + merged @ w_out[l]
        x = x + _peer(_rmsnorm(x, g_norm_ffn[l]), w_pq[l], sub_keys[l], u_emb[l], v_emb[l])
    return x
```

```python
import functools

import jax
import jax.numpy as jnp
from jax import lax
from jax.experimental import pallas as pl
from jax.experimental.pallas import tpu as pltpu

EPS = 1e-6
ROPE_THETA = 10000.0
GLA_TAU = 16.0
GLA_CHUNK = 64
PEER_TOPK = 16

LANES = 128
SUBLANES = 8
VMEM_LIMIT_BYTES = 56 * 1024 * 1024

F32 = jnp.float32
BF16 = jnp.bfloat16
NEG_INF = float("-inf")


def _params(semantics):
    return pltpu.CompilerParams(dimension_semantics=semantics, vmem_limit_bytes=VMEM_LIMIT_BYTES)


def _rms(x, gain, n=None):
    ss = jnp.sum(x * x, axis=-1, keepdims=True)
    n = x.shape[-1] if n is None else n
    return x * lax.rsqrt(ss * (1.0 / n) + EPS) * gain


def _gelu_exact(x):
    return 0.5 * x * (1.0 + lax.erf(x * (0.5 ** 0.5)))


def _dot(a, b):
    return jnp.dot(a, b, preferred_element_type=F32)


def _dot_nt(a, b):
    return lax.dot_general(a, b, (((1,), (1,)), ((), ())), preferred_element_type=F32)


def _dot_tn(a, b):
    return lax.dot_general(a, b, (((0,), (0,)), ((), ())), preferred_element_type=F32)


def _in_proj_kernel(x_ref, g_ref, w_ref, ws_ref, p_ref, ps_ref, h_scr):
    @pl.when(pl.program_id(1) == 0)
    def _():
        h = _rms(x_ref[...], g_ref[...]).astype(BF16)
        h_scr[...] = h
        ps_ref[...] = _dot(h, ws_ref[...])

    p_ref[...] = _dot(h_scr[...], w_ref[...]).astype(p_ref.dtype)


def _in_proj(x, g, w_main, w_small, *, tm, tn):
    t, d = x.shape
    n = w_main.shape[1]
    return pl.pallas_call(
        _in_proj_kernel,
        grid=(t // tm, n // tn),
        in_specs=[
            pl.BlockSpec((tm, d), lambda i, j: (i, 0)),
            pl.BlockSpec((1, d), lambda i, j: (0, 0)),
            pl.BlockSpec((d, tn), lambda i, j: (0, j)),
            pl.BlockSpec((d, LANES), lambda i, j: (0, 0)),
        ],
        out_specs=[
            pl.BlockSpec((tm, tn), lambda i, j: (i, j)),
            pl.BlockSpec((tm, LANES), lambda i, j: (i, 0)),
        ],
        out_shape=[
            jax.ShapeDtypeStruct((t, n), BF16),
            jax.ShapeDtypeStruct((t, LANES), F32),
        ],
        scratch_shapes=[pltpu.VMEM((tm, d), BF16)],
        compiler_params=_params(("parallel", "arbitrary")),
    )(x, g, w_main, w_small)


def _rope_tables(pos_ref, invf_ref, rope):
    ang = pos_ref[...].astype(F32) * invf_ref[...]
    cos, sin = jnp.cos(ang), jnp.sin(ang)
    lane = lax.broadcasted_iota(jnp.int32, ang.shape, 1)
    half = rope // 2
    c = jnp.where(lane < rope, cos, 0.0)
    s_lo = jnp.where(lane < half, -sin, 0.0)
    s_hi = jnp.where(lane < half, 0.0, jnp.where(lane < rope, sin, 0.0))
    return c, s_lo, s_hi


def _apply_rope(pe, c, s_lo, s_hi, rope):
    half = rope // 2
    from_hi = pltpu.roll(pe, LANES - half, 1)
    from_lo = pltpu.roll(pe, half, 1)
    return pe * c + from_hi * s_lo + from_lo * s_hi


def _mla_q_kernel(cq_ref, gcq_ref, w_ref, gq_ref, pos_ref, invf_ref, q_ref,
                  h_scr, c_scr, slo_scr, shi_scr, *, nope, rope):
    @pl.when(pl.program_id(1) == 0)
    def _():
        h_scr[...] = _rms(cq_ref[...].astype(F32), gcq_ref[...]).astype(BF16)
        c, s_lo, s_hi = _rope_tables(pos_ref, invf_ref, rope)
        c_scr[...] = c
        slo_scr[...] = s_lo
        shi_scr[...] = s_hi

    y = _dot(h_scr[...], w_ref[...])
    g = gq_ref[...]
    qn = _rms(y[:, :nope], g[:, :nope])
    pe = _rms(y[:, nope:], g[:, nope:], n=rope)
    pe = _apply_rope(pe, c_scr[...], slo_scr[...], shi_scr[...], rope)
    q_ref[:, :nope] = qn.astype(q_ref.dtype)
    q_ref[:, nope:] = pe.astype(q_ref.dtype)


def _mla_q(p, cq_blk, g_cq, w_uq_p, gq, pos, invf, *, heads, rank, nope, rope, tm):
    t = p.shape[0]
    hw = nope + LANES
    return pl.pallas_call(
        functools.partial(_mla_q_kernel, nope=nope, rope=rope),
        grid=(t // tm, heads),
        in_specs=[
            pl.BlockSpec((tm, rank), lambda i, j: (i, cq_blk)),
            pl.BlockSpec((1, rank), lambda i, j: (0, 0)),
            pl.BlockSpec((rank, hw), lambda i, j: (0, j)),
            pl.BlockSpec((1, hw), lambda i, j: (0, 0)),
            pl.BlockSpec((tm, 1), lambda i, j: (i, 0)),
            pl.BlockSpec((1, LANES), lambda i, j: (0, 0)),
        ],
        out_specs=pl.BlockSpec((tm, hw), lambda i, j: (i, j)),
        out_shape=jax.ShapeDtypeStruct((t, heads * hw), BF16),
        scratch_shapes=[
            pltpu.VMEM((tm, rank), BF16),
            pltpu.VMEM((tm, LANES), F32),
            pltpu.VMEM((tm, LANES), F32),
            pltpu.VMEM((tm, LANES), F32),
        ],
        compiler_params=_params(("parallel", "arbitrary")),
    )(p, g_cq, w_uq_p, gq, pos, invf)


def _mla_kv_kernel(ckv_ref, gckv_ref, w_ref, gkn_ref, small_ref, gkr_ref, pos_ref, invf_ref,
                   k_ref, v_ref, h_scr, kpe_scr, *, nope, rope):
    @pl.when(pl.program_id(1) == 0)
    def _():
        h_scr[...] = _rms(ckv_ref[...].astype(F32), gckv_ref[...]).astype(BF16)
        c, s_lo, s_hi = _rope_tables(pos_ref, invf_ref, rope)
        sm = small_ref[...]
        lane = lax.broadcasted_iota(jnp.int32, sm.shape, 1)
        pe = _rms(jnp.where(lane < rope, sm, 0.0), gkr_ref[...], n=rope)
        kpe_scr[...] = _apply_rope(pe, c, s_lo, s_hi, rope).astype(BF16)

    y = _dot(h_scr[...], w_ref[...])
    k_ref[:, :nope] = _rms(y[:, :nope], gkn_ref[...]).astype(k_ref.dtype)
    k_ref[:, nope:] = kpe_scr[...]
    v_ref[...] = y[:, nope:].astype(v_ref.dtype)


def _mla_kv(p, ckv_blk, g_ckv, w_ukv, g_kn, small, gkr, pos, invf, *, heads, rank, nope, rope, dv, tm):
    t = p.shape[0]
    kw = nope + LANES
    return pl.pallas_call(
        functools.partial(_mla_kv_kernel, nope=nope, rope=rope),
        grid=(t // tm, heads),
        in_specs=[
            pl.BlockSpec((tm, rank), lambda i, j: (i, ckv_blk)),
            pl.BlockSpec((1, rank), lambda i, j: (0, 0)),
            pl.BlockSpec((rank, nope + dv), lambda i, j: (0, j)),
            pl.BlockSpec((1, nope), lambda i, j: (0, 0)),
            pl.BlockSpec((tm, LANES), lambda i, j: (i, 0)),
            pl.BlockSpec((1, LANES), lambda i, j: (0, 0)),
            pl.BlockSpec((tm, 1), lambda i, j: (i, 0)),
            pl.BlockSpec((1, LANES), lambda i, j: (0, 0)),
        ],
        out_specs=[
            pl.BlockSpec((tm, kw), lambda i, j: (i, j)),
            pl.BlockSpec((tm, dv), lambda i, j: (i, j)),
        ],
        out_shape=[
            jax.ShapeDtypeStruct((t, heads * kw), BF16),
            jax.ShapeDtypeStruct((t, heads * dv), BF16),
        ],
        scratch_shapes=[pltpu.VMEM((tm, rank), BF16), pltpu.VMEM((tm, LANES), BF16)],
        compiler_params=_params(("parallel", "arbitrary")),
    )(p, g_ckv, w_ukv, g_kn, small, gkr, pos, invf)


def _attn_kernel(q_ref, k_ref, v_ref, o_ref, *, tq, tk):
    qi = pl.program_id(2)
    q = q_ref[...]
    dv = v_ref.shape[1]
    row = qi * tq + lax.broadcasted_iota(jnp.int32, (tq, tk), 0)
    col0 = lax.broadcasted_iota(jnp.int32, (tq, tk), 1)

    def body(kb, carry):
        m, l, acc = carry
        k0 = pl.multiple_of(kb * tk, tk)
        s = _dot_nt(q, k_ref[pl.ds(k0, tk), :])
        s = jnp.where(col0 + k0 <= row, s, NEG_INF)
        m_new = jnp.maximum(m, jnp.max(s, axis=-1, keepdims=True))
        alpha = jnp.exp(m - m_new)
        pr = jnp.exp(s - m_new)
        l = alpha * l + jnp.sum(pr, axis=-1, keepdims=True)
        acc = alpha * acc + _dot(pr.astype(BF16), v_ref[pl.ds(k0, tk), :])
        return m_new, l, acc

    init = (jnp.full((tq, 1), NEG_INF, F32), jnp.zeros((tq, 1), F32), jnp.zeros((tq, dv), F32))
    nkb = ((qi + 1) * tq + tk - 1) // tk
    _, l, acc = lax.fori_loop(0, nkb, body, init)
    o_ref[...] = (acc / l).astype(o_ref.dtype)


def _mla_attn(q, k, v, *, batch, seq, heads, dk, dv, tq, tk):
    t = q.shape[0]
    nq = seq // tq
    return pl.pallas_call(
        functools.partial(_attn_kernel, tq=tq, tk=tk),
        grid=(batch, heads, nq),
        in_specs=[
            pl.BlockSpec((tq, dk), lambda b, h, i: (b * nq + i, h)),
            pl.BlockSpec((seq, dk), lambda b, h, i: (b, h)),
            pl.BlockSpec((seq, dv), lambda b, h, i: (b, h)),
        ],
        out_specs=pl.BlockSpec((tq, dv), lambda b, h, i: (b * nq + i, h)),
        out_shape=jax.ShapeDtypeStruct((t, heads * dv), BF16),
        compiler_params=_params(("parallel", "parallel", "arbitrary")),
    )(q, k, v)


def _gla_kernel(q_ref, k_ref, v_ref, og_ref, small_ref, w2_ref, ba_ref, gon_ref, o_ref, st_scr,
                *, seq, dk, dv, chunk):
    c = chunk
    st_scr[...] = jnp.zeros_like(st_scr)
    r_i = lax.broadcasted_iota(jnp.int32, (c, c), 0)
    c_i = lax.broadcasted_iota(jnp.int32, (c, c), 1)
    tri = jnp.where(c_i <= r_i, 1.0, 0.0).astype(BF16)
    row_id = lax.broadcasted_iota(jnp.int32, (c, 1), 0)
    w2 = w2_ref[...]
    ba = ba_ref[...]
    gon = gon_ref[...]
    q_scale = dk ** -0.5

    def chunk_step(ci, carry):
        r0 = pl.multiple_of(ci * c, c)
        qc = q_ref[pl.ds(r0, c), :].astype(F32) * q_scale
        kc = k_ref[pl.ds(r0, c), :].astype(F32)
        vc = v_ref[pl.ds(r0, c), :]
        z = _dot(small_ref[pl.ds(r0, c), :].astype(BF16), w2) + ba
        la = jax.nn.log_sigmoid(z) * (1.0 / GLA_TAU)
        hi = la.astype(BF16)
        r1 = la - hi.astype(F32)
        mid = r1.astype(BF16)
        lo = (r1 - mid.astype(F32)).astype(BF16)
        b = _dot(tri, hi) + _dot(tri, mid) + _dot(tri, lo)

        st = st_scr[...]
        inter = _dot_nt((qc * jnp.exp(b)).astype(BF16), st.astype(BF16))

        att = jnp.zeros((c, c), F32)
        for j in range(c):
            lo_r = (j // SUBLANES) * SUBLANES
            d = b[lo_r:, :] - b[j:j + 1, :]
            e = jnp.exp(jnp.where(row_id[lo_r:, :] >= j, d, NEG_INF))
            col = jnp.sum(qc[lo_r:, :] * kc[j:j + 1, :] * e, axis=-1, keepdims=True)
            if lo_r:
                col = jnp.concatenate([jnp.zeros((lo_r, 1), F32), col], axis=0)
            att = jnp.where(c_i == j, col, att)
        o = inter + _dot(att.astype(BF16), vc)

        b_last = b[c - 1:c, :]
        k_dec = (kc * jnp.exp(b_last - b)).astype(BF16)
        st_scr[...] = st * jnp.exp(b_last) + _dot_tn(vc, k_dec)

        og = og_ref[pl.ds(r0, c), :].astype(F32)
        out = _rms(o, gon) * (og * jax.nn.sigmoid(og))
        o_ref[pl.ds(r0, c), :] = out.astype(o_ref.dtype)
        return carry

    lax.fori_loop(0, seq // c, chunk_step, 0)


def _gla(p, small, w2p, b_a, g_on, *, batch, seq, heads, dk, dv, q_blk, k_blk, v_blk, og_blk):
    t = p.shape[0]
    return pl.pallas_call(
        functools.partial(_gla_kernel, seq=seq, dk=dk, dv=dv, chunk=GLA_CHUNK),
        grid=(batch, heads),
        in_specs=[
            pl.BlockSpec((seq, dk), lambda b, h: (b, q_blk + h)),
            pl.BlockSpec((seq, dk), lambda b, h: (b, k_blk + h)),
            pl.BlockSpec((seq, dv), lambda b, h: (b, v_blk + h)),
            pl.BlockSpec((seq, dv), lambda b, h: (b, og_blk + h)),
            pl.BlockSpec((seq, LANES), lambda b, h: (b, 0)),
            pl.BlockSpec((LANES, dk), lambda b, h: (0, h)),
            pl.BlockSpec((1, dk), lambda b, h: (0, h)),
            pl.BlockSpec((1, dv), lambda b, h: (0, 0)),
        ],
        out_specs=pl.BlockSpec((seq, dv), lambda b, h: (b, h)),
        out_shape=jax.ShapeDtypeStruct((t, heads * dv), BF16),
        scratch_shapes=[pltpu.VMEM((dv, dk), F32)],
        compiler_params=_params(("parallel", "parallel")),
    )(p, p, p, p, small, w2p, b_a, g_on)


def _out_proj_kernel(x_ref, ga_ref, gb_ref, oa_ref, ob_ref, bg_ref, w_ref, o_ref, m_scr):
    @pl.when(pl.program_id(1) == 0)
    def _():
        bg = bg_ref[...]
        sa = jax.nn.sigmoid(ga_ref[...].astype(F32) + bg[0:1, :])
        sb = jax.nn.sigmoid(gb_ref[...].astype(F32) + bg[1:2, :])
        m_scr[...] = (sa * oa_ref[...].astype(F32) + sb * ob_ref[...].astype(F32)).astype(BF16)

    o_ref[...] = x_ref[...] + _dot(m_scr[...], w_ref[...])


def _out_proj(x, p, o_mla, o_gla, b_gate, w_out, *, ga_blk, gb_blk, tm, tn):
    t, d = x.shape
    return pl.pallas_call(
        _out_proj_kernel,
        grid=(t // tm, d // tn),
        in_specs=[
            pl.BlockSpec((tm, tn), lambda i, j: (i, j)),
            pl.BlockSpec((tm, d), lambda i, j: (i, ga_blk)),
            pl.BlockSpec((tm, d), lambda i, j: (i, gb_blk)),
            pl.BlockSpec((tm, d), lambda i, j: (i, 0)),
            pl.BlockSpec((tm, d), lambda i, j: (i, 0)),
            pl.BlockSpec((2, d), lambda i, j: (0, 0)),
            pl.BlockSpec((d, tn), lambda i, j: (0, j)),
        ],
        out_specs=pl.BlockSpec((tm, tn), lambda i, j: (i, j)),
        out_shape=jax.ShapeDtypeStruct((t, d), F32),
        scratch_shapes=[pltpu.VMEM((tm, d), BF16)],
        compiler_params=_params(("parallel", "arbitrary")),
    )(x, p, p, o_mla, o_gla, b_gate, w_out)


def _topk_rows(s, k, payload=None):
    n = s.shape[0]
    iota = lax.broadcasted_iota(jnp.int32, s.shape, 0).astype(F32)
    kiota = lax.broadcasted_iota(jnp.int32, (k, s.shape[1]), 0)
    vals = jnp.zeros((k, s.shape[1]), F32)
    picks = jnp.zeros((k, s.shape[1]), F32)
    for r in range(k):
        m = jnp.max(s, axis=0, keepdims=True)
        idx = jnp.min(jnp.where(s == m, iota, float(n)), axis=0, keepdims=True)
        hit = iota == idx
        if payload is None:
            pick = idx
        else:
            pick = jnp.sum(jnp.where(hit, payload, 0.0), axis=0, keepdims=True)
        vals = jnp.where(kiota == r, m, vals)
        picks = jnp.where(kiota == r, pick, picks)
        s = jnp.where(hit, NEG_INF, s)
    return vals, picks


def _peer_route_kernel(x_ref, g_ref, w_ref, sk_ref, h_ref, ids_ref, gates_ref, *, heads, nkeys, half, topk):
    h = _rms(x_ref[...], g_ref[...]).astype(BF16)
    h_ref[...] = h
    qf = _dot(h, w_ref[...])
    for hd in range(heads):
        tops = []
        for part in range(2):
            o = (hd * 2 + part) * half
            qh = qf[:, o:o + half].astype(BF16)
            keys = sk_ref[(hd * 2 + part) * nkeys:(hd * 2 + part + 1) * nkeys, :]
            tops.append(_topk_rows(_dot_nt(keys, qh), topk))
        (s1, i1), (s2, i2) = tops
        cand_s = jnp.concatenate([s1[a:a + 1, :] + s2 for a in range(topk)], axis=0)
        cand_id = jnp.concatenate([i1[a:a + 1, :] * float(nkeys) + i2 for a in range(topk)], axis=0)
        best_s, best_id = _topk_rows(cand_s, topk, payload=cand_id)
        e = jnp.exp(best_s - best_s[0:1, :])
        gate = e / jnp.sum(e, axis=0, keepdims=True)
        ids_ref[hd * topk:(hd + 1) * topk, :] = best_id.astype(jnp.int32)
        gates_ref[hd * topk:(hd + 1) * topk, :] = gate


def _peer_route(x, g, w_pq, sk2d, *, heads, nkeys, half, topk, tm):
    t, d = x.shape
    dq = w_pq.shape[1]
    return pl.pallas_call(
        functools.partial(_peer_route_kernel, heads=heads, nkeys=nkeys, half=half, topk=topk),
        grid=(t // tm,),
        in_specs=[
            pl.BlockSpec((tm, d), lambda i: (i, 0)),
            pl.BlockSpec((1, d), lambda i: (0, 0)),
            pl.BlockSpec((d, dq), lambda i: (0, 0)),
            pl.BlockSpec((heads * 2 * nkeys, half), lambda i: (0, 0)),
        ],
        out_specs=[
            pl.BlockSpec((tm, d), lambda i: (i, 0)),
            pl.BlockSpec((heads * topk, tm), lambda i: (0, i)),
            pl.BlockSpec((heads * topk, tm), lambda i: (0, i)),
        ],
        out_shape=[
            jax.ShapeDtypeStruct((t, d), BF16),
            jax.ShapeDtypeStruct((heads * topk, t), jnp.int32),
            jax.ShapeDtypeStruct((heads * topk, t), F32),
        ],
        compiler_params=_params(("parallel",)),
    )(x, g, w_pq, sk2d)


def _peer_mix_kernel(ids_hbm, uv_hbm, x_ref, h_ref, gates_ref, o_ref, ids_smem, buf, ids_sem, row_sems,
                     *, tb, picks, d):
    step = pl.program_id(0)
    ids_copy = pltpu.make_async_copy(ids_hbm.at[step], ids_smem, ids_sem)
    ids_copy.start()
    ids_copy.wait()

    def row_copy(tok, j, slot):
        eid = ids_smem[tok * picks + j]
        return pltpu.make_async_copy(uv_hbm.at[pl.ds(eid, 1), :], buf.at[slot, pl.ds(j, 1), :], row_sems.at[slot])

    def issue(tok, slot):
        for j in range(picks):
            row_copy(tok, j, slot).start()

    def wait_all(slot):
        pltpu.make_async_copy(uv_hbm.at[pl.ds(0, picks), :], buf.at[slot], row_sems.at[slot]).wait()

    issue(0, 0)

    def group(g, carry):
        rows = []
        gates = gates_ref[g]
        for u in range(SUBLANES):
            tok = g * SUBLANES + u
            slot = u % 2

            @pl.when(tok + 1 < tb)
            def _():
                issue(tok + 1, 1 - slot)

            wait_all(slot)
            r8 = pl.multiple_of(g * SUBLANES, SUBLANES)
            h8 = h_ref[pl.ds(r8, SUBLANES), :].astype(F32)
            ht = h8[u:u + 1, :]
            act = jnp.sum(buf[slot, :, :d] * ht, axis=-1, keepdims=True)
            w = gates[:, u:u + 1] * _gelu_exact(act)
            rows.append(jnp.sum(buf[slot, :, d:] * w, axis=0, keepdims=True))
        r8 = pl.multiple_of(g * SUBLANES, SUBLANES)
        o_ref[pl.ds(r8, SUBLANES), :] = x_ref[pl.ds(r8, SUBLANES), :] + jnp.concatenate(rows, axis=0)
        return carry

    lax.fori_loop(0, tb // SUBLANES, group, 0)


def _peer_mix(ids_blk, uv, x, h, gates3, *, tb):
    t, d = x.shape
    picks = gates3.shape[1]
    return pl.pallas_call(
        functools.partial(_peer_mix_kernel, tb=tb, picks=picks, d=d),
        grid=(t // tb,),
        in_specs=[
            pl.BlockSpec(memory_space=pl.ANY),
            pl.BlockSpec(memory_space=pl.ANY),
            pl.BlockSpec((tb, d), lambda i: (i, 0)),
            pl.BlockSpec((tb, d), lambda i: (i, 0)),
            pl.BlockSpec((tb // SUBLANES, picks, SUBLANES), lambda i: (i, 0, 0)),
        ],
        out_specs=pl.BlockSpec((tb, d), lambda i: (i, 0)),
        out_shape=jax.ShapeDtypeStruct((t, d), F32),
        scratch_shapes=[
            pltpu.SMEM((tb * picks,), jnp.int32),
            pltpu.VMEM((2, picks, 2 * d), F32),
            pltpu.SemaphoreType.DMA(()),
            pltpu.SemaphoreType.DMA((2,)),
        ],
        compiler_params=_params(("arbitrary",)),
    )(ids_blk, uv, x, h, gates3)


def _pad_cols(w, width):
    return jnp.pad(w, ((0, 0), (0, width - w.shape[1])))


def _layer(x2, pos, g_norm_mix, w_in, b_gate, g_cq, w_uq, g_ckv, w_ukv, g_qn, g_qr, g_kn, g_kr,
           w_a2, b_a, g_gla_out, w_out, g_norm_ffn, w_pq, sub_keys, u_emb, v_emb, *, batch, seq, tiles):
    t, d = x2.shape
    q_rank, kv_rank = g_cq.shape[0], g_ckv.shape[0]
    nope, rope = g_qn.shape[0], g_qr.shape[0]
    mla_heads = w_uq.shape[1] // (nope + rope)
    mla_v = w_ukv.shape[1] // mla_heads - nope
    gate_rank, gla_dk_all = w_a2.shape
    gla_dv = g_gla_out.shape[0]
    gla_heads = d // gla_dv
    gla_dk = gla_dk_all // gla_heads
    peer_heads, _, nkeys, half = sub_keys.shape
    assert nope == LANES and mla_v == LANES and rope <= LANES and rope % 2 == 0
    assert mla_heads * mla_v == d and gla_heads * gla_dv == d
    assert rope + gate_rank <= LANES and nkeys == LANES and half == LANES

    widths = (q_rank, kv_rank, rope, gla_dk_all, gla_dk_all, d, gate_rank, d, d, d)
    offs = [0]
    for wd in widths:
        offs.append(offs[-1] + wd)
    assert offs[-1] == w_in.shape[1]
    seg = lambda i: w_in[:, offs[i]:offs[i + 1]]
    w_main = jnp.concatenate([seg(5), seg(7), seg(8), seg(9), seg(3), seg(4), seg(0), seg(1)], axis=1).astype(BF16)
    w_small = _pad_cols(jnp.concatenate([seg(2), seg(6)], axis=1), LANES).astype(BF16)
    v_blk, og_blk = 0, d // gla_dv
    ga_blk, gb_blk = 2, 3
    q_blk = 4 * d // gla_dk
    k_blk = q_blk + gla_heads
    cq_off = 4 * d + 2 * gla_dk_all
    assert cq_off % q_rank == 0 and (cq_off + q_rank) % kv_rank == 0
    cq_blk = cq_off // q_rank
    ckv_blk = (cq_off + q_rank) // kv_rank

    p, small = _in_proj(x2, g_norm_mix[None, :], w_main, w_small, tm=tiles["in_tm"], tn=tiles["in_tn"])

    inv_freq = ROPE_THETA ** (-jnp.arange(0, rope, 2, dtype=F32) / rope)
    invf = _pad_cols(jnp.concatenate([inv_freq, inv_freq])[None, :], LANES)
    scale = (nope + rope) ** -0.5
    gq = _pad_cols(jnp.concatenate([g_qn, g_qr])[None, :] * scale, nope + LANES)
    gkr = _pad_cols(g_kr[None, :], LANES)
    w_uq_p = jnp.pad(w_uq.reshape(q_rank, mla_heads, nope + rope),
                     ((0, 0), (0, 0), (0, LANES - rope))).reshape(q_rank, -1).astype(BF16)
    q = _mla_q(p, cq_blk, g_cq[None, :], w_uq_p, gq, pos, invf,
               heads=mla_heads, rank=q_rank, nope=nope, rope=rope, tm=tiles["mla_tm"])
    k, v = _mla_kv(p, ckv_blk, g_ckv[None, :], w_ukv.astype(BF16), g_kn[None, :], small, gkr, pos, invf,
                   heads=mla_heads, rank=kv_rank, nope=nope, rope=rope, dv=mla_v, tm=tiles["mla_tm"])
    o_mla = _mla_attn(q, k, v, batch=batch, seq=seq, heads=mla_heads, dk=nope + LANES, dv=mla_v,
                      tq=tiles["attn_tq"], tk=tiles["attn_tk"])

    w2p = jnp.zeros((LANES, gla_dk_all), F32).at[rope:rope + gate_rank].set(w_a2).astype(BF16)
    o_gla = _gla(p, small, w2p, b_a[None, :], g_gla_out[None, :], batch=batch, seq=seq, heads=gla_heads,
                 dk=gla_dk, dv=gla_dv, q_blk=q_blk, k_blk=k_blk, v_blk=v_blk, og_blk=og_blk)

    x2 = _out_proj(x2, p, o_mla, o_gla, b_gate, w_out.astype(BF16), ga_blk=ga_blk, gb_blk=gb_blk,
                   tm=tiles["out_tm"], tn=tiles["out_tn"])

    sk2d = sub_keys.reshape(peer_heads * 2 * nkeys, half).astype(BF16)
    h2, ids_t, gates_t = _peer_route(x2, g_norm_ffn[None, :], w_pq.astype(BF16), sk2d, heads=peer_heads,
                                     nkeys=nkeys, half=half, topk=PEER_TOPK, tm=tiles["route_tm"])
    picks = peer_heads * PEER_TOPK
    tb = tiles["mix_tb"]
    ids_blk = ids_t.T.reshape(t // tb, tb * picks)
    gates3 = gates_t.reshape(picks, t // SUBLANES, SUBLANES).transpose(1, 0, 2)
    uv = jnp.concatenate([u_emb, v_emb], axis=1)
    return _peer_mix(ids_blk, uv, x2, h2, gates3, tb=tb)


_TILES = dict(in_tm=1024, in_tn=512, mla_tm=1024, attn_tq=512, attn_tk=512,
              out_tm=512, out_tn=512, route_tm=256, mix_tb=32)


def kernel(x, positions, g_norm_mix, w_in, b_gate, g_cq, w_uq, g_ckv, w_ukv, g_qn, g_qr, g_kn, g_kr,
           w_a2, b_a, g_gla_out, w_out, g_norm_ffn, w_pq, sub_keys, u_emb, v_emb, tiles=None):
    tiles = _TILES if tiles is None else tiles
    batch, seq, d = x.shape
    x2 = x.reshape(batch * seq, d)
    pos = positions.reshape(batch * seq, 1)
    for l in range(g_norm_mix.shape[0]):
        x2 = _layer(x2, pos, g_norm_mix[l], w_in[l], b_gate[l], g_cq[l], w_uq[l], g_ckv[l], w_ukv[l],
                    g_qn[l], g_qr[l], g_kn[l], g_kr[l], w_a2[l], b_a[l], g_gla_out[l], w_out[l],
                    g_norm_ffn[l], w_pq[l], sub_keys[l], u_emb[l], v_emb[l],
                    batch=batch, seq=seq, tiles=tiles)
    return x2.reshape(batch, seq, d)
```

```python
import functools

import jax
import jax.numpy as jnp
from jax import lax
from jax.experimental import pallas as pl
from jax.experimental.pallas import tpu as pltpu
from jax.experimental.pallas import tpu_sc as plsc

EPS = 1e-6
ROPE_THETA = 10000.0
GLA_TAU = 16.0
GLA_CHUNK = 64
PEER_TOPK = 16

LANES = 128
SUBLANES = 8
VMEM_LIMIT_BYTES = 56 * 1024 * 1024
SC_CORES = 2
SC_SUBCORES = 16

F32 = jnp.float32
BF16 = jnp.bfloat16
NEG_INF = float("-inf")


def _params(semantics):
    return pltpu.CompilerParams(dimension_semantics=semantics, vmem_limit_bytes=VMEM_LIMIT_BYTES)


def _rms(x, gain, n=None):
    ss = jnp.sum(x * x, axis=-1, keepdims=True)
    n = x.shape[-1] if n is None else n
    return x * lax.rsqrt(ss * (1.0 / n) + EPS) * gain


def _gelu_exact(x):
    return 0.5 * x * (1.0 + lax.erf(x * (0.5 ** 0.5)))


def _dot(a, b):
    return jnp.dot(a, b, preferred_element_type=F32)


def _dot_nt(a, b):
    return lax.dot_general(a, b, (((1,), (1,)), ((), ())), preferred_element_type=F32)


def _dot_tn(a, b):
    return lax.dot_general(a, b, (((0,), (0,)), ((), ())), preferred_element_type=F32)


def _in_proj_kernel(x_ref, g_ref, w_ref, ws_ref, p_ref, ps_ref, h_scr):
    @pl.when(pl.program_id(1) == 0)
    def _():
        h = _rms(x_ref[...], g_ref[...]).astype(BF16)
        h_scr[...] = h
        ps_ref[...] = _dot(h, ws_ref[...])

    p_ref[...] = _dot(h_scr[...], w_ref[...]).astype(p_ref.dtype)


def _in_proj(x, g, w_main, w_small, *, tm, tn):
    t, d = x.shape
    n = w_main.shape[1]
    return pl.pallas_call(
        _in_proj_kernel,
        grid=(t // tm, n // tn),
        in_specs=[
            pl.BlockSpec((tm, d), lambda i, j: (i, 0)),
            pl.BlockSpec((1, d), lambda i, j: (0, 0)),
            pl.BlockSpec((d, tn), lambda i, j: (0, j)),
            pl.BlockSpec((d, LANES), lambda i, j: (0, 0)),
        ],
        out_specs=[
            pl.BlockSpec((tm, tn), lambda i, j: (i, j)),
            pl.BlockSpec((tm, LANES), lambda i, j: (i, 0)),
        ],
        out_shape=[
            jax.ShapeDtypeStruct((t, n), BF16),
            jax.ShapeDtypeStruct((t, LANES), F32),
        ],
        scratch_shapes=[pltpu.VMEM((tm, d), BF16)],
        compiler_params=_params(("parallel", "arbitrary")),
    )(x, g, w_main, w_small)


def _rope_tables(pos_ref, invf_ref, rope):
    ang = pos_ref[...].astype(F32) * invf_ref[...]
    cos, sin = jnp.cos(ang), jnp.sin(ang)
    lane = lax.broadcasted_iota(jnp.int32, ang.shape, 1)
    half = rope // 2
    c = jnp.where(lane < rope, cos, 0.0)
    s_lo = jnp.where(lane < half, -sin, 0.0)
    s_hi = jnp.where(lane < half, 0.0, jnp.where(lane < rope, sin, 0.0))
    return c, s_lo, s_hi


def _apply_rope(pe, c, s_lo, s_hi, rope):
    half = rope // 2
    from_hi = pltpu.roll(pe, LANES - half, 1)
    from_lo = pltpu.roll(pe, half, 1)
    return pe * c + from_hi * s_lo + from_lo * s_hi


def _mla_q_kernel(cq_ref, gcq_ref, w_ref, gq_ref, pos_ref, invf_ref, q_ref,
                  h_scr, c_scr, slo_scr, shi_scr, *, nope, rope):
    @pl.when(pl.program_id(1) == 0)
    def _():
        h_scr[...] = _rms(cq_ref[...].astype(F32), gcq_ref[...]).astype(BF16)
        c, s_lo, s_hi = _rope_tables(pos_ref, invf_ref, rope)
        c_scr[...] = c
        slo_scr[...] = s_lo
        shi_scr[...] = s_hi

    y = _dot(h_scr[...], w_ref[...])
    g = gq_ref[...]
    qn = _rms(y[:, :nope], g[:, :nope])
    pe = _rms(y[:, nope:], g[:, nope:], n=rope)
    pe = _apply_rope(pe, c_scr[...], slo_scr[...], shi_scr[...], rope)
    q_ref[:, :nope] = qn.astype(q_ref.dtype)
    q_ref[:, nope:] = pe.astype(q_ref.dtype)


def _mla_q(p, cq_blk, g_cq, w_uq_p, gq, pos, invf, *, heads, rank, nope, rope, tm):
    t = p.shape[0]
    hw = nope + LANES
    return pl.pallas_call(
        functools.partial(_mla_q_kernel, nope=nope, rope=rope),
        grid=(t // tm, heads),
        in_specs=[
            pl.BlockSpec((tm, rank), lambda i, j: (i, cq_blk)),
            pl.BlockSpec((1, rank), lambda i, j: (0, 0)),
            pl.BlockSpec((rank, hw), lambda i, j: (0, j)),
            pl.BlockSpec((1, hw), lambda i, j: (0, 0)),
            pl.BlockSpec((tm, 1), lambda i, j: (i, 0)),
            pl.BlockSpec((1, LANES), lambda i, j: (0, 0)),
        ],
        out_specs=pl.BlockSpec((tm, hw), lambda i, j: (i, j)),
        out_shape=jax.ShapeDtypeStruct((t, heads * hw), BF16),
        scratch_shapes=[
            pltpu.VMEM((tm, rank), BF16),
            pltpu.VMEM((tm, LANES), F32),
            pltpu.VMEM((tm, LANES), F32),
            pltpu.VMEM((tm, LANES), F32),
        ],
        compiler_params=_params(("parallel", "arbitrary")),
    )(p, g_cq, w_uq_p, gq, pos, invf)


def _mla_kv_kernel(ckv_ref, gckv_ref, w_ref, gkn_ref, small_ref, gkr_ref, pos_ref, invf_ref,
                   k_ref, v_ref, h_scr, kpe_scr, *, nope, rope):
    @pl.when(pl.program_id(1) == 0)
    def _():
        h_scr[...] = _rms(ckv_ref[...].astype(F32), gckv_ref[...]).astype(BF16)
        c, s_lo, s_hi = _rope_tables(pos_ref, invf_ref, rope)
        sm = small_ref[...]
        lane = lax.broadcasted_iota(jnp.int32, sm.shape, 1)
        pe = _rms(jnp.where(lane < rope, sm, 0.0), gkr_ref[...], n=rope)
        kpe_scr[...] = _apply_rope(pe, c, s_lo, s_hi, rope).astype(BF16)

    y = _dot(h_scr[...], w_ref[...])
    k_ref[:, :nope] = _rms(y[:, :nope], gkn_ref[...]).astype(k_ref.dtype)
    k_ref[:, nope:] = kpe_scr[...]
    v_ref[...] = y[:, nope:].astype(v_ref.dtype)


def _mla_kv(p, ckv_blk, g_ckv, w_ukv, g_kn, small, gkr, pos, invf, *, heads, rank, nope, rope, dv, tm):
    t = p.shape[0]
    kw = nope + LANES
    return pl.pallas_call(
        functools.partial(_mla_kv_kernel, nope=nope, rope=rope),
        grid=(t // tm, heads),
        in_specs=[
            pl.BlockSpec((tm, rank), lambda i, j: (i, ckv_blk)),
            pl.BlockSpec((1, rank), lambda i, j: (0, 0)),
            pl.BlockSpec((rank, nope + dv), lambda i, j: (0, j)),
            pl.BlockSpec((1, nope), lambda i, j: (0, 0)),
            pl.BlockSpec((tm, LANES), lambda i, j: (i, 0)),
            pl.BlockSpec((1, LANES), lambda i, j: (0, 0)),
            pl.BlockSpec((tm, 1), lambda i, j: (i, 0)),
            pl.BlockSpec((1, LANES), lambda i, j: (0, 0)),
        ],
        out_specs=[
            pl.BlockSpec((tm, kw), lambda i, j: (i, j)),
            pl.BlockSpec((tm, dv), lambda i, j: (i, j)),
        ],
        out_shape=[
            jax.ShapeDtypeStruct((t, heads * kw), BF16),
            jax.ShapeDtypeStruct((t, heads * dv), BF16),
        ],
        scratch_shapes=[pltpu.VMEM((tm, rank), BF16), pltpu.VMEM((tm, LANES), BF16)],
        compiler_params=_params(("parallel", "arbitrary")),
    )(p, g_ckv, w_ukv, g_kn, small, gkr, pos, invf)


def _attn_kernel(q_ref, k_ref, v_ref, o_ref, *, tq, tk):
    qi = pl.program_id(2)
    q = q_ref[...]
    dv = v_ref.shape[1]
    row = qi * tq + lax.broadcasted_iota(jnp.int32, (tq, tk), 0)
    col0 = lax.broadcasted_iota(jnp.int32, (tq, tk), 1)

    def body(kb, carry):
        m, l, acc = carry
        k0 = pl.multiple_of(kb * tk, tk)
        s = _dot_nt(q, k_ref[pl.ds(k0, tk), :])
        s = jnp.where(col0 + k0 <= row, s, NEG_INF)
        m_new = jnp.maximum(m, jnp.max(s, axis=-1, keepdims=True))
        alpha = jnp.exp(m - m_new)
        pr = jnp.exp(s - m_new)
        l = alpha * l + jnp.sum(pr, axis=-1, keepdims=True)
        acc = alpha * acc + _dot(pr.astype(BF16), v_ref[pl.ds(k0, tk), :])
        return m_new, l, acc

    init = (jnp.full((tq, 1), NEG_INF, F32), jnp.zeros((tq, 1), F32), jnp.zeros((tq, dv), F32))
    nkb = ((qi + 1) * tq + tk - 1) // tk
    _, l, acc = lax.fori_loop(0, nkb, body, init)
    o_ref[...] = (acc / l).astype(o_ref.dtype)


def _mla_attn(q, k, v, *, batch, seq, heads, dk, dv, tq, tk):
    t = q.shape[0]
    nq = seq // tq
    return pl.pallas_call(
        functools.partial(_attn_kernel, tq=tq, tk=tk),
        grid=(batch, heads, nq),
        in_specs=[
            pl.BlockSpec((tq, dk), lambda b, h, i: (b * nq + i, h)),
            pl.BlockSpec((seq, dk), lambda b, h, i: (b, h)),
            pl.BlockSpec((seq, dv), lambda b, h, i: (b, h)),
        ],
        out_specs=pl.BlockSpec((tq, dv), lambda b, h, i: (b * nq + i, h)),
        out_shape=jax.ShapeDtypeStruct((t, heads * dv), BF16),
        compiler_params=_params(("parallel", "parallel", "arbitrary")),
    )(q, k, v)


def _gla_kernel(q_ref, k_ref, v_ref, og_ref, small_ref, w2_ref, ba_ref, gon_ref, o_ref, st_scr,
                *, seq, dk, dv, chunk):
    c = chunk
    st_scr[...] = jnp.zeros_like(st_scr)
    r_i = lax.broadcasted_iota(jnp.int32, (c, c), 0)
    c_i = lax.broadcasted_iota(jnp.int32, (c, c), 1)
    tri = jnp.where(c_i <= r_i, 1.0, 0.0).astype(BF16)
    row_id = lax.broadcasted_iota(jnp.int32, (c, 1), 0)
    w2 = w2_ref[...]
    ba = ba_ref[...]
    gon = gon_ref[...]
    q_scale = dk ** -0.5

    def chunk_step(ci, carry):
        r0 = pl.multiple_of(ci * c, c)
        qc = q_ref[pl.ds(r0, c), :].astype(F32) * q_scale
        kc = k_ref[pl.ds(r0, c), :].astype(F32)
        vc = v_ref[pl.ds(r0, c), :]
        z = _dot(small_ref[pl.ds(r0, c), :].astype(BF16), w2) + ba
        la = jax.nn.log_sigmoid(z) * (1.0 / GLA_TAU)
        hi = la.astype(BF16)
        r1 = la - hi.astype(F32)
        mid = r1.astype(BF16)
        lo = (r1 - mid.astype(F32)).astype(BF16)
        b = _dot(tri, hi) + _dot(tri, mid) + _dot(tri, lo)

        st = st_scr[...]
        inter = _dot_nt((qc * jnp.exp(b)).astype(BF16), st.astype(BF16))

        att = jnp.zeros((c, c), F32)
        for j in range(c):
            lo_r = (j // SUBLANES) * SUBLANES
            d = b[lo_r:, :] - b[j:j + 1, :]
            e = jnp.exp(jnp.where(row_id[lo_r:, :] >= j, d, NEG_INF))
            col = jnp.sum(qc[lo_r:, :] * kc[j:j + 1, :] * e, axis=-1, keepdims=True)
            if lo_r:
                col = jnp.concatenate([jnp.zeros((lo_r, 1), F32), col], axis=0)
            att = jnp.where(c_i == j, col, att)
        o = inter + _dot(att.astype(BF16), vc)

        b_last = b[c - 1:c, :]
        k_dec = (kc * jnp.exp(b_last - b)).astype(BF16)
        st_scr[...] = st * jnp.exp(b_last) + _dot_tn(vc, k_dec)

        og = og_ref[pl.ds(r0, c), :].astype(F32)
        out = _rms(o, gon) * (og * jax.nn.sigmoid(og))
        o_ref[pl.ds(r0, c), :] = out.astype(o_ref.dtype)
        return carry

    lax.fori_loop(0, seq // c, chunk_step, 0)


def _gla(p, small, w2p, b_a, g_on, *, batch, seq, heads, dk, dv, q_blk, k_blk, v_blk, og_blk):
    t = p.shape[0]
    return pl.pallas_call(
        functools.partial(_gla_kernel, seq=seq, dk=dk, dv=dv, chunk=GLA_CHUNK),
        grid=(batch, heads),
        in_specs=[
            pl.BlockSpec((seq, dk), lambda b, h: (b, q_blk + h)),
            pl.BlockSpec((seq, dk), lambda b, h: (b, k_blk + h)),
            pl.BlockSpec((seq, dv), lambda b, h: (b, v_blk + h)),
            pl.BlockSpec((seq, dv), lambda b, h: (b, og_blk + h)),
            pl.BlockSpec((seq, LANES), lambda b, h: (b, 0)),
            pl.BlockSpec((LANES, dk), lambda b, h: (0, h)),
            pl.BlockSpec((1, dk), lambda b, h: (0, h)),
            pl.BlockSpec((1, dv), lambda b, h: (0, 0)),
        ],
        out_specs=pl.BlockSpec((seq, dv), lambda b, h: (b, h)),
        out_shape=jax.ShapeDtypeStruct((t, heads * dv), BF16),
        scratch_shapes=[pltpu.VMEM((dv, dk), F32)],
        compiler_params=_params(("parallel", "parallel")),
    )(p, p, p, p, small, w2p, b_a, g_on)


def _out_proj_kernel(x_ref, ga_ref, gb_ref, oa_ref, ob_ref, bg_ref, w_ref, o_ref, m_scr):
    @pl.when(pl.program_id(1) == 0)
    def _():
        bg = bg_ref[...]
        sa = jax.nn.sigmoid(ga_ref[...].astype(F32) + bg[0:1, :])
        sb = jax.nn.sigmoid(gb_ref[...].astype(F32) + bg[1:2, :])
        m_scr[...] = (sa * oa_ref[...].astype(F32) + sb * ob_ref[...].astype(F32)).astype(BF16)

    o_ref[...] = x_ref[...] + _dot(m_scr[...], w_ref[...])


def _out_proj(x, p, o_mla, o_gla, b_gate, w_out, *, ga_blk, gb_blk, tm, tn):
    t, d = x.shape
    return pl.pallas_call(
        _out_proj_kernel,
        grid=(t // tm, d // tn),
        in_specs=[
            pl.BlockSpec((tm, tn), lambda i, j: (i, j)),
            pl.BlockSpec((tm, d), lambda i, j: (i, ga_blk)),
            pl.BlockSpec((tm, d), lambda i, j: (i, gb_blk)),
            pl.BlockSpec((tm, d), lambda i, j: (i, 0)),
            pl.BlockSpec((tm, d), lambda i, j: (i, 0)),
            pl.BlockSpec((2, d), lambda i, j: (0, 0)),
            pl.BlockSpec((d, tn), lambda i, j: (0, j)),
        ],
        out_specs=pl.BlockSpec((tm, tn), lambda i, j: (i, j)),
        out_shape=jax.ShapeDtypeStruct((t, d), F32),
        scratch_shapes=[pltpu.VMEM((tm, d), BF16)],
        compiler_params=_params(("parallel", "arbitrary")),
    )(x, p, p, o_mla, o_gla, b_gate, w_out)


def _topk_rows(s, k, payload=None):
    n = s.shape[0]
    iota = lax.broadcasted_iota(jnp.int32, s.shape, 0).astype(F32)
    kiota = lax.broadcasted_iota(jnp.int32, (k, s.shape[1]), 0)
    vals = jnp.zeros((k, s.shape[1]), F32)
    picks = jnp.zeros((k, s.shape[1]), F32)
    for r in range(k):
        m = jnp.max(s, axis=0, keepdims=True)
        idx = jnp.min(jnp.where(s == m, iota, float(n)), axis=0, keepdims=True)
        hit = iota == idx
        if payload is None:
            pick = idx
        else:
            pick = jnp.sum(jnp.where(hit, payload, 0.0), axis=0, keepdims=True)
        vals = jnp.where(kiota == r, m, vals)
        picks = jnp.where(kiota == r, pick, picks)
        s = jnp.where(hit, NEG_INF, s)
    return vals, picks


def _peer_route_kernel(x_ref, g_ref, w_ref, sk_ref, h_ref, ids_ref, gates_ref, *, heads, nkeys, half, topk):
    h = _rms(x_ref[...], g_ref[...]).astype(BF16)
    h_ref[...] = h
    qf = _dot(h, w_ref[...])
    for hd in range(heads):
        tops = []
        for part in range(2):
            o = (hd * 2 + part) * half
            qh = qf[:, o:o + half].astype(BF16)
            keys = sk_ref[(hd * 2 + part) * nkeys:(hd * 2 + part + 1) * nkeys, :]
            tops.append(_topk_rows(_dot_nt(keys, qh), topk))
        (s1, i1), (s2, i2) = tops
        cand_s = jnp.concatenate([s1[a:a + 1, :] + s2 for a in range(topk)], axis=0)
        cand_id = jnp.concatenate([i1[a:a + 1, :] * float(nkeys) + i2 for a in range(topk)], axis=0)
        best_s, best_id = _topk_rows(cand_s, topk, payload=cand_id)
        e = jnp.exp(best_s - best_s[0:1, :])
        gate = e / jnp.sum(e, axis=0, keepdims=True)
        ids_ref[hd * topk:(hd + 1) * topk, :] = best_id.astype(jnp.int32)
        gates_ref[hd * topk:(hd + 1) * topk, :] = gate


def _peer_route(x, g, w_pq, sk2d, *, heads, nkeys, half, topk, tm):
    t, d = x.shape
    dq = w_pq.shape[1]
    return pl.pallas_call(
        functools.partial(_peer_route_kernel, heads=heads, nkeys=nkeys, half=half, topk=topk),
        grid=(t // tm,),
        in_specs=[
            pl.BlockSpec((tm, d), lambda i: (i, 0)),
            pl.BlockSpec((1, d), lambda i: (0, 0)),
            pl.BlockSpec((d, dq), lambda i: (0, 0)),
            pl.BlockSpec((heads * 2 * nkeys, half), lambda i: (0, 0)),
        ],
        out_specs=[
            pl.BlockSpec((tm, d), lambda i: (i, 0)),
            pl.BlockSpec((heads * topk, tm), lambda i: (0, i)),
            pl.BlockSpec((heads * topk, tm), lambda i: (0, i)),
        ],
        out_shape=[
            jax.ShapeDtypeStruct((t, d), BF16),
            jax.ShapeDtypeStruct((heads * topk, t), jnp.int32),
            jax.ShapeDtypeStruct((heads * topk, t), F32),
        ],
        compiler_params=_params(("parallel",)),
    )(x, g, w_pq, sk2d)


def _pack_expert_table(u_emb, v_emb):
    def pack(w):
        bits = lax.bitcast_convert_type(w.astype(BF16), jnp.uint16).astype(jnp.uint32)
        half = w.shape[1] // 2
        return bits[:, :half] | (bits[:, half:] << 16)

    return lax.bitcast_convert_type(jnp.concatenate([pack(u_emb), pack(v_emb)], axis=1), jnp.int32)


def _unpack_words(w):
    lo = lax.bitcast_convert_type(w << 16, F32)
    hi = lax.bitcast_convert_type(w & jnp.int32(-65536), F32)
    return lo, hi


def _expert_mix(words, h, gate, d):
    half = d // 2
    u_lo, u_hi = _unpack_words(words[:, :half])
    act = jnp.sum(u_lo * h[:, :half] + u_hi * h[:, half:], axis=-1, keepdims=True)
    w = gate * _gelu_exact(act)
    v_lo, v_hi = _unpack_words(words[:, half:])
    return jnp.concatenate([jnp.sum(v_lo * w, axis=0, keepdims=True),
                            jnp.sum(v_hi * w, axis=0, keepdims=True)], axis=1)


def _peer_mix_kernel(ids_hbm, uv_hbm, x_ref, h_ref, gates_ref, o_ref, ids_smem, buf, ids_sem, row_sems,
                     *, tb, picks, d):
    step = pl.program_id(0)
    ids_copy = pltpu.make_async_copy(ids_hbm.at[step], ids_smem, ids_sem)
    ids_copy.start()
    ids_copy.wait()

    def row_copy(tok, j, slot):
        eid = ids_smem[tok * picks + j]
        return pltpu.make_async_copy(uv_hbm.at[pl.ds(eid, 1), :], buf.at[slot, pl.ds(j, 1), :], row_sems.at[slot])

    def issue(tok, slot):
        for j in range(picks):
            row_copy(tok, j, slot).start()

    def wait_all(slot):
        pltpu.make_async_copy(uv_hbm.at[pl.ds(0, picks), :], buf.at[slot], row_sems.at[slot]).wait()

    issue(0, 0)

    def group(g, carry):
        rows = []
        gates = gates_ref[g]
        for u in range(SUBLANES):
            tok = g * SUBLANES + u
            slot = u % 2

            @pl.when(tok + 1 < tb)
            def _():
                issue(tok + 1, 1 - slot)

            wait_all(slot)
            r8 = pl.multiple_of(g * SUBLANES, SUBLANES)
            h8 = h_ref[pl.ds(r8, SUBLANES), :].astype(F32)
            rows.append(_expert_mix(buf[slot], h8[u:u + 1, :], gates[:, u:u + 1], d))
        r8 = pl.multiple_of(g * SUBLANES, SUBLANES)
        o_ref[pl.ds(r8, SUBLANES), :] = x_ref[pl.ds(r8, SUBLANES), :] + jnp.concatenate(rows, axis=0)
        return carry

    lax.fori_loop(0, tb // SUBLANES, group, 0)


def _peer_mix_direct(ids_blk, uv, x, h, gates3, *, tb):
    t, d = x.shape
    picks = gates3.shape[1]
    return pl.pallas_call(
        functools.partial(_peer_mix_kernel, tb=tb, picks=picks, d=d),
        grid=(t // tb,),
        in_specs=[
            pl.BlockSpec(memory_space=pl.ANY),
            pl.BlockSpec(memory_space=pl.ANY),
            pl.BlockSpec((tb, d), lambda i: (i, 0)),
            pl.BlockSpec((tb, d), lambda i: (i, 0)),
            pl.BlockSpec((tb // SUBLANES, picks, SUBLANES), lambda i: (i, 0, 0)),
        ],
        out_specs=pl.BlockSpec((tb, d), lambda i: (i, 0)),
        out_shape=jax.ShapeDtypeStruct((t, d), F32),
        scratch_shapes=[
            pltpu.SMEM((tb * picks,), jnp.int32),
            pltpu.VMEM((2, picks, d), jnp.int32),
            pltpu.SemaphoreType.DMA(()),
            pltpu.SemaphoreType.DMA((2,)),
        ],
        compiler_params=_params(("arbitrary",)),
    )(ids_blk, uv, x, h, gates3)


def _peer_mix_staged_kernel(rows_ref, x_ref, h_ref, gates_ref, o_ref, *, tb, picks, d):
    for g in range(tb // SUBLANES):
        gates = gates_ref[g]
        h8 = h_ref[g * SUBLANES:(g + 1) * SUBLANES, :].astype(F32)
        rows = []
        for u in range(SUBLANES):
            r0 = (g * SUBLANES + u) * picks
            rows.append(_expert_mix(rows_ref[r0:r0 + picks, :], h8[u:u + 1, :], gates[:, u:u + 1], d))
        sl = slice(g * SUBLANES, (g + 1) * SUBLANES)
        o_ref[sl, :] = x_ref[sl, :] + jnp.concatenate(rows, axis=0)


def _peer_mix_staged(staged, x, h, gates3, *, tb):
    t, d = x.shape
    picks = gates3.shape[1]
    return pl.pallas_call(
        functools.partial(_peer_mix_staged_kernel, tb=tb, picks=picks, d=d),
        grid=(t // tb,),
        in_specs=[
            pl.BlockSpec((tb * picks, d), lambda i: (i, 0)),
            pl.BlockSpec((tb, d), lambda i: (i, 0)),
            pl.BlockSpec((tb, d), lambda i: (i, 0)),
            pl.BlockSpec((tb // SUBLANES, picks, SUBLANES), lambda i: (i, 0, 0)),
        ],
        out_specs=pl.BlockSpec((tb, d), lambda i: (i, 0)),
        out_shape=jax.ShapeDtypeStruct((t, d), F32),
        compiler_params=_params(("parallel",)),
    )(staged, x, h, gates3)


def _sc_gather_rows(table, idx, *, chunk):
    n_rows = idx.shape[0]
    d = table.shape[1]
    workers = SC_CORES * SC_SUBCORES
    assert n_rows % (workers * chunk) == 0 and chunk % SUBLANES == 0 and chunk <= LANES
    rows_per_worker = n_rows // workers
    mesh = plsc.VectorSubcoreMesh(core_axis_name="c", subcore_axis_name="s")

    @functools.partial(
        pl.kernel, mesh=mesh,
        out_type=jax.ShapeDtypeStruct((n_rows, d), table.dtype),
        scratch_types=[
            pltpu.VMEM((chunk,), jnp.int32),
            pltpu.VMEM((chunk, d), table.dtype),
            pltpu.SemaphoreType.DMA,
        ],
    )
    def gather(table_hbm, idx_hbm, out_hbm, idx_v, rows_v, sem):
        worker = lax.axis_index("s") * SC_CORES + lax.axis_index("c")
        base = worker * rows_per_worker

        @pl.loop(0, rows_per_worker // chunk)
        def _(c):
            off = pl.multiple_of(base + c * chunk, SUBLANES)
            pltpu.sync_copy(idx_hbm.at[pl.ds(off, chunk)], idx_v)
            pltpu.async_copy(table_hbm.at[idx_v], rows_v, sem).wait()
            pltpu.sync_copy(rows_v, out_hbm.at[pl.ds(off, chunk)])

    return gather(table, idx)


def _pad_cols(w, width):
    return jnp.pad(w, ((0, 0), (0, width - w.shape[1])))


def _layer(x2, pos, g_norm_mix, w_in, b_gate, g_cq, w_uq, g_ckv, w_ukv, g_qn, g_qr, g_kn, g_kr,
           w_a2, b_a, g_gla_out, w_out, g_norm_ffn, w_pq, sub_keys, u_emb, v_emb, *, batch, seq, tiles):
    t, d = x2.shape
    q_rank, kv_rank = g_cq.shape[0], g_ckv.shape[0]
    nope, rope = g_qn.shape[0], g_qr.shape[0]
    mla_heads = w_uq.shape[1] // (nope + rope)
    mla_v = w_ukv.shape[1] // mla_heads - nope
    gate_rank, gla_dk_all = w_a2.shape
    gla_dv = g_gla_out.shape[0]
    gla_heads = d // gla_dv
    gla_dk = gla_dk_all // gla_heads
    peer_heads, _, nkeys, half = sub_keys.shape
    assert nope == LANES and mla_v == LANES and rope <= LANES and rope % 2 == 0
    assert mla_heads * mla_v == d and gla_heads * gla_dv == d
    assert rope + gate_rank <= LANES and nkeys == LANES and half == LANES

    widths = (q_rank, kv_rank, rope, gla_dk_all, gla_dk_all, d, gate_rank, d, d, d)
    offs = [0]
    for wd in widths:
        offs.append(offs[-1] + wd)
    assert offs[-1] == w_in.shape[1]
    seg = lambda i: w_in[:, offs[i]:offs[i + 1]]
    w_main = jnp.concatenate([seg(5), seg(7), seg(8), seg(9), seg(3), seg(4), seg(0), seg(1)], axis=1).astype(BF16)
    w_small = _pad_cols(jnp.concatenate([seg(2), seg(6)], axis=1), LANES).astype(BF16)
    v_blk, og_blk = 0, d // gla_dv
    ga_blk, gb_blk = 2, 3
    q_blk = 4 * d // gla_dk
    k_blk = q_blk + gla_heads
    cq_off = 4 * d + 2 * gla_dk_all
    assert cq_off % q_rank == 0 and (cq_off + q_rank) % kv_rank == 0
    cq_blk = cq_off // q_rank
    ckv_blk = (cq_off + q_rank) // kv_rank

    p, small = _in_proj(x2, g_norm_mix[None, :], w_main, w_small, tm=tiles["in_tm"], tn=tiles["in_tn"])

    inv_freq = ROPE_THETA ** (-jnp.arange(0, rope, 2, dtype=F32) / rope)
    invf = _pad_cols(jnp.concatenate([inv_freq, inv_freq])[None, :], LANES)
    scale = (nope + rope) ** -0.5
    gq = _pad_cols(jnp.concatenate([g_qn, g_qr])[None, :] * scale, nope + LANES)
    gkr = _pad_cols(g_kr[None, :], LANES)
    w_uq_p = jnp.pad(w_uq.reshape(q_rank, mla_heads, nope + rope),
                     ((0, 0), (0, 0), (0, LANES - rope))).reshape(q_rank, -1).astype(BF16)
    q = _mla_q(p, cq_blk, g_cq[None, :], w_uq_p, gq, pos, invf,
               heads=mla_heads, rank=q_rank, nope=nope, rope=rope, tm=tiles["mla_tm"])
    k, v = _mla_kv(p, ckv_blk, g_ckv[None, :], w_ukv.astype(BF16), g_kn[None, :], small, gkr, pos, invf,
                   heads=mla_heads, rank=kv_rank, nope=nope, rope=rope, dv=mla_v, tm=tiles["mla_tm"])
    o_mla = _mla_attn(q, k, v, batch=batch, seq=seq, heads=mla_heads, dk=nope + LANES, dv=mla_v,
                      tq=tiles["attn_tq"], tk=tiles["attn_tk"])

    w2p = jnp.zeros((LANES, gla_dk_all), F32).at[rope:rope + gate_rank].set(w_a2).astype(BF16)
    o_gla = _gla(p, small, w2p, b_a[None, :], g_gla_out[None, :], batch=batch, seq=seq, heads=gla_heads,
                 dk=gla_dk, dv=gla_dv, q_blk=q_blk, k_blk=k_blk, v_blk=v_blk, og_blk=og_blk)

    x2 = _out_proj(x2, p, o_mla, o_gla, b_gate, w_out.astype(BF16), ga_blk=ga_blk, gb_blk=gb_blk,
                   tm=tiles["out_tm"], tn=tiles["out_tn"])

    sk2d = sub_keys.reshape(peer_heads * 2 * nkeys, half).astype(BF16)
    h2, ids_t, gates_t = _peer_route(x2, g_norm_ffn[None, :], w_pq.astype(BF16), sk2d, heads=peer_heads,
                                     nkeys=nkeys, half=half, topk=PEER_TOPK, tm=tiles["route_tm"])
    picks = peer_heads * PEER_TOPK
    tb = tiles["mix_tb"]
    ids_tok = ids_t.T
    gates3 = gates_t.reshape(picks, t // SUBLANES, SUBLANES).transpose(1, 0, 2)
    uv = _pack_expert_table(u_emb, v_emb)
    t_direct = tiles["mix_direct_tokens"]
    g_direct = t_direct // SUBLANES
    out_direct = _peer_mix_direct(ids_tok[:t_direct].reshape(t_direct // tb, tb * picks), uv,
                                  x2[:t_direct], h2[:t_direct], gates3[:g_direct], tb=tb)
    staged = _sc_gather_rows(uv, ids_tok[t_direct:].reshape(-1), chunk=tiles["sc_chunk"])
    out_staged = _peer_mix_staged(staged, x2[t_direct:], h2[t_direct:], gates3[g_direct:],
                                  tb=tiles["mix_staged_tb"])
    return jnp.concatenate([out_direct, out_staged], axis=0)


_TILES = dict(in_tm=1024, in_tn=512, mla_tm=1024, attn_tq=512, attn_tk=512,
              out_tm=512, out_tn=512, route_tm=256, mix_tb=32,
              mix_direct_tokens=7168, mix_staged_tb=16, sc_chunk=32)


def kernel(x, positions, g_norm_mix, w_in, b_gate, g_cq, w_uq, g_ckv, w_ukv, g_qn, g_qr, g_kn, g_kr,
           w_a2, b_a, g_gla_out, w_out, g_norm_ffn, w_pq, sub_keys, u_emb, v_emb, tiles=None):
    tiles = _TILES if tiles is None else tiles
    batch, seq, d = x.shape
    x2 = x.reshape(batch * seq, d)
    pos = positions.reshape(batch * seq, 1)
    for l in range(g_norm_mix.shape[0]):
        x2 = _layer(x2, pos, g_norm_mix[l], w_in[l], b_gate[l], g_cq[l], w_uq[l], g_ckv[l], w_ukv[l],
                    g_qn[l], g_qr[l], g_kn[l], g_kr[l], w_a2[l], b_a[l], g_gla_out[l], w_out[l],
                    g_norm_ffn[l], w_pq[l], sub_keys[l], u_emb[l], v_emb[l],
                    batch=batch, seq=seq, tiles=tiles)
    return x2.reshape(batch, seq, d)
```

```python
import functools

import jax
import jax.numpy as jnp
from jax import lax
from jax.experimental import pallas as pl
from jax.experimental.pallas import tpu as pltpu
from jax.experimental.pallas import tpu_sc as plsc

EPS = 1e-6
ROPE_THETA = 10000.0
GLA_TAU = 16.0
GLA_CHUNK = 64
PEER_TOPK = 16

LANES = 128
SUBLANES = 8
VMEM_LIMIT_BYTES = 56 * 1024 * 1024
SC_CORES = 2
SC_SUBCORES = 16

F32 = jnp.float32
BF16 = jnp.bfloat16
NEG_INF = float("-inf")


def _params(semantics):
    return pltpu.CompilerParams(dimension_semantics=semantics, vmem_limit_bytes=VMEM_LIMIT_BYTES)


def _rms(x, gain, n=None):
    ss = jnp.sum(x * x, axis=-1, keepdims=True)
    n = x.shape[-1] if n is None else n
    return x * lax.rsqrt(ss * (1.0 / n) + EPS) * gain


def _gelu_exact(x):
    return 0.5 * x * (1.0 + lax.erf(x * (0.5 ** 0.5)))


def _dot(a, b):
    return jnp.dot(a, b, preferred_element_type=F32)


def _dot_nt(a, b):
    return lax.dot_general(a, b, (((1,), (1,)), ((), ())), preferred_element_type=F32)


def _dot_tn(a, b):
    return lax.dot_general(a, b, (((0,), (0,)), ((), ())), preferred_element_type=F32)


def _in_proj_kernel(x_ref, g_ref, w_ref, ws_ref, p_ref, ps_ref, h_scr):
    @pl.when(pl.program_id(1) == 0)
    def _():
        h = _rms(x_ref[...], g_ref[...]).astype(BF16)
        h_scr[...] = h
        ps_ref[...] = _dot(h, ws_ref[...])

    p_ref[...] = _dot(h_scr[...], w_ref[...]).astype(p_ref.dtype)


def _in_proj(x, g, w_main, w_small, *, tm, tn):
    t, d = x.shape
    n = w_main.shape[1]
    return pl.pallas_call(
        _in_proj_kernel,
        grid=(t // tm, n // tn),
        in_specs=[
            pl.BlockSpec((tm, d), lambda i, j: (i, 0)),
            pl.BlockSpec((1, d), lambda i, j: (0, 0)),
            pl.BlockSpec((d, tn), lambda i, j: (0, j)),
            pl.BlockSpec((d, LANES), lambda i, j: (0, 0)),
        ],
        out_specs=[
            pl.BlockSpec((tm, tn), lambda i, j: (i, j)),
            pl.BlockSpec((tm, LANES), lambda i, j: (i, 0)),
        ],
        out_shape=[
            jax.ShapeDtypeStruct((t, n), BF16),
            jax.ShapeDtypeStruct((t, LANES), F32),
        ],
        scratch_shapes=[pltpu.VMEM((tm, d), BF16)],
        compiler_params=_params(("parallel", "arbitrary")),
    )(x, g, w_main, w_small)


def _rope_tables(pos_ref, invf_ref, rope):
    ang = pos_ref[...].astype(F32) * invf_ref[...]
    cos, sin = jnp.cos(ang), jnp.sin(ang)
    lane = lax.broadcasted_iota(jnp.int32, ang.shape, 1)
    half = rope // 2
    c = jnp.where(lane < rope, cos, 0.0)
    s_lo = jnp.where(lane < half, -sin, 0.0)
    s_hi = jnp.where(lane < half, 0.0, jnp.where(lane < rope, sin, 0.0))
    return c, s_lo, s_hi


def _apply_rope(pe, c, s_lo, s_hi, rope):
    half = rope // 2
    from_hi = pltpu.roll(pe, LANES - half, 1)
    from_lo = pltpu.roll(pe, half, 1)
    return pe * c + from_hi * s_lo + from_lo * s_hi


def _mla_q_kernel(cq_ref, gcq_ref, w_ref, gq_ref, pos_ref, invf_ref, q_ref,
                  h_scr, c_scr, slo_scr, shi_scr, *, nope, rope):
    @pl.when(pl.program_id(1) == 0)
    def _():
        h_scr[...] = _rms(cq_ref[...].astype(F32), gcq_ref[...]).astype(BF16)
        c, s_lo, s_hi = _rope_tables(pos_ref, invf_ref, rope)
        c_scr[...] = c
        slo_scr[...] = s_lo
        shi_scr[...] = s_hi

    y = _dot(h_scr[...], w_ref[...])
    g = gq_ref[...]
    qn = _rms(y[:, :nope], g[:, :nope])
    pe = _rms(y[:, nope:], g[:, nope:], n=rope)
    pe = _apply_rope(pe, c_scr[...], slo_scr[...], shi_scr[...], rope)
    q_ref[:, :nope] = qn.astype(q_ref.dtype)
    q_ref[:, nope:] = pe.astype(q_ref.dtype)


def _mla_q(p, cq_blk, g_cq, w_uq_p, gq, pos, invf, *, heads, rank, nope, rope, tm):
    t = p.shape[0]
    hw = nope + LANES
    return pl.pallas_call(
        functools.partial(_mla_q_kernel, nope=nope, rope=rope),
        grid=(t // tm, heads),
        in_specs=[
            pl.BlockSpec((tm, rank), lambda i, j: (i, cq_blk)),
            pl.BlockSpec((1, rank), lambda i, j: (0, 0)),
            pl.BlockSpec((rank, hw), lambda i, j: (0, j)),
            pl.BlockSpec((1, hw), lambda i, j: (0, 0)),
            pl.BlockSpec((tm, 1), lambda i, j: (i, 0)),
            pl.BlockSpec((1, LANES), lambda i, j: (0, 0)),
        ],
        out_specs=pl.BlockSpec((tm, hw), lambda i, j: (i, j)),
        out_shape=jax.ShapeDtypeStruct((t, heads * hw), BF16),
        scratch_shapes=[
            pltpu.VMEM((tm, rank), BF16),
            pltpu.VMEM((tm, LANES), F32),
            pltpu.VMEM((tm, LANES), F32),
            pltpu.VMEM((tm, LANES), F32),
        ],
        compiler_params=_params(("parallel", "arbitrary")),
    )(p, g_cq, w_uq_p, gq, pos, invf)


def _mla_kv_kernel(ckv_ref, gckv_ref, w_ref, gkn_ref, small_ref, gkr_ref, pos_ref, invf_ref,
                   k_ref, v_ref, h_scr, kpe_scr, *, nope, rope):
    @pl.when(pl.program_id(1) == 0)
    def _():
        h_scr[...] = _rms(ckv_ref[...].astype(F32), gckv_ref[...]).astype(BF16)
        c, s_lo, s_hi = _rope_tables(pos_ref, invf_ref, rope)
        sm = small_ref[...]
        lane = lax.broadcasted_iota(jnp.int32, sm.shape, 1)
        pe = _rms(jnp.where(lane < rope, sm, 0.0), gkr_ref[...], n=rope)
        kpe_scr[...] = _apply_rope(pe, c, s_lo, s_hi, rope).astype(BF16)

    y = _dot(h_scr[...], w_ref[...])
    k_ref[:, :nope] = _rms(y[:, :nope], gkn_ref[...]).astype(k_ref.dtype)
    k_ref[:, nope:] = kpe_scr[...]
    v_ref[...] = y[:, nope:].astype(v_ref.dtype)


def _mla_kv(p, ckv_blk, g_ckv, w_ukv, g_kn, small, gkr, pos, invf, *, heads, rank, nope, rope, dv, tm):
    t = p.shape[0]
    kw = nope + LANES
    return pl.pallas_call(
        functools.partial(_mla_kv_kernel, nope=nope, rope=rope),
        grid=(t // tm, heads),
        in_specs=[
            pl.BlockSpec((tm, rank), lambda i, j: (i, ckv_blk)),
            pl.BlockSpec((1, rank), lambda i, j: (0, 0)),
            pl.BlockSpec((rank, nope + dv), lambda i, j: (0, j)),
            pl.BlockSpec((1, nope), lambda i, j: (0, 0)),
            pl.BlockSpec((tm, LANES), lambda i, j: (i, 0)),
            pl.BlockSpec((1, LANES), lambda i, j: (0, 0)),
            pl.BlockSpec((tm, 1), lambda i, j: (i, 0)),
            pl.BlockSpec((1, LANES), lambda i, j: (0, 0)),
        ],
        out_specs=[
            pl.BlockSpec((tm, kw), lambda i, j: (i, j)),
            pl.BlockSpec((tm, dv), lambda i, j: (i, j)),
        ],
        out_shape=[
            jax.ShapeDtypeStruct((t, heads * kw), BF16),
            jax.ShapeDtypeStruct((t, heads * dv), BF16),
        ],
        scratch_shapes=[pltpu.VMEM((tm, rank), BF16), pltpu.VMEM((tm, LANES), BF16)],
        compiler_params=_params(("parallel", "arbitrary")),
    )(p, g_ckv, w_ukv, g_kn, small, gkr, pos, invf)


def _attn_kernel(q_ref, k_ref, v_ref, o_ref, *, tq, tk):
    qi = pl.program_id(2)
    q = q_ref[...]
    dv = v_ref.shape[1]
    row = qi * tq + lax.broadcasted_iota(jnp.int32, (tq, tk), 0)
    col0 = lax.broadcasted_iota(jnp.int32, (tq, tk), 1)

    def body(kb, carry):
        m, l, acc = carry
        k0 = pl.multiple_of(kb * tk, tk)
        s = _dot_nt(q, k_ref[pl.ds(k0, tk), :])
        s = jnp.where(col0 + k0 <= row, s, NEG_INF)
        m_new = jnp.maximum(m, jnp.max(s, axis=-1, keepdims=True))
        alpha = jnp.exp(m - m_new)
        pr = jnp.exp(s - m_new)
        l = alpha * l + jnp.sum(pr, axis=-1, keepdims=True)
        acc = alpha * acc + _dot(pr.astype(BF16), v_ref[pl.ds(k0, tk), :])
        return m_new, l, acc

    init = (jnp.full((tq, 1), NEG_INF, F32), jnp.zeros((tq, 1), F32), jnp.zeros((tq, dv), F32))
    nkb = ((qi + 1) * tq + tk - 1) // tk
    _, l, acc = lax.fori_loop(0, nkb, body, init)
    o_ref[...] = (acc / l).astype(o_ref.dtype)


def _mla_attn(q, k, v, *, batch, seq, heads, dk, dv, tq, tk):
    t = q.shape[0]
    nq = seq // tq
    return pl.pallas_call(
        functools.partial(_attn_kernel, tq=tq, tk=tk),
        grid=(batch, heads, nq),
        in_specs=[
            pl.BlockSpec((tq, dk), lambda b, h, i: (b * nq + i, h)),
            pl.BlockSpec((seq, dk), lambda b, h, i: (b, h)),
            pl.BlockSpec((seq, dv), lambda b, h, i: (b, h)),
        ],
        out_specs=pl.BlockSpec((tq, dv), lambda b, h, i: (b * nq + i, h)),
        out_shape=jax.ShapeDtypeStruct((t, heads * dv), BF16),
        compiler_params=_params(("parallel", "parallel", "arbitrary")),
    )(q, k, v)


def _gla_kernel(q_ref, k_ref, v_ref, og_ref, small_ref, w2_ref, ba_ref, gon_ref, o_ref, st_scr,
                *, seq, dk, dv, chunk):
    c = chunk
    st_scr[...] = jnp.zeros_like(st_scr)
    r_i = lax.broadcasted_iota(jnp.int32, (c, c), 0)
    c_i = lax.broadcasted_iota(jnp.int32, (c, c), 1)
    tri = jnp.where(c_i <= r_i, 1.0, 0.0).astype(BF16)
    row_id = lax.broadcasted_iota(jnp.int32, (c, 1), 0)
    w2 = w2_ref[...]
    ba = ba_ref[...]
    gon = gon_ref[...]
    q_scale = dk ** -0.5

    def chunk_step(ci, carry):
        r0 = pl.multiple_of(ci * c, c)
        qc = q_ref[pl.ds(r0, c), :].astype(F32) * q_scale
        kc = k_ref[pl.ds(r0, c), :].astype(F32)
        vc = v_ref[pl.ds(r0, c), :]
        z = _dot(small_ref[pl.ds(r0, c), :].astype(BF16), w2) + ba
        la = jax.nn.log_sigmoid(z) * (1.0 / GLA_TAU)
        hi = la.astype(BF16)
        r1 = la - hi.astype(F32)
        mid = r1.astype(BF16)
        lo = (r1 - mid.astype(F32)).astype(BF16)
        b = _dot(tri, hi) + _dot(tri, mid) + _dot(tri, lo)

        st = st_scr[...]
        inter = _dot_nt((qc * jnp.exp(b)).astype(BF16), st.astype(BF16))

        att = jnp.zeros((c, c), F32)
        for j in range(c):
            lo_r = (j // SUBLANES) * SUBLANES
            d = b[lo_r:, :] - b[j:j + 1, :]
            e = jnp.exp(jnp.where(row_id[lo_r:, :] >= j, d, NEG_INF))
            col = jnp.sum(qc[lo_r:, :] * kc[j:j + 1, :] * e, axis=-1, keepdims=True)
            if lo_r:
                col = jnp.concatenate([jnp.zeros((lo_r, 1), F32), col], axis=0)
            att = jnp.where(c_i == j, col, att)
        o = inter + _dot(att.astype(BF16), vc)

        b_last = b[c - 1:c, :]
        k_dec = (kc * jnp.exp(b_last - b)).astype(BF16)
        st_scr[...] = st * jnp.exp(b_last) + _dot_tn(vc, k_dec)

        og = og_ref[pl.ds(r0, c), :].astype(F32)
        out = _rms(o, gon) * (og * jax.nn.sigmoid(og))
        o_ref[pl.ds(r0, c), :] = out.astype(o_ref.dtype)
        return carry

    lax.fori_loop(0, seq // c, chunk_step, 0)


def _gla(p, small, w2p, b_a, g_on, *, batch, seq, heads, dk, dv, q_blk, k_blk, v_blk, og_blk):
    t = p.shape[0]
    return pl.pallas_call(
        functools.partial(_gla_kernel, seq=seq, dk=dk, dv=dv, chunk=GLA_CHUNK),
        grid=(batch, heads),
        in_specs=[
            pl.BlockSpec((seq, dk), lambda b, h: (b, q_blk + h)),
            pl.BlockSpec((seq, dk), lambda b, h: (b, k_blk + h)),
            pl.BlockSpec((seq, dv), lambda b, h: (b, v_blk + h)),
            pl.BlockSpec((seq, dv), lambda b, h: (b, og_blk + h)),
            pl.BlockSpec((seq, LANES), lambda b, h: (b, 0)),
            pl.BlockSpec((LANES, dk), lambda b, h: (0, h)),
            pl.BlockSpec((1, dk), lambda b, h: (0, h)),
            pl.BlockSpec((1, dv), lambda b, h: (0, 0)),
        ],
        out_specs=pl.BlockSpec((seq, dv), lambda b, h: (b, h)),
        out_shape=jax.ShapeDtypeStruct((t, heads * dv), BF16),
        scratch_shapes=[pltpu.VMEM((dv, dk), F32)],
        compiler_params=_params(("parallel", "parallel")),
    )(p, p, p, p, small, w2p, b_a, g_on)


def _out_proj_kernel(x_ref, ga_ref, gb_ref, oa_ref, ob_ref, bg_ref, w_ref, o_ref, m_scr):
    @pl.when(pl.program_id(1) == 0)
    def _():
        bg = bg_ref[...]
        sa = jax.nn.sigmoid(ga_ref[...].astype(F32) + bg[0:1, :])
        sb = jax.nn.sigmoid(gb_ref[...].astype(F32) + bg[1:2, :])
        m_scr[...] = (sa * oa_ref[...].astype(F32) + sb * ob_ref[...].astype(F32)).astype(BF16)

    o_ref[...] = x_ref[...] + _dot(m_scr[...], w_ref[...])


def _out_proj(x, p, o_mla, o_gla, b_gate, w_out, *, ga_blk, gb_blk, tm, tn):
    t, d = x.shape
    return pl.pallas_call(
        _out_proj_kernel,
        grid=(t // tm, d // tn),
        in_specs=[
            pl.BlockSpec((tm, tn), lambda i, j: (i, j)),
            pl.BlockSpec((tm, d), lambda i, j: (i, ga_blk)),
            pl.BlockSpec((tm, d), lambda i, j: (i, gb_blk)),
            pl.BlockSpec((tm, d), lambda i, j: (i, 0)),
            pl.BlockSpec((tm, d), lambda i, j: (i, 0)),
            pl.BlockSpec((2, d), lambda i, j: (0, 0)),
            pl.BlockSpec((d, tn), lambda i, j: (0, j)),
        ],
        out_specs=pl.BlockSpec((tm, tn), lambda i, j: (i, j)),
        out_shape=jax.ShapeDtypeStruct((t, d), F32),
        scratch_shapes=[pltpu.VMEM((tm, d), BF16)],
        compiler_params=_params(("parallel", "arbitrary")),
    )(x, p, p, o_mla, o_gla, b_gate, w_out)


def _topk_rows(s, k, payload=None):
    n = s.shape[0]
    iota = lax.broadcasted_iota(jnp.int32, s.shape, 0).astype(F32)
    kiota = lax.broadcasted_iota(jnp.int32, (k, s.shape[1]), 0)
    vals = jnp.zeros((k, s.shape[1]), F32)
    picks = jnp.zeros((k, s.shape[1]), F32)
    for r in range(k):
        m = jnp.max(s, axis=0, keepdims=True)
        idx = jnp.min(jnp.where(s == m, iota, float(n)), axis=0, keepdims=True)
        hit = iota == idx
        if payload is None:
            pick = idx
        else:
            pick = jnp.sum(jnp.where(hit, payload, 0.0), axis=0, keepdims=True)
        vals = jnp.where(kiota == r, m, vals)
        picks = jnp.where(kiota == r, pick, picks)
        s = jnp.where(hit, NEG_INF, s)
    return vals, picks


def _peer_route_kernel(x_ref, g_ref, w_ref, sk_ref, h_ref, ids_ref, gates_ref, *, heads, nkeys, half, topk):
    h = _rms(x_ref[...], g_ref[...]).astype(BF16)
    h_ref[...] = h
    qf = _dot(h, w_ref[...])
    for hd in range(heads):
        tops = []
        for part in range(2):
            o = (hd * 2 + part) * half
            qh = qf[:, o:o + half].astype(BF16)
            keys = sk_ref[(hd * 2 + part) * nkeys:(hd * 2 + part + 1) * nkeys, :]
            tops.append(_topk_rows(_dot_nt(keys, qh), topk))
        (s1, i1), (s2, i2) = tops
        assert topk == 2 * SUBLANES
        hs = SUBLANES
        pair_s = [s1[0:1, :] + s2] + [s1[a:a + 1, :] + s2[:hs, :] for a in range(1, hs)] + [s1[hs:, :] + s2[0:1, :]]
        pair_id = ([i1[0:1, :] * float(nkeys) + i2]
                   + [i1[a:a + 1, :] * float(nkeys) + i2[:hs, :] for a in range(1, hs)]
                   + [i1[hs:, :] * float(nkeys) + i2[0:1, :]])
        best_s, best_id = _topk_rows(jnp.concatenate(pair_s, axis=0), topk,
                                     payload=jnp.concatenate(pair_id, axis=0))
        e = jnp.exp(best_s - best_s[0:1, :])
        gate = e / jnp.sum(e, axis=0, keepdims=True)
        ids_ref[hd * topk:(hd + 1) * topk, :] = best_id.astype(jnp.int32)
        gates_ref[hd * topk:(hd + 1) * topk, :] = gate


def _peer_route(x, g, w_pq, sk2d, *, heads, nkeys, half, topk, tm):
    t, d = x.shape
    dq = w_pq.shape[1]
    return pl.pallas_call(
        functools.partial(_peer_route_kernel, heads=heads, nkeys=nkeys, half=half, topk=topk),
        grid=(t // tm,),
        in_specs=[
            pl.BlockSpec((tm, d), lambda i: (i, 0)),
            pl.BlockSpec((1, d), lambda i: (0, 0)),
            pl.BlockSpec((d, dq), lambda i: (0, 0)),
            pl.BlockSpec((heads * 2 * nkeys, half), lambda i: (0, 0)),
        ],
        out_specs=[
            pl.BlockSpec((tm, d), lambda i: (i, 0)),
            pl.BlockSpec((heads * topk, tm), lambda i: (0, i)),
            pl.BlockSpec((heads * topk, tm), lambda i: (0, i)),
        ],
        out_shape=[
            jax.ShapeDtypeStruct((t, d), BF16),
            jax.ShapeDtypeStruct((heads * topk, t), jnp.int32),
            jax.ShapeDtypeStruct((heads * topk, t), F32),
        ],
        compiler_params=_params(("parallel",)),
    )(x, g, w_pq, sk2d)


def _pack_expert_table(u_emb, v_emb):
    def pack(w):
        bits = lax.bitcast_convert_type(w.astype(BF16), jnp.uint16).astype(jnp.uint32)
        half = w.shape[1] // 2
        return bits[:, :half] | (bits[:, half:] << 16)

    return lax.bitcast_convert_type(jnp.concatenate([pack(u_emb), pack(v_emb)], axis=1), jnp.int32)


def _unpack_words(w):
    lo = lax.bitcast_convert_type(w << 16, F32)
    hi = lax.bitcast_convert_type(w & jnp.int32(-65536), F32)
    return lo, hi


def _expert_mix(words, h, gate, d):
    half = d // 2
    u_lo, u_hi = _unpack_words(words[:, :half])
    act = jnp.sum(u_lo * h[:, :half] + u_hi * h[:, half:], axis=-1, keepdims=True)
    w = gate * _gelu_exact(act)
    v_lo, v_hi = _unpack_words(words[:, half:])
    return jnp.concatenate([jnp.sum(v_lo * w, axis=0, keepdims=True),
                            jnp.sum(v_hi * w, axis=0, keepdims=True)], axis=1)


def _peer_mix_kernel(ids_hbm, uv_hbm, x_ref, h_ref, gates_ref, o_ref, ids_smem, buf, ids_sem, row_sems,
                     *, tb, picks, d):
    step = pl.program_id(0)
    ids_copy = pltpu.make_async_copy(ids_hbm.at[step], ids_smem, ids_sem)
    ids_copy.start()
    ids_copy.wait()

    def row_copy(tok, j, slot):
        eid = ids_smem[tok * picks + j]
        return pltpu.make_async_copy(uv_hbm.at[pl.ds(eid, 1), :], buf.at[slot, pl.ds(j, 1), :], row_sems.at[slot])

    def issue(tok, slot):
        for j in range(picks):
            row_copy(tok, j, slot).start()

    def wait_all(slot):
        pltpu.make_async_copy(uv_hbm.at[pl.ds(0, picks), :], buf.at[slot], row_sems.at[slot]).wait()

    issue(0, 0)

    def group(g, carry):
        rows = []
        gates = gates_ref[g]
        for u in range(SUBLANES):
            tok = g * SUBLANES + u
            slot = u % 2

            @pl.when(tok + 1 < tb)
            def _():
                issue(tok + 1, 1 - slot)

            wait_all(slot)
            r8 = pl.multiple_of(g * SUBLANES, SUBLANES)
            h8 = h_ref[pl.ds(r8, SUBLANES), :].astype(F32)
            rows.append(_expert_mix(buf[slot], h8[u:u + 1, :], gates[:, u:u + 1], d))
        r8 = pl.multiple_of(g * SUBLANES, SUBLANES)
        o_ref[pl.ds(r8, SUBLANES), :] = x_ref[pl.ds(r8, SUBLANES), :] + jnp.concatenate(rows, axis=0)
        return carry

    lax.fori_loop(0, tb // SUBLANES, group, 0)


def _peer_mix_direct(ids_blk, uv, x, h, gates3, *, tb, n_tokens):
    t, d = x.shape
    picks = gates3.shape[1]
    return pl.pallas_call(
        functools.partial(_peer_mix_kernel, tb=tb, picks=picks, d=d),
        grid=(n_tokens // tb,),
        input_output_aliases={2: 0},
        in_specs=[
            pl.BlockSpec(memory_space=pl.ANY),
            pl.BlockSpec(memory_space=pl.ANY),
            pl.BlockSpec((tb, d), lambda i: (i, 0)),
            pl.BlockSpec((tb, d), lambda i: (i, 0)),
            pl.BlockSpec((tb // SUBLANES, picks, SUBLANES), lambda i: (i, 0, 0)),
        ],
        out_specs=pl.BlockSpec((tb, d), lambda i: (i, 0)),
        out_shape=jax.ShapeDtypeStruct((t, d), F32),
        scratch_shapes=[
            pltpu.SMEM((tb * picks,), jnp.int32),
            pltpu.VMEM((2, picks, d), jnp.int32),
            pltpu.SemaphoreType.DMA(()),
            pltpu.SemaphoreType.DMA((2,)),
        ],
        compiler_params=_params(("arbitrary",)),
    )(ids_blk, uv, x, h, gates3)


def _peer_mix_staged_kernel(rows_ref, x_ref, h_ref, gates_ref, o_ref, *, tb, picks, d):
    for g in range(tb // SUBLANES):
        gates = gates_ref[g]
        h8 = h_ref[g * SUBLANES:(g + 1) * SUBLANES, :].astype(F32)
        rows = []
        for u in range(SUBLANES):
            r0 = (g * SUBLANES + u) * picks
            rows.append(_expert_mix(rows_ref[r0:r0 + picks, :], h8[u:u + 1, :], gates[:, u:u + 1], d))
        sl = slice(g * SUBLANES, (g + 1) * SUBLANES)
        o_ref[sl, :] = x_ref[sl, :] + jnp.concatenate(rows, axis=0)


def _peer_mix_staged(staged, x, h, gates3, *, tb, first_token):
    t, d = x.shape
    picks = gates3.shape[1]
    n_tokens = staged.shape[0] // picks
    assert first_token % tb == 0 and first_token + n_tokens == t
    b0 = first_token // tb
    return pl.pallas_call(
        functools.partial(_peer_mix_staged_kernel, tb=tb, picks=picks, d=d),
        grid=(n_tokens // tb,),
        input_output_aliases={1: 0},
        in_specs=[
            pl.BlockSpec((tb * picks, d), lambda i: (i, 0)),
            pl.BlockSpec((tb, d), lambda i: (i + b0, 0)),
            pl.BlockSpec((tb, d), lambda i: (i + b0, 0)),
            pl.BlockSpec((tb // SUBLANES, picks, SUBLANES), lambda i: (i + b0, 0, 0)),
        ],
        out_specs=pl.BlockSpec((tb, d), lambda i: (i + b0, 0)),
        out_shape=jax.ShapeDtypeStruct((t, d), F32),
        compiler_params=_params(("parallel",)),
    )(staged, x, h, gates3)


def _sc_gather_rows(table, idx, *, first_row, chunk):
    n_rows = idx.shape[0] - first_row
    d = table.shape[1]
    workers = SC_CORES * SC_SUBCORES
    assert n_rows % (workers * 2 * chunk) == 0 and chunk % SUBLANES == 0 and chunk <= LANES
    assert first_row % SUBLANES == 0
    rows_per_worker = n_rows // workers
    n_pairs = rows_per_worker // (2 * chunk)
    mesh = plsc.VectorSubcoreMesh(core_axis_name="c", subcore_axis_name="s")

    @functools.partial(
        pl.kernel, mesh=mesh,
        out_type=jax.ShapeDtypeStruct((n_rows, d), table.dtype),
        scratch_types=[
            pltpu.VMEM((chunk,), jnp.int32), pltpu.VMEM((chunk,), jnp.int32),
            pltpu.VMEM((chunk, d), table.dtype), pltpu.VMEM((chunk, d), table.dtype),
            pltpu.SemaphoreType.DMA, pltpu.SemaphoreType.DMA,
            pltpu.SemaphoreType.DMA, pltpu.SemaphoreType.DMA,
        ],
    )
    def gather_kernel(table_hbm, idx_hbm, out_hbm, idx0, idx1, rows0, rows1, gsem0, gsem1, wsem0, wsem1):
        idx_v, rows_v, gsem, wsem = (idx0, idx1), (rows0, rows1), (gsem0, gsem1), (wsem0, wsem1)
        worker = lax.axis_index("s") * SC_CORES + lax.axis_index("c")
        base = worker * rows_per_worker

        def out_rows(c):
            return pl.ds(pl.multiple_of(base + c * chunk, SUBLANES), chunk)

        def load_idx(slot, c):
            src = pl.ds(pl.multiple_of(first_row + base + c * chunk, SUBLANES), chunk)
            pltpu.sync_copy(idx_hbm.at[src], idx_v[slot])

        def gather(slot):
            return pltpu.make_async_copy(table_hbm.at[idx_v[slot]], rows_v[slot], gsem[slot])

        def writeout(slot, c):
            return pltpu.make_async_copy(rows_v[slot], out_hbm.at[out_rows(c)], wsem[slot])

        load_idx(0, 0)
        gather(0).start()

        @pl.loop(0, n_pairs)
        def _(p):
            c0 = 2 * p

            @pl.when(p > 0)
            def _():
                writeout(1, c0 - 1).wait()

            load_idx(1, c0 + 1)
            gather(1).start()
            gather(0).wait()
            writeout(0, c0).start()

            @pl.when(p + 1 < n_pairs)
            def _():
                load_idx(0, c0 + 2)
                writeout(0, c0).wait()
                gather(0).start()

            gather(1).wait()
            writeout(1, c0 + 1).start()

        writeout(0, 2 * n_pairs - 2).wait()
        writeout(1, 2 * n_pairs - 1).wait()

    return gather_kernel(table, idx)


def _pad_cols(w, width):
    return jnp.pad(w, ((0, 0), (0, width - w.shape[1])))


def _layer(x2, pos, g_norm_mix, w_in, b_gate, g_cq, w_uq, g_ckv, w_ukv, g_qn, g_qr, g_kn, g_kr,
           w_a2, b_a, g_gla_out, w_out, g_norm_ffn, w_pq, sub_keys, u_emb, v_emb, *, batch, seq, tiles):
    t, d = x2.shape
    q_rank, kv_rank = g_cq.shape[0], g_ckv.shape[0]
    nope, rope = g_qn.shape[0], g_qr.shape[0]
    mla_heads = w_uq.shape[1] // (nope + rope)
    mla_v = w_ukv.shape[1] // mla_heads - nope
    gate_rank, gla_dk_all = w_a2.shape
    gla_dv = g_gla_out.shape[0]
    gla_heads = d // gla_dv
    gla_dk = gla_dk_all // gla_heads
    peer_heads, _, nkeys, half = sub_keys.shape
    assert nope == LANES and mla_v == LANES and rope <= LANES and rope % 2 == 0
    assert mla_heads * mla_v == d and gla_heads * gla_dv == d
    assert rope + gate_rank <= LANES and nkeys == LANES and half == LANES

    widths = (q_rank, kv_rank, rope, gla_dk_all, gla_dk_all, d, gate_rank, d, d, d)
    offs = [0]
    for wd in widths:
        offs.append(offs[-1] + wd)
    assert offs[-1] == w_in.shape[1]
    seg = lambda i: w_in[:, offs[i]:offs[i + 1]]
    w_main = jnp.concatenate([seg(5), seg(7), seg(8), seg(9), seg(3), seg(4), seg(0), seg(1)], axis=1).astype(BF16)
    w_small = _pad_cols(jnp.concatenate([seg(2), seg(6)], axis=1), LANES).astype(BF16)
    v_blk, og_blk = 0, d // gla_dv
    ga_blk, gb_blk = 2, 3
    q_blk = 4 * d // gla_dk
    k_blk = q_blk + gla_heads
    cq_off = 4 * d + 2 * gla_dk_all
    assert cq_off % q_rank == 0 and (cq_off + q_rank) % kv_rank == 0
    cq_blk = cq_off // q_rank
    ckv_blk = (cq_off + q_rank) // kv_rank

    p, small = _in_proj(x2, g_norm_mix[None, :], w_main, w_small, tm=tiles["in_tm"], tn=tiles["in_tn"])

    inv_freq = ROPE_THETA ** (-jnp.arange(0, rope, 2, dtype=F32) / rope)
    invf = _pad_cols(jnp.concatenate([inv_freq, inv_freq])[None, :], LANES)
    scale = (nope + rope) ** -0.5
    gq = _pad_cols(jnp.concatenate([g_qn, g_qr])[None, :] * scale, nope + LANES)
    gkr = _pad_cols(g_kr[None, :], LANES)
    w_uq_p = jnp.pad(w_uq.reshape(q_rank, mla_heads, nope + rope),
                     ((0, 0), (0, 0), (0, LANES - rope))).reshape(q_rank, -1).astype(BF16)
    q = _mla_q(p, cq_blk, g_cq[None, :], w_uq_p, gq, pos, invf,
               heads=mla_heads, rank=q_rank, nope=nope, rope=rope, tm=tiles["mla_tm"])
    k, v = _mla_kv(p, ckv_blk, g_ckv[None, :], w_ukv.astype(BF16), g_kn[None, :], small, gkr, pos, invf,
                   heads=mla_heads, rank=kv_rank, nope=nope, rope=rope, dv=mla_v, tm=tiles["mla_tm"])
    o_mla = _mla_attn(q, k, v, batch=batch, seq=seq, heads=mla_heads, dk=nope + LANES, dv=mla_v,
                      tq=tiles["attn_tq"], tk=tiles["attn_tk"])

    w2p = jnp.zeros((LANES, gla_dk_all), F32).at[rope:rope + gate_rank].set(w_a2).astype(BF16)
    o_gla = _gla(p, small, w2p, b_a[None, :], g_gla_out[None, :], batch=batch, seq=seq, heads=gla_heads,
                 dk=gla_dk, dv=gla_dv, q_blk=q_blk, k_blk=k_blk, v_blk=v_blk, og_blk=og_blk)

    x2 = _out_proj(x2, p, o_mla, o_gla, b_gate, w_out.astype(BF16), ga_blk=ga_blk, gb_blk=gb_blk,
                   tm=tiles["out_tm"], tn=tiles["out_tn"])

    sk2d = sub_keys.reshape(peer_heads * 2 * nkeys, half).astype(BF16)
    h2, ids_t, gates_t = _peer_route(x2, g_norm_ffn[None, :], w_pq.astype(BF16), sk2d, heads=peer_heads,
                                     nkeys=nkeys, half=half, topk=PEER_TOPK, tm=tiles["route_tm"])
    picks = peer_heads * PEER_TOPK
    tb = tiles["mix_tb"]
    ids_tok = ids_t.T
    gates3 = gates_t.reshape(picks, t // SUBLANES, SUBLANES).transpose(1, 0, 2)
    uv = _pack_expert_table(u_emb, v_emb)
    t_direct = tiles["mix_direct_tokens"]
    x2 = _peer_mix_direct(ids_tok.reshape(t // tb, tb * picks), uv, x2, h2, gates3, tb=tb, n_tokens=t_direct)
    staged = _sc_gather_rows(uv, ids_tok.reshape(-1), first_row=t_direct * picks, chunk=tiles["sc_chunk"])
    return _peer_mix_staged(staged, x2, h2, gates3, tb=tiles["mix_staged_tb"], first_token=t_direct)


_TILES = dict(in_tm=1024, in_tn=512, mla_tm=1024, attn_tq=512, attn_tk=512,
              out_tm=512, out_tn=512, route_tm=256, mix_tb=32,
              mix_direct_tokens=6144, mix_staged_tb=16, sc_chunk=16)


def kernel(x, positions, g_norm_mix, w_in, b_gate, g_cq, w_uq, g_ckv, w_ukv, g_qn, g_qr, g_kn, g_kr,
           w_a2, b_a, g_gla_out, w_out, g_norm_ffn, w_pq, sub_keys, u_emb, v_emb, tiles=None):
    tiles = _TILES if tiles is None else tiles
    batch, seq, d = x.shape
    x2 = x.reshape(batch * seq, d)
    pos = positions.reshape(batch * seq, 1)
    for l in range(g_norm_mix.shape[0]):
        x2 = _layer(x2, pos, g_norm_mix[l], w_in[l], b_gate[l], g_cq[l], w_uq[l], g_ckv[l], w_ukv[l],
                    g_qn[l], g_qr[l], g_kn[l], g_kr[l], w_a2[l], b_a[l], g_gla_out[l], w_out[l],
                    g_norm_ffn[l], w_pq[l], sub_keys[l], u_emb[l], v_emb[l],
                    batch=batch, seq=seq, tiles=tiles)
    return x2.reshape(batch, seq, d)
```

```python
import functools

import jax
import jax.numpy as jnp
from jax import lax
from jax.experimental import pallas as pl
from jax.experimental.pallas import tpu as pltpu
from jax.experimental.pallas import tpu_sc as plsc

EPS = 1e-6
ROPE_THETA = 10000.0
GLA_TAU = 16.0
GLA_CHUNK = 64
PEER_TOPK = 16

LANES = 128
SUBLANES = 8
VMEM_LIMIT_BYTES = 56 * 1024 * 1024
SC_CORES = 2
SC_SUBCORES = 16

F32 = jnp.float32
BF16 = jnp.bfloat16
NEG_INF = float("-inf")


def _params(semantics):
    return pltpu.CompilerParams(dimension_semantics=semantics, vmem_limit_bytes=VMEM_LIMIT_BYTES)


def _rms(x, gain, n=None):
    ss = jnp.sum(x * x, axis=-1, keepdims=True)
    n = x.shape[-1] if n is None else n
    return x * lax.rsqrt(ss * (1.0 / n) + EPS) * gain


def _gelu_exact(x):
    return 0.5 * x * (1.0 + lax.erf(x * (0.5 ** 0.5)))


def _dot(a, b):
    return jnp.dot(a, b, preferred_element_type=F32)


def _dot_nt(a, b):
    return lax.dot_general(a, b, (((1,), (1,)), ((), ())), preferred_element_type=F32)


def _dot_tn(a, b):
    return lax.dot_general(a, b, (((0,), (0,)), ((), ())), preferred_element_type=F32)


def _in_proj_kernel(x_ref, g_ref, w_ref, ws_ref, p_ref, ps_ref, h_scr):
    @pl.when(pl.program_id(1) == 0)
    def _():
        h = _rms(x_ref[...], g_ref[...]).astype(BF16)
        h_scr[...] = h
        ps_ref[...] = _dot(h, ws_ref[...])

    p_ref[...] = _dot(h_scr[...], w_ref[...]).astype(p_ref.dtype)


def _in_proj(x, g, w_main, w_small, *, tm, tn):
    t, d = x.shape
    n = w_main.shape[1]
    return pl.pallas_call(
        _in_proj_kernel,
        grid=(t // tm, n // tn),
        in_specs=[
            pl.BlockSpec((tm, d), lambda i, j: (i, 0)),
            pl.BlockSpec((1, d), lambda i, j: (0, 0)),
            pl.BlockSpec((d, tn), lambda i, j: (0, j)),
            pl.BlockSpec((d, LANES), lambda i, j: (0, 0)),
        ],
        out_specs=[
            pl.BlockSpec((tm, tn), lambda i, j: (i, j)),
            pl.BlockSpec((tm, LANES), lambda i, j: (i, 0)),
        ],
        out_shape=[
            jax.ShapeDtypeStruct((t, n), BF16),
            jax.ShapeDtypeStruct((t, LANES), F32),
        ],
        scratch_shapes=[pltpu.VMEM((tm, d), BF16)],
        compiler_params=_params(("parallel", "arbitrary")),
    )(x, g, w_main, w_small)


def _rope_tables(pos_ref, invf_ref, rope):
    ang = pos_ref[...].astype(F32) * invf_ref[...]
    cos, sin = jnp.cos(ang), jnp.sin(ang)
    lane = lax.broadcasted_iota(jnp.int32, ang.shape, 1)
    half = rope // 2
    c = jnp.where(lane < rope, cos, 0.0)
    s_lo = jnp.where(lane < half, -sin, 0.0)
    s_hi = jnp.where(lane < half, 0.0, jnp.where(lane < rope, sin, 0.0))
    return c, s_lo, s_hi


def _apply_rope(pe, c, s_lo, s_hi, rope):
    half = rope // 2
    from_hi = pltpu.roll(pe, LANES - half, 1)
    from_lo = pltpu.roll(pe, half, 1)
    return pe * c + from_hi * s_lo + from_lo * s_hi


def _mla_q_kernel(cq_ref, gcq_ref, w_ref, gq_ref, pos_ref, invf_ref, q_ref,
                  h_scr, c_scr, slo_scr, shi_scr, *, nope, rope):
    @pl.when(pl.program_id(1) == 0)
    def _():
        h_scr[...] = _rms(cq_ref[...].astype(F32), gcq_ref[...]).astype(BF16)
        c, s_lo, s_hi = _rope_tables(pos_ref, invf_ref, rope)
        c_scr[...] = c
        slo_scr[...] = s_lo
        shi_scr[...] = s_hi

    y = _dot(h_scr[...], w_ref[...])
    g = gq_ref[...]
    qn = _rms(y[:, :nope], g[:, :nope])
    pe = _rms(y[:, nope:], g[:, nope:], n=rope)
    pe = _apply_rope(pe, c_scr[...], slo_scr[...], shi_scr[...], rope)
    q_ref[:, :nope] = qn.astype(q_ref.dtype)
    q_ref[:, nope:] = pe.astype(q_ref.dtype)


def _mla_q(p, cq_blk, g_cq, w_uq_p, gq, pos, invf, *, heads, rank, nope, rope, tm):
    t = p.shape[0]
    hw = nope + LANES
    return pl.pallas_call(
        functools.partial(_mla_q_kernel, nope=nope, rope=rope),
        grid=(t // tm, heads),
        in_specs=[
            pl.BlockSpec((tm, rank), lambda i, j: (i, cq_blk)),
            pl.BlockSpec((1, rank), lambda i, j: (0, 0)),
            pl.BlockSpec((rank, hw), lambda i, j: (0, j)),
            pl.BlockSpec((1, hw), lambda i, j: (0, 0)),
            pl.BlockSpec((tm, 1), lambda i, j: (i, 0)),
            pl.BlockSpec((1, LANES), lambda i, j: (0, 0)),
        ],
        out_specs=pl.BlockSpec((tm, hw), lambda i, j: (i, j)),
        out_shape=jax.ShapeDtypeStruct((t, heads * hw), BF16),
        scratch_shapes=[
            pltpu.VMEM((tm, rank), BF16),
            pltpu.VMEM((tm, LANES), F32),
            pltpu.VMEM((tm, LANES), F32),
            pltpu.VMEM((tm, LANES), F32),
        ],
        compiler_params=_params(("parallel", "arbitrary")),
    )(p, g_cq, w_uq_p, gq, pos, invf)


def _mla_kv_kernel(ckv_ref, gckv_ref, w_ref, gkn_ref, small_ref, gkr_ref, pos_ref, invf_ref,
                   k_ref, v_ref, h_scr, kpe_scr, *, nope, rope):
    @pl.when(pl.program_id(1) == 0)
    def _():
        h_scr[...] = _rms(ckv_ref[...].astype(F32), gckv_ref[...]).astype(BF16)
        c, s_lo, s_hi = _rope_tables(pos_ref, invf_ref, rope)
        sm = small_ref[...]
        lane = lax.broadcasted_iota(jnp.int32, sm.shape, 1)
        pe = _rms(jnp.where(lane < rope, sm, 0.0), gkr_ref[...], n=rope)
        kpe_scr[...] = _apply_rope(pe, c, s_lo, s_hi, rope).astype(BF16)

    y = _dot(h_scr[...], w_ref[...])
    k_ref[:, :nope] = _rms(y[:, :nope], gkn_ref[...]).astype(k_ref.dtype)
    k_ref[:, nope:] = kpe_scr[...]
    v_ref[...] = y[:, nope:].astype(v_ref.dtype)


def _mla_kv(p, ckv_blk, g_ckv, w_ukv, g_kn, small, gkr, pos, invf, *, heads, rank, nope, rope, dv, tm):
    t = p.shape[0]
    kw = nope + LANES
    return pl.pallas_call(
        functools.partial(_mla_kv_kernel, nope=nope, rope=rope),
        grid=(t // tm, heads),
        in_specs=[
            pl.BlockSpec((tm, rank), lambda i, j: (i, ckv_blk)),
            pl.BlockSpec((1, rank), lambda i, j: (0, 0)),
            pl.BlockSpec((rank, nope + dv), lambda i, j: (0, j)),
            pl.BlockSpec((1, nope), lambda i, j: (0, 0)),
            pl.BlockSpec((tm, LANES), lambda i, j: (i, 0)),
            pl.BlockSpec((1, LANES), lambda i, j: (0, 0)),
            pl.BlockSpec((tm, 1), lambda i, j: (i, 0)),
            pl.BlockSpec((1, LANES), lambda i, j: (0, 0)),
        ],
        out_specs=[
            pl.BlockSpec((tm, kw), lambda i, j: (i, j)),
            pl.BlockSpec((tm, dv), lambda i, j: (i, j)),
        ],
        out_shape=[
            jax.ShapeDtypeStruct((t, heads * kw), BF16),
            jax.ShapeDtypeStruct((t, heads * dv), BF16),
        ],
        scratch_shapes=[pltpu.VMEM((tm, rank), BF16), pltpu.VMEM((tm, LANES), BF16)],
        compiler_params=_params(("parallel", "arbitrary")),
    )(p, g_ckv, w_ukv, g_kn, small, gkr, pos, invf)


def _attn_kernel(q_ref, k_ref, v_ref, o_ref, *, tq, tk):
    qi = pl.program_id(2)
    q = q_ref[...]
    dv = v_ref.shape[1]
    row = qi * tq + lax.broadcasted_iota(jnp.int32, (tq, tk), 0)
    col0 = lax.broadcasted_iota(jnp.int32, (tq, tk), 1)

    def body(kb, carry):
        m, l, acc = carry
        k0 = pl.multiple_of(kb * tk, tk)
        s = _dot_nt(q, k_ref[pl.ds(k0, tk), :])
        s = jnp.where(col0 + k0 <= row, s, NEG_INF)
        m_new = jnp.maximum(m, jnp.max(s, axis=-1, keepdims=True))
        alpha = jnp.exp(m - m_new)
        pr = jnp.exp(s - m_new)
        l = alpha * l + jnp.sum(pr, axis=-1, keepdims=True)
        acc = alpha * acc + _dot(pr.astype(BF16), v_ref[pl.ds(k0, tk), :])
        return m_new, l, acc

    init = (jnp.full((tq, 1), NEG_INF, F32), jnp.zeros((tq, 1), F32), jnp.zeros((tq, dv), F32))
    nkb = ((qi + 1) * tq + tk - 1) // tk
    _, l, acc = lax.fori_loop(0, nkb, body, init)
    o_ref[...] = (acc / l).astype(o_ref.dtype)


def _mla_attn(q, k, v, *, batch, seq, heads, dk, dv, tq, tk):
    t = q.shape[0]
    nq = seq // tq
    return pl.pallas_call(
        functools.partial(_attn_kernel, tq=tq, tk=tk),
        grid=(batch, heads, nq),
        in_specs=[
            pl.BlockSpec((tq, dk), lambda b, h, i: (b * nq + i, h)),
            pl.BlockSpec((seq, dk), lambda b, h, i: (b, h)),
            pl.BlockSpec((seq, dv), lambda b, h, i: (b, h)),
        ],
        out_specs=pl.BlockSpec((tq, dv), lambda b, h, i: (b * nq + i, h)),
        out_shape=jax.ShapeDtypeStruct((t, heads * dv), BF16),
        compiler_params=_params(("parallel", "parallel", "arbitrary")),
    )(q, k, v)


def _gla_kernel(q_ref, k_ref, v_ref, og_ref, small_ref, w2_ref, ba_ref, gon_ref, o_ref, st_scr,
                *, seq, dk, dv, chunk):
    c = chunk
    st_scr[...] = jnp.zeros_like(st_scr)
    r_i = lax.broadcasted_iota(jnp.int32, (c, c), 0)
    c_i = lax.broadcasted_iota(jnp.int32, (c, c), 1)
    tri = jnp.where(c_i <= r_i, 1.0, 0.0).astype(BF16)
    row_id = lax.broadcasted_iota(jnp.int32, (c, 1), 0)
    w2 = w2_ref[...]
    ba = ba_ref[...]
    gon = gon_ref[...]
    q_scale = dk ** -0.5

    def chunk_step(ci, carry):
        r0 = pl.multiple_of(ci * c, c)
        qc = q_ref[pl.ds(r0, c), :].astype(F32) * q_scale
        kc = k_ref[pl.ds(r0, c), :].astype(F32)
        vc = v_ref[pl.ds(r0, c), :]
        z = _dot(small_ref[pl.ds(r0, c), :].astype(BF16), w2) + ba
        la = jax.nn.log_sigmoid(z) * (1.0 / GLA_TAU)
        hi = la.astype(BF16)
        r1 = la - hi.astype(F32)
        mid = r1.astype(BF16)
        lo = (r1 - mid.astype(F32)).astype(BF16)
        b = _dot(tri, hi) + _dot(tri, mid) + _dot(tri, lo)

        st = st_scr[...]
        inter = _dot_nt((qc * jnp.exp(b)).astype(BF16), st.astype(BF16))

        att = jnp.zeros((c, c), F32)
        for j in range(c):
            lo_r = (j // SUBLANES) * SUBLANES
            d = b[lo_r:, :] - b[j:j + 1, :]
            e = jnp.exp(jnp.where(row_id[lo_r:, :] >= j, d, NEG_INF))
            col = jnp.sum(qc[lo_r:, :] * kc[j:j + 1, :] * e, axis=-1, keepdims=True)
            if lo_r:
                col = jnp.concatenate([jnp.zeros((lo_r, 1), F32), col], axis=0)
            att = jnp.where(c_i == j, col, att)
        o = inter + _dot(att.astype(BF16), vc)

        b_last = b[c - 1:c, :]
        k_dec = (kc * jnp.exp(b_last - b)).astype(BF16)
        st_scr[...] = st * jnp.exp(b_last) + _dot_tn(vc, k_dec)

        og = og_ref[pl.ds(r0, c), :].astype(F32)
        out = _rms(o, gon) * (og * jax.nn.sigmoid(og))
        o_ref[pl.ds(r0, c), :] = out.astype(o_ref.dtype)
        return carry

    lax.fori_loop(0, seq // c, chunk_step, 0)


def _gla(p, small, w2p, b_a, g_on, *, batch, seq, heads, dk, dv, q_blk, k_blk, v_blk, og_blk):
    t = p.shape[0]
    return pl.pallas_call(
        functools.partial(_gla_kernel, seq=seq, dk=dk, dv=dv, chunk=GLA_CHUNK),
        grid=(batch, heads),
        in_specs=[
            pl.BlockSpec((seq, dk), lambda b, h: (b, q_blk + h)),
            pl.BlockSpec((seq, dk), lambda b, h: (b, k_blk + h)),
            pl.BlockSpec((seq, dv), lambda b, h: (b, v_blk + h)),
            pl.BlockSpec((seq, dv), lambda b, h: (b, og_blk + h)),
            pl.BlockSpec((seq, LANES), lambda b, h: (b, 0)),
            pl.BlockSpec((LANES, dk), lambda b, h: (0, h)),
            pl.BlockSpec((1, dk), lambda b, h: (0, h)),
            pl.BlockSpec((1, dv), lambda b, h: (0, 0)),
        ],
        out_specs=pl.BlockSpec((seq, dv), lambda b, h: (b, h)),
        out_shape=jax.ShapeDtypeStruct((t, heads * dv), BF16),
        scratch_shapes=[pltpu.VMEM((dv, dk), F32)],
        compiler_params=_params(("parallel", "parallel")),
    )(p, p, p, p, small, w2p, b_a, g_on)


def _out_proj_kernel(x_ref, ga_ref, gb_ref, oa_ref, ob_ref, bg_ref, w_ref, o_ref, m_scr):
    @pl.when(pl.program_id(1) == 0)
    def _():
        bg = bg_ref[...]
        sa = jax.nn.sigmoid(ga_ref[...].astype(F32) + bg[0:1, :])
        sb = jax.nn.sigmoid(gb_ref[...].astype(F32) + bg[1:2, :])
        m_scr[...] = (sa * oa_ref[...].astype(F32) + sb * ob_ref[...].astype(F32)).astype(BF16)

    o_ref[...] = x_ref[...] + _dot(m_scr[...], w_ref[...])


def _out_proj(x, p, o_mla, o_gla, b_gate, w_out, *, ga_blk, gb_blk, tm, tn):
    t, d = x.shape
    return pl.pallas_call(
        _out_proj_kernel,
        grid=(t // tm, d // tn),
        in_specs=[
            pl.BlockSpec((tm, tn), lambda i, j: (i, j)),
            pl.BlockSpec((tm, d), lambda i, j: (i, ga_blk)),
            pl.BlockSpec((tm, d), lambda i, j: (i, gb_blk)),
            pl.BlockSpec((tm, d), lambda i, j: (i, 0)),
            pl.BlockSpec((tm, d), lambda i, j: (i, 0)),
            pl.BlockSpec((2, d), lambda i, j: (0, 0)),
            pl.BlockSpec((d, tn), lambda i, j: (0, j)),
        ],
        out_specs=pl.BlockSpec((tm, tn), lambda i, j: (i, j)),
        out_shape=jax.ShapeDtypeStruct((t, d), F32),
        scratch_shapes=[pltpu.VMEM((tm, d), BF16)],
        compiler_params=_params(("parallel", "arbitrary")),
    )(x, p, p, o_mla, o_gla, b_gate, w_out)


def _topk_rows(s, k, payload=None):
    n = s.shape[0]
    iota = lax.broadcasted_iota(jnp.int32, s.shape, 0).astype(F32)
    kiota = lax.broadcasted_iota(jnp.int32, (k, s.shape[1]), 0)
    vals = jnp.zeros((k, s.shape[1]), F32)
    picks = jnp.zeros((k, s.shape[1]), F32)
    for r in range(k):
        m = jnp.max(s, axis=0, keepdims=True)
        idx = jnp.min(jnp.where(s == m, iota, float(n)), axis=0, keepdims=True)
        hit = iota == idx
        if payload is None:
            pick = idx
        else:
            pick = jnp.sum(jnp.where(hit, payload, 0.0), axis=0, keepdims=True)
        vals = jnp.where(kiota == r, m, vals)
        picks = jnp.where(kiota == r, pick, picks)
        s = jnp.where(hit, NEG_INF, s)
    return vals, picks


def _peer_route_kernel(x_ref, g_ref, w_ref, sk_ref, h_ref, ids_ref, gates_ref, *, heads, nkeys, half, topk):
    h = _rms(x_ref[...], g_ref[...]).astype(BF16)
    h_ref[...] = h
    qf = _dot(h, w_ref[...])
    for hd in range(heads):
        tops = []
        for part in range(2):
            o = (hd * 2 + part) * half
            qh = qf[:, o:o + half].astype(BF16)
            keys = sk_ref[(hd * 2 + part) * nkeys:(hd * 2 + part + 1) * nkeys, :]
            tops.append(_topk_rows(_dot_nt(keys, qh), topk))
        (s1, i1), (s2, i2) = tops
        assert topk == 2 * SUBLANES
        hs = SUBLANES
        pair_s = [s1[0:1, :] + s2] + [s1[a:a + 1, :] + s2[:hs, :] for a in range(1, hs)] + [s1[hs:, :] + s2[0:1, :]]
        pair_id = ([i1[0:1, :] * float(nkeys) + i2]
                   + [i1[a:a + 1, :] * float(nkeys) + i2[:hs, :] for a in range(1, hs)]
                   + [i1[hs:, :] * float(nkeys) + i2[0:1, :]])
        best_s, best_id = _topk_rows(jnp.concatenate(pair_s, axis=0), topk,
                                     payload=jnp.concatenate(pair_id, axis=0))
        e = jnp.exp(best_s - best_s[0:1, :])
        gate = e / jnp.sum(e, axis=0, keepdims=True)
        ids_ref[hd * topk:(hd + 1) * topk, :] = best_id.astype(jnp.int32)
        gates_ref[hd * topk:(hd + 1) * topk, :] = gate


def _peer_route(x, g, w_pq, sk2d, *, heads, nkeys, half, topk, tm):
    t, d = x.shape
    dq = w_pq.shape[1]
    return pl.pallas_call(
        functools.partial(_peer_route_kernel, heads=heads, nkeys=nkeys, half=half, topk=topk),
        grid=(t // tm,),
        in_specs=[
            pl.BlockSpec((tm, d), lambda i: (i, 0)),
            pl.BlockSpec((1, d), lambda i: (0, 0)),
            pl.BlockSpec((d, dq), lambda i: (0, 0)),
            pl.BlockSpec((heads * 2 * nkeys, half), lambda i: (0, 0)),
        ],
        out_specs=[
            pl.BlockSpec((tm, d), lambda i: (i, 0)),
            pl.BlockSpec((heads * topk, tm), lambda i: (0, i)),
            pl.BlockSpec((heads * topk, tm), lambda i: (0, i)),
        ],
        out_shape=[
            jax.ShapeDtypeStruct((t, d), BF16),
            jax.ShapeDtypeStruct((heads * topk, t), jnp.int32),
            jax.ShapeDtypeStruct((heads * topk, t), F32),
        ],
        compiler_params=_params(("parallel",)),
    )(x, g, w_pq, sk2d)


def _pack_expert_table(u_emb, v_emb):
    def pack(w):
        bits = lax.bitcast_convert_type(w.astype(BF16), jnp.uint16).astype(jnp.uint32)
        half = w.shape[1] // 2
        return bits[:, :half] | (bits[:, half:] << 16)

    return lax.bitcast_convert_type(jnp.concatenate([pack(u_emb), pack(v_emb)], axis=1), jnp.int32)


def _unpack_words(w):
    lo = lax.bitcast_convert_type(w << 16, F32)
    hi = lax.bitcast_convert_type(w & jnp.int32(-65536), F32)
    return lo, hi


def _expert_mix(words, h, gate, d):
    half = d // 2
    u_lo, u_hi = _unpack_words(words[:, :half])
    act = jnp.sum(u_lo * h[:, :half] + u_hi * h[:, half:], axis=-1, keepdims=True)
    w = gate * _gelu_exact(act)
    v_lo, v_hi = _unpack_words(words[:, half:])
    return jnp.concatenate([jnp.sum(v_lo * w, axis=0, keepdims=True),
                            jnp.sum(v_hi * w, axis=0, keepdims=True)], axis=1)


def _peer_mix_kernel(ids_hbm, uv_hbm, x_ref, h_ref, gates_ref, after_hbm, o_ref, ids_smem, buf, ids_sem, row_sems,
                     *, tb, picks, d):
    del after_hbm
    step = pl.program_id(0)
    ids_copy = pltpu.make_async_copy(ids_hbm.at[step], ids_smem, ids_sem)
    ids_copy.start()
    ids_copy.wait()

    def row_copy(tok, j, slot):
        eid = ids_smem[tok * picks + j]
        return pltpu.make_async_copy(uv_hbm.at[pl.ds(eid, 1), :], buf.at[slot, pl.ds(j, 1), :], row_sems.at[slot])

    def issue(tok, slot):
        for j in range(picks):
            row_copy(tok, j, slot).start()

    def wait_all(slot):
        pltpu.make_async_copy(uv_hbm.at[pl.ds(0, picks), :], buf.at[slot], row_sems.at[slot]).wait()

    issue(0, 0)

    def group(g, carry):
        rows = []
        gates = gates_ref[g]
        for u in range(SUBLANES):
            tok = g * SUBLANES + u
            slot = u % 2

            @pl.when(tok + 1 < tb)
            def _():
                issue(tok + 1, 1 - slot)

            wait_all(slot)
            r8 = pl.multiple_of(g * SUBLANES, SUBLANES)
            h8 = h_ref[pl.ds(r8, SUBLANES), :].astype(F32)
            rows.append(_expert_mix(buf[slot], h8[u:u + 1, :], gates[:, u:u + 1], d))
        r8 = pl.multiple_of(g * SUBLANES, SUBLANES)
        o_ref[pl.ds(r8, SUBLANES), :] = x_ref[pl.ds(r8, SUBLANES), :] + jnp.concatenate(rows, axis=0)
        return carry

    lax.fori_loop(0, tb // SUBLANES, group, 0)


def _peer_mix_direct(ids_blk, uv, x, h, gates3, after, *, tb, n_tokens):
    t, d = x.shape
    picks = gates3.shape[1]
    return pl.pallas_call(
        functools.partial(_peer_mix_kernel, tb=tb, picks=picks, d=d),
        grid=(n_tokens // tb,),
        input_output_aliases={2: 0},
        in_specs=[
            pl.BlockSpec(memory_space=pl.ANY),
            pl.BlockSpec(memory_space=pl.ANY),
            pl.BlockSpec((tb, d), lambda i: (i, 0)),
            pl.BlockSpec((tb, d), lambda i: (i, 0)),
            pl.BlockSpec((tb // SUBLANES, picks, SUBLANES), lambda i: (i, 0, 0)),
            pl.BlockSpec(memory_space=pl.ANY),
        ],
        out_specs=pl.BlockSpec((tb, d), lambda i: (i, 0)),
        out_shape=jax.ShapeDtypeStruct((t, d), F32),
        scratch_shapes=[
            pltpu.SMEM((tb * picks,), jnp.int32),
            pltpu.VMEM((2, picks, d), jnp.int32),
            pltpu.SemaphoreType.DMA(()),
            pltpu.SemaphoreType.DMA((2,)),
        ],
        compiler_params=_params(("arbitrary",)),
    )(ids_blk, uv, x, h, gates3, after)


def _peer_mix_staged_kernel(rows_ref, x_ref, h_ref, gates_ref, o_ref, *, tb, picks, d):
    for g in range(tb // SUBLANES):
        gates = gates_ref[g]
        h8 = h_ref[g * SUBLANES:(g + 1) * SUBLANES, :].astype(F32)
        rows = []
        for u in range(SUBLANES):
            r0 = (g * SUBLANES + u) * picks
            rows.append(_expert_mix(rows_ref[r0:r0 + picks, :], h8[u:u + 1, :], gates[:, u:u + 1], d))
        sl = slice(g * SUBLANES, (g + 1) * SUBLANES)
        o_ref[sl, :] = x_ref[sl, :] + jnp.concatenate(rows, axis=0)


def _peer_mix_staged(staged, x, h, gates3, *, tb, first_token):
    t, d = x.shape
    picks = gates3.shape[1]
    n_tokens = staged.shape[0] // picks
    assert first_token % tb == 0 and first_token + n_tokens == t
    b0 = first_token // tb
    return pl.pallas_call(
        functools.partial(_peer_mix_staged_kernel, tb=tb, picks=picks, d=d),
        grid=(n_tokens // tb,),
        input_output_aliases={1: 0},
        in_specs=[
            pl.BlockSpec((tb * picks, d), lambda i: (i, 0)),
            pl.BlockSpec((tb, d), lambda i: (i + b0, 0)),
            pl.BlockSpec((tb, d), lambda i: (i + b0, 0)),
            pl.BlockSpec((tb // SUBLANES, picks, SUBLANES), lambda i: (i + b0, 0, 0)),
        ],
        out_specs=pl.BlockSpec((tb, d), lambda i: (i + b0, 0)),
        out_shape=jax.ShapeDtypeStruct((t, d), F32),
        compiler_params=_params(("parallel",)),
    )(staged, x, h, gates3)


def _sc_gather_rows(table, idx, *, first_row, chunk):
    n_rows = idx.shape[0] - first_row
    d = table.shape[1]
    workers = SC_CORES * SC_SUBCORES
    assert n_rows % (workers * 2 * chunk) == 0 and chunk % SUBLANES == 0 and chunk <= LANES
    assert first_row % SUBLANES == 0
    rows_per_worker = n_rows // workers
    n_pairs = rows_per_worker // (2 * chunk)
    mesh = plsc.VectorSubcoreMesh(core_axis_name="c", subcore_axis_name="s")

    @functools.partial(
        pl.kernel, mesh=mesh,
        out_type=jax.ShapeDtypeStruct((n_rows, d), table.dtype),
        scratch_types=[
            pltpu.VMEM((chunk,), jnp.int32), pltpu.VMEM((chunk,), jnp.int32),
            pltpu.VMEM((chunk, d), table.dtype), pltpu.VMEM((chunk, d), table.dtype),
            pltpu.SemaphoreType.DMA, pltpu.SemaphoreType.DMA,
            pltpu.SemaphoreType.DMA, pltpu.SemaphoreType.DMA,
        ],
    )
    def gather_kernel(table_hbm, idx_hbm, out_hbm, idx0, idx1, rows0, rows1, gsem0, gsem1, wsem0, wsem1):
        idx_v, rows_v, gsem, wsem = (idx0, idx1), (rows0, rows1), (gsem0, gsem1), (wsem0, wsem1)
        worker = lax.axis_index("s") * SC_CORES + lax.axis_index("c")
        base = worker * rows_per_worker

        def out_rows(c):
            return pl.ds(pl.multiple_of(base + c * chunk, SUBLANES), chunk)

        def load_idx(slot, c):
            src = pl.ds(pl.multiple_of(first_row + base + c * chunk, SUBLANES), chunk)
            pltpu.sync_copy(idx_hbm.at[src], idx_v[slot])

        def gather(slot):
            return pltpu.make_async_copy(table_hbm.at[idx_v[slot]], rows_v[slot], gsem[slot])

        def writeout(slot, c):
            return pltpu.make_async_copy(rows_v[slot], out_hbm.at[out_rows(c)], wsem[slot])

        load_idx(0, 0)
        gather(0).start()

        @pl.loop(0, n_pairs)
        def _(p):
            c0 = 2 * p

            @pl.when(p > 0)
            def _():
                writeout(1, c0 - 1).wait()

            load_idx(1, c0 + 1)
            gather(1).start()
            gather(0).wait()
            writeout(0, c0).start()

            @pl.when(p + 1 < n_pairs)
            def _():
                load_idx(0, c0 + 2)
                writeout(0, c0).wait()
                gather(0).start()

            gather(1).wait()
            writeout(1, c0 + 1).start()

        writeout(0, 2 * n_pairs - 2).wait()
        writeout(1, 2 * n_pairs - 1).wait()

    return gather_kernel(table, idx)


def _pad_cols(w, width):
    return jnp.pad(w, ((0, 0), (0, width - w.shape[1])))


def _prepare_layer(g_norm_mix, w_in, b_gate, g_cq, w_uq, g_ckv, w_ukv, g_qn, g_qr, g_kn, g_kr,
                   w_a2, b_a, g_gla_out, w_out, g_norm_ffn, w_pq, sub_keys, u_emb, v_emb):
    d = w_in.shape[0]
    q_rank, kv_rank = g_cq.shape[0], g_ckv.shape[0]
    nope, rope = g_qn.shape[0], g_qr.shape[0]
    mla_heads = w_uq.shape[1] // (nope + rope)
    mla_v = w_ukv.shape[1] // mla_heads - nope
    gate_rank, gla_dk_all = w_a2.shape
    gla_dv = g_gla_out.shape[0]
    gla_heads = d // gla_dv
    gla_dk = gla_dk_all // gla_heads
    peer_heads, _, nkeys, half = sub_keys.shape
    assert nope == LANES and mla_v == LANES and rope <= LANES and rope % 2 == 0
    assert mla_heads * mla_v == d and gla_heads * gla_dv == d
    assert rope + gate_rank <= LANES and nkeys == LANES and half == LANES

    widths = (q_rank, kv_rank, rope, gla_dk_all, gla_dk_all, d, gate_rank, d, d, d)
    offs = [0]
    for wd in widths:
        offs.append(offs[-1] + wd)
    assert offs[-1] == w_in.shape[1]
    seg = lambda i: w_in[:, offs[i]:offs[i + 1]]
    w_main = jnp.concatenate([seg(5), seg(7), seg(8), seg(9), seg(3), seg(4), seg(0), seg(1)], axis=1).astype(BF16)
    w_small = _pad_cols(jnp.concatenate([seg(2), seg(6)], axis=1), LANES).astype(BF16)
    cq_off = 4 * d + 2 * gla_dk_all
    assert cq_off % q_rank == 0 and (cq_off + q_rank) % kv_rank == 0
    inv_freq = ROPE_THETA ** (-jnp.arange(0, rope, 2, dtype=F32) / rope)
    scale = (nope + rope) ** -0.5
    return dict(
        dims=dict(q_rank=q_rank, kv_rank=kv_rank, nope=nope, rope=rope, mla_heads=mla_heads, mla_v=mla_v,
                  gla_heads=gla_heads, gla_dk=gla_dk, gla_dv=gla_dv, peer_heads=peer_heads, nkeys=nkeys, half=half,
                  v_blk=0, og_blk=d // gla_dv, ga_blk=2, gb_blk=3, q_blk=4 * d // gla_dk,
                  k_blk=4 * d // gla_dk + gla_heads, cq_blk=cq_off // q_rank, ckv_blk=(cq_off + q_rank) // kv_rank),
        g_norm_mix=g_norm_mix[None, :], w_main=w_main, w_small=w_small,
        invf=_pad_cols(jnp.concatenate([inv_freq, inv_freq])[None, :], LANES),
        gq=_pad_cols(jnp.concatenate([g_qn, g_qr])[None, :] * scale, nope + LANES),
        gkr=_pad_cols(g_kr[None, :], LANES), g_cq=g_cq[None, :], g_ckv=g_ckv[None, :], g_kn=g_kn[None, :],
        w_uq=jnp.pad(w_uq.reshape(q_rank, mla_heads, nope + rope),
                     ((0, 0), (0, 0), (0, LANES - rope))).reshape(q_rank, -1).astype(BF16),
        w_ukv=w_ukv.astype(BF16),
        w2p=jnp.zeros((LANES, gla_dk_all), F32).at[rope:rope + gate_rank].set(w_a2).astype(BF16),
        b_a=b_a[None, :], g_on=g_gla_out[None, :], b_gate=b_gate, w_out=w_out.astype(BF16),
        g_norm_ffn=g_norm_ffn[None, :], w_pq=w_pq.astype(BF16),
        sk2d=sub_keys.reshape(peer_heads * 2 * nkeys, half).astype(BF16),
        uv=_pack_expert_table(u_emb, v_emb),
    )


def _mixers_and_route(x2, pos, w, *, batch, seq, tiles):
    t = x2.shape[0]
    dm = w["dims"]
    p, small = _in_proj(x2, w["g_norm_mix"], w["w_main"], w["w_small"], tm=tiles["in_tm"], tn=tiles["in_tn"])
    q = _mla_q(p, dm["cq_blk"], w["g_cq"], w["w_uq"], w["gq"], pos, w["invf"], heads=dm["mla_heads"],
               rank=dm["q_rank"], nope=dm["nope"], rope=dm["rope"], tm=tiles["mla_tm"])
    k, v = _mla_kv(p, dm["ckv_blk"], w["g_ckv"], w["w_ukv"], w["g_kn"], small, w["gkr"], pos, w["invf"],
                   heads=dm["mla_heads"], rank=dm["kv_rank"], nope=dm["nope"], rope=dm["rope"], dv=dm["mla_v"],
                   tm=tiles["mla_tm"])
    o_mla = _mla_attn(q, k, v, batch=batch, seq=seq, heads=dm["mla_heads"], dk=dm["nope"] + LANES, dv=dm["mla_v"],
                      tq=tiles["attn_tq"], tk=tiles["attn_tk"])
    o_gla = _gla(p, small, w["w2p"], w["b_a"], w["g_on"], batch=batch, seq=seq, heads=dm["gla_heads"],
                 dk=dm["gla_dk"], dv=dm["gla_dv"], q_blk=dm["q_blk"], k_blk=dm["k_blk"], v_blk=dm["v_blk"],
                 og_blk=dm["og_blk"])
    x2 = _out_proj(x2, p, o_mla, o_gla, w["b_gate"], w["w_out"], ga_blk=dm["ga_blk"], gb_blk=dm["gb_blk"],
                   tm=tiles["out_tm"], tn=tiles["out_tn"])
    h2, ids_t, gates_t = _peer_route(x2, w["g_norm_ffn"], w["w_pq"], w["sk2d"], heads=dm["peer_heads"],
                                     nkeys=dm["nkeys"], half=dm["half"], topk=PEER_TOPK, tm=tiles["route_tm"])
    picks = dm["peer_heads"] * PEER_TOPK
    gates3 = gates_t.reshape(picks, t // SUBLANES, SUBLANES).transpose(1, 0, 2)
    return x2, h2, ids_t.T, gates3


def _layer(x2, pos, w, *, batch, seq, tiles):
    t, d = x2.shape
    direct_tokens = tiles["mix_direct_tokens"]
    groups = len(direct_tokens)
    tg, bg = t // groups, batch // groups
    routed = [_mixers_and_route(x2[g * tg:(g + 1) * tg], pos[g * tg:(g + 1) * tg], w,
                                batch=bg, seq=seq, tiles=tiles) for g in range(groups)]
    tb = tiles["mix_tb"]
    mixed = []
    for g, (xg, hg, ids_tok, gates3) in enumerate(routed):
        if direct_tokens[g]:
            after = routed[min(g + 1, groups - 1)][2]
            xg = _peer_mix_direct(ids_tok.reshape(tg // tb, -1), w["uv"], xg, hg, gates3, after,
                                  tb=tb, n_tokens=direct_tokens[g])
        mixed.append(xg)
    out = []
    for g, (_, hg, ids_tok, gates3) in enumerate(routed):
        xg = mixed[g]
        if direct_tokens[g] < tg:
            staged = _sc_gather_rows(w["uv"], ids_tok.reshape(-1), first_row=direct_tokens[g] * ids_tok.shape[1],
                                     chunk=tiles["sc_chunk"])
            xg = _peer_mix_staged(staged, xg, hg, gates3, tb=tiles["mix_staged_tb"], first_token=direct_tokens[g])
        out.append(xg)
    return jnp.concatenate(out, axis=0)


_TILES = dict(in_tm=1024, in_tn=512, mla_tm=1024, attn_tq=512, attn_tk=512,
              out_tm=512, out_tn=512, route_tm=256, mix_tb=32,
              mix_direct_tokens=(3840, 0), mix_staged_tb=16, sc_chunk=16)


def kernel(x, positions, g_norm_mix, w_in, b_gate, g_cq, w_uq, g_ckv, w_ukv, g_qn, g_qr, g_kn, g_kr,
           w_a2, b_a, g_gla_out, w_out, g_norm_ffn, w_pq, sub_keys, u_emb, v_emb, tiles=None):
    tiles = _TILES if tiles is None else tiles
    batch, seq, d = x.shape
    x2 = x.reshape(batch * seq, d)
    pos = positions.reshape(batch * seq, 1)
    for l in range(g_norm_mix.shape[0]):
        w = _prepare_layer(g_norm_mix[l], w_in[l], b_gate[l], g_cq[l], w_uq[l], g_ckv[l], w_ukv[l], g_qn[l],
                           g_qr[l], g_kn[l], g_kr[l], w_a2[l], b_a[l], g_gla_out[l], w_out[l], g_norm_ffn[l],
                           w_pq[l], sub_keys[l], u_emb[l], v_emb[l])
        x2 = _layer(x2, pos, w, batch=batch, seq=seq, tiles=tiles)
    return x2.reshape(batch, seq, d)
```

```python
import functools

import jax
import jax.numpy as jnp
from jax import lax
from jax.experimental import pallas as pl
from jax.experimental.pallas import tpu as pltpu
from jax.experimental.pallas import tpu_sc as plsc

EPS = 1e-6
ROPE_THETA = 10000.0
GLA_TAU = 16.0
GLA_CHUNK = 64
PEER_TOPK = 16

LANES = 128
SUBLANES = 8
VMEM_LIMIT_BYTES = 56 * 1024 * 1024
MIX_SLOTS = 4
SC_CORES = 2
SC_SUBCORES = 16

F32 = jnp.float32
BF16 = jnp.bfloat16
NEG_INF = float("-inf")


def _params(semantics):
    return pltpu.CompilerParams(dimension_semantics=semantics, vmem_limit_bytes=VMEM_LIMIT_BYTES)


def _rms(x, gain, n=None):
    ss = jnp.sum(x * x, axis=-1, keepdims=True)
    n = x.shape[-1] if n is None else n
    return x * lax.rsqrt(ss * (1.0 / n) + EPS) * gain


def _gelu_exact(x):
    return 0.5 * x * (1.0 + lax.erf(x * (0.5 ** 0.5)))


def _dot(a, b):
    return jnp.dot(a, b, preferred_element_type=F32)


def _dot_nt(a, b):
    return lax.dot_general(a, b, (((1,), (1,)), ((), ())), preferred_element_type=F32)


def _dot_tn(a, b):
    return lax.dot_general(a, b, (((0,), (0,)), ((), ())), preferred_element_type=F32)


def _in_proj_kernel(x_ref, g_ref, w_ref, ws_ref, p_ref, ps_ref, h_scr):
    @pl.when(pl.program_id(1) == 0)
    def _():
        h = _rms(x_ref[...], g_ref[...]).astype(BF16)
        h_scr[...] = h
        ps_ref[...] = _dot(h, ws_ref[...])

    p_ref[...] = _dot(h_scr[...], w_ref[...]).astype(p_ref.dtype)


def _in_proj(x, g, w_main, w_small, *, tm, tn):
    t, d = x.shape
    n = w_main.shape[1]
    return pl.pallas_call(
        _in_proj_kernel,
        grid=(t // tm, n // tn),
        in_specs=[
            pl.BlockSpec((tm, d), lambda i, j: (i, 0)),
            pl.BlockSpec((1, d), lambda i, j: (0, 0)),
            pl.BlockSpec((d, tn), lambda i, j: (0, j)),
            pl.BlockSpec((d, LANES), lambda i, j: (0, 0)),
        ],
        out_specs=[
            pl.BlockSpec((tm, tn), lambda i, j: (i, j)),
            pl.BlockSpec((tm, LANES), lambda i, j: (i, 0)),
        ],
        out_shape=[
            jax.ShapeDtypeStruct((t, n), BF16),
            jax.ShapeDtypeStruct((t, LANES), F32),
        ],
        scratch_shapes=[pltpu.VMEM((tm, d), BF16)],
        compiler_params=_params(("parallel", "arbitrary")),
    )(x, g, w_main, w_small)


def _rope_tables(pos_ref, invf_ref, rope):
    ang = pos_ref[...].astype(F32) * invf_ref[...]
    cos, sin = jnp.cos(ang), jnp.sin(ang)
    lane = lax.broadcasted_iota(jnp.int32, ang.shape, 1)
    half = rope // 2
    c = jnp.where(lane < rope, cos, 0.0)
    s_lo = jnp.where(lane < half, -sin, 0.0)
    s_hi = jnp.where(lane < half, 0.0, jnp.where(lane < rope, sin, 0.0))
    return c, s_lo, s_hi


def _apply_rope(pe, c, s_lo, s_hi, rope):
    half = rope // 2
    from_hi = pltpu.roll(pe, LANES - half, 1)
    from_lo = pltpu.roll(pe, half, 1)
    return pe * c + from_hi * s_lo + from_lo * s_hi


def _mla_q_kernel(cq_ref, gcq_ref, w_ref, gq_ref, pos_ref, invf_ref, q_ref,
                  h_scr, c_scr, slo_scr, shi_scr, *, nope, rope):
    @pl.when(pl.program_id(1) == 0)
    def _():
        h_scr[...] = _rms(cq_ref[...].astype(F32), gcq_ref[...]).astype(BF16)
        c, s_lo, s_hi = _rope_tables(pos_ref, invf_ref, rope)
        c_scr[...] = c
        slo_scr[...] = s_lo
        shi_scr[...] = s_hi

    y = _dot(h_scr[...], w_ref[...])
    g = gq_ref[...]
    qn = _rms(y[:, :nope], g[:, :nope])
    pe = _rms(y[:, nope:], g[:, nope:], n=rope)
    pe = _apply_rope(pe, c_scr[...], slo_scr[...], shi_scr[...], rope)
    q_ref[:, :nope] = qn.astype(q_ref.dtype)
    q_ref[:, nope:] = pe.astype(q_ref.dtype)


def _mla_q(p, cq_blk, g_cq, w_uq_p, gq, pos, invf, *, heads, rank, nope, rope, tm):
    t = p.shape[0]
    hw = nope + LANES
    return pl.pallas_call(
        functools.partial(_mla_q_kernel, nope=nope, rope=rope),
        grid=(t // tm, heads),
        in_specs=[
            pl.BlockSpec((tm, rank), lambda i, j: (i, cq_blk)),
            pl.BlockSpec((1, rank), lambda i, j: (0, 0)),
            pl.BlockSpec((rank, hw), lambda i, j: (0, j)),
            pl.BlockSpec((1, hw), lambda i, j: (0, 0)),
            pl.BlockSpec((tm, 1), lambda i, j: (i, 0)),
            pl.BlockSpec((1, LANES), lambda i, j: (0, 0)),
        ],
        out_specs=pl.BlockSpec((tm, hw), lambda i, j: (i, j)),
        out_shape=jax.ShapeDtypeStruct((t, heads * hw), BF16),
        scratch_shapes=[
            pltpu.VMEM((tm, rank), BF16),
            pltpu.VMEM((tm, LANES), F32),
            pltpu.VMEM((tm, LANES), F32),
            pltpu.VMEM((tm, LANES), F32),
        ],
        compiler_params=_params(("parallel", "arbitrary")),
    )(p, g_cq, w_uq_p, gq, pos, invf)


def _mla_kv_kernel(ckv_ref, gckv_ref, w_ref, gkn_ref, small_ref, gkr_ref, pos_ref, invf_ref,
                   k_ref, v_ref, h_scr, kpe_scr, *, nope, rope):
    @pl.when(pl.program_id(1) == 0)
    def _():
        h_scr[...] = _rms(ckv_ref[...].astype(F32), gckv_ref[...]).astype(BF16)
        c, s_lo, s_hi = _rope_tables(pos_ref, invf_ref, rope)
        sm = small_ref[...]
        lane = lax.broadcasted_iota(jnp.int32, sm.shape, 1)
        pe = _rms(jnp.where(lane < rope, sm, 0.0), gkr_ref[...], n=rope)
        kpe_scr[...] = _apply_rope(pe, c, s_lo, s_hi, rope).astype(BF16)

    y = _dot(h_scr[...], w_ref[...])
    k_ref[:, :nope] = _rms(y[:, :nope], gkn_ref[...]).astype(k_ref.dtype)
    k_ref[:, nope:] = kpe_scr[...]
    v_ref[...] = y[:, nope:].astype(v_ref.dtype)


def _mla_kv(p, ckv_blk, g_ckv, w_ukv, g_kn, small, gkr, pos, invf, *, heads, rank, nope, rope, dv, tm):
    t = p.shape[0]
    kw = nope + LANES
    return pl.pallas_call(
        functools.partial(_mla_kv_kernel, nope=nope, rope=rope),
        grid=(t // tm, heads),
        in_specs=[
            pl.BlockSpec((tm, rank), lambda i, j: (i, ckv_blk)),
            pl.BlockSpec((1, rank), lambda i, j: (0, 0)),
            pl.BlockSpec((rank, nope + dv), lambda i, j: (0, j)),
            pl.BlockSpec((1, nope), lambda i, j: (0, 0)),
            pl.BlockSpec((tm, LANES), lambda i, j: (i, 0)),
            pl.BlockSpec((1, LANES), lambda i, j: (0, 0)),
            pl.BlockSpec((tm, 1), lambda i, j: (i, 0)),
            pl.BlockSpec((1, LANES), lambda i, j: (0, 0)),
        ],
        out_specs=[
            pl.BlockSpec((tm, kw), lambda i, j: (i, j)),
            pl.BlockSpec((tm, dv), lambda i, j: (i, j)),
        ],
        out_shape=[
            jax.ShapeDtypeStruct((t, heads * kw), BF16),
            jax.ShapeDtypeStruct((t, heads * dv), BF16),
        ],
        scratch_shapes=[pltpu.VMEM((tm, rank), BF16), pltpu.VMEM((tm, LANES), BF16)],
        compiler_params=_params(("parallel", "arbitrary")),
    )(p, g_ckv, w_ukv, g_kn, small, gkr, pos, invf)


def _attn_kernel(q_ref, k_ref, v_ref, o_ref, *, tq, tk):
    qi = pl.program_id(2)
    q = q_ref[...]
    dv = v_ref.shape[1]
    row = qi * tq + lax.broadcasted_iota(jnp.int32, (tq, tk), 0)
    col0 = lax.broadcasted_iota(jnp.int32, (tq, tk), 1)

    def body(kb, carry, masked):
        m, l, acc = carry
        k0 = pl.multiple_of(kb * tk, tk)
        s = _dot_nt(q, k_ref[pl.ds(k0, tk), :])
        if masked:
            s = jnp.where(col0 + k0 <= row, s, NEG_INF)
        m_new = jnp.maximum(m, jnp.max(s, axis=-1, keepdims=True))
        alpha = jnp.exp(m - m_new)
        pr = jnp.exp(s - m_new)
        l = alpha * l + jnp.sum(pr, axis=-1, keepdims=True)
        acc = alpha * acc + _dot(pr.astype(BF16), v_ref[pl.ds(k0, tk), :])
        return m_new, l, acc

    init = (jnp.full((tq, 1), NEG_INF, F32), jnp.zeros((tq, 1), F32), jnp.zeros((tq, dv), F32))
    n_below = (qi * tq) // tk
    nkb = ((qi + 1) * tq + tk - 1) // tk
    carry = lax.fori_loop(0, n_below, functools.partial(body, masked=False), init)
    _, l, acc = lax.fori_loop(n_below, nkb, functools.partial(body, masked=True), carry)
    o_ref[...] = (acc / l).astype(o_ref.dtype)


def _mla_attn(q, k, v, *, batch, seq, heads, dk, dv, tq, tk):
    t = q.shape[0]
    nq = seq // tq
    return pl.pallas_call(
        functools.partial(_attn_kernel, tq=tq, tk=tk),
        grid=(batch, heads, nq),
        in_specs=[
            pl.BlockSpec((tq, dk), lambda b, h, i: (b * nq + i, h)),
            pl.BlockSpec((seq, dk), lambda b, h, i: (b, h)),
            pl.BlockSpec((seq, dv), lambda b, h, i: (b, h)),
        ],
        out_specs=pl.BlockSpec((tq, dv), lambda b, h, i: (b * nq + i, h)),
        out_shape=jax.ShapeDtypeStruct((t, heads * dv), BF16),
        compiler_params=_params(("parallel", "parallel", "arbitrary")),
    )(q, k, v)


def _gla_kernel(q_ref, k_ref, v_ref, og_ref, small_ref, w2_ref, ba_ref, gon_ref, o_ref, st_scr,
                *, seq, dk, dv, chunk):
    c = chunk
    st_scr[...] = jnp.zeros_like(st_scr)
    r_i = lax.broadcasted_iota(jnp.int32, (c, c), 0)
    c_i = lax.broadcasted_iota(jnp.int32, (c, c), 1)
    tri = jnp.where(c_i <= r_i, 1.0, 0.0).astype(BF16)
    row_id = lax.broadcasted_iota(jnp.int32, (c, 1), 0)
    w2 = w2_ref[...]
    ba = ba_ref[...]
    gon = gon_ref[...]
    q_scale = dk ** -0.5

    def chunk_step(ci, carry):
        r0 = pl.multiple_of(ci * c, c)
        qc = q_ref[pl.ds(r0, c), :].astype(F32) * q_scale
        kc = k_ref[pl.ds(r0, c), :].astype(F32)
        vc = v_ref[pl.ds(r0, c), :]
        z = _dot(small_ref[pl.ds(r0, c), :].astype(BF16), w2) + ba
        la = jax.nn.log_sigmoid(z) * (1.0 / GLA_TAU)
        hi = la.astype(BF16)
        r1 = la - hi.astype(F32)
        mid = r1.astype(BF16)
        lo = (r1 - mid.astype(F32)).astype(BF16)
        b = _dot(tri, hi) + _dot(tri, mid) + _dot(tri, lo)

        st = st_scr[...]
        inter = _dot_nt((qc * jnp.exp(b)).astype(BF16), st.astype(BF16))

        att = jnp.zeros((c, c), F32)
        for j in range(c):
            lo_r = (j // SUBLANES) * SUBLANES
            d = b[lo_r:, :] - b[j:j + 1, :]
            e = jnp.exp(jnp.where(row_id[lo_r:, :] >= j, d, NEG_INF))
            col = jnp.sum(qc[lo_r:, :] * kc[j:j + 1, :] * e, axis=-1, keepdims=True)
            if lo_r:
                col = jnp.concatenate([jnp.zeros((lo_r, 1), F32), col], axis=0)
            att = jnp.where(c_i == j, col, att)
        o = inter + _dot(att.astype(BF16), vc)

        b_last = b[c - 1:c, :]
        k_dec = (kc * jnp.exp(b_last - b)).astype(BF16)
        st_scr[...] = st * jnp.exp(b_last) + _dot_tn(vc, k_dec)

        og = og_ref[pl.ds(r0, c), :].astype(F32)
        out = _rms(o, gon) * (og * jax.nn.sigmoid(og))
        o_ref[pl.ds(r0, c), :] = out.astype(o_ref.dtype)
        return carry

    lax.fori_loop(0, seq // c, chunk_step, 0)


def _gla(p, small, w2p, b_a, g_on, *, batch, seq, heads, dk, dv, q_blk, k_blk, v_blk, og_blk):
    t = p.shape[0]
    return pl.pallas_call(
        functools.partial(_gla_kernel, seq=seq, dk=dk, dv=dv, chunk=GLA_CHUNK),
        grid=(batch, heads),
        in_specs=[
            pl.BlockSpec((seq, dk), lambda b, h: (b, q_blk + h)),
            pl.BlockSpec((seq, dk), lambda b, h: (b, k_blk + h)),
            pl.BlockSpec((seq, dv), lambda b, h: (b, v_blk + h)),
            pl.BlockSpec((seq, dv), lambda b, h: (b, og_blk + h)),
            pl.BlockSpec((seq, LANES), lambda b, h: (b, 0)),
            pl.BlockSpec((LANES, dk), lambda b, h: (0, h)),
            pl.BlockSpec((1, dk), lambda b, h: (0, h)),
            pl.BlockSpec((1, dv), lambda b, h: (0, 0)),
        ],
        out_specs=pl.BlockSpec((seq, dv), lambda b, h: (b, h)),
        out_shape=jax.ShapeDtypeStruct((t, heads * dv), BF16),
        scratch_shapes=[pltpu.VMEM((dv, dk), F32)],
        compiler_params=_params(("parallel", "parallel")),
    )(p, p, p, p, small, w2p, b_a, g_on)


def _out_proj_kernel(x_ref, ga_ref, gb_ref, oa_ref, ob_ref, bg_ref, w_ref, o_ref, m_scr):
    @pl.when(pl.program_id(1) == 0)
    def _():
        bg = bg_ref[...]
        sa = jax.nn.sigmoid(ga_ref[...].astype(F32) + bg[0:1, :])
        sb = jax.nn.sigmoid(gb_ref[...].astype(F32) + bg[1:2, :])
        m_scr[...] = (sa * oa_ref[...].astype(F32) + sb * ob_ref[...].astype(F32)).astype(BF16)

    o_ref[...] = x_ref[...] + _dot(m_scr[...], w_ref[...])


def _out_proj(x, p, o_mla, o_gla, b_gate, w_out, *, ga_blk, gb_blk, tm, tn):
    t, d = x.shape
    return pl.pallas_call(
        _out_proj_kernel,
        grid=(t // tm, d // tn),
        in_specs=[
            pl.BlockSpec((tm, tn), lambda i, j: (i, j)),
            pl.BlockSpec((tm, d), lambda i, j: (i, ga_blk)),
            pl.BlockSpec((tm, d), lambda i, j: (i, gb_blk)),
            pl.BlockSpec((tm, d), lambda i, j: (i, 0)),
            pl.BlockSpec((tm, d), lambda i, j: (i, 0)),
            pl.BlockSpec((2, d), lambda i, j: (0, 0)),
            pl.BlockSpec((d, tn), lambda i, j: (0, j)),
        ],
        out_specs=pl.BlockSpec((tm, tn), lambda i, j: (i, j)),
        out_shape=jax.ShapeDtypeStruct((t, d), F32),
        scratch_shapes=[pltpu.VMEM((tm, d), BF16)],
        compiler_params=_params(("parallel", "arbitrary")),
    )(x, p, p, o_mla, o_gla, b_gate, w_out)


def _topk_rows(s, k, payload=None):
    n = s.shape[0]
    iota = lax.broadcasted_iota(jnp.int32, s.shape, 0).astype(F32)
    kiota = lax.broadcasted_iota(jnp.int32, (k, s.shape[1]), 0)
    vals = jnp.zeros((k, s.shape[1]), F32)
    picks = jnp.zeros((k, s.shape[1]), F32)
    for r in range(k):
        m = jnp.max(s, axis=0, keepdims=True)
        idx = jnp.min(jnp.where(s == m, iota, float(n)), axis=0, keepdims=True)
        hit = iota == idx
        if payload is None:
            pick = idx
        else:
            pick = jnp.sum(jnp.where(hit, payload, 0.0), axis=0, keepdims=True)
        vals = jnp.where(kiota == r, m, vals)
        picks = jnp.where(kiota == r, pick, picks)
        s = jnp.where(hit, NEG_INF, s)
    return vals, picks


def _peer_route_kernel(x_ref, g_ref, w_ref, sk_ref, h_ref, ids_ref, gates_ref, *, heads, nkeys, half, topk):
    h = _rms(x_ref[...], g_ref[...]).astype(BF16)
    h_ref[...] = h
    qf = _dot(h, w_ref[...])
    for hd in range(heads):
        tops = []
        for part in range(2):
            o = (hd * 2 + part) * half
            qh = qf[:, o:o + half].astype(BF16)
            keys = sk_ref[(hd * 2 + part) * nkeys:(hd * 2 + part + 1) * nkeys, :]
            tops.append(_topk_rows(_dot_nt(keys, qh), topk))
        (s1, i1), (s2, i2) = tops
        assert topk == 2 * SUBLANES
        hs = SUBLANES
        pair_s = [s1[0:1, :] + s2] + [s1[a:a + 1, :] + s2[:hs, :] for a in range(1, hs)] + [s1[hs:, :] + s2[0:1, :]]
        pair_id = ([i1[0:1, :] * float(nkeys) + i2]
                   + [i1[a:a + 1, :] * float(nkeys) + i2[:hs, :] for a in range(1, hs)]
                   + [i1[hs:, :] * float(nkeys) + i2[0:1, :]])
        best_s, best_id = _topk_rows(jnp.concatenate(pair_s, axis=0), topk,
                                     payload=jnp.concatenate(pair_id, axis=0))
        e = jnp.exp(best_s - best_s[0:1, :])
        gate = e / jnp.sum(e, axis=0, keepdims=True)
        ids_ref[hd * topk:(hd + 1) * topk, :] = best_id.astype(jnp.int32)
        gates_ref[hd * topk:(hd + 1) * topk, :] = gate


def _peer_route(x, g, w_pq, sk2d, *, heads, nkeys, half, topk, tm):
    t, d = x.shape
    dq = w_pq.shape[1]
    return pl.pallas_call(
        functools.partial(_peer_route_kernel, heads=heads, nkeys=nkeys, half=half, topk=topk),
        grid=(t // tm,),
        in_specs=[
            pl.BlockSpec((tm, d), lambda i: (i, 0)),
            pl.BlockSpec((1, d), lambda i: (0, 0)),
            pl.BlockSpec((d, dq), lambda i: (0, 0)),
            pl.BlockSpec((heads * 2 * nkeys, half), lambda i: (0, 0)),
        ],
        out_specs=[
            pl.BlockSpec((tm, d), lambda i: (i, 0)),
            pl.BlockSpec((heads * topk, tm), lambda i: (0, i)),
            pl.BlockSpec((heads * topk, tm), lambda i: (0, i)),
        ],
        out_shape=[
            jax.ShapeDtypeStruct((t, d), BF16),
            jax.ShapeDtypeStruct((heads * topk, t), jnp.int32),
            jax.ShapeDtypeStruct((heads * topk, t), F32),
        ],
        compiler_params=_params(("parallel",)),
    )(x, g, w_pq, sk2d)


def _pack_expert_table(u_emb, v_emb):
    def pack(w):
        bits = lax.bitcast_convert_type(w.astype(BF16), jnp.uint16).astype(jnp.uint32)
        half = w.shape[1] // 2
        return bits[:, :half] | (bits[:, half:] << 16)

    return lax.bitcast_convert_type(jnp.concatenate([pack(u_emb), pack(v_emb)], axis=1), jnp.int32)


def _unpack_words(w):
    lo = lax.bitcast_convert_type(w << 16, F32)
    hi = lax.bitcast_convert_type(w & jnp.int32(-65536), F32)
    return lo, hi


def _expert_mix(words, h, gate, d):
    half = d // 2
    u_lo, u_hi = _unpack_words(words[:, :half])
    act = jnp.sum(u_lo * h[:, :half] + u_hi * h[:, half:], axis=-1, keepdims=True)
    w = gate * _gelu_exact(act)
    v_lo, v_hi = _unpack_words(words[:, half:])
    return jnp.concatenate([jnp.sum(v_lo * w, axis=0, keepdims=True),
                            jnp.sum(v_hi * w, axis=0, keepdims=True)], axis=1)


def _peer_mix_kernel(ids_hbm, uv_hbm, x_ref, h_ref, gates_ref, after_hbm, o_ref, ids_smem, buf, ids_sem, row_sems,
                     *, tb, picks, d):
    del after_hbm
    step = pl.program_id(0)
    ids_copy = pltpu.make_async_copy(ids_hbm.at[step], ids_smem, ids_sem)
    ids_copy.start()
    ids_copy.wait()

    def row_copy(tok, j, slot):
        eid = ids_smem[tok * picks + j]
        return pltpu.make_async_copy(uv_hbm.at[pl.ds(eid, 1), :], buf.at[slot, pl.ds(j, 1), :], row_sems.at[slot])

    def issue(tok, slot):
        for j in range(picks):
            row_copy(tok, j, slot).start()

    def wait_all(slot):
        pltpu.make_async_copy(uv_hbm.at[pl.ds(0, picks), :], buf.at[slot], row_sems.at[slot]).wait()

    ahead = MIX_SLOTS - 1
    for s in range(ahead):
        issue(s, s)

    def group(g, carry):
        rows = []
        gates = gates_ref[g]
        for u in range(SUBLANES):
            tok = g * SUBLANES + u
            slot = u % MIX_SLOTS

            @pl.when(tok + ahead < tb)
            def _():
                issue(tok + ahead, (u + ahead) % MIX_SLOTS)

            wait_all(slot)
            r8 = pl.multiple_of(g * SUBLANES, SUBLANES)
            h8 = h_ref[pl.ds(r8, SUBLANES), :].astype(F32)
            rows.append(_expert_mix(buf[slot], h8[u:u + 1, :], gates[:, u:u + 1], d))
        r8 = pl.multiple_of(g * SUBLANES, SUBLANES)
        o_ref[pl.ds(r8, SUBLANES), :] = x_ref[pl.ds(r8, SUBLANES), :] + jnp.concatenate(rows, axis=0)
        return carry

    lax.fori_loop(0, tb // SUBLANES, group, 0)


def _peer_mix_direct(ids_blk, uv, x, h, gates3, after, *, tb, n_tokens):
    t, d = x.shape
    picks = gates3.shape[1]
    return pl.pallas_call(
        functools.partial(_peer_mix_kernel, tb=tb, picks=picks, d=d),
        grid=(n_tokens // tb,),
        input_output_aliases={2: 0},
        in_specs=[
            pl.BlockSpec(memory_space=pl.ANY),
            pl.BlockSpec(memory_space=pl.ANY),
            pl.BlockSpec((tb, d), lambda i: (i, 0)),
            pl.BlockSpec((tb, d), lambda i: (i, 0)),
            pl.BlockSpec((tb // SUBLANES, picks, SUBLANES), lambda i: (i, 0, 0)),
            pl.BlockSpec(memory_space=pl.ANY),
        ],
        out_specs=pl.BlockSpec((tb, d), lambda i: (i, 0)),
        out_shape=jax.ShapeDtypeStruct((t, d), F32),
        scratch_shapes=[
            pltpu.SMEM((tb * picks,), jnp.int32),
            pltpu.VMEM((MIX_SLOTS, picks, d), jnp.int32),
            pltpu.SemaphoreType.DMA(()),
            pltpu.SemaphoreType.DMA((MIX_SLOTS,)),
        ],
        compiler_params=_params(("arbitrary",)),
    )(ids_blk, uv, x, h, gates3, after)


def _peer_mix_staged_kernel(rows_ref, x_ref, h_ref, gates_ref, o_ref, *, tb, picks, d):
    for g in range(tb // SUBLANES):
        gates = gates_ref[g]
        h8 = h_ref[g * SUBLANES:(g + 1) * SUBLANES, :].astype(F32)
        rows = []
        for u in range(SUBLANES):
            r0 = (g * SUBLANES + u) * picks
            rows.append(_expert_mix(rows_ref[r0:r0 + picks, :], h8[u:u + 1, :], gates[:, u:u + 1], d))
        sl = slice(g * SUBLANES, (g + 1) * SUBLANES)
        o_ref[sl, :] = x_ref[sl, :] + jnp.concatenate(rows, axis=0)


def _peer_mix_staged(staged, x, h, gates3, *, tb, first_token):
    t, d = x.shape
    picks = gates3.shape[1]
    n_tokens = staged.shape[0] // picks
    assert first_token % tb == 0 and first_token + n_tokens == t
    b0 = first_token // tb
    return pl.pallas_call(
        functools.partial(_peer_mix_staged_kernel, tb=tb, picks=picks, d=d),
        grid=(n_tokens // tb,),
        input_output_aliases={1: 0},
        in_specs=[
            pl.BlockSpec((tb * picks, d), lambda i: (i, 0)),
            pl.BlockSpec((tb, d), lambda i: (i + b0, 0)),
            pl.BlockSpec((tb, d), lambda i: (i + b0, 0)),
            pl.BlockSpec((tb // SUBLANES, picks, SUBLANES), lambda i: (i + b0, 0, 0)),
        ],
        out_specs=pl.BlockSpec((tb, d), lambda i: (i + b0, 0)),
        out_shape=jax.ShapeDtypeStruct((t, d), F32),
        compiler_params=_params(("parallel",)),
    )(staged, x, h, gates3)


def _sc_gather_rows(table, idx, *, first_row, chunk):
    n_rows = idx.shape[0] - first_row
    d = table.shape[1]
    workers = SC_CORES * SC_SUBCORES
    assert n_rows % (workers * 2 * chunk) == 0 and chunk % SUBLANES == 0 and chunk <= LANES
    assert first_row % SUBLANES == 0
    rows_per_worker = n_rows // workers
    n_pairs = rows_per_worker // (2 * chunk)
    mesh = plsc.VectorSubcoreMesh(core_axis_name="c", subcore_axis_name="s")

    @functools.partial(
        pl.kernel, mesh=mesh,
        out_type=jax.ShapeDtypeStruct((n_rows, d), table.dtype),
        scratch_types=[
            pltpu.VMEM((chunk,), jnp.int32), pltpu.VMEM((chunk,), jnp.int32),
            pltpu.VMEM((chunk, d), table.dtype), pltpu.VMEM((chunk, d), table.dtype),
            pltpu.SemaphoreType.DMA, pltpu.SemaphoreType.DMA,
            pltpu.SemaphoreType.DMA, pltpu.SemaphoreType.DMA,
        ],
    )
    def gather_kernel(table_hbm, idx_hbm, out_hbm, idx0, idx1, rows0, rows1, gsem0, gsem1, wsem0, wsem1):
        idx_v, rows_v, gsem, wsem = (idx0, idx1), (rows0, rows1), (gsem0, gsem1), (wsem0, wsem1)
        worker = lax.axis_index("s") * SC_CORES + lax.axis_index("c")
        base = worker * rows_per_worker

        def out_rows(c):
            return pl.ds(pl.multiple_of(base + c * chunk, SUBLANES), chunk)

        def load_idx(slot, c):
            src = pl.ds(pl.multiple_of(first_row + base + c * chunk, SUBLANES), chunk)
            pltpu.sync_copy(idx_hbm.at[src], idx_v[slot])

        def gather(slot):
            return pltpu.make_async_copy(table_hbm.at[idx_v[slot]], rows_v[slot], gsem[slot])

        def writeout(slot, c):
            return pltpu.make_async_copy(rows_v[slot], out_hbm.at[out_rows(c)], wsem[slot])

        load_idx(0, 0)
        gather(0).start()

        @pl.loop(0, n_pairs)
        def _(p):
            c0 = 2 * p

            @pl.when(p > 0)
            def _():
                writeout(1, c0 - 1).wait()

            load_idx(1, c0 + 1)
            gather(1).start()
            gather(0).wait()
            writeout(0, c0).start()

            @pl.when(p + 1 < n_pairs)
            def _():
                load_idx(0, c0 + 2)
                writeout(0, c0).wait()
                gather(0).start()

            gather(1).wait()
            writeout(1, c0 + 1).start()

        writeout(0, 2 * n_pairs - 2).wait()
        writeout(1, 2 * n_pairs - 1).wait()

    return gather_kernel(table, idx)


def _pad_cols(w, width):
    return jnp.pad(w, ((0, 0), (0, width - w.shape[1])))


def _prepare_layer(g_norm_mix, w_in, b_gate, g_cq, w_uq, g_ckv, w_ukv, g_qn, g_qr, g_kn, g_kr,
                   w_a2, b_a, g_gla_out, w_out, g_norm_ffn, w_pq, sub_keys, u_emb, v_emb):
    d = w_in.shape[0]
    q_rank, kv_rank = g_cq.shape[0], g_ckv.shape[0]
    nope, rope = g_qn.shape[0], g_qr.shape[0]
    mla_heads = w_uq.shape[1] // (nope + rope)
    mla_v = w_ukv.shape[1] // mla_heads - nope
    gate_rank, gla_dk_all = w_a2.shape
    gla_dv = g_gla_out.shape[0]
    gla_heads = d // gla_dv
    gla_dk = gla_dk_all // gla_heads
    peer_heads, _, nkeys, half = sub_keys.shape
    assert nope == LANES and mla_v == LANES and rope <= LANES and rope % 2 == 0
    assert mla_heads * mla_v == d and gla_heads * gla_dv == d
    assert rope + gate_rank <= LANES and nkeys == LANES and half == LANES

    widths = (q_rank, kv_rank, rope, gla_dk_all, gla_dk_all, d, gate_rank, d, d, d)
    offs = [0]
    for wd in widths:
        offs.append(offs[-1] + wd)
    assert offs[-1] == w_in.shape[1]
    seg = lambda i: w_in[:, offs[i]:offs[i + 1]]
    w_main = jnp.concatenate([seg(5), seg(7), seg(8), seg(9), seg(3), seg(4), seg(0), seg(1)], axis=1).astype(BF16)
    w_small = _pad_cols(jnp.concatenate([seg(2), seg(6)], axis=1), LANES).astype(BF16)
    cq_off = 4 * d + 2 * gla_dk_all
    assert cq_off % q_rank == 0 and (cq_off + q_rank) % kv_rank == 0
    inv_freq = ROPE_THETA ** (-jnp.arange(0, rope, 2, dtype=F32) / rope)
    scale = (nope + rope) ** -0.5
    return dict(
        dims=dict(q_rank=q_rank, kv_rank=kv_rank, nope=nope, rope=rope, mla_heads=mla_heads, mla_v=mla_v,
                  gla_heads=gla_heads, gla_dk=gla_dk, gla_dv=gla_dv, peer_heads=peer_heads, nkeys=nkeys, half=half,
                  v_blk=0, og_blk=d // gla_dv, ga_blk=2, gb_blk=3, q_blk=4 * d // gla_dk,
                  k_blk=4 * d // gla_dk + gla_heads, cq_blk=cq_off // q_rank, ckv_blk=(cq_off + q_rank) // kv_rank),
        g_norm_mix=g_norm_mix[None, :], w_main=w_main, w_small=w_small,
        invf=_pad_cols(jnp.concatenate([inv_freq, inv_freq])[None, :], LANES),
        gq=_pad_cols(jnp.concatenate([g_qn, g_qr])[None, :] * scale, nope + LANES),
        gkr=_pad_cols(g_kr[None, :], LANES), g_cq=g_cq[None, :], g_ckv=g_ckv[None, :], g_kn=g_kn[None, :],
        w_uq=jnp.pad(w_uq.reshape(q_rank, mla_heads, nope + rope),
                     ((0, 0), (0, 0), (0, LANES - rope))).reshape(q_rank, -1).astype(BF16),
        w_ukv=w_ukv.astype(BF16),
        w2p=jnp.zeros((LANES, gla_dk_all), F32).at[rope:rope + gate_rank].set(w_a2).astype(BF16),
        b_a=b_a[None, :], g_on=g_gla_out[None, :], b_gate=b_gate, w_out=w_out.astype(BF16),
        g_norm_ffn=g_norm_ffn[None, :], w_pq=w_pq.astype(BF16),
        sk2d=sub_keys.reshape(peer_heads * 2 * nkeys, half).astype(BF16),
        uv=_pack_expert_table(u_emb, v_emb),
    )


def _mixers_and_route(x2, pos, w, *, batch, seq, tiles):
    t = x2.shape[0]
    dm = w["dims"]
    p, small = _in_proj(x2, w["g_norm_mix"], w["w_main"], w["w_small"], tm=tiles["in_tm"], tn=tiles["in_tn"])
    q = _mla_q(p, dm["cq_blk"], w["g_cq"], w["w_uq"], w["gq"], pos, w["invf"], heads=dm["mla_heads"],
               rank=dm["q_rank"], nope=dm["nope"], rope=dm["rope"], tm=tiles["mla_tm"])
    k, v = _mla_kv(p, dm["ckv_blk"], w["g_ckv"], w["w_ukv"], w["g_kn"], small, w["gkr"], pos, w["invf"],
                   heads=dm["mla_heads"], rank=dm["kv_rank"], nope=dm["nope"], rope=dm["rope"], dv=dm["mla_v"],
                   tm=tiles["mla_tm"])
    o_mla = _mla_attn(q, k, v, batch=batch, seq=seq, heads=dm["mla_heads"], dk=dm["nope"] + LANES, dv=dm["mla_v"],
                      tq=tiles["attn_tq"], tk=tiles["attn_tk"])
    o_gla = _gla(p, small, w["w2p"], w["b_a"], w["g_on"], batch=batch, seq=seq, heads=dm["gla_heads"],
                 dk=dm["gla_dk"], dv=dm["gla_dv"], q_blk=dm["q_blk"], k_blk=dm["k_blk"], v_blk=dm["v_blk"],
                 og_blk=dm["og_blk"])
    x2 = _out_proj(x2, p, o_mla, o_gla, w["b_gate"], w["w_out"], ga_blk=dm["ga_blk"], gb_blk=dm["gb_blk"],
                   tm=tiles["out_tm"], tn=tiles["out_tn"])
    h2, ids_t, gates_t = _peer_route(x2, w["g_norm_ffn"], w["w_pq"], w["sk2d"], heads=dm["peer_heads"],
                                     nkeys=dm["nkeys"], half=dm["half"], topk=PEER_TOPK, tm=tiles["route_tm"])
    picks = dm["peer_heads"] * PEER_TOPK
    gates3 = gates_t.reshape(picks, t // SUBLANES, SUBLANES).transpose(1, 0, 2)
    return x2, h2, ids_t.T, gates3


def _layer(x2, pos, w, *, batch, seq, tiles):
    t, d = x2.shape
    direct_tokens = tiles["mix_direct_tokens"]
    groups = len(direct_tokens)
    tg, bg = t // groups, batch // groups
    routed = [_mixers_and_route(x2[g * tg:(g + 1) * tg], pos[g * tg:(g + 1) * tg], w,
                                batch=bg, seq=seq, tiles=tiles) for g in range(groups)]
    tb = tiles["mix_tb"]
    mixed = []
    for g, (xg, hg, ids_tok, gates3) in enumerate(routed):
        if direct_tokens[g]:
            after = routed[min(g + 1, groups - 1)][2]
            xg = _peer_mix_direct(ids_tok.reshape(tg // tb, -1), w["uv"], xg, hg, gates3, after,
                                  tb=tb, n_tokens=direct_tokens[g])
        mixed.append(xg)
    out = []
    for g, (_, hg, ids_tok, gates3) in enumerate(routed):
        xg = mixed[g]
        if direct_tokens[g] < tg:
            staged = _sc_gather_rows(w["uv"], ids_tok.reshape(-1), first_row=direct_tokens[g] * ids_tok.shape[1],
                                     chunk=tiles["sc_chunk"])
            xg = _peer_mix_staged(staged, xg, hg, gates3, tb=tiles["mix_staged_tb"], first_token=direct_tokens[g])
        out.append(xg)
    return jnp.concatenate(out, axis=0)


_TILES = dict(in_tm=1024, in_tn=512, mla_tm=1024, attn_tq=512, attn_tk=512,
              out_tm=512, out_tn=512, route_tm=256, mix_tb=64,
              mix_direct_tokens=(0, 0, 0, 2048), mix_staged_tb=16, sc_chunk=16)


def kernel(x, positions, g_norm_mix, w_in, b_gate, g_cq, w_uq, g_ckv, w_ukv, g_qn, g_qr, g_kn, g_kr,
           w_a2, b_a, g_gla_out, w_out, g_norm_ffn, w_pq, sub_keys, u_emb, v_emb, tiles=None):
    tiles = _TILES if tiles is None else tiles
    batch, seq, d = x.shape
    x2 = x.reshape(batch * seq, d)
    pos = positions.reshape(batch * seq, 1)
    for l in range(g_norm_mix.shape[0]):
        w = _prepare_layer(g_norm_mix[l], w_in[l], b_gate[l], g_cq[l], w_uq[l], g_ckv[l], w_ukv[l], g_qn[l],
                           g_qr[l], g_kn[l], g_kr[l], w_a2[l], b_a[l], g_gla_out[l], w_out[l], g_norm_ffn[l],
                           w_pq[l], sub_keys[l], u_emb[l], v_emb[l])
        x2 = _layer(x2, pos, w, batch=batch, seq=seq, tiles=tiles)
    return x2.reshape(batch, seq, d)
```

```python
import functools

import jax
import jax.numpy as jnp
from jax import lax
from jax.experimental import pallas as pl
from jax.experimental.pallas import tpu as pltpu
from jax.experimental.pallas import tpu_sc as plsc

EPS = 1e-6
ROPE_THETA = 10000.0
GLA_TAU = 16.0
GLA_CHUNK = 64
PEER_TOPK = 16

LANES = 128
SUBLANES = 8
VMEM_LIMIT_BYTES = 56 * 1024 * 1024
MIX_SLOTS = 4
SC_CORES = 2
SC_SUBCORES = 16

F32 = jnp.float32
BF16 = jnp.bfloat16
NEG_INF = float("-inf")


def _params(semantics):
    return pltpu.CompilerParams(dimension_semantics=semantics, vmem_limit_bytes=VMEM_LIMIT_BYTES)


def _rms(x, gain, n=None):
    ss = jnp.sum(x * x, axis=-1, keepdims=True)
    n = x.shape[-1] if n is None else n
    return x * lax.rsqrt(ss * (1.0 / n) + EPS) * gain


def _gelu_exact(x):
    return 0.5 * x * (1.0 + lax.erf(x * (0.5 ** 0.5)))


def _dot(a, b):
    return jnp.dot(a, b, preferred_element_type=F32)


def _dot_nt(a, b):
    return lax.dot_general(a, b, (((1,), (1,)), ((), ())), preferred_element_type=F32)


def _dot_tn(a, b):
    return lax.dot_general(a, b, (((0,), (0,)), ((), ())), preferred_element_type=F32)


def _in_proj_kernel(x_ref, g_ref, w_ref, ws_ref, p_ref, ps_ref, h_scr):
    @pl.when(pl.program_id(1) == 0)
    def _():
        h = _rms(x_ref[...], g_ref[...]).astype(BF16)
        h_scr[...] = h
        ps_ref[...] = _dot(h, ws_ref[...])

    p_ref[...] = _dot(h_scr[...], w_ref[...]).astype(p_ref.dtype)


def _in_proj(x, g, w_main, w_small, *, tm, tn):
    t, d = x.shape
    n = w_main.shape[1]
    return pl.pallas_call(
        _in_proj_kernel,
        grid=(t // tm, n // tn),
        in_specs=[
            pl.BlockSpec((tm, d), lambda i, j: (i, 0)),
            pl.BlockSpec((1, d), lambda i, j: (0, 0)),
            pl.BlockSpec((d, tn), lambda i, j: (0, j)),
            pl.BlockSpec((d, LANES), lambda i, j: (0, 0)),
        ],
        out_specs=[
            pl.BlockSpec((tm, tn), lambda i, j: (i, j)),
            pl.BlockSpec((tm, LANES), lambda i, j: (i, 0)),
        ],
        out_shape=[
            jax.ShapeDtypeStruct((t, n), BF16),
            jax.ShapeDtypeStruct((t, LANES), F32),
        ],
        scratch_shapes=[pltpu.VMEM((tm, d), BF16)],
        compiler_params=_params(("parallel", "arbitrary")),
    )(x, g, w_main, w_small)


def _rope_tables(pos_ref, invf_ref, rope):
    ang = pos_ref[...].astype(F32) * invf_ref[...]
    cos, sin = jnp.cos(ang), jnp.sin(ang)
    lane = lax.broadcasted_iota(jnp.int32, ang.shape, 1)
    half = rope // 2
    c = jnp.where(lane < rope, cos, 0.0)
    s_lo = jnp.where(lane < half, -sin, 0.0)
    s_hi = jnp.where(lane < half, 0.0, jnp.where(lane < rope, sin, 0.0))
    return c, s_lo, s_hi


def _apply_rope(pe, c, s_lo, s_hi, rope):
    half = rope // 2
    from_hi = pltpu.roll(pe, LANES - half, 1)
    from_lo = pltpu.roll(pe, half, 1)
    return pe * c + from_hi * s_lo + from_lo * s_hi


def _mla_q_kernel(cq_ref, gcq_ref, w_ref, gq_ref, pos_ref, invf_ref, q_ref,
                  h_scr, c_scr, slo_scr, shi_scr, *, nope, rope):
    @pl.when(pl.program_id(1) == 0)
    def _():
        h_scr[...] = _rms(cq_ref[...].astype(F32), gcq_ref[...]).astype(BF16)
        c, s_lo, s_hi = _rope_tables(pos_ref, invf_ref, rope)
        c_scr[...] = c
        slo_scr[...] = s_lo
        shi_scr[...] = s_hi

    y = _dot(h_scr[...], w_ref[...])
    g = gq_ref[...]
    qn = _rms(y[:, :nope], g[:, :nope])
    pe = _rms(y[:, nope:], g[:, nope:], n=rope)
    pe = _apply_rope(pe, c_scr[...], slo_scr[...], shi_scr[...], rope)
    q_ref[:, :nope] = qn.astype(q_ref.dtype)
    q_ref[:, nope:] = pe.astype(q_ref.dtype)


def _mla_q(p, cq_blk, g_cq, w_uq_p, gq, pos, invf, *, heads, rank, nope, rope, tm):
    t = p.shape[0]
    hw = nope + LANES
    return pl.pallas_call(
        functools.partial(_mla_q_kernel, nope=nope, rope=rope),
        grid=(t // tm, heads),
        in_specs=[
            pl.BlockSpec((tm, rank), lambda i, j: (i, cq_blk)),
            pl.BlockSpec((1, rank), lambda i, j: (0, 0)),
            pl.BlockSpec((rank, hw), lambda i, j: (0, j)),
            pl.BlockSpec((1, hw), lambda i, j: (0, 0)),
            pl.BlockSpec((tm, 1), lambda i, j: (i, 0)),
            pl.BlockSpec((1, LANES), lambda i, j: (0, 0)),
        ],
        out_specs=pl.BlockSpec((tm, hw), lambda i, j: (i, j)),
        out_shape=jax.ShapeDtypeStruct((t, heads * hw), BF16),
        scratch_shapes=[
            pltpu.VMEM((tm, rank), BF16),
            pltpu.VMEM((tm, LANES), F32),
            pltpu.VMEM((tm, LANES), F32),
            pltpu.VMEM((tm, LANES), F32),
        ],
        compiler_params=_params(("parallel", "arbitrary")),
    )(p, g_cq, w_uq_p, gq, pos, invf)


def _mla_kv_kernel(ckv_ref, gckv_ref, w_ref, gkn_ref, small_ref, gkr_ref, pos_ref, invf_ref,
                   k_ref, v_ref, h_scr, kpe_scr, *, nope, rope):
    @pl.when(pl.program_id(1) == 0)
    def _():
        h_scr[...] = _rms(ckv_ref[...].astype(F32), gckv_ref[...]).astype(BF16)
        c, s_lo, s_hi = _rope_tables(pos_ref, invf_ref, rope)
        sm = small_ref[...]
        lane = lax.broadcasted_iota(jnp.int32, sm.shape, 1)
        pe = _rms(jnp.where(lane < rope, sm, 0.0), gkr_ref[...], n=rope)
        kpe_scr[...] = _apply_rope(pe, c, s_lo, s_hi, rope).astype(BF16)

    y = _dot(h_scr[...], w_ref[...])
    k_ref[:, :nope] = _rms(y[:, :nope], gkn_ref[...]).astype(k_ref.dtype)
    k_ref[:, nope:] = kpe_scr[...]
    v_ref[...] = y[:, nope:].astype(v_ref.dtype)


def _mla_kv(p, ckv_blk, g_ckv, w_ukv, g_kn, small, gkr, pos, invf, *, heads, rank, nope, rope, dv, tm):
    t = p.shape[0]
    kw = nope + LANES
    return pl.pallas_call(
        functools.partial(_mla_kv_kernel, nope=nope, rope=rope),
        grid=(t // tm, heads),
        in_specs=[
            pl.BlockSpec((tm, rank), lambda i, j: (i, ckv_blk)),
            pl.BlockSpec((1, rank), lambda i, j: (0, 0)),
            pl.BlockSpec((rank, nope + dv), lambda i, j: (0, j)),
            pl.BlockSpec((1, nope), lambda i, j: (0, 0)),
            pl.BlockSpec((tm, LANES), lambda i, j: (i, 0)),
            pl.BlockSpec((1, LANES), lambda i, j: (0, 0)),
            pl.BlockSpec((tm, 1), lambda i, j: (i, 0)),
            pl.BlockSpec((1, LANES), lambda i, j: (0, 0)),
        ],
        out_specs=[
            pl.BlockSpec((tm, kw), lambda i, j: (i, j)),
            pl.BlockSpec((tm, dv), lambda i, j: (i, j)),
        ],
        out_shape=[
            jax.ShapeDtypeStruct((t, heads * kw), BF16),
            jax.ShapeDtypeStruct((t, heads * dv), BF16),
        ],
        scratch_shapes=[pltpu.VMEM((tm, rank), BF16), pltpu.VMEM((tm, LANES), BF16)],
        compiler_params=_params(("parallel", "arbitrary")),
    )(p, g_ckv, w_ukv, g_kn, small, gkr, pos, invf)


def _attn_kernel(q_ref, k_ref, v_ref, o_ref, *, tq, tk):
    qi = pl.program_id(2)
    q = q_ref[...]
    dv = v_ref.shape[1]
    row = qi * tq + lax.broadcasted_iota(jnp.int32, (tq, tk), 0)
    col0 = lax.broadcasted_iota(jnp.int32, (tq, tk), 1)

    def body(kb, carry, masked):
        m, l, acc = carry
        k0 = pl.multiple_of(kb * tk, tk)
        s = _dot_nt(q, k_ref[pl.ds(k0, tk), :])
        if masked:
            s = jnp.where(col0 + k0 <= row, s, NEG_INF)
        m_new = jnp.maximum(m, jnp.max(s, axis=-1, keepdims=True))
        alpha = jnp.exp(m - m_new)
        pr = jnp.exp(s - m_new)
        l = alpha * l + jnp.sum(pr, axis=-1, keepdims=True)
        acc = alpha * acc + _dot(pr.astype(BF16), v_ref[pl.ds(k0, tk), :])
        return m_new, l, acc

    init = (jnp.full((tq, 1), NEG_INF, F32), jnp.zeros((tq, 1), F32), jnp.zeros((tq, dv), F32))
    n_below = (qi * tq) // tk
    nkb = ((qi + 1) * tq + tk - 1) // tk
    carry = lax.fori_loop(0, n_below, functools.partial(body, masked=False), init)
    _, l, acc = lax.fori_loop(n_below, nkb, functools.partial(body, masked=True), carry)
    o_ref[...] = (acc / l).astype(o_ref.dtype)


def _mla_attn(q, k, v, *, batch, seq, heads, dk, dv, tq, tk):
    t = q.shape[0]
    nq = seq // tq
    return pl.pallas_call(
        functools.partial(_attn_kernel, tq=tq, tk=tk),
        grid=(batch, heads, nq),
        in_specs=[
            pl.BlockSpec((tq, dk), lambda b, h, i: (b * nq + i, h)),
            pl.BlockSpec((seq, dk), lambda b, h, i: (b, h)),
            pl.BlockSpec((seq, dv), lambda b, h, i: (b, h)),
        ],
        out_specs=pl.BlockSpec((tq, dv), lambda b, h, i: (b * nq + i, h)),
        out_shape=jax.ShapeDtypeStruct((t, heads * dv), BF16),
        compiler_params=_params(("parallel", "parallel", "arbitrary")),
    )(q, k, v)


def _gla_kernel(q_ref, k_ref, v_ref, og_ref, small_ref, w2_ref, ba_ref, gon_ref, o_ref, st_scr,
                *, seq, dk, dv, chunk):
    c = chunk
    st_scr[...] = jnp.zeros_like(st_scr)
    r_i = lax.broadcasted_iota(jnp.int32, (c, c), 0)
    c_i = lax.broadcasted_iota(jnp.int32, (c, c), 1)
    tri = jnp.where(c_i <= r_i, 1.0, 0.0).astype(BF16)
    row_id = lax.broadcasted_iota(jnp.int32, (c, 1), 0)
    w2 = w2_ref[...]
    ba = ba_ref[...]
    gon = gon_ref[...]
    q_scale = dk ** -0.5

    def chunk_step(ci, carry):
        r0 = pl.multiple_of(ci * c, c)
        qc = q_ref[pl.ds(r0, c), :].astype(F32) * q_scale
        kc = k_ref[pl.ds(r0, c), :].astype(F32)
        vc = v_ref[pl.ds(r0, c), :]
        z = _dot(small_ref[pl.ds(r0, c), :].astype(BF16), w2) + ba
        la = jax.nn.log_sigmoid(z) * (1.0 / GLA_TAU)
        hi = la.astype(BF16)
        r1 = la - hi.astype(F32)
        mid = r1.astype(BF16)
        lo = (r1 - mid.astype(F32)).astype(BF16)
        b = _dot(tri, hi) + _dot(tri, mid) + _dot(tri, lo)

        st = st_scr[...]
        inter = _dot_nt((qc * jnp.exp(b)).astype(BF16), st.astype(BF16))

        att = jnp.zeros((c, c), F32)
        for j in range(c):
            lo_r = (j // SUBLANES) * SUBLANES
            d = b[lo_r:, :] - b[j:j + 1, :]
            e = jnp.exp(jnp.where(row_id[lo_r:, :] >= j, d, NEG_INF))
            col = jnp.sum(qc[lo_r:, :] * kc[j:j + 1, :] * e, axis=-1, keepdims=True)
            if lo_r:
                col = jnp.concatenate([jnp.zeros((lo_r, 1), F32), col], axis=0)
            att = jnp.where(c_i == j, col, att)
        o = inter + _dot(att.astype(BF16), vc)

        b_last = b[c - 1:c, :]
        k_dec = (kc * jnp.exp(b_last - b)).astype(BF16)
        st_scr[...] = st * jnp.exp(b_last) + _dot_tn(vc, k_dec)

        og = og_ref[pl.ds(r0, c), :].astype(F32)
        out = _rms(o, gon) * (og * jax.nn.sigmoid(og))
        o_ref[pl.ds(r0, c), :] = out.astype(o_ref.dtype)
        return carry

    lax.fori_loop(0, seq // c, chunk_step, 0)


def _gla(p, small, w2p, b_a, g_on, *, batch, seq, heads, dk, dv, q_blk, k_blk, v_blk, og_blk):
    t = p.shape[0]
    return pl.pallas_call(
        functools.partial(_gla_kernel, seq=seq, dk=dk, dv=dv, chunk=GLA_CHUNK),
        grid=(batch, heads),
        in_specs=[
            pl.BlockSpec((seq, dk), lambda b, h: (b, q_blk + h)),
            pl.BlockSpec((seq, dk), lambda b, h: (b, k_blk + h)),
            pl.BlockSpec((seq, dv), lambda b, h: (b, v_blk + h)),
            pl.BlockSpec((seq, dv), lambda b, h: (b, og_blk + h)),
            pl.BlockSpec((seq, LANES), lambda b, h: (b, 0)),
            pl.BlockSpec((LANES, dk), lambda b, h: (0, h)),
            pl.BlockSpec((1, dk), lambda b, h: (0, h)),
            pl.BlockSpec((1, dv), lambda b, h: (0, 0)),
        ],
        out_specs=pl.BlockSpec((seq, dv), lambda b, h: (b, h)),
        out_shape=jax.ShapeDtypeStruct((t, heads * dv), BF16),
        scratch_shapes=[pltpu.VMEM((dv, dk), F32)],
        compiler_params=_params(("parallel", "parallel")),
    )(p, p, p, p, small, w2p, b_a, g_on)


def _out_proj_kernel(x_ref, ga_ref, gb_ref, oa_ref, ob_ref, bg_ref, w_ref, o_ref, m_scr):
    @pl.when(pl.program_id(1) == 0)
    def _():
        bg = bg_ref[...]
        sa = jax.nn.sigmoid(ga_ref[...].astype(F32) + bg[0:1, :])
        sb = jax.nn.sigmoid(gb_ref[...].astype(F32) + bg[1:2, :])
        m_scr[...] = (sa * oa_ref[...].astype(F32) + sb * ob_ref[...].astype(F32)).astype(BF16)

    o_ref[...] = x_ref[...] + _dot(m_scr[...], w_ref[...])


def _out_proj(x, p, o_mla, o_gla, b_gate, w_out, *, ga_blk, gb_blk, tm, tn):
    t, d = x.shape
    return pl.pallas_call(
        _out_proj_kernel,
        grid=(t // tm, d // tn),
        in_specs=[
            pl.BlockSpec((tm, tn), lambda i, j: (i, j)),
            pl.BlockSpec((tm, d), lambda i, j: (i, ga_blk)),
            pl.BlockSpec((tm, d), lambda i, j: (i, gb_blk)),
            pl.BlockSpec((tm, d), lambda i, j: (i, 0)),
            pl.BlockSpec((tm, d), lambda i, j: (i, 0)),
            pl.BlockSpec((2, d), lambda i, j: (0, 0)),
            pl.BlockSpec((d, tn), lambda i, j: (0, j)),
        ],
        out_specs=pl.BlockSpec((tm, tn), lambda i, j: (i, j)),
        out_shape=jax.ShapeDtypeStruct((t, d), F32),
        scratch_shapes=[pltpu.VMEM((tm, d), BF16)],
        compiler_params=_params(("parallel", "arbitrary")),
    )(x, p, p, o_mla, o_gla, b_gate, w_out)


def _topk_rows(s, k, payload=None):
    n = s.shape[0]
    iota = lax.broadcasted_iota(jnp.int32, s.shape, 0).astype(F32)
    kiota = lax.broadcasted_iota(jnp.int32, (k, s.shape[1]), 0)
    vals = jnp.zeros((k, s.shape[1]), F32)
    picks = jnp.zeros((k, s.shape[1]), F32)
    for r in range(k):
        m = jnp.max(s, axis=0, keepdims=True)
        idx = jnp.min(jnp.where(s == m, iota, float(n)), axis=0, keepdims=True)
        hit = iota == idx
        if payload is None:
            pick = idx
        else:
            pick = jnp.sum(jnp.where(hit, payload, 0.0), axis=0, keepdims=True)
        vals = jnp.where(kiota == r, m, vals)
        picks = jnp.where(kiota == r, pick, picks)
        s = jnp.where(hit, NEG_INF, s)
    return vals, picks


def _peer_route_kernel(x_ref, g_ref, w_ref, sk_ref, h_ref, ids_ref, gates_ref, *, heads, nkeys, half, topk):
    h = _rms(x_ref[...], g_ref[...]).astype(BF16)
    h_ref[...] = h
    qf = _dot(h, w_ref[...])
    for hd in range(heads):
        tops = []
        for part in range(2):
            o = (hd * 2 + part) * half
            qh = qf[:, o:o + half].astype(BF16)
            keys = sk_ref[(hd * 2 + part) * nkeys:(hd * 2 + part + 1) * nkeys, :]
            tops.append(_topk_rows(_dot_nt(keys, qh), topk))
        (s1, i1), (s2, i2) = tops
        assert topk == 2 * SUBLANES
        hs = SUBLANES
        pair_s = [s1[0:1, :] + s2] + [s1[a:a + 1, :] + s2[:hs, :] for a in range(1, hs)] + [s1[hs:, :] + s2[0:1, :]]
        pair_id = ([i1[0:1, :] * float(nkeys) + i2]
                   + [i1[a:a + 1, :] * float(nkeys) + i2[:hs, :] for a in range(1, hs)]
                   + [i1[hs:, :] * float(nkeys) + i2[0:1, :]])
        best_s, best_id = _topk_rows(jnp.concatenate(pair_s, axis=0), topk,
                                     payload=jnp.concatenate(pair_id, axis=0))
        e = jnp.exp(best_s - best_s[0:1, :])
        gate = e / jnp.sum(e, axis=0, keepdims=True)
        ids_ref[hd * topk:(hd + 1) * topk, :] = best_id.astype(jnp.int32)
        gates_ref[hd * topk:(hd + 1) * topk, :] = gate


def _peer_route(x, g, w_pq, sk2d, *, heads, nkeys, half, topk, tm):
    t, d = x.shape
    dq = w_pq.shape[1]
    return pl.pallas_call(
        functools.partial(_peer_route_kernel, heads=heads, nkeys=nkeys, half=half, topk=topk),
        grid=(t // tm,),
        in_specs=[
            pl.BlockSpec((tm, d), lambda i: (i, 0)),
            pl.BlockSpec((1, d), lambda i: (0, 0)),
            pl.BlockSpec((d, dq), lambda i: (0, 0)),
            pl.BlockSpec((heads * 2 * nkeys, half), lambda i: (0, 0)),
        ],
        out_specs=[
            pl.BlockSpec((tm, d), lambda i: (i, 0)),
            pl.BlockSpec((heads * topk, tm), lambda i: (0, i)),
            pl.BlockSpec((heads * topk, tm), lambda i: (0, i)),
        ],
        out_shape=[
            jax.ShapeDtypeStruct((t, d), BF16),
            jax.ShapeDtypeStruct((heads * topk, t), jnp.int32),
            jax.ShapeDtypeStruct((heads * topk, t), F32),
        ],
        compiler_params=_params(("parallel",)),
    )(x, g, w_pq, sk2d)


def _pack_expert_table(u_emb, v_emb):
    def pack(w):
        bits = lax.bitcast_convert_type(w.astype(BF16), jnp.uint16).astype(jnp.uint32)
        half = w.shape[1] // 2
        return bits[:, :half] | (bits[:, half:] << 16)

    return lax.bitcast_convert_type(jnp.concatenate([pack(u_emb), pack(v_emb)], axis=1), jnp.int32)


def _unpack_words(w):
    lo = lax.bitcast_convert_type(w << 16, F32)
    hi = lax.bitcast_convert_type(w & jnp.int32(-65536), F32)
    return lo, hi


def _expert_mix(words, h, gate, d):
    half = d // 2
    u_lo, u_hi = _unpack_words(words[:, :half])
    act = jnp.sum(u_lo * h[:, :half] + u_hi * h[:, half:], axis=-1, keepdims=True)
    w = gate * _gelu_exact(act)
    v_lo, v_hi = _unpack_words(words[:, half:])
    return jnp.concatenate([jnp.sum(v_lo * w, axis=0, keepdims=True),
                            jnp.sum(v_hi * w, axis=0, keepdims=True)], axis=1)


def _peer_mix_kernel(ids_hbm, uv_hbm, x_ref, h_ref, gates_ref, o_ref, ids_smem, buf, ids_sem, row_sems,
                     *, tb, picks, d):
    step = pl.program_id(0)
    ids_copy = pltpu.make_async_copy(ids_hbm.at[step], ids_smem, ids_sem)
    ids_copy.start()
    ids_copy.wait()

    def row_copy(tok, j, slot):
        eid = ids_smem[tok * picks + j]
        return pltpu.make_async_copy(uv_hbm.at[pl.ds(eid, 1), :], buf.at[slot, pl.ds(j, 1), :], row_sems.at[slot])

    def issue(tok, slot):
        for j in range(picks):
            row_copy(tok, j, slot).start()

    def wait_all(slot):
        pltpu.make_async_copy(uv_hbm.at[pl.ds(0, picks), :], buf.at[slot], row_sems.at[slot]).wait()

    ahead = MIX_SLOTS - 1
    for s in range(ahead):
        issue(s, s)

    def group(g, carry):
        rows = []
        gates = gates_ref[g]
        for u in range(SUBLANES):
            tok = g * SUBLANES + u
            slot = u % MIX_SLOTS

            @pl.when(tok + ahead < tb)
            def _():
                issue(tok + ahead, (u + ahead) % MIX_SLOTS)

            wait_all(slot)
            r8 = pl.multiple_of(g * SUBLANES, SUBLANES)
            h8 = h_ref[pl.ds(r8, SUBLANES), :].astype(F32)
            rows.append(_expert_mix(buf[slot], h8[u:u + 1, :], gates[:, u:u + 1], d))
        r8 = pl.multiple_of(g * SUBLANES, SUBLANES)
        o_ref[pl.ds(r8, SUBLANES), :] = x_ref[pl.ds(r8, SUBLANES), :] + jnp.concatenate(rows, axis=0)
        return carry

    lax.fori_loop(0, tb // SUBLANES, group, 0)


def _peer_mix_direct(ids_blk, uv, x, h, gates3, *, tb, n_tokens):
    t, d = x.shape
    picks = gates3.shape[1]
    return pl.pallas_call(
        functools.partial(_peer_mix_kernel, tb=tb, picks=picks, d=d),
        grid=(n_tokens // tb,),
        input_output_aliases={2: 0},
        in_specs=[
            pl.BlockSpec(memory_space=pl.ANY),
            pl.BlockSpec(memory_space=pl.ANY),
            pl.BlockSpec((tb, d), lambda i: (i, 0)),
            pl.BlockSpec((tb, d), lambda i: (i, 0)),
            pl.BlockSpec((tb // SUBLANES, picks, SUBLANES), lambda i: (i, 0, 0)),
        ],
        out_specs=pl.BlockSpec((tb, d), lambda i: (i, 0)),
        out_shape=jax.ShapeDtypeStruct((t, d), F32),
        scratch_shapes=[
            pltpu.SMEM((tb * picks,), jnp.int32),
            pltpu.VMEM((MIX_SLOTS, picks, d), jnp.int32),
            pltpu.SemaphoreType.DMA(()),
            pltpu.SemaphoreType.DMA((MIX_SLOTS,)),
        ],
        compiler_params=_params(("arbitrary",)),
    )(ids_blk, uv, x, h, gates3)


def _peer_mix_staged_kernel(rows_ref, x_ref, h_ref, gates_ref, o_ref, *, tb, picks, d):
    for g in range(tb // SUBLANES):
        gates = gates_ref[g]
        h8 = h_ref[g * SUBLANES:(g + 1) * SUBLANES, :].astype(F32)
        rows = []
        for u in range(SUBLANES):
            r0 = (g * SUBLANES + u) * picks
            rows.append(_expert_mix(rows_ref[r0:r0 + picks, :], h8[u:u + 1, :], gates[:, u:u + 1], d))
        sl = slice(g * SUBLANES, (g + 1) * SUBLANES)
        o_ref[sl, :] = x_ref[sl, :] + jnp.concatenate(rows, axis=0)


def _peer_mix_staged(staged, x, h, gates3, *, tb, first_token):
    t, d = x.shape
    picks = gates3.shape[1]
    n_tokens = staged.shape[0] // picks
    assert first_token % tb == 0 and first_token + n_tokens == t
    b0 = first_token // tb
    return pl.pallas_call(
        functools.partial(_peer_mix_staged_kernel, tb=tb, picks=picks, d=d),
        grid=(n_tokens // tb,),
        input_output_aliases={1: 0},
        in_specs=[
            pl.BlockSpec((tb * picks, d), lambda i: (i, 0)),
            pl.BlockSpec((tb, d), lambda i: (i + b0, 0)),
            pl.BlockSpec((tb, d), lambda i: (i + b0, 0)),
            pl.BlockSpec((tb // SUBLANES, picks, SUBLANES), lambda i: (i + b0, 0, 0)),
        ],
        out_specs=pl.BlockSpec((tb, d), lambda i: (i + b0, 0)),
        out_shape=jax.ShapeDtypeStruct((t, d), F32),
        compiler_params=_params(("parallel",)),
    )(staged, x, h, gates3)


def _sc_gather_rows(table, idx, *, first_row, chunk):
    n_rows = idx.shape[0] - first_row
    d = table.shape[1]
    workers = SC_CORES * SC_SUBCORES
    assert n_rows % (workers * 2 * chunk) == 0 and chunk % SUBLANES == 0 and chunk <= LANES
    assert first_row % SUBLANES == 0
    rows_per_worker = n_rows // workers
    n_pairs = rows_per_worker // (2 * chunk)
    mesh = plsc.VectorSubcoreMesh(core_axis_name="c", subcore_axis_name="s")

    @functools.partial(
        pl.kernel, mesh=mesh,
        out_type=jax.ShapeDtypeStruct((n_rows, d), table.dtype),
        scratch_types=[
            pltpu.VMEM((chunk,), jnp.int32), pltpu.VMEM((chunk,), jnp.int32),
            pltpu.VMEM((chunk, d), table.dtype), pltpu.VMEM((chunk, d), table.dtype),
            pltpu.SemaphoreType.DMA, pltpu.SemaphoreType.DMA,
            pltpu.SemaphoreType.DMA, pltpu.SemaphoreType.DMA,
        ],
    )
    def gather_kernel(table_hbm, idx_hbm, out_hbm, idx0, idx1, rows0, rows1, gsem0, gsem1, wsem0, wsem1):
        idx_v, rows_v, gsem, wsem = (idx0, idx1), (rows0, rows1), (gsem0, gsem1), (wsem0, wsem1)
        worker = lax.axis_index("s") * SC_CORES + lax.axis_index("c")
        base = worker * rows_per_worker

        def out_rows(c):
            return pl.ds(pl.multiple_of(base + c * chunk, SUBLANES), chunk)

        def load_idx(slot, c):
            src = pl.ds(pl.multiple_of(first_row + base + c * chunk, SUBLANES), chunk)
            pltpu.sync_copy(idx_hbm.at[src], idx_v[slot])

        def gather(slot):
            return pltpu.make_async_copy(table_hbm.at[idx_v[slot]], rows_v[slot], gsem[slot])

        def writeout(slot, c):
            return pltpu.make_async_copy(rows_v[slot], out_hbm.at[out_rows(c)], wsem[slot])

        load_idx(0, 0)
        gather(0).start()

        @pl.loop(0, n_pairs)
        def _(p):
            c0 = 2 * p

            @pl.when(p > 0)
            def _():
                writeout(1, c0 - 1).wait()

            load_idx(1, c0 + 1)
            gather(1).start()
            gather(0).wait()
            writeout(0, c0).start()

            @pl.when(p + 1 < n_pairs)
            def _():
                load_idx(0, c0 + 2)
                writeout(0, c0).wait()
                gather(0).start()

            gather(1).wait()
            writeout(1, c0 + 1).start()

        writeout(0, 2 * n_pairs - 2).wait()
        writeout(1, 2 * n_pairs - 1).wait()

    return gather_kernel(table, idx)


def _pad_cols(w, width):
    return jnp.pad(w, ((0, 0), (0, width - w.shape[1])))


def _prepare_layer(g_norm_mix, w_in, b_gate, g_cq, w_uq, g_ckv, w_ukv, g_qn, g_qr, g_kn, g_kr,
                   w_a2, b_a, g_gla_out, w_out, g_norm_ffn, w_pq, sub_keys, u_emb, v_emb):
    d = w_in.shape[0]
    q_rank, kv_rank = g_cq.shape[0], g_ckv.shape[0]
    nope, rope = g_qn.shape[0], g_qr.shape[0]
    mla_heads = w_uq.shape[1] // (nope + rope)
    mla_v = w_ukv.shape[1] // mla_heads - nope
    gate_rank, gla_dk_all = w_a2.shape
    gla_dv = g_gla_out.shape[0]
    gla_heads = d // gla_dv
    gla_dk = gla_dk_all // gla_heads
    peer_heads, _, nkeys, half = sub_keys.shape
    assert nope == LANES and mla_v == LANES and rope <= LANES and rope % 2 == 0
    assert mla_heads * mla_v == d and gla_heads * gla_dv == d
    assert rope + gate_rank <= LANES and nkeys == LANES and half == LANES

    widths = (q_rank, kv_rank, rope, gla_dk_all, gla_dk_all, d, gate_rank, d, d, d)
    offs = [0]
    for wd in widths:
        offs.append(offs[-1] + wd)
    assert offs[-1] == w_in.shape[1]
    seg = lambda i: w_in[:, offs[i]:offs[i + 1]]
    w_main = jnp.concatenate([seg(5), seg(7), seg(8), seg(9), seg(3), seg(4), seg(0), seg(1)], axis=1).astype(BF16)
    w_small = _pad_cols(jnp.concatenate([seg(2), seg(6)], axis=1), LANES).astype(BF16)
    cq_off = 4 * d + 2 * gla_dk_all
    assert cq_off % q_rank == 0 and (cq_off + q_rank) % kv_rank == 0
    inv_freq = ROPE_THETA ** (-jnp.arange(0, rope, 2, dtype=F32) / rope)
    scale = (nope + rope) ** -0.5
    return dict(
        dims=dict(q_rank=q_rank, kv_rank=kv_rank, nope=nope, rope=rope, mla_heads=mla_heads, mla_v=mla_v,
                  gla_heads=gla_heads, gla_dk=gla_dk, gla_dv=gla_dv, peer_heads=peer_heads, nkeys=nkeys, half=half,
                  v_blk=0, og_blk=d // gla_dv, ga_blk=2, gb_blk=3, q_blk=4 * d // gla_dk,
                  k_blk=4 * d // gla_dk + gla_heads, cq_blk=cq_off // q_rank, ckv_blk=(cq_off + q_rank) // kv_rank),
        g_norm_mix=g_norm_mix[None, :], w_main=w_main, w_small=w_small,
        invf=_pad_cols(jnp.concatenate([inv_freq, inv_freq])[None, :], LANES),
        gq=_pad_cols(jnp.concatenate([g_qn, g_qr])[None, :] * scale, nope + LANES),
        gkr=_pad_cols(g_kr[None, :], LANES), g_cq=g_cq[None, :], g_ckv=g_ckv[None, :], g_kn=g_kn[None, :],
        w_uq=jnp.pad(w_uq.reshape(q_rank, mla_heads, nope + rope),
                     ((0, 0), (0, 0), (0, LANES - rope))).reshape(q_rank, -1).astype(BF16),
        w_ukv=w_ukv.astype(BF16),
        w2p=jnp.zeros((LANES, gla_dk_all), F32).at[rope:rope + gate_rank].set(w_a2).astype(BF16),
        b_a=b_a[None, :], g_on=g_gla_out[None, :], b_gate=b_gate, w_out=w_out.astype(BF16),
        g_norm_ffn=g_norm_ffn[None, :], w_pq=w_pq.astype(BF16),
        sk2d=sub_keys.reshape(peer_heads * 2 * nkeys, half).astype(BF16),
        uv=_pack_expert_table(u_emb, v_emb),
    )


def _mixers_and_route(x2, pos, w, *, batch, seq, tiles):
    t = x2.shape[0]
    dm = w["dims"]
    p, small = _in_proj(x2, w["g_norm_mix"], w["w_main"], w["w_small"], tm=tiles["in_tm"], tn=tiles["in_tn"])
    q = _mla_q(p, dm["cq_blk"], w["g_cq"], w["w_uq"], w["gq"], pos, w["invf"], heads=dm["mla_heads"],
               rank=dm["q_rank"], nope=dm["nope"], rope=dm["rope"], tm=tiles["mla_tm"])
    k, v = _mla_kv(p, dm["ckv_blk"], w["g_ckv"], w["w_ukv"], w["g_kn"], small, w["gkr"], pos, w["invf"],
                   heads=dm["mla_heads"], rank=dm["kv_rank"], nope=dm["nope"], rope=dm["rope"], dv=dm["mla_v"],
                   tm=tiles["mla_tm"])
    o_mla = _mla_attn(q, k, v, batch=batch, seq=seq, heads=dm["mla_heads"], dk=dm["nope"] + LANES, dv=dm["mla_v"],
                      tq=tiles["attn_tq"], tk=tiles["attn_tk"])
    o_gla = _gla(p, small, w["w2p"], w["b_a"], w["g_on"], batch=batch, seq=seq, heads=dm["gla_heads"],
                 dk=dm["gla_dk"], dv=dm["gla_dv"], q_blk=dm["q_blk"], k_blk=dm["k_blk"], v_blk=dm["v_blk"],
                 og_blk=dm["og_blk"])
    x2 = _out_proj(x2, p, o_mla, o_gla, w["b_gate"], w["w_out"], ga_blk=dm["ga_blk"], gb_blk=dm["gb_blk"],
                   tm=tiles["out_tm"], tn=tiles["out_tn"])
    h2, ids_t, gates_t = _peer_route(x2, w["g_norm_ffn"], w["w_pq"], w["sk2d"], heads=dm["peer_heads"],
                                     nkeys=dm["nkeys"], half=dm["half"], topk=PEER_TOPK, tm=tiles["route_tm"])
    picks = dm["peer_heads"] * PEER_TOPK
    gates3 = gates_t.reshape(picks, t // SUBLANES, SUBLANES).transpose(1, 0, 2)
    return x2, h2, ids_t.T, gates3


def _layer(x2, pos, w, *, batch, seq, tiles):
    t, d = x2.shape
    direct_tokens = tiles["mix_direct_tokens"]
    groups = len(direct_tokens)
    tg, bg = t // groups, batch // groups
    routed = [_mixers_and_route(x2[g * tg:(g + 1) * tg], pos[g * tg:(g + 1) * tg], w,
                                batch=bg, seq=seq, tiles=tiles) for g in range(groups)]
    tb = tiles["mix_tb"]
    mixed = []
    for g, (xg, hg, ids_tok, gates3) in enumerate(routed):
        if direct_tokens[g]:
            xg = _peer_mix_direct(ids_tok.reshape(tg // tb, -1), w["uv"], xg, hg, gates3,
                                  tb=tb, n_tokens=direct_tokens[g])
        mixed.append(xg)
    out = []
    for g, (_, hg, ids_tok, gates3) in enumerate(routed):
        xg = mixed[g]
        if direct_tokens[g] < tg:
            staged = _sc_gather_rows(w["uv"], ids_tok.reshape(-1), first_row=direct_tokens[g] * ids_tok.shape[1],
                                     chunk=tiles["sc_chunk"])
            xg = _peer_mix_staged(staged, xg, hg, gates3, tb=tiles["mix_staged_tb"], first_token=direct_tokens[g])
        out.append(xg)
    return jnp.concatenate(out, axis=0)


_TILES = dict(in_tm=1024, in_tn=512, mla_tm=1024, attn_tq=512, attn_tk=512,
              out_tm=512, out_tn=512, route_tm=256, mix_tb=64,
              mix_direct_tokens=(448,) * 8, mix_staged_tb=16, sc_chunk=16)


def kernel(x, positions, g_norm_mix, w_in, b_gate, g_cq, w_uq, g_ckv, w_ukv, g_qn, g_qr, g_kn, g_kr,
           w_a2, b_a, g_gla_out, w_out, g_norm_ffn, w_pq, sub_keys, u_emb, v_emb, tiles=None):
    tiles = _TILES if tiles is None else tiles
    batch, seq, d = x.shape
    x2 = x.reshape(batch * seq, d)
    pos = positions.reshape(batch * seq, 1)
    for l in range(g_norm_mix.shape[0]):
        w = _prepare_layer(g_norm_mix[l], w_in[l], b_gate[l], g_cq[l], w_uq[l], g_ckv[l], w_ukv[l], g_qn[l],
                           g_qr[l], g_kn[l], g_kr[l], w_a2[l], b_a[l], g_gla_out[l], w_out[l], g_norm_ffn[l],
                           w_pq[l], sub_keys[l], u_emb[l], v_emb[l])
        x2 = _layer(x2, pos, w, batch=batch, seq=seq, tiles=tiles)
    return x2.reshape(batch, seq, d)
```

```python
import functools

import jax
import jax.numpy as jnp
from jax import lax
from jax.experimental import pallas as pl
from jax.experimental.pallas import tpu as pltpu
from jax.experimental.pallas import tpu_sc as plsc

EPS = 1e-6
ROPE_THETA = 10000.0
GLA_TAU = 16.0
GLA_CHUNK = 64
PEER_TOPK = 16

LANES = 128
SUBLANES = 8
VMEM_LIMIT_BYTES = 56 * 1024 * 1024
MIX_SLOTS = 4
SC_CORES = 2
SC_SUBCORES = 16

F32 = jnp.float32
BF16 = jnp.bfloat16
NEG_INF = float("-inf")


def _params(semantics):
    return pltpu.CompilerParams(dimension_semantics=semantics, vmem_limit_bytes=VMEM_LIMIT_BYTES)


def _rms(x, gain, n=None):
    ss = jnp.sum(x * x, axis=-1, keepdims=True)
    n = x.shape[-1] if n is None else n
    return x * lax.rsqrt(ss * (1.0 / n) + EPS) * gain


def _gelu_exact(x):
    return 0.5 * x * (1.0 + lax.erf(x * (0.5 ** 0.5)))


def _dot(a, b):
    return jnp.dot(a, b, preferred_element_type=F32)


def _dot_nt(a, b):
    return lax.dot_general(a, b, (((1,), (1,)), ((), ())), preferred_element_type=F32)


def _dot_tn(a, b):
    return lax.dot_general(a, b, (((0,), (0,)), ((), ())), preferred_element_type=F32)


def _in_proj_kernel(x_ref, g_ref, w_ref, ws_ref, p_ref, ps_ref, h_scr):
    @pl.when(pl.program_id(1) == 0)
    def _():
        h = _rms(x_ref[...], g_ref[...]).astype(BF16)
        h_scr[...] = h
        ps_ref[...] = _dot(h, ws_ref[...])

    p_ref[...] = _dot(h_scr[...], w_ref[...]).astype(p_ref.dtype)


def _in_proj(x, g, w_main, w_small, *, tm, tn):
    t, d = x.shape
    n = w_main.shape[1]
    x_mode = pl.Buffered(1) if t == tm else None
    return pl.pallas_call(
        _in_proj_kernel,
        grid=(t // tm, n // tn),
        in_specs=[
            pl.BlockSpec((tm, d), lambda i, j: (i, 0), pipeline_mode=x_mode),
            pl.BlockSpec((1, d), lambda i, j: (0, 0)),
            pl.BlockSpec((d, tn), lambda i, j: (0, j)),
            pl.BlockSpec((d, LANES), lambda i, j: (0, 0)),
        ],
        out_specs=[
            pl.BlockSpec((tm, tn), lambda i, j: (i, j)),
            pl.BlockSpec((tm, LANES), lambda i, j: (i, 0)),
        ],
        out_shape=[
            jax.ShapeDtypeStruct((t, n), BF16),
            jax.ShapeDtypeStruct((t, LANES), F32),
        ],
        scratch_shapes=[pltpu.VMEM((tm, d), BF16)],
        compiler_params=_params(("parallel", "arbitrary")),
    )(x, g, w_main, w_small)


def _rope_tables(pos_ref, invf_ref, rope):
    ang = pos_ref[...].astype(F32) * invf_ref[...]
    cos, sin = jnp.cos(ang), jnp.sin(ang)
    lane = lax.broadcasted_iota(jnp.int32, ang.shape, 1)
    half = rope // 2
    c = jnp.where(lane < rope, cos, 0.0)
    s_lo = jnp.where(lane < half, -sin, 0.0)
    s_hi = jnp.where(lane < half, 0.0, jnp.where(lane < rope, sin, 0.0))
    return c, s_lo, s_hi


def _apply_rope(pe, c, s_lo, s_hi, rope):
    half = rope // 2
    from_hi = pltpu.roll(pe, LANES - half, 1)
    from_lo = pltpu.roll(pe, half, 1)
    return pe * c + from_hi * s_lo + from_lo * s_hi


def _mla_q_kernel(cq_ref, gcq_ref, w_ref, gq_ref, pos_ref, invf_ref, q_ref,
                  h_scr, c_scr, slo_scr, shi_scr, *, nope, rope):
    @pl.when(pl.program_id(1) == 0)
    def _():
        h_scr[...] = _rms(cq_ref[...].astype(F32), gcq_ref[...]).astype(BF16)
        c, s_lo, s_hi = _rope_tables(pos_ref, invf_ref, rope)
        c_scr[...] = c
        slo_scr[...] = s_lo
        shi_scr[...] = s_hi

    y = _dot(h_scr[...], w_ref[...])
    g = gq_ref[...]
    qn = _rms(y[:, :nope], g[:, :nope])
    pe = _rms(y[:, nope:], g[:, nope:], n=rope)
    pe = _apply_rope(pe, c_scr[...], slo_scr[...], shi_scr[...], rope)
    q_ref[:, :nope] = qn.astype(q_ref.dtype)
    q_ref[:, nope:] = pe.astype(q_ref.dtype)


def _mla_q(p, cq_blk, g_cq, w_uq_p, gq, pos, invf, *, heads, rank, nope, rope, tm):
    t = p.shape[0]
    hw = nope + LANES
    return pl.pallas_call(
        functools.partial(_mla_q_kernel, nope=nope, rope=rope),
        grid=(t // tm, heads),
        in_specs=[
            pl.BlockSpec((tm, rank), lambda i, j: (i, cq_blk)),
            pl.BlockSpec((1, rank), lambda i, j: (0, 0)),
            pl.BlockSpec((rank, hw), lambda i, j: (0, j)),
            pl.BlockSpec((1, hw), lambda i, j: (0, 0)),
            pl.BlockSpec((tm, 1), lambda i, j: (i, 0)),
            pl.BlockSpec((1, LANES), lambda i, j: (0, 0)),
        ],
        out_specs=pl.BlockSpec((tm, hw), lambda i, j: (i, j)),
        out_shape=jax.ShapeDtypeStruct((t, heads * hw), BF16),
        scratch_shapes=[
            pltpu.VMEM((tm, rank), BF16),
            pltpu.VMEM((tm, LANES), F32),
            pltpu.VMEM((tm, LANES), F32),
            pltpu.VMEM((tm, LANES), F32),
        ],
        compiler_params=_params(("parallel", "arbitrary")),
    )(p, g_cq, w_uq_p, gq, pos, invf)


def _mla_kv_kernel(ckv_ref, gckv_ref, w_ref, gkn_ref, small_ref, gkr_ref, pos_ref, invf_ref,
                   k_ref, v_ref, h_scr, kpe_scr, *, nope, rope):
    @pl.when(pl.program_id(1) == 0)
    def _():
        h_scr[...] = _rms(ckv_ref[...].astype(F32), gckv_ref[...]).astype(BF16)
        c, s_lo, s_hi = _rope_tables(pos_ref, invf_ref, rope)
        sm = small_ref[...]
        lane = lax.broadcasted_iota(jnp.int32, sm.shape, 1)
        pe = _rms(jnp.where(lane < rope, sm, 0.0), gkr_ref[...], n=rope)
        kpe_scr[...] = _apply_rope(pe, c, s_lo, s_hi, rope).astype(BF16)

    y = _dot(h_scr[...], w_ref[...])
    k_ref[:, :nope] = _rms(y[:, :nope], gkn_ref[...]).astype(k_ref.dtype)
    k_ref[:, nope:] = kpe_scr[...]
    v_ref[...] = y[:, nope:].astype(v_ref.dtype)


def _mla_kv(p, ckv_blk, g_ckv, w_ukv, g_kn, small, gkr, pos, invf, *, heads, rank, nope, rope, dv, tm):
    t = p.shape[0]
    kw = nope + LANES
    return pl.pallas_call(
        functools.partial(_mla_kv_kernel, nope=nope, rope=rope),
        grid=(t // tm, heads),
        in_specs=[
            pl.BlockSpec((tm, rank), lambda i, j: (i, ckv_blk)),
            pl.BlockSpec((1, rank), lambda i, j: (0, 0)),
            pl.BlockSpec((rank, nope + dv), lambda i, j: (0, j)),
            pl.BlockSpec((1, nope), lambda i, j: (0, 0)),
            pl.BlockSpec((tm, LANES), lambda i, j: (i, 0)),
            pl.BlockSpec((1, LANES), lambda i, j: (0, 0)),
            pl.BlockSpec((tm, 1), lambda i, j: (i, 0)),
            pl.BlockSpec((1, LANES), lambda i, j: (0, 0)),
        ],
        out_specs=[
            pl.BlockSpec((tm, kw), lambda i, j: (i, j)),
            pl.BlockSpec((tm, dv), lambda i, j: (i, j)),
        ],
        out_shape=[
            jax.ShapeDtypeStruct((t, heads * kw), BF16),
            jax.ShapeDtypeStruct((t, heads * dv), BF16),
        ],
        scratch_shapes=[pltpu.VMEM((tm, rank), BF16), pltpu.VMEM((tm, LANES), BF16)],
        compiler_params=_params(("parallel", "arbitrary")),
    )(p, g_ckv, w_ukv, g_kn, small, gkr, pos, invf)


def _attn_kernel(q_ref, k_ref, v_ref, o_ref, *, tq, tk):
    qi = pl.program_id(2)
    q = q_ref[...]
    dv = v_ref.shape[1]
    row = qi * tq + lax.broadcasted_iota(jnp.int32, (tq, tk), 0)
    col0 = lax.broadcasted_iota(jnp.int32, (tq, tk), 1)

    def body(kb, carry, masked):
        m, l, acc = carry
        k0 = pl.multiple_of(kb * tk, tk)
        s = _dot_nt(q, k_ref[pl.ds(k0, tk), :])
        if masked:
            s = jnp.where(col0 + k0 <= row, s, NEG_INF)
        m_new = jnp.maximum(m, jnp.max(s, axis=-1, keepdims=True))
        alpha = jnp.exp(m - m_new)
        pr = jnp.exp(s - m_new)
        l = alpha * l + jnp.sum(pr, axis=-1, keepdims=True)
        acc = alpha * acc + _dot(pr.astype(BF16), v_ref[pl.ds(k0, tk), :])
        return m_new, l, acc

    init = (jnp.full((tq, 1), NEG_INF, F32), jnp.zeros((tq, 1), F32), jnp.zeros((tq, dv), F32))
    n_below = (qi * tq) // tk
    nkb = ((qi + 1) * tq + tk - 1) // tk
    carry = lax.fori_loop(0, n_below, functools.partial(body, masked=False), init)
    _, l, acc = lax.fori_loop(n_below, nkb, functools.partial(body, masked=True), carry)
    o_ref[...] = (acc / l).astype(o_ref.dtype)


def _mla_attn(q, k, v, *, batch, seq, heads, dk, dv, tq, tk):
    t = q.shape[0]
    nq = seq // tq
    return pl.pallas_call(
        functools.partial(_attn_kernel, tq=tq, tk=tk),
        grid=(batch, heads, nq),
        in_specs=[
            pl.BlockSpec((tq, dk), lambda b, h, i: (b * nq + i, h)),
            pl.BlockSpec((seq, dk), lambda b, h, i: (b, h)),
            pl.BlockSpec((seq, dv), lambda b, h, i: (b, h)),
        ],
        out_specs=pl.BlockSpec((tq, dv), lambda b, h, i: (b * nq + i, h)),
        out_shape=jax.ShapeDtypeStruct((t, heads * dv), BF16),
        compiler_params=_params(("parallel", "parallel", "arbitrary")),
    )(q, k, v)


def _gla_kernel(q_ref, k_ref, v_ref, og_ref, small_ref, w2_ref, ba_ref, gon_ref, o_ref, st_scr,
                *, seq, dk, dv, chunk):
    c = chunk
    st_scr[...] = jnp.zeros_like(st_scr)
    r_i = lax.broadcasted_iota(jnp.int32, (c, c), 0)
    c_i = lax.broadcasted_iota(jnp.int32, (c, c), 1)
    tri = jnp.where(c_i <= r_i, 1.0, 0.0).astype(BF16)
    row_id = lax.broadcasted_iota(jnp.int32, (c, 1), 0)
    w2 = w2_ref[...]
    ba = ba_ref[...]
    gon = gon_ref[...]
    q_scale = dk ** -0.5

    def chunk_step(ci, carry):
        r0 = pl.multiple_of(ci * c, c)
        qc = q_ref[pl.ds(r0, c), :].astype(F32) * q_scale
        kc = k_ref[pl.ds(r0, c), :].astype(F32)
        vc = v_ref[pl.ds(r0, c), :]
        z = _dot(small_ref[pl.ds(r0, c), :].astype(BF16), w2) + ba
        la = jax.nn.log_sigmoid(z) * (1.0 / GLA_TAU)
        hi = la.astype(BF16)
        r1 = la - hi.astype(F32)
        mid = r1.astype(BF16)
        lo = (r1 - mid.astype(F32)).astype(BF16)
        b = _dot(tri, hi) + _dot(tri, mid) + _dot(tri, lo)

        st = st_scr[...]
        inter = _dot_nt((qc * jnp.exp(b)).astype(BF16), st.astype(BF16))

        att = jnp.zeros((c, c), F32)
        for j in range(c):
            lo_r = (j // SUBLANES) * SUBLANES
            d = b[lo_r:, :] - b[j:j + 1, :]
            e = jnp.exp(jnp.where(row_id[lo_r:, :] >= j, d, NEG_INF))
            col = jnp.sum(qc[lo_r:, :] * kc[j:j + 1, :] * e, axis=-1, keepdims=True)
            if lo_r:
                col = jnp.concatenate([jnp.zeros((lo_r, 1), F32), col], axis=0)
            att = jnp.where(c_i == j, col, att)
        o = inter + _dot(att.astype(BF16), vc)

        b_last = b[c - 1:c, :]
        k_dec = (kc * jnp.exp(b_last - b)).astype(BF16)
        st_scr[...] = st * jnp.exp(b_last) + _dot_tn(vc, k_dec)

        og = og_ref[pl.ds(r0, c), :].astype(F32)
        out = _rms(o, gon) * (og * jax.nn.sigmoid(og))
        o_ref[pl.ds(r0, c), :] = out.astype(o_ref.dtype)
        return carry

    lax.fori_loop(0, seq // c, chunk_step, 0)


def _gla(p, small, w2p, b_a, g_on, *, batch, seq, heads, dk, dv, q_blk, k_blk, v_blk, og_blk):
    t = p.shape[0]
    return pl.pallas_call(
        functools.partial(_gla_kernel, seq=seq, dk=dk, dv=dv, chunk=GLA_CHUNK),
        grid=(batch, heads),
        in_specs=[
            pl.BlockSpec((seq, dk), lambda b, h: (b, q_blk + h)),
            pl.BlockSpec((seq, dk), lambda b, h: (b, k_blk + h)),
            pl.BlockSpec((seq, dv), lambda b, h: (b, v_blk + h)),
            pl.BlockSpec((seq, dv), lambda b, h: (b, og_blk + h)),
            pl.BlockSpec((seq, LANES), lambda b, h: (b, 0)),
            pl.BlockSpec((LANES, dk), lambda b, h: (0, h)),
            pl.BlockSpec((1, dk), lambda b, h: (0, h)),
            pl.BlockSpec((1, dv), lambda b, h: (0, 0)),
        ],
        out_specs=pl.BlockSpec((seq, dv), lambda b, h: (b, h)),
        out_shape=jax.ShapeDtypeStruct((t, heads * dv), BF16),
        scratch_shapes=[pltpu.VMEM((dv, dk), F32)],
        compiler_params=_params(("parallel", "parallel")),
    )(p, p, p, p, small, w2p, b_a, g_on)


def _out_proj_kernel(x_ref, ga_ref, gb_ref, oa_ref, ob_ref, bg_ref, w_ref, o_ref, m_scr):
    @pl.when(pl.program_id(1) == 0)
    def _():
        bg = bg_ref[...]
        sa = jax.nn.sigmoid(ga_ref[...].astype(F32) + bg[0:1, :])
        sb = jax.nn.sigmoid(gb_ref[...].astype(F32) + bg[1:2, :])
        m_scr[...] = (sa * oa_ref[...].astype(F32) + sb * ob_ref[...].astype(F32)).astype(BF16)

    o_ref[...] = x_ref[...] + _dot(m_scr[...], w_ref[...])


def _out_proj(x, p, o_mla, o_gla, b_gate, w_out, *, ga_blk, gb_blk, tm, tn):
    t, d = x.shape
    return pl.pallas_call(
        _out_proj_kernel,
        grid=(t // tm, d // tn),
        in_specs=[
            pl.BlockSpec((tm, tn), lambda i, j: (i, j)),
            pl.BlockSpec((tm, d), lambda i, j: (i, ga_blk)),
            pl.BlockSpec((tm, d), lambda i, j: (i, gb_blk)),
            pl.BlockSpec((tm, d), lambda i, j: (i, 0)),
            pl.BlockSpec((tm, d), lambda i, j: (i, 0)),
            pl.BlockSpec((2, d), lambda i, j: (0, 0)),
            pl.BlockSpec((d, tn), lambda i, j: (0, j)),
        ],
        out_specs=pl.BlockSpec((tm, tn), lambda i, j: (i, j)),
        out_shape=jax.ShapeDtypeStruct((t, d), F32),
        scratch_shapes=[pltpu.VMEM((tm, d), BF16)],
        compiler_params=_params(("parallel", "arbitrary")),
    )(x, p, p, o_mla, o_gla, b_gate, w_out)


def _topk_rows(s, k, payload=None):
    n = s.shape[0]
    iota = lax.broadcasted_iota(jnp.int32, s.shape, 0).astype(F32)
    kiota = lax.broadcasted_iota(jnp.int32, (k, s.shape[1]), 0)
    vals = jnp.zeros((k, s.shape[1]), F32)
    picks = jnp.zeros((k, s.shape[1]), F32)
    for r in range(k):
        m = jnp.max(s, axis=0, keepdims=True)
        idx = jnp.min(jnp.where(s == m, iota, float(n)), axis=0, keepdims=True)
        hit = iota == idx
        if payload is None:
            pick = idx
        else:
            pick = jnp.sum(jnp.where(hit, payload, 0.0), axis=0, keepdims=True)
        vals = jnp.where(kiota == r, m, vals)
        picks = jnp.where(kiota == r, pick, picks)
        s = jnp.where(hit, NEG_INF, s)
    return vals, picks


def _peer_route_kernel(x_ref, g_ref, w_ref, sk_ref, h_ref, ids_ref, gates_ref, *, heads, nkeys, half, topk):
    h = _rms(x_ref[...], g_ref[...]).astype(BF16)
    h_ref[...] = h
    qf = _dot(h, w_ref[...])
    for hd in range(heads):
        tops = []
        for part in range(2):
            o = (hd * 2 + part) * half
            qh = qf[:, o:o + half].astype(BF16)
            keys = sk_ref[(hd * 2 + part) * nkeys:(hd * 2 + part + 1) * nkeys, :]
            tops.append(_topk_rows(_dot_nt(keys, qh), topk))
        (s1, i1), (s2, i2) = tops
        assert topk == 2 * SUBLANES
        hs = SUBLANES
        pair_s = [s1[0:1, :] + s2] + [s1[a:a + 1, :] + s2[:hs, :] for a in range(1, hs)] + [s1[hs:, :] + s2[0:1, :]]
        pair_id = ([i1[0:1, :] * float(nkeys) + i2]
                   + [i1[a:a + 1, :] * float(nkeys) + i2[:hs, :] for a in range(1, hs)]
                   + [i1[hs:, :] * float(nkeys) + i2[0:1, :]])
        best_s, best_id = _topk_rows(jnp.concatenate(pair_s, axis=0), topk,
                                     payload=jnp.concatenate(pair_id, axis=0))
        e = jnp.exp(best_s - best_s[0:1, :])
        gate = e / jnp.sum(e, axis=0, keepdims=True)
        ids_ref[hd * topk:(hd + 1) * topk, :] = best_id.astype(jnp.int32)
        gates_ref[hd * topk:(hd + 1) * topk, :] = gate


def _peer_route(x, g, w_pq, sk2d, *, heads, nkeys, half, topk, tm):
    t, d = x.shape
    dq = w_pq.shape[1]
    return pl.pallas_call(
        functools.partial(_peer_route_kernel, heads=heads, nkeys=nkeys, half=half, topk=topk),
        grid=(t // tm,),
        in_specs=[
            pl.BlockSpec((tm, d), lambda i: (i, 0)),
            pl.BlockSpec((1, d), lambda i: (0, 0)),
            pl.BlockSpec((d, dq), lambda i: (0, 0)),
            pl.BlockSpec((heads * 2 * nkeys, half), lambda i: (0, 0)),
        ],
        out_specs=[
            pl.BlockSpec((tm, d), lambda i: (i, 0)),
            pl.BlockSpec((heads * topk, tm), lambda i: (0, i)),
            pl.BlockSpec((heads * topk, tm), lambda i: (0, i)),
        ],
        out_shape=[
            jax.ShapeDtypeStruct((t, d), BF16),
            jax.ShapeDtypeStruct((heads * topk, t), jnp.int32),
            jax.ShapeDtypeStruct((heads * topk, t), F32),
        ],
        compiler_params=_params(("parallel",)),
    )(x, g, w_pq, sk2d)


def _pack_expert_table(u_emb, v_emb):
    def pack(w):
        bits = lax.bitcast_convert_type(w.astype(BF16), jnp.uint16).astype(jnp.uint32)
        half = w.shape[1] // 2
        return bits[:, :half] | (bits[:, half:] << 16)

    return lax.bitcast_convert_type(jnp.concatenate([pack(u_emb), pack(v_emb)], axis=1), jnp.int32)


def _unpack_words(w):
    lo = lax.bitcast_convert_type(w << 16, F32)
    hi = lax.bitcast_convert_type(w & jnp.int32(-65536), F32)
    return lo, hi


def _expert_mix(words, h, gate, d):
    half = d // 2
    u_lo, u_hi = _unpack_words(words[:, :half])
    act = jnp.sum(u_lo * h[:, :half] + u_hi * h[:, half:], axis=-1, keepdims=True)
    w = gate * _gelu_exact(act)
    v_lo, v_hi = _unpack_words(words[:, half:])
    return jnp.concatenate([jnp.sum(v_lo * w, axis=0, keepdims=True),
                            jnp.sum(v_hi * w, axis=0, keepdims=True)], axis=1)


def _peer_mix_kernel(ids_hbm, uv_hbm, x_ref, h_ref, gates_ref, o_ref, ids_smem, buf, ids_sem, row_sems,
                     *, tb, picks, d):
    step = pl.program_id(0)
    ids_copy = pltpu.make_async_copy(ids_hbm.at[step], ids_smem, ids_sem)
    ids_copy.start()
    ids_copy.wait()

    def row_copy(tok, j, slot):
        eid = ids_smem[tok * picks + j]
        return pltpu.make_async_copy(uv_hbm.at[pl.ds(eid, 1), :], buf.at[slot, pl.ds(j, 1), :], row_sems.at[slot])

    def issue(tok, slot):
        for j in range(picks):
            row_copy(tok, j, slot).start()

    def wait_all(slot):
        pltpu.make_async_copy(uv_hbm.at[pl.ds(0, picks), :], buf.at[slot], row_sems.at[slot]).wait()

    ahead = MIX_SLOTS - 1
    for s in range(ahead):
        issue(s, s)

    def group(g, carry):
        rows = []
        gates = gates_ref[g]
        for u in range(SUBLANES):
            tok = g * SUBLANES + u
            slot = u % MIX_SLOTS

            @pl.when(tok + ahead < tb)
            def _():
                issue(tok + ahead, (u + ahead) % MIX_SLOTS)

            wait_all(slot)
            r8 = pl.multiple_of(g * SUBLANES, SUBLANES)
            h8 = h_ref[pl.ds(r8, SUBLANES), :].astype(F32)
            rows.append(_expert_mix(buf[slot], h8[u:u + 1, :], gates[:, u:u + 1], d))
        r8 = pl.multiple_of(g * SUBLANES, SUBLANES)
        o_ref[pl.ds(r8, SUBLANES), :] = x_ref[pl.ds(r8, SUBLANES), :] + jnp.concatenate(rows, axis=0)
        return carry

    lax.fori_loop(0, tb // SUBLANES, group, 0)


def _peer_mix_direct(ids_blk, uv, x, h, gates3, *, tb, n_tokens):
    t, d = x.shape
    picks = gates3.shape[1]
    return pl.pallas_call(
        functools.partial(_peer_mix_kernel, tb=tb, picks=picks, d=d),
        grid=(n_tokens // tb,),
        input_output_aliases={2: 0},
        in_specs=[
            pl.BlockSpec(memory_space=pl.ANY),
            pl.BlockSpec(memory_space=pl.ANY),
            pl.BlockSpec((tb, d), lambda i: (i, 0)),
            pl.BlockSpec((tb, d), lambda i: (i, 0)),
            pl.BlockSpec((tb // SUBLANES, picks, SUBLANES), lambda i: (i, 0, 0)),
        ],
        out_specs=pl.BlockSpec((tb, d), lambda i: (i, 0)),
        out_shape=jax.ShapeDtypeStruct((t, d), F32),
        scratch_shapes=[
            pltpu.SMEM((tb * picks,), jnp.int32),
            pltpu.VMEM((MIX_SLOTS, picks, d), jnp.int32),
            pltpu.SemaphoreType.DMA(()),
            pltpu.SemaphoreType.DMA((MIX_SLOTS,)),
        ],
        compiler_params=_params(("arbitrary",)),
    )(ids_blk, uv, x, h, gates3)


def _peer_mix_staged_kernel(rows_ref, x_ref, h_ref, gates_ref, o_ref, *, tb, picks, d):
    for g in range(tb // SUBLANES):
        gates = gates_ref[g]
        h8 = h_ref[g * SUBLANES:(g + 1) * SUBLANES, :].astype(F32)
        rows = []
        for u in range(SUBLANES):
            r0 = (g * SUBLANES + u) * picks
            rows.append(_expert_mix(rows_ref[r0:r0 + picks, :], h8[u:u + 1, :], gates[:, u:u + 1], d))
        sl = slice(g * SUBLANES, (g + 1) * SUBLANES)
        o_ref[sl, :] = x_ref[sl, :] + jnp.concatenate(rows, axis=0)


def _peer_mix_staged(staged, x, h, gates3, *, tb, first_token):
    t, d = x.shape
    picks = gates3.shape[1]
    n_tokens = staged.shape[0] // picks
    assert first_token % tb == 0 and first_token + n_tokens == t
    b0 = first_token // tb
    return pl.pallas_call(
        functools.partial(_peer_mix_staged_kernel, tb=tb, picks=picks, d=d),
        grid=(n_tokens // tb,),
        input_output_aliases={1: 0},
        in_specs=[
            pl.BlockSpec((tb * picks, d), lambda i: (i, 0)),
            pl.BlockSpec((tb, d), lambda i: (i + b0, 0)),
            pl.BlockSpec((tb, d), lambda i: (i + b0, 0)),
            pl.BlockSpec((tb // SUBLANES, picks, SUBLANES), lambda i: (i + b0, 0, 0)),
        ],
        out_specs=pl.BlockSpec((tb, d), lambda i: (i + b0, 0)),
        out_shape=jax.ShapeDtypeStruct((t, d), F32),
        compiler_params=_params(("parallel",)),
    )(staged, x, h, gates3)


def _sc_gather_rows(table, idx, *, first_row, chunk):
    n_rows = idx.shape[0] - first_row
    d = table.shape[1]
    workers = SC_CORES * SC_SUBCORES
    assert n_rows % (workers * 2 * chunk) == 0 and chunk % SUBLANES == 0 and chunk <= LANES
    assert first_row % SUBLANES == 0
    rows_per_worker = n_rows // workers
    n_pairs = rows_per_worker // (2 * chunk)
    mesh = plsc.VectorSubcoreMesh(core_axis_name="c", subcore_axis_name="s")

    @functools.partial(
        pl.kernel, mesh=mesh,
        out_type=jax.ShapeDtypeStruct((n_rows, d), table.dtype),
        scratch_types=[
            pltpu.VMEM((chunk,), jnp.int32), pltpu.VMEM((chunk,), jnp.int32),
            pltpu.VMEM((chunk, d), table.dtype), pltpu.VMEM((chunk, d), table.dtype),
            pltpu.SemaphoreType.DMA, pltpu.SemaphoreType.DMA,
            pltpu.SemaphoreType.DMA, pltpu.SemaphoreType.DMA,
        ],
    )
    def gather_kernel(table_hbm, idx_hbm, out_hbm, idx0, idx1, rows0, rows1, gsem0, gsem1, wsem0, wsem1):
        idx_v, rows_v, gsem, wsem = (idx0, idx1), (rows0, rows1), (gsem0, gsem1), (wsem0, wsem1)
        worker = lax.axis_index("s") * SC_CORES + lax.axis_index("c")
        base = worker * rows_per_worker

        def out_rows(c):
            return pl.ds(pl.multiple_of(base + c * chunk, SUBLANES), chunk)

        def load_idx(slot, c):
            src = pl.ds(pl.multiple_of(first_row + base + c * chunk, SUBLANES), chunk)
            pltpu.sync_copy(idx_hbm.at[src], idx_v[slot])

        def gather(slot):
            return pltpu.make_async_copy(table_hbm.at[idx_v[slot]], rows_v[slot], gsem[slot])

        def writeout(slot, c):
            return pltpu.make_async_copy(rows_v[slot], out_hbm.at[out_rows(c)], wsem[slot])

        load_idx(0, 0)
        gather(0).start()

        @pl.loop(0, n_pairs)
        def _(p):
            c0 = 2 * p

            @pl.when(p > 0)
            def _():
                writeout(1, c0 - 1).wait()

            load_idx(1, c0 + 1)
            gather(1).start()
            gather(0).wait()
            writeout(0, c0).start()

            @pl.when(p + 1 < n_pairs)
            def _():
                load_idx(0, c0 + 2)
                writeout(0, c0).wait()
                gather(0).start()

            gather(1).wait()
            writeout(1, c0 + 1).start()

        writeout(0, 2 * n_pairs - 2).wait()
        writeout(1, 2 * n_pairs - 1).wait()

    return gather_kernel(table, idx)


def _pad_cols(w, width):
    return jnp.pad(w, ((0, 0), (0, width - w.shape[1])))


def _prepare_layer(g_norm_mix, w_in, b_gate, g_cq, w_uq, g_ckv, w_ukv, g_qn, g_qr, g_kn, g_kr,
                   w_a2, b_a, g_gla_out, w_out, g_norm_ffn, w_pq, sub_keys, u_emb, v_emb):
    d = w_in.shape[0]
    q_rank, kv_rank = g_cq.shape[0], g_ckv.shape[0]
    nope, rope = g_qn.shape[0], g_qr.shape[0]
    mla_heads = w_uq.shape[1] // (nope + rope)
    mla_v = w_ukv.shape[1] // mla_heads - nope
    gate_rank, gla_dk_all = w_a2.shape
    gla_dv = g_gla_out.shape[0]
    gla_heads = d // gla_dv
    gla_dk = gla_dk_all // gla_heads
    peer_heads, _, nkeys, half = sub_keys.shape
    assert nope == LANES and mla_v == LANES and rope <= LANES and rope % 2 == 0
    assert mla_heads * mla_v == d and gla_heads * gla_dv == d
    assert rope + gate_rank <= LANES and nkeys == LANES and half == LANES

    widths = (q_rank, kv_rank, rope, gla_dk_all, gla_dk_all, d, gate_rank, d, d, d)
    offs = [0]
    for wd in widths:
        offs.append(offs[-1] + wd)
    assert offs[-1] == w_in.shape[1]
    seg = lambda i: w_in[:, offs[i]:offs[i + 1]]
    w_main = jnp.concatenate([seg(5), seg(7), seg(8), seg(9), seg(3), seg(4), seg(0), seg(1)], axis=1).astype(BF16)
    w_small = _pad_cols(jnp.concatenate([seg(2), seg(6)], axis=1), LANES).astype(BF16)
    cq_off = 4 * d + 2 * gla_dk_all
    assert cq_off % q_rank == 0 and (cq_off + q_rank) % kv_rank == 0
    inv_freq = ROPE_THETA ** (-jnp.arange(0, rope, 2, dtype=F32) / rope)
    scale = (nope + rope) ** -0.5
    return dict(
        dims=dict(q_rank=q_rank, kv_rank=kv_rank, nope=nope, rope=rope, mla_heads=mla_heads, mla_v=mla_v,
                  gla_heads=gla_heads, gla_dk=gla_dk, gla_dv=gla_dv, peer_heads=peer_heads, nkeys=nkeys, half=half,
                  v_blk=0, og_blk=d // gla_dv, ga_blk=2, gb_blk=3, q_blk=4 * d // gla_dk,
                  k_blk=4 * d // gla_dk + gla_heads, cq_blk=cq_off // q_rank, ckv_blk=(cq_off + q_rank) // kv_rank),
        g_norm_mix=g_norm_mix[None, :], w_main=w_main, w_small=w_small,
        invf=_pad_cols(jnp.concatenate([inv_freq, inv_freq])[None, :], LANES),
        gq=_pad_cols(jnp.concatenate([g_qn, g_qr])[None, :] * scale, nope + LANES),
        gkr=_pad_cols(g_kr[None, :], LANES), g_cq=g_cq[None, :], g_ckv=g_ckv[None, :], g_kn=g_kn[None, :],
        w_uq=jnp.pad(w_uq.reshape(q_rank, mla_heads, nope + rope),
                     ((0, 0), (0, 0), (0, LANES - rope))).reshape(q_rank, -1).astype(BF16),
        w_ukv=w_ukv.astype(BF16),
        w2p=jnp.zeros((LANES, gla_dk_all), F32).at[rope:rope + gate_rank].set(w_a2).astype(BF16),
        b_a=b_a[None, :], g_on=g_gla_out[None, :], b_gate=b_gate, w_out=w_out.astype(BF16),
        g_norm_ffn=g_norm_ffn[None, :], w_pq=w_pq.astype(BF16),
        sk2d=sub_keys.reshape(peer_heads * 2 * nkeys, half).astype(BF16),
        uv=_pack_expert_table(u_emb, v_emb),
    )


def _mixers_and_route(x2, pos, w, *, batch, seq, tiles):
    t = x2.shape[0]
    dm = w["dims"]
    p, small = _in_proj(x2, w["g_norm_mix"], w["w_main"], w["w_small"], tm=tiles["in_tm"], tn=tiles["in_tn"])
    q = _mla_q(p, dm["cq_blk"], w["g_cq"], w["w_uq"], w["gq"], pos, w["invf"], heads=dm["mla_heads"],
               rank=dm["q_rank"], nope=dm["nope"], rope=dm["rope"], tm=tiles["mla_tm"])
    k, v = _mla_kv(p, dm["ckv_blk"], w["g_ckv"], w["w_ukv"], w["g_kn"], small, w["gkr"], pos, w["invf"],
                   heads=dm["mla_heads"], rank=dm["kv_rank"], nope=dm["nope"], rope=dm["rope"], dv=dm["mla_v"],
                   tm=tiles["mla_tm"])
    o_mla = _mla_attn(q, k, v, batch=batch, seq=seq, heads=dm["mla_heads"], dk=dm["nope"] + LANES, dv=dm["mla_v"],
                      tq=tiles["attn_tq"], tk=tiles["attn_tk"])
    o_gla = _gla(p, small, w["w2p"], w["b_a"], w["g_on"], batch=batch, seq=seq, heads=dm["gla_heads"],
                 dk=dm["gla_dk"], dv=dm["gla_dv"], q_blk=dm["q_blk"], k_blk=dm["k_blk"], v_blk=dm["v_blk"],
                 og_blk=dm["og_blk"])
    x2 = _out_proj(x2, p, o_mla, o_gla, w["b_gate"], w["w_out"], ga_blk=dm["ga_blk"], gb_blk=dm["gb_blk"],
                   tm=tiles["out_tm"], tn=tiles["out_tn"])
    h2, ids_t, gates_t = _peer_route(x2, w["g_norm_ffn"], w["w_pq"], w["sk2d"], heads=dm["peer_heads"],
                                     nkeys=dm["nkeys"], half=dm["half"], topk=PEER_TOPK, tm=tiles["route_tm"])
    picks = dm["peer_heads"] * PEER_TOPK
    gates3 = gates_t.reshape(picks, t // SUBLANES, SUBLANES).transpose(1, 0, 2)
    return x2, h2, ids_t.T, gates3


def _layer(x2, pos, w, *, batch, seq, tiles):
    t, d = x2.shape
    direct_tokens = tiles["mix_direct_tokens"]
    groups = len(direct_tokens)
    tg, bg = t // groups, batch // groups
    routed = [_mixers_and_route(x2[g * tg:(g + 1) * tg], pos[g * tg:(g + 1) * tg], w,
                                batch=bg, seq=seq, tiles=tiles) for g in range(groups)]
    tb = tiles["mix_tb"]
    mixed = []
    for g, (xg, hg, ids_tok, gates3) in enumerate(routed):
        if direct_tokens[g]:
            xg = _peer_mix_direct(ids_tok.reshape(tg // tb, -1), w["uv"], xg, hg, gates3,
                                  tb=tb, n_tokens=direct_tokens[g])
        mixed.append(xg)
    out = []
    for g, (_, hg, ids_tok, gates3) in enumerate(routed):
        xg = mixed[g]
        if direct_tokens[g] < tg:
            staged = _sc_gather_rows(w["uv"], ids_tok.reshape(-1), first_row=direct_tokens[g] * ids_tok.shape[1],
                                     chunk=tiles["sc_chunk"])
            xg = _peer_mix_staged(staged, xg, hg, gates3, tb=tiles["mix_staged_tb"], first_token=direct_tokens[g])
        out.append(xg)
    return jnp.concatenate(out, axis=0)


_TILES = dict(in_tm=2048, in_tn=512, mla_tm=2048, attn_tq=1024, attn_tk=1024,
              out_tm=1024, out_tn=512, route_tm=256, mix_tb=64,
              mix_direct_tokens=(192,) * 8, mix_staged_tb=16, sc_chunk=16)


def kernel(x, positions, g_norm_mix, w_in, b_gate, g_cq, w_uq, g_ckv, w_ukv, g_qn, g_qr, g_kn, g_kr,
           w_a2, b_a, g_gla_out, w_out, g_norm_ffn, w_pq, sub_keys, u_emb, v_emb, tiles=None):
    tiles = _TILES if tiles is None else tiles
    batch, seq, d = x.shape
    x2 = x.reshape(batch * seq, d)
    pos = positions.reshape(batch * seq, 1)
    for l in range(g_norm_mix.shape[0]):
        w = _prepare_layer(g_norm_mix[l], w_in[l], b_gate[l], g_cq[l], w_uq[l], g_ckv[l], w_ukv[l], g_qn[l],
                           g_qr[l], g_kn[l], g_kr[l], w_a2[l], b_a[l], g_gla_out[l], w_out[l], g_norm_ffn[l],
                           w_pq[l], sub_keys[l], u_emb[l], v_emb[l])
        x2 = _layer(x2, pos, w, batch=batch, seq=seq, tiles=tiles)
    return x2.reshape(batch, seq, d)
```

```python
import functools

import jax
import jax.numpy as jnp
from jax import lax
from jax.experimental import pallas as pl
from jax.experimental.pallas import tpu as pltpu
from jax.experimental.pallas import tpu_sc as plsc

EPS = 1e-6
ROPE_THETA = 10000.0
GLA_TAU = 16.0
GLA_CHUNK = 64
PEER_TOPK = 16

LANES = 128
SUBLANES = 8
VMEM_LIMIT_BYTES = 56 * 1024 * 1024
MIX_SLOTS = 4
SC_CORES = 2
SC_SUBCORES = 16

F32 = jnp.float32
BF16 = jnp.bfloat16
NEG_INF = float("-inf")


def _params(semantics):
    return pltpu.CompilerParams(dimension_semantics=semantics, vmem_limit_bytes=VMEM_LIMIT_BYTES)


def _rms(x, gain, n=None):
    ss = jnp.sum(x * x, axis=-1, keepdims=True)
    n = x.shape[-1] if n is None else n
    return x * lax.rsqrt(ss * (1.0 / n) + EPS) * gain


def _gelu_exact(x):
    return 0.5 * x * (1.0 + lax.erf(x * (0.5 ** 0.5)))


def _dot(a, b):
    return jnp.dot(a, b, preferred_element_type=F32)


def _dot_nt(a, b):
    return lax.dot_general(a, b, (((1,), (1,)), ((), ())), preferred_element_type=F32)


def _dot_tn(a, b):
    return lax.dot_general(a, b, (((0,), (0,)), ((), ())), preferred_element_type=F32)


def _in_proj_kernel(x_ref, g_ref, w_ref, ws_ref, p_ref, ps_ref, h_scr):
    @pl.when(pl.program_id(1) == 0)
    def _():
        h = _rms(x_ref[...], g_ref[...]).astype(BF16)
        h_scr[...] = h
        ps_ref[...] = _dot(h, ws_ref[...])

    p_ref[...] = _dot(h_scr[...], w_ref[...]).astype(p_ref.dtype)


def _in_proj(x, g, w_main, w_small, *, tm, tn):
    t, d = x.shape
    n = w_main.shape[1]
    x_mode = pl.Buffered(1) if t == tm else None
    return pl.pallas_call(
        _in_proj_kernel,
        grid=(t // tm, n // tn),
        in_specs=[
            pl.BlockSpec((tm, d), lambda i, j: (i, 0), pipeline_mode=x_mode),
            pl.BlockSpec((1, d), lambda i, j: (0, 0)),
            pl.BlockSpec((d, tn), lambda i, j: (0, j)),
            pl.BlockSpec((d, LANES), lambda i, j: (0, 0)),
        ],
        out_specs=[
            pl.BlockSpec((tm, tn), lambda i, j: (i, j)),
            pl.BlockSpec((tm, LANES), lambda i, j: (i, 0)),
        ],
        out_shape=[
            jax.ShapeDtypeStruct((t, n), BF16),
            jax.ShapeDtypeStruct((t, LANES), F32),
        ],
        scratch_shapes=[pltpu.VMEM((tm, d), BF16)],
        compiler_params=_params(("parallel", "arbitrary")),
    )(x, g, w_main, w_small)


def _rope_tables(pos_ref, invf_ref, rope):
    ang = pos_ref[...].astype(F32) * invf_ref[...]
    cos, sin = jnp.cos(ang), jnp.sin(ang)
    lane = lax.broadcasted_iota(jnp.int32, ang.shape, 1)
    half = rope // 2
    c = jnp.where(lane < rope, cos, 0.0)
    s_lo = jnp.where(lane < half, -sin, 0.0)
    s_hi = jnp.where(lane < half, 0.0, jnp.where(lane < rope, sin, 0.0))
    return c, s_lo, s_hi


def _apply_rope(pe, c, s_lo, s_hi, rope):
    half = rope // 2
    from_hi = pltpu.roll(pe, LANES - half, 1)
    from_lo = pltpu.roll(pe, half, 1)
    return pe * c + from_hi * s_lo + from_lo * s_hi


def _mla_q_kernel(cq_ref, gcq_ref, w_ref, gq_ref, pos_ref, invf_ref, q_ref,
                  h_scr, c_scr, slo_scr, shi_scr, *, nope, rope):
    @pl.when(pl.program_id(1) == 0)
    def _():
        h_scr[...] = _rms(cq_ref[...].astype(F32), gcq_ref[...]).astype(BF16)
        c, s_lo, s_hi = _rope_tables(pos_ref, invf_ref, rope)
        c_scr[...] = c
        slo_scr[...] = s_lo
        shi_scr[...] = s_hi

    y = _dot(h_scr[...], w_ref[...])
    g = gq_ref[...]
    qn = _rms(y[:, :nope], g[:, :nope])
    pe = _rms(y[:, nope:], g[:, nope:], n=rope)
    pe = _apply_rope(pe, c_scr[...], slo_scr[...], shi_scr[...], rope)
    q_ref[:, :nope] = qn.astype(q_ref.dtype)
    q_ref[:, nope:] = pe.astype(q_ref.dtype)


def _mla_q(p, cq_blk, g_cq, w_uq_p, gq, pos, invf, *, heads, rank, nope, rope, tm):
    t = p.shape[0]
    hw = nope + LANES
    return pl.pallas_call(
        functools.partial(_mla_q_kernel, nope=nope, rope=rope),
        grid=(t // tm, heads),
        in_specs=[
            pl.BlockSpec((tm, rank), lambda i, j: (i, cq_blk)),
            pl.BlockSpec((1, rank), lambda i, j: (0, 0)),
            pl.BlockSpec((rank, hw), lambda i, j: (0, j)),
            pl.BlockSpec((1, hw), lambda i, j: (0, 0)),
            pl.BlockSpec((tm, 1), lambda i, j: (i, 0)),
            pl.BlockSpec((1, LANES), lambda i, j: (0, 0)),
        ],
        out_specs=pl.BlockSpec((tm, hw), lambda i, j: (i, j)),
        out_shape=jax.ShapeDtypeStruct((t, heads * hw), BF16),
        scratch_shapes=[
            pltpu.VMEM((tm, rank), BF16),
            pltpu.VMEM((tm, LANES), F32),
            pltpu.VMEM((tm, LANES), F32),
            pltpu.VMEM((tm, LANES), F32),
        ],
        compiler_params=_params(("parallel", "arbitrary")),
    )(p, g_cq, w_uq_p, gq, pos, invf)


def _mla_kv_kernel(ckv_ref, gckv_ref, w_ref, gkn_ref, small_ref, gkr_ref, pos_ref, invf_ref,
                   k_ref, v_ref, h_scr, kpe_scr, *, nope, rope):
    @pl.when(pl.program_id(1) == 0)
    def _():
        h_scr[...] = _rms(ckv_ref[...].astype(F32), gckv_ref[...]).astype(BF16)
        c, s_lo, s_hi = _rope_tables(pos_ref, invf_ref, rope)
        sm = small_ref[...]
        lane = lax.broadcasted_iota(jnp.int32, sm.shape, 1)
        pe = _rms(jnp.where(lane < rope, sm, 0.0), gkr_ref[...], n=rope)
        kpe_scr[...] = _apply_rope(pe, c, s_lo, s_hi, rope).astype(BF16)

    y = _dot(h_scr[...], w_ref[...])
    k_ref[:, :nope] = _rms(y[:, :nope], gkn_ref[...]).astype(k_ref.dtype)
    k_ref[:, nope:] = kpe_scr[...]
    v_ref[...] = y[:, nope:].astype(v_ref.dtype)


def _mla_kv(p, ckv_blk, g_ckv, w_ukv, g_kn, small, gkr, pos, invf, *, heads, rank, nope, rope, dv, tm):
    t = p.shape[0]
    kw = nope + LANES
    return pl.pallas_call(
        functools.partial(_mla_kv_kernel, nope=nope, rope=rope),
        grid=(t // tm, heads),
        in_specs=[
            pl.BlockSpec((tm, rank), lambda i, j: (i, ckv_blk)),
            pl.BlockSpec((1, rank), lambda i, j: (0, 0)),
            pl.BlockSpec((rank, nope + dv), lambda i, j: (0, j)),
            pl.BlockSpec((1, nope), lambda i, j: (0, 0)),
            pl.BlockSpec((tm, LANES), lambda i, j: (i, 0)),
            pl.BlockSpec((1, LANES), lambda i, j: (0, 0)),
            pl.BlockSpec((tm, 1), lambda i, j: (i, 0)),
            pl.BlockSpec((1, LANES), lambda i, j: (0, 0)),
        ],
        out_specs=[
            pl.BlockSpec((tm, kw), lambda i, j: (i, j)),
            pl.BlockSpec((tm, dv), lambda i, j: (i, j)),
        ],
        out_shape=[
            jax.ShapeDtypeStruct((t, heads * kw), BF16),
            jax.ShapeDtypeStruct((t, heads * dv), BF16),
        ],
        scratch_shapes=[pltpu.VMEM((tm, rank), BF16), pltpu.VMEM((tm, LANES), BF16)],
        compiler_params=_params(("parallel", "arbitrary")),
    )(p, g_ckv, w_ukv, g_kn, small, gkr, pos, invf)


def _attn_kernel(q_ref, k_ref, v_ref, o_ref, *, tq, tk):
    qi = pl.program_id(2)
    q = q_ref[...]
    dv = v_ref.shape[1]
    row = qi * tq + lax.broadcasted_iota(jnp.int32, (tq, tk), 0)
    col0 = lax.broadcasted_iota(jnp.int32, (tq, tk), 1)

    def body(kb, carry, masked):
        m, l, acc = carry
        k0 = pl.multiple_of(kb * tk, tk)
        s = _dot_nt(q, k_ref[pl.ds(k0, tk), :])
        if masked:
            s = jnp.where(col0 + k0 <= row, s, NEG_INF)
        m_new = jnp.maximum(m, jnp.max(s, axis=-1, keepdims=True))
        alpha = jnp.exp(m - m_new)
        pr = jnp.exp(s - m_new)
        l = alpha * l + jnp.sum(pr, axis=-1, keepdims=True)
        acc = alpha * acc + _dot(pr.astype(BF16), v_ref[pl.ds(k0, tk), :])
        return m_new, l, acc

    init = (jnp.full((tq, 1), NEG_INF, F32), jnp.zeros((tq, 1), F32), jnp.zeros((tq, dv), F32))
    n_below = (qi * tq) // tk
    nkb = ((qi + 1) * tq + tk - 1) // tk
    carry = lax.fori_loop(0, n_below, functools.partial(body, masked=False), init)
    _, l, acc = lax.fori_loop(n_below, nkb, functools.partial(body, masked=True), carry)
    o_ref[...] = (acc / l).astype(o_ref.dtype)


def _mla_attn(q, k, v, *, batch, seq, heads, dk, dv, tq, tk):
    t = q.shape[0]
    nq = seq // tq
    return pl.pallas_call(
        functools.partial(_attn_kernel, tq=tq, tk=tk),
        grid=(batch, heads, nq),
        in_specs=[
            pl.BlockSpec((tq, dk), lambda b, h, i: (b * nq + i, h)),
            pl.BlockSpec((seq, dk), lambda b, h, i: (b, h)),
            pl.BlockSpec((seq, dv), lambda b, h, i: (b, h)),
        ],
        out_specs=pl.BlockSpec((tq, dv), lambda b, h, i: (b * nq + i, h)),
        out_shape=jax.ShapeDtypeStruct((t, heads * dv), BF16),
        compiler_params=_params(("parallel", "parallel", "arbitrary")),
    )(q, k, v)


def _gla_kernel(q_ref, k_ref, v_ref, og_ref, small_ref, w2_ref, ba_ref, gon_ref, o_ref, st_scr,
                *, seq, dk, dv, chunk):
    c = chunk
    st_scr[...] = jnp.zeros_like(st_scr)
    r_i = lax.broadcasted_iota(jnp.int32, (c, c), 0)
    c_i = lax.broadcasted_iota(jnp.int32, (c, c), 1)
    tri = jnp.where(c_i <= r_i, 1.0, 0.0).astype(BF16)
    row_id = lax.broadcasted_iota(jnp.int32, (c, 1), 0)
    w2 = w2_ref[...]
    ba = ba_ref[...]
    gon = gon_ref[...]
    q_scale = dk ** -0.5

    def chunk_step(ci, carry):
        r0 = pl.multiple_of(ci * c, c)
        qc = q_ref[pl.ds(r0, c), :].astype(F32) * q_scale
        kc = k_ref[pl.ds(r0, c), :].astype(F32)
        vc = v_ref[pl.ds(r0, c), :]
        z = _dot(small_ref[pl.ds(r0, c), :].astype(BF16), w2) + ba
        la = jax.nn.log_sigmoid(z) * (1.0 / GLA_TAU)
        hi = la.astype(BF16)
        r1 = la - hi.astype(F32)
        mid = r1.astype(BF16)
        lo = (r1 - mid.astype(F32)).astype(BF16)
        b = _dot(tri, hi) + _dot(tri, mid) + _dot(tri, lo)

        st = st_scr[...]
        inter = _dot_nt((qc * jnp.exp(b)).astype(BF16), st.astype(BF16))

        att = jnp.zeros((c, c), F32)
        for j in range(c):
            lo_r = (j // SUBLANES) * SUBLANES
            d = b[lo_r:, :] - b[j:j + 1, :]
            e = jnp.exp(jnp.where(row_id[lo_r:, :] >= j, d, NEG_INF))
            col = jnp.sum(qc[lo_r:, :] * kc[j:j + 1, :] * e, axis=-1, keepdims=True)
            if lo_r:
                col = jnp.concatenate([jnp.zeros((lo_r, 1), F32), col], axis=0)
            att = jnp.where(c_i == j, col, att)
        o = inter + _dot(att.astype(BF16), vc)

        b_last = b[c - 1:c, :]
        k_dec = (kc * jnp.exp(b_last - b)).astype(BF16)
        st_scr[...] = st * jnp.exp(b_last) + _dot_tn(vc, k_dec)

        og = og_ref[pl.ds(r0, c), :].astype(F32)
        out = _rms(o, gon) * (og * jax.nn.sigmoid(og))
        o_ref[pl.ds(r0, c), :] = out.astype(o_ref.dtype)
        return carry

    lax.fori_loop(0, seq // c, chunk_step, 0)


def _gla(p, small, w2p, b_a, g_on, *, batch, seq, heads, dk, dv, q_blk, k_blk, v_blk, og_blk):
    t = p.shape[0]
    return pl.pallas_call(
        functools.partial(_gla_kernel, seq=seq, dk=dk, dv=dv, chunk=GLA_CHUNK),
        grid=(batch, heads),
        in_specs=[
            pl.BlockSpec((seq, dk), lambda b, h: (b, q_blk + h)),
            pl.BlockSpec((seq, dk), lambda b, h: (b, k_blk + h)),
            pl.BlockSpec((seq, dv), lambda b, h: (b, v_blk + h)),
            pl.BlockSpec((seq, dv), lambda b, h: (b, og_blk + h)),
            pl.BlockSpec((seq, LANES), lambda b, h: (b, 0)),
            pl.BlockSpec((LANES, dk), lambda b, h: (0, h)),
            pl.BlockSpec((1, dk), lambda b, h: (0, h)),
            pl.BlockSpec((1, dv), lambda b, h: (0, 0)),
        ],
        out_specs=pl.BlockSpec((seq, dv), lambda b, h: (b, h)),
        out_shape=jax.ShapeDtypeStruct((t, heads * dv), BF16),
        scratch_shapes=[pltpu.VMEM((dv, dk), F32)],
        compiler_params=_params(("parallel", "parallel")),
    )(p, p, p, p, small, w2p, b_a, g_on)


def _out_proj_kernel(x_ref, ga_ref, gb_ref, oa_ref, ob_ref, bg_ref, w_ref, o_ref, m_scr):
    @pl.when(pl.program_id(1) == 0)
    def _():
        bg = bg_ref[...]
        sa = jax.nn.sigmoid(ga_ref[...].astype(F32) + bg[0:1, :])
        sb = jax.nn.sigmoid(gb_ref[...].astype(F32) + bg[1:2, :])
        m_scr[...] = (sa * oa_ref[...].astype(F32) + sb * ob_ref[...].astype(F32)).astype(BF16)

    o_ref[...] = x_ref[...] + _dot(m_scr[...], w_ref[...])


def _out_proj(x, p, o_mla, o_gla, b_gate, w_out, *, ga_blk, gb_blk, tm, tn):
    t, d = x.shape
    return pl.pallas_call(
        _out_proj_kernel,
        grid=(t // tm, d // tn),
        in_specs=[
            pl.BlockSpec((tm, tn), lambda i, j: (i, j)),
            pl.BlockSpec((tm, d), lambda i, j: (i, ga_blk)),
            pl.BlockSpec((tm, d), lambda i, j: (i, gb_blk)),
            pl.BlockSpec((tm, d), lambda i, j: (i, 0)),
            pl.BlockSpec((tm, d), lambda i, j: (i, 0)),
            pl.BlockSpec((2, d), lambda i, j: (0, 0)),
            pl.BlockSpec((d, tn), lambda i, j: (0, j)),
        ],
        out_specs=pl.BlockSpec((tm, tn), lambda i, j: (i, j)),
        out_shape=jax.ShapeDtypeStruct((t, d), F32),
        scratch_shapes=[pltpu.VMEM((tm, d), BF16)],
        compiler_params=_params(("parallel", "arbitrary")),
    )(x, p, p, o_mla, o_gla, b_gate, w_out)


def _topk_rows(s, k, payload=None):
    n = s.shape[0]
    iota = lax.broadcasted_iota(jnp.int32, s.shape, 0).astype(F32)
    kiota = lax.broadcasted_iota(jnp.int32, (k, s.shape[1]), 0)
    vals = jnp.zeros((k, s.shape[1]), F32)
    picks = jnp.zeros((k, s.shape[1]), F32)
    for r in range(k):
        m = jnp.max(s, axis=0, keepdims=True)
        idx = jnp.min(jnp.where(s == m, iota, float(n)), axis=0, keepdims=True)
        hit = iota == idx
        if payload is None:
            pick = idx
        else:
            pick = jnp.sum(jnp.where(hit, payload, 0.0), axis=0, keepdims=True)
        vals = jnp.where(kiota == r, m, vals)
        picks = jnp.where(kiota == r, pick, picks)
        s = jnp.where(hit, NEG_INF, s)
    return vals, picks


def _peer_route_kernel(x_ref, g_ref, w_ref, sk_ref, h_ref, ids_ref, gates_ref, *, heads, nkeys, half, topk):
    h = _rms(x_ref[...], g_ref[...]).astype(BF16)
    h_ref[...] = h
    qf = _dot(h, w_ref[...])
    for hd in range(heads):
        tops = []
        for part in range(2):
            o = (hd * 2 + part) * half
            qh = qf[:, o:o + half].astype(BF16)
            keys = sk_ref[(hd * 2 + part) * nkeys:(hd * 2 + part + 1) * nkeys, :]
            tops.append(_topk_rows(_dot_nt(keys, qh), topk))
        (s1, i1), (s2, i2) = tops
        assert topk == 2 * SUBLANES
        hs = SUBLANES
        pair_s = [s1[0:1, :] + s2] + [s1[a:a + 1, :] + s2[:hs, :] for a in range(1, hs)] + [s1[hs:, :] + s2[0:1, :]]
        pair_id = ([i1[0:1, :] * float(nkeys) + i2]
                   + [i1[a:a + 1, :] * float(nkeys) + i2[:hs, :] for a in range(1, hs)]
                   + [i1[hs:, :] * float(nkeys) + i2[0:1, :]])
        best_s, best_id = _topk_rows(jnp.concatenate(pair_s, axis=0), topk,
                                     payload=jnp.concatenate(pair_id, axis=0))
        e = jnp.exp(best_s - best_s[0:1, :])
        gate = e / jnp.sum(e, axis=0, keepdims=True)
        ids_ref[hd * topk:(hd + 1) * topk, :] = best_id.astype(jnp.int32)
        gates_ref[hd * topk:(hd + 1) * topk, :] = gate


def _peer_route(x, g, w_pq, sk2d, *, heads, nkeys, half, topk, tm):
    t, d = x.shape
    dq = w_pq.shape[1]
    return pl.pallas_call(
        functools.partial(_peer_route_kernel, heads=heads, nkeys=nkeys, half=half, topk=topk),
        grid=(t // tm,),
        in_specs=[
            pl.BlockSpec((tm, d), lambda i: (i, 0)),
            pl.BlockSpec((1, d), lambda i: (0, 0)),
            pl.BlockSpec((d, dq), lambda i: (0, 0)),
            pl.BlockSpec((heads * 2 * nkeys, half), lambda i: (0, 0)),
        ],
        out_specs=[
            pl.BlockSpec((tm, d), lambda i: (i, 0)),
            pl.BlockSpec((heads * topk, tm), lambda i: (0, i)),
            pl.BlockSpec((heads * topk, tm), lambda i: (0, i)),
        ],
        out_shape=[
            jax.ShapeDtypeStruct((t, d), BF16),
            jax.ShapeDtypeStruct((heads * topk, t), jnp.int32),
            jax.ShapeDtypeStruct((heads * topk, t), F32),
        ],
        compiler_params=_params(("parallel",)),
    )(x, g, w_pq, sk2d)


def _pack_expert_table(u_emb, v_emb):
    def pack(w):
        bits = lax.bitcast_convert_type(w.astype(BF16), jnp.uint16).astype(jnp.uint32)
        half = w.shape[1] // 2
        return bits[:, :half] | (bits[:, half:] << 16)

    return lax.bitcast_convert_type(jnp.concatenate([pack(u_emb), pack(v_emb)], axis=1), jnp.int32)


def _unpack_words(w):
    lo = lax.bitcast_convert_type(w << 16, F32)
    hi = lax.bitcast_convert_type(w & jnp.int32(-65536), F32)
    return lo, hi


def _expert_mix(words, h, gate, d):
    half = d // 2
    u_lo, u_hi = _unpack_words(words[:, :half])
    act = jnp.sum(u_lo * h[:, :half] + u_hi * h[:, half:], axis=-1, keepdims=True)
    w = gate * _gelu_exact(act)
    v_lo, v_hi = _unpack_words(words[:, half:])
    return jnp.concatenate([jnp.sum(v_lo * w, axis=0, keepdims=True),
                            jnp.sum(v_hi * w, axis=0, keepdims=True)], axis=1)


def _peer_mix_kernel(ids_hbm, uv_hbm, x_ref, h_ref, gates_ref, o_ref, ids_smem, buf, ids_sem, row_sems,
                     *, tb, picks, d):
    step = pl.program_id(0)
    ids_copy = pltpu.make_async_copy(ids_hbm.at[step], ids_smem, ids_sem)
    ids_copy.start()
    ids_copy.wait()

    def row_copy(tok, j, slot):
        eid = ids_smem[tok * picks + j]
        return pltpu.make_async_copy(uv_hbm.at[pl.ds(eid, 1), :], buf.at[slot, pl.ds(j, 1), :], row_sems.at[slot])

    def issue(tok, slot):
        for j in range(picks):
            row_copy(tok, j, slot).start()

    def wait_all(slot):
        pltpu.make_async_copy(uv_hbm.at[pl.ds(0, picks), :], buf.at[slot], row_sems.at[slot]).wait()

    ahead = MIX_SLOTS - 1
    for s in range(ahead):
        issue(s, s)

    def group(g, carry):
        rows = []
        gates = gates_ref[g]
        for u in range(SUBLANES):
            tok = g * SUBLANES + u
            slot = u % MIX_SLOTS

            @pl.when(tok + ahead < tb)
            def _():
                issue(tok + ahead, (u + ahead) % MIX_SLOTS)

            wait_all(slot)
            r8 = pl.multiple_of(g * SUBLANES, SUBLANES)
            h8 = h_ref[pl.ds(r8, SUBLANES), :].astype(F32)
            rows.append(_expert_mix(buf[slot], h8[u:u + 1, :], gates[:, u:u + 1], d))
        r8 = pl.multiple_of(g * SUBLANES, SUBLANES)
        o_ref[pl.ds(r8, SUBLANES), :] = x_ref[pl.ds(r8, SUBLANES), :] + jnp.concatenate(rows, axis=0)
        return carry

    lax.fori_loop(0, tb // SUBLANES, group, 0)


def _peer_mix_direct(ids_blk, uv, x, h, gates3, *, tb, n_tokens):
    t, d = x.shape
    picks = gates3.shape[1]
    return pl.pallas_call(
        functools.partial(_peer_mix_kernel, tb=tb, picks=picks, d=d),
        grid=(n_tokens // tb,),
        input_output_aliases={2: 0},
        in_specs=[
            pl.BlockSpec(memory_space=pl.ANY),
            pl.BlockSpec(memory_space=pl.ANY),
            pl.BlockSpec((tb, d), lambda i: (i, 0)),
            pl.BlockSpec((tb, d), lambda i: (i, 0)),
            pl.BlockSpec((tb // SUBLANES, picks, SUBLANES), lambda i: (i, 0, 0)),
        ],
        out_specs=pl.BlockSpec((tb, d), lambda i: (i, 0)),
        out_shape=jax.ShapeDtypeStruct((t, d), F32),
        scratch_shapes=[
            pltpu.SMEM((tb * picks,), jnp.int32),
            pltpu.VMEM((MIX_SLOTS, picks, d), jnp.int32),
            pltpu.SemaphoreType.DMA(()),
            pltpu.SemaphoreType.DMA((MIX_SLOTS,)),
        ],
        compiler_params=_params(("arbitrary",)),
    )(ids_blk, uv, x, h, gates3)


def _peer_mix_staged_kernel(rows_ref, x_ref, h_ref, gates_ref, o_ref, *, tb, picks, d):
    for g in range(tb // SUBLANES):
        gates = gates_ref[g]
        h8 = h_ref[g * SUBLANES:(g + 1) * SUBLANES, :].astype(F32)
        rows = []
        for u in range(SUBLANES):
            r0 = (g * SUBLANES + u) * picks
            rows.append(_expert_mix(rows_ref[r0:r0 + picks, :], h8[u:u + 1, :], gates[:, u:u + 1], d))
        sl = slice(g * SUBLANES, (g + 1) * SUBLANES)
        o_ref[sl, :] = x_ref[sl, :] + jnp.concatenate(rows, axis=0)


def _peer_mix_staged(staged, x, h, gates3, *, tb, first_token):
    t, d = x.shape
    picks = gates3.shape[1]
    n_tokens = staged.shape[0] // picks
    assert first_token % tb == 0 and first_token + n_tokens == t
    b0 = first_token // tb
    return pl.pallas_call(
        functools.partial(_peer_mix_staged_kernel, tb=tb, picks=picks, d=d),
        grid=(n_tokens // tb,),
        input_output_aliases={1: 0},
        in_specs=[
            pl.BlockSpec((tb * picks, d), lambda i: (i, 0)),
            pl.BlockSpec((tb, d), lambda i: (i + b0, 0)),
            pl.BlockSpec((tb, d), lambda i: (i + b0, 0)),
            pl.BlockSpec((tb // SUBLANES, picks, SUBLANES), lambda i: (i + b0, 0, 0)),
        ],
        out_specs=pl.BlockSpec((tb, d), lambda i: (i + b0, 0)),
        out_shape=jax.ShapeDtypeStruct((t, d), F32),
        compiler_params=_params(("parallel",)),
    )(staged, x, h, gates3)


def _sc_gather_rows(table, idx, *, first_row, chunk):
    n_rows = idx.shape[0] - first_row
    d = table.shape[1]
    workers = SC_CORES * SC_SUBCORES
    assert n_rows % (workers * 2 * chunk) == 0 and chunk % SUBLANES == 0 and chunk <= LANES
    assert first_row % SUBLANES == 0
    rows_per_worker = n_rows // workers
    n_pairs = rows_per_worker // (2 * chunk)
    mesh = plsc.VectorSubcoreMesh(core_axis_name="c", subcore_axis_name="s")

    @functools.partial(
        pl.kernel, mesh=mesh,
        out_type=jax.ShapeDtypeStruct((n_rows, d), table.dtype),
        scratch_types=[
            pltpu.VMEM((chunk,), jnp.int32), pltpu.VMEM((chunk,), jnp.int32),
            pltpu.VMEM((chunk, d), table.dtype), pltpu.VMEM((chunk, d), table.dtype),
            pltpu.SemaphoreType.DMA, pltpu.SemaphoreType.DMA,
            pltpu.SemaphoreType.DMA, pltpu.SemaphoreType.DMA,
        ],
    )
    def gather_kernel(table_hbm, idx_hbm, out_hbm, idx0, idx1, rows0, rows1, gsem0, gsem1, wsem0, wsem1):
        idx_v, rows_v, gsem, wsem = (idx0, idx1), (rows0, rows1), (gsem0, gsem1), (wsem0, wsem1)
        worker = lax.axis_index("s") * SC_CORES + lax.axis_index("c")
        base = worker * rows_per_worker

        def out_rows(c):
            return pl.ds(pl.multiple_of(base + c * chunk, SUBLANES), chunk)

        def load_idx(slot, c):
            src = pl.ds(pl.multiple_of(first_row + base + c * chunk, SUBLANES), chunk)
            pltpu.sync_copy(idx_hbm.at[src], idx_v[slot])

        def gather(slot):
            return pltpu.make_async_copy(table_hbm.at[idx_v[slot]], rows_v[slot], gsem[slot])

        def writeout(slot, c):
            return pltpu.make_async_copy(rows_v[slot], out_hbm.at[out_rows(c)], wsem[slot])

        load_idx(0, 0)
        gather(0).start()

        @pl.loop(0, n_pairs)
        def _(p):
            c0 = 2 * p

            @pl.when(p > 0)
            def _():
                writeout(1, c0 - 1).wait()

            load_idx(1, c0 + 1)
            gather(1).start()
            gather(0).wait()
            writeout(0, c0).start()

            @pl.when(p + 1 < n_pairs)
            def _():
                load_idx(0, c0 + 2)
                writeout(0, c0).wait()
                gather(0).start()

            gather(1).wait()
            writeout(1, c0 + 1).start()

        writeout(0, 2 * n_pairs - 2).wait()
        writeout(1, 2 * n_pairs - 1).wait()

    return gather_kernel(table, idx)


def _pad_cols(w, width):
    return jnp.pad(w, ((0, 0), (0, width - w.shape[1])))


def _prepare_layer(g_norm_mix, w_in, b_gate, g_cq, w_uq, g_ckv, w_ukv, g_qn, g_qr, g_kn, g_kr,
                   w_a2, b_a, g_gla_out, w_out, g_norm_ffn, w_pq, sub_keys, u_emb, v_emb):
    d = w_in.shape[0]
    q_rank, kv_rank = g_cq.shape[0], g_ckv.shape[0]
    nope, rope = g_qn.shape[0], g_qr.shape[0]
    mla_heads = w_uq.shape[1] // (nope + rope)
    mla_v = w_ukv.shape[1] // mla_heads - nope
    gate_rank, gla_dk_all = w_a2.shape
    gla_dv = g_gla_out.shape[0]
    gla_heads = d // gla_dv
    gla_dk = gla_dk_all // gla_heads
    peer_heads, _, nkeys, half = sub_keys.shape
    assert nope == LANES and mla_v == LANES and rope <= LANES and rope % 2 == 0
    assert mla_heads * mla_v == d and gla_heads * gla_dv == d
    assert rope + gate_rank <= LANES and nkeys == LANES and half == LANES

    widths = (q_rank, kv_rank, rope, gla_dk_all, gla_dk_all, d, gate_rank, d, d, d)
    offs = [0]
    for wd in widths:
        offs.append(offs[-1] + wd)
    assert offs[-1] == w_in.shape[1]
    seg = lambda i: w_in[:, offs[i]:offs[i + 1]]
    w_main = jnp.concatenate([seg(5), seg(7), seg(8), seg(9), seg(3), seg(4), seg(0), seg(1)], axis=1).astype(BF16)
    w_small = _pad_cols(jnp.concatenate([seg(2), seg(6)], axis=1), LANES).astype(BF16)
    cq_off = 4 * d + 2 * gla_dk_all
    assert cq_off % q_rank == 0 and (cq_off + q_rank) % kv_rank == 0
    inv_freq = ROPE_THETA ** (-jnp.arange(0, rope, 2, dtype=F32) / rope)
    scale = (nope + rope) ** -0.5
    return dict(
        dims=dict(q_rank=q_rank, kv_rank=kv_rank, nope=nope, rope=rope, mla_heads=mla_heads, mla_v=mla_v,
                  gla_heads=gla_heads, gla_dk=gla_dk, gla_dv=gla_dv, peer_heads=peer_heads, nkeys=nkeys, half=half,
                  v_blk=0, og_blk=d // gla_dv, ga_blk=2, gb_blk=3, q_blk=4 * d // gla_dk,
                  k_blk=4 * d // gla_dk + gla_heads, cq_blk=cq_off // q_rank, ckv_blk=(cq_off + q_rank) // kv_rank),
        g_norm_mix=g_norm_mix[None, :], w_main=w_main, w_small=w_small,
        invf=_pad_cols(jnp.concatenate([inv_freq, inv_freq])[None, :], LANES),
        gq=_pad_cols(jnp.concatenate([g_qn, g_qr])[None, :] * scale, nope + LANES),
        gkr=_pad_cols(g_kr[None, :], LANES), g_cq=g_cq[None, :], g_ckv=g_ckv[None, :], g_kn=g_kn[None, :],
        w_uq=jnp.pad(w_uq.reshape(q_rank, mla_heads, nope + rope),
                     ((0, 0), (0, 0), (0, LANES - rope))).reshape(q_rank, -1).astype(BF16),
        w_ukv=w_ukv.astype(BF16),
        w2p=jnp.zeros((LANES, gla_dk_all), F32).at[rope:rope + gate_rank].set(w_a2).astype(BF16),
        b_a=b_a[None, :], g_on=g_gla_out[None, :], b_gate=b_gate, w_out=w_out.astype(BF16),
        g_norm_ffn=g_norm_ffn[None, :], w_pq=w_pq.astype(BF16),
        sk2d=sub_keys.reshape(peer_heads * 2 * nkeys, half).astype(BF16),
        uv=_pack_expert_table(u_emb, v_emb),
    )


def _mixers_and_route(x2, pos, w, *, batch, seq, tiles):
    t = x2.shape[0]
    dm = w["dims"]
    p, small = _in_proj(x2, w["g_norm_mix"], w["w_main"], w["w_small"], tm=tiles["in_tm"], tn=tiles["in_tn"])
    q = _mla_q(p, dm["cq_blk"], w["g_cq"], w["w_uq"], w["gq"], pos, w["invf"], heads=dm["mla_heads"],
               rank=dm["q_rank"], nope=dm["nope"], rope=dm["rope"], tm=tiles["mla_tm"])
    k, v = _mla_kv(p, dm["ckv_blk"], w["g_ckv"], w["w_ukv"], w["g_kn"], small, w["gkr"], pos, w["invf"],
                   heads=dm["mla_heads"], rank=dm["kv_rank"], nope=dm["nope"], rope=dm["rope"], dv=dm["mla_v"],
                   tm=tiles["mla_tm"])
    o_mla = _mla_attn(q, k, v, batch=batch, seq=seq, heads=dm["mla_heads"], dk=dm["nope"] + LANES, dv=dm["mla_v"],
                      tq=tiles["attn_tq"], tk=tiles["attn_tk"])
    o_gla = _gla(p, small, w["w2p"], w["b_a"], w["g_on"], batch=batch, seq=seq, heads=dm["gla_heads"],
                 dk=dm["gla_dk"], dv=dm["gla_dv"], q_blk=dm["q_blk"], k_blk=dm["k_blk"], v_blk=dm["v_blk"],
                 og_blk=dm["og_blk"])
    x2 = _out_proj(x2, p, o_mla, o_gla, w["b_gate"], w["w_out"], ga_blk=dm["ga_blk"], gb_blk=dm["gb_blk"],
                   tm=tiles["out_tm"], tn=tiles["out_tn"])
    h2, ids_t, gates_t = _peer_route(x2, w["g_norm_ffn"], w["w_pq"], w["sk2d"], heads=dm["peer_heads"],
                                     nkeys=dm["nkeys"], half=dm["half"], topk=PEER_TOPK, tm=tiles["route_tm"])
    picks = dm["peer_heads"] * PEER_TOPK
    gates3 = gates_t.reshape(picks, t // SUBLANES, SUBLANES).transpose(1, 0, 2)
    return x2, h2, ids_t.T, gates3


def _layer(x2, pos, w, *, batch, seq, tiles):
    t, d = x2.shape
    direct_tokens = tiles["mix_direct_tokens"]
    groups = len(direct_tokens)
    tg, bg = t // groups, batch // groups
    routed = [_mixers_and_route(x2[g * tg:(g + 1) * tg], pos[g * tg:(g + 1) * tg], w,
                                batch=bg, seq=seq, tiles=tiles) for g in range(groups)]
    tb = tiles["mix_tb"]
    mixed = []
    for g, (xg, hg, ids_tok, gates3) in enumerate(routed):
        if direct_tokens[g]:
            xg = _peer_mix_direct(ids_tok.reshape(tg // tb, -1), w["uv"], xg, hg, gates3,
                                  tb=tb, n_tokens=direct_tokens[g])
        mixed.append(xg)
    out = []
    for g, (_, hg, ids_tok, gates3) in enumerate(routed):
        xg = mixed[g]
        if direct_tokens[g] < tg:
            staged = _sc_gather_rows(w["uv"], ids_tok.reshape(-1), first_row=direct_tokens[g] * ids_tok.shape[1],
                                     chunk=tiles["sc_chunk"])
            xg = _peer_mix_staged(staged, xg, hg, gates3, tb=tiles["mix_staged_tb"], first_token=direct_tokens[g])
        out.append(xg)
    return jnp.concatenate(out, axis=0)


_TILES = dict(in_tm=2048, in_tn=512, mla_tm=2048, attn_tq=1024, attn_tk=1024,
              out_tm=1024, out_tn=512, route_tm=256, mix_tb=64,
              mix_direct_tokens=(320,) * 8, mix_staged_tb=16, sc_chunk=16)


def kernel(x, positions, g_norm_mix, w_in, b_gate, g_cq, w_uq, g_ckv, w_ukv, g_qn, g_qr, g_kn, g_kr,
           w_a2, b_a, g_gla_out, w_out, g_norm_ffn, w_pq, sub_keys, u_emb, v_emb, tiles=None):
    tiles = _TILES if tiles is None else tiles
    batch, seq, d = x.shape
    x2 = x.reshape(batch * seq, d)
    pos = positions.reshape(batch * seq, 1)
    for l in range(g_norm_mix.shape[0]):
        w = _prepare_layer(g_norm_mix[l], w_in[l], b_gate[l], g_cq[l], w_uq[l], g_ckv[l], w_ukv[l], g_qn[l],
                           g_qr[l], g_kn[l], g_kr[l], w_a2[l], b_a[l], g_gla_out[l], w_out[l], g_norm_ffn[l],
                           w_pq[l], sub_keys[l], u_emb[l], v_emb[l])
        x2 = _layer(x2, pos, w, batch=batch, seq=seq, tiles=tiles)
    return x2.reshape(batch, seq, d)
```

```python
import functools

import jax
import jax.numpy as jnp
from jax import lax
from jax.experimental import pallas as pl
from jax.experimental.pallas import tpu as pltpu
from jax.experimental.pallas import tpu_sc as plsc

EPS = 1e-6
ROPE_THETA = 10000.0
GLA_TAU = 16.0
GLA_CHUNK = 64
PEER_TOPK = 16

LANES = 128
SUBLANES = 8
VMEM_LIMIT_BYTES = 56 * 1024 * 1024
MIX_SLOTS = 4
SC_CORES = 2
SC_SUBCORES = 16

F32 = jnp.float32
BF16 = jnp.bfloat16
NEG_INF = float("-inf")


def _params(semantics):
    return pltpu.CompilerParams(dimension_semantics=semantics, vmem_limit_bytes=VMEM_LIMIT_BYTES)


def _rms(x, gain, n=None):
    ss = jnp.sum(x * x, axis=-1, keepdims=True)
    n = x.shape[-1] if n is None else n
    return x * lax.rsqrt(ss * (1.0 / n) + EPS) * gain


def _gelu_exact(x):
    return 0.5 * x * (1.0 + lax.erf(x * (0.5 ** 0.5)))


def _dot(a, b):
    return jnp.dot(a, b, preferred_element_type=F32)


def _dot_nt(a, b):
    return lax.dot_general(a, b, (((1,), (1,)), ((), ())), preferred_element_type=F32)


def _dot_tn(a, b):
    return lax.dot_general(a, b, (((0,), (0,)), ((), ())), preferred_element_type=F32)


def _in_proj_kernel(x_ref, g_ref, w_ref, ws_ref, p_ref, ps_ref, h_scr):
    @pl.when(pl.program_id(1) == 0)
    def _():
        h = _rms(x_ref[...], g_ref[...]).astype(BF16)
        h_scr[...] = h
        ps_ref[...] = _dot(h, ws_ref[...])

    p_ref[...] = _dot(h_scr[...], w_ref[...]).astype(p_ref.dtype)


def _in_proj(x, g, w_main, w_small, *, tm, tn):
    t, d = x.shape
    n = w_main.shape[1]
    x_mode = pl.Buffered(1) if t == tm else None
    return pl.pallas_call(
        _in_proj_kernel,
        grid=(t // tm, n // tn),
        in_specs=[
            pl.BlockSpec((tm, d), lambda i, j: (i, 0), pipeline_mode=x_mode),
            pl.BlockSpec((1, d), lambda i, j: (0, 0)),
            pl.BlockSpec((d, tn), lambda i, j: (0, j)),
            pl.BlockSpec((d, LANES), lambda i, j: (0, 0)),
        ],
        out_specs=[
            pl.BlockSpec((tm, tn), lambda i, j: (i, j)),
            pl.BlockSpec((tm, LANES), lambda i, j: (i, 0)),
        ],
        out_shape=[
            jax.ShapeDtypeStruct((t, n), BF16),
            jax.ShapeDtypeStruct((t, LANES), F32),
        ],
        scratch_shapes=[pltpu.VMEM((tm, d), BF16)],
        compiler_params=_params(("parallel", "arbitrary")),
    )(x, g, w_main, w_small)


def _rope_tables(pos_ref, invf_ref, rope):
    ang = pos_ref[...].astype(F32) * invf_ref[...]
    cos, sin = jnp.cos(ang), jnp.sin(ang)
    lane = lax.broadcasted_iota(jnp.int32, ang.shape, 1)
    half = rope // 2
    c = jnp.where(lane < rope, cos, 0.0)
    s_lo = jnp.where(lane < half, -sin, 0.0)
    s_hi = jnp.where(lane < half, 0.0, jnp.where(lane < rope, sin, 0.0))
    return c, s_lo, s_hi


def _apply_rope(pe, c, s_lo, s_hi, rope):
    half = rope // 2
    from_hi = pltpu.roll(pe, LANES - half, 1)
    from_lo = pltpu.roll(pe, half, 1)
    return pe * c + from_hi * s_lo + from_lo * s_hi


def _mla_q_kernel(cq_ref, gcq_ref, w_ref, gq_ref, pos_ref, invf_ref, q_ref,
                  h_scr, c_scr, slo_scr, shi_scr, *, nope, rope):
    @pl.when(pl.program_id(1) == 0)
    def _():
        h_scr[...] = _rms(cq_ref[...].astype(F32), gcq_ref[...]).astype(BF16)
        c, s_lo, s_hi = _rope_tables(pos_ref, invf_ref, rope)
        c_scr[...] = c
        slo_scr[...] = s_lo
        shi_scr[...] = s_hi

    y = _dot(h_scr[...], w_ref[...])
    g = gq_ref[...]
    qn = _rms(y[:, :nope], g[:, :nope])
    pe = _rms(y[:, nope:], g[:, nope:], n=rope)
    pe = _apply_rope(pe, c_scr[...], slo_scr[...], shi_scr[...], rope)
    q_ref[:, :nope] = qn.astype(q_ref.dtype)
    q_ref[:, nope:] = pe.astype(q_ref.dtype)


def _mla_q(p, cq_blk, g_cq, w_uq_p, gq, pos, invf, *, heads, rank, nope, rope, tm):
    t = p.shape[0]
    hw = nope + LANES
    return pl.pallas_call(
        functools.partial(_mla_q_kernel, nope=nope, rope=rope),
        grid=(t // tm, heads),
        in_specs=[
            pl.BlockSpec((tm, rank), lambda i, j: (i, cq_blk)),
            pl.BlockSpec((1, rank), lambda i, j: (0, 0)),
            pl.BlockSpec((rank, hw), lambda i, j: (0, j)),
            pl.BlockSpec((1, hw), lambda i, j: (0, 0)),
            pl.BlockSpec((tm, 1), lambda i, j: (i, 0)),
            pl.BlockSpec((1, LANES), lambda i, j: (0, 0)),
        ],
        out_specs=pl.BlockSpec((tm, hw), lambda i, j: (i, j)),
        out_shape=jax.ShapeDtypeStruct((t, heads * hw), BF16),
        scratch_shapes=[
            pltpu.VMEM((tm, rank), BF16),
            pltpu.VMEM((tm, LANES), F32),
            pltpu.VMEM((tm, LANES), F32),
            pltpu.VMEM((tm, LANES), F32),
        ],
        compiler_params=_params(("parallel", "arbitrary")),
    )(p, g_cq, w_uq_p, gq, pos, invf)


def _mla_kv_kernel(ckv_ref, gckv_ref, w_ref, gkn_ref, small_ref, gkr_ref, pos_ref, invf_ref,
                   k_ref, v_ref, h_scr, kpe_scr, *, nope, rope):
    @pl.when(pl.program_id(1) == 0)
    def _():
        h_scr[...] = _rms(ckv_ref[...].astype(F32), gckv_ref[...]).astype(BF16)
        c, s_lo, s_hi = _rope_tables(pos_ref, invf_ref, rope)
        sm = small_ref[...]
        lane = lax.broadcasted_iota(jnp.int32, sm.shape, 1)
        pe = _rms(jnp.where(lane < rope, sm, 0.0), gkr_ref[...], n=rope)
        kpe_scr[...] = _apply_rope(pe, c, s_lo, s_hi, rope).astype(BF16)

    y = _dot(h_scr[...], w_ref[...])
    k_ref[:, :nope] = _rms(y[:, :nope], gkn_ref[...]).astype(k_ref.dtype)
    k_ref[:, nope:] = kpe_scr[...]
    v_ref[...] = y[:, nope:].astype(v_ref.dtype)


def _mla_kv(p, ckv_blk, g_ckv, w_ukv, g_kn, small, gkr, pos, invf, *, heads, rank, nope, rope, dv, tm):
    t = p.shape[0]
    kw = nope + LANES
    return pl.pallas_call(
        functools.partial(_mla_kv_kernel, nope=nope, rope=rope),
        grid=(t // tm, heads),
        in_specs=[
            pl.BlockSpec((tm, rank), lambda i, j: (i, ckv_blk)),
            pl.BlockSpec((1, rank), lambda i, j: (0, 0)),
            pl.BlockSpec((rank, nope + dv), lambda i, j: (0, j)),
            pl.BlockSpec((1, nope), lambda i, j: (0, 0)),
            pl.BlockSpec((tm, LANES), lambda i, j: (i, 0)),
            pl.BlockSpec((1, LANES), lambda i, j: (0, 0)),
            pl.BlockSpec((tm, 1), lambda i, j: (i, 0)),
            pl.BlockSpec((1, LANES), lambda i, j: (0, 0)),
        ],
        out_specs=[
            pl.BlockSpec((tm, kw), lambda i, j: (i, j)),
            pl.BlockSpec((tm, dv), lambda i, j: (i, j)),
        ],
        out_shape=[
            jax.ShapeDtypeStruct((t, heads * kw), BF16),
            jax.ShapeDtypeStruct((t, heads * dv), BF16),
        ],
        scratch_shapes=[pltpu.VMEM((tm, rank), BF16), pltpu.VMEM((tm, LANES), BF16)],
        compiler_params=_params(("parallel", "arbitrary")),
    )(p, g_ckv, w_ukv, g_kn, small, gkr, pos, invf)


def _attn_kernel(q_ref, k_ref, v_ref, o_ref, *, tq, tk):
    qi = pl.program_id(2)
    q = q_ref[...]
    dv = v_ref.shape[1]
    row = qi * tq + lax.broadcasted_iota(jnp.int32, (tq, tk), 0)
    col0 = lax.broadcasted_iota(jnp.int32, (tq, tk), 1)

    def body(kb, carry, masked):
        m, l, acc = carry
        k0 = pl.multiple_of(kb * tk, tk)
        s = _dot_nt(q, k_ref[pl.ds(k0, tk), :])
        if masked:
            s = jnp.where(col0 + k0 <= row, s, NEG_INF)
        m_new = jnp.maximum(m, jnp.max(s, axis=-1, keepdims=True))
        alpha = jnp.exp(m - m_new)
        pr = jnp.exp(s - m_new)
        l = alpha * l + jnp.sum(pr, axis=-1, keepdims=True)
        acc = alpha * acc + _dot(pr.astype(BF16), v_ref[pl.ds(k0, tk), :])
        return m_new, l, acc

    init = (jnp.full((tq, 1), NEG_INF, F32), jnp.zeros((tq, 1), F32), jnp.zeros((tq, dv), F32))
    n_below = (qi * tq) // tk
    nkb = ((qi + 1) * tq + tk - 1) // tk
    carry = lax.fori_loop(0, n_below, functools.partial(body, masked=False), init)
    _, l, acc = lax.fori_loop(n_below, nkb, functools.partial(body, masked=True), carry)
    o_ref[...] = (acc / l).astype(o_ref.dtype)


def _mla_attn(q, k, v, *, batch, seq, heads, dk, dv, tq, tk):
    t = q.shape[0]
    nq = seq // tq
    return pl.pallas_call(
        functools.partial(_attn_kernel, tq=tq, tk=tk),
        grid=(batch, heads, nq),
        in_specs=[
            pl.BlockSpec((tq, dk), lambda b, h, i: (b * nq + i, h)),
            pl.BlockSpec((seq, dk), lambda b, h, i: (b, h)),
            pl.BlockSpec((seq, dv), lambda b, h, i: (b, h)),
        ],
        out_specs=pl.BlockSpec((tq, dv), lambda b, h, i: (b * nq + i, h)),
        out_shape=jax.ShapeDtypeStruct((t, heads * dv), BF16),
        compiler_params=_params(("parallel", "parallel", "arbitrary")),
    )(q, k, v)


def _gla_kernel(q_ref, k_ref, v_ref, og_ref, small_ref, w2_ref, ba_ref, gon_ref, o_ref, st_scr,
                *, seq, dk, dv, chunk):
    c = chunk
    st_scr[...] = jnp.zeros_like(st_scr)
    r_i = lax.broadcasted_iota(jnp.int32, (c, c), 0)
    c_i = lax.broadcasted_iota(jnp.int32, (c, c), 1)
    tri = jnp.where(c_i <= r_i, 1.0, 0.0).astype(BF16)
    row_id = lax.broadcasted_iota(jnp.int32, (c, 1), 0)
    w2 = w2_ref[...]
    ba = ba_ref[...]
    gon = gon_ref[...]
    q_scale = dk ** -0.5

    def chunk_step(ci, carry):
        r0 = pl.multiple_of(ci * c, c)
        qc = q_ref[pl.ds(r0, c), :].astype(F32) * q_scale
        kc = k_ref[pl.ds(r0, c), :].astype(F32)
        vc = v_ref[pl.ds(r0, c), :]
        z = _dot(small_ref[pl.ds(r0, c), :].astype(BF16), w2) + ba
        la = jax.nn.log_sigmoid(z) * (1.0 / GLA_TAU)
        hi = la.astype(BF16)
        r1 = la - hi.astype(F32)
        mid = r1.astype(BF16)
        lo = (r1 - mid.astype(F32)).astype(BF16)
        b = _dot(tri, hi) + _dot(tri, mid) + _dot(tri, lo)

        st = st_scr[...]
        inter = _dot_nt((qc * jnp.exp(b)).astype(BF16), st.astype(BF16))

        att = jnp.zeros((c, c), F32)
        for j in range(c):
            lo_r = (j // SUBLANES) * SUBLANES
            d = b[lo_r:, :] - b[j:j + 1, :]
            e = jnp.exp(jnp.where(row_id[lo_r:, :] >= j, d, NEG_INF))
            col = jnp.sum(qc[lo_r:, :] * kc[j:j + 1, :] * e, axis=-1, keepdims=True)
            if lo_r:
                col = jnp.concatenate([jnp.zeros((lo_r, 1), F32), col], axis=0)
            att = jnp.where(c_i == j, col, att)
        o = inter + _dot(att.astype(BF16), vc)

        b_last = b[c - 1:c, :]
        k_dec = (kc * jnp.exp(b_last - b)).astype(BF16)
        st_scr[...] = st * jnp.exp(b_last) + _dot_tn(vc, k_dec)

        og = og_ref[pl.ds(r0, c), :].astype(F32)
        out = _rms(o, gon) * (og * jax.nn.sigmoid(og))
        o_ref[pl.ds(r0, c), :] = out.astype(o_ref.dtype)
        return carry

    lax.fori_loop(0, seq // c, chunk_step, 0)


def _gla(p, small, w2p, b_a, g_on, *, batch, seq, heads, dk, dv, q_blk, k_blk, v_blk, og_blk):
    t = p.shape[0]
    return pl.pallas_call(
        functools.partial(_gla_kernel, seq=seq, dk=dk, dv=dv, chunk=GLA_CHUNK),
        grid=(batch, heads),
        in_specs=[
            pl.BlockSpec((seq, dk), lambda b, h: (b, q_blk + h)),
            pl.BlockSpec((seq, dk), lambda b, h: (b, k_blk + h)),
            pl.BlockSpec((seq, dv), lambda b, h: (b, v_blk + h)),
            pl.BlockSpec((seq, dv), lambda b, h: (b, og_blk + h)),
            pl.BlockSpec((seq, LANES), lambda b, h: (b, 0)),
            pl.BlockSpec((LANES, dk), lambda b, h: (0, h)),
            pl.BlockSpec((1, dk), lambda b, h: (0, h)),
            pl.BlockSpec((1, dv), lambda b, h: (0, 0)),
        ],
        out_specs=pl.BlockSpec((seq, dv), lambda b, h: (b, h)),
        out_shape=jax.ShapeDtypeStruct((t, heads * dv), BF16),
        scratch_shapes=[pltpu.VMEM((dv, dk), F32)],
        compiler_params=_params(("parallel", "parallel")),
    )(p, p, p, p, small, w2p, b_a, g_on)


def _out_proj_kernel(x_ref, ga_ref, gb_ref, oa_ref, ob_ref, bg_ref, w_ref, o_ref, m_scr):
    @pl.when(pl.program_id(1) == 0)
    def _():
        bg = bg_ref[...]
        sa = jax.nn.sigmoid(ga_ref[...].astype(F32) + bg[0:1, :])
        sb = jax.nn.sigmoid(gb_ref[...].astype(F32) + bg[1:2, :])
        m_scr[...] = (sa * oa_ref[...].astype(F32) + sb * ob_ref[...].astype(F32)).astype(BF16)

    o_ref[...] = x_ref[...] + _dot(m_scr[...], w_ref[...])


def _out_proj(x, p, o_mla, o_gla, b_gate, w_out, *, ga_blk, gb_blk, tm, tn):
    t, d = x.shape
    return pl.pallas_call(
        _out_proj_kernel,
        grid=(t // tm, d // tn),
        in_specs=[
            pl.BlockSpec((tm, tn), lambda i, j: (i, j)),
            pl.BlockSpec((tm, d), lambda i, j: (i, ga_blk)),
            pl.BlockSpec((tm, d), lambda i, j: (i, gb_blk)),
            pl.BlockSpec((tm, d), lambda i, j: (i, 0)),
            pl.BlockSpec((tm, d), lambda i, j: (i, 0)),
            pl.BlockSpec((2, d), lambda i, j: (0, 0)),
            pl.BlockSpec((d, tn), lambda i, j: (0, j)),
        ],
        out_specs=pl.BlockSpec((tm, tn), lambda i, j: (i, j)),
        out_shape=jax.ShapeDtypeStruct((t, d), F32),
        scratch_shapes=[pltpu.VMEM((tm, d), BF16)],
        compiler_params=_params(("parallel", "arbitrary")),
    )(x, p, p, o_mla, o_gla, b_gate, w_out)


def _topk_rows(s, k, payload=None):
    n = s.shape[0]
    iota = lax.broadcasted_iota(jnp.int32, s.shape, 0).astype(F32)
    kiota = lax.broadcasted_iota(jnp.int32, (k, s.shape[1]), 0)
    vals = jnp.zeros((k, s.shape[1]), F32)
    picks = jnp.zeros((k, s.shape[1]), F32)
    for r in range(k):
        m = jnp.max(s, axis=0, keepdims=True)
        idx = jnp.min(jnp.where(s == m, iota, float(n)), axis=0, keepdims=True)
        hit = iota == idx
        if payload is None:
            pick = idx
        else:
            pick = jnp.sum(jnp.where(hit, payload, 0.0), axis=0, keepdims=True)
        vals = jnp.where(kiota == r, m, vals)
        picks = jnp.where(kiota == r, pick, picks)
        s = jnp.where(hit, NEG_INF, s)
    return vals, picks


def _peer_route_kernel(x_ref, g_ref, w_ref, sk_ref, h_ref, ids_ref, gates_ref, *, heads, nkeys, half, topk):
    h = _rms(x_ref[...], g_ref[...]).astype(BF16)
    h_ref[...] = h
    qf = _dot(h, w_ref[...])
    for hd in range(heads):
        tops = []
        for part in range(2):
            o = (hd * 2 + part) * half
            qh = qf[:, o:o + half].astype(BF16)
            keys = sk_ref[(hd * 2 + part) * nkeys:(hd * 2 + part + 1) * nkeys, :]
            tops.append(_topk_rows(_dot_nt(keys, qh), topk))
        (s1, i1), (s2, i2) = tops
        assert topk == 2 * SUBLANES
        hs = SUBLANES
        pair_s = [s1[0:1, :] + s2] + [s1[a:a + 1, :] + s2[:hs, :] for a in range(1, hs)] + [s1[hs:, :] + s2[0:1, :]]
        pair_id = ([i1[0:1, :] * float(nkeys) + i2]
                   + [i1[a:a + 1, :] * float(nkeys) + i2[:hs, :] for a in range(1, hs)]
                   + [i1[hs:, :] * float(nkeys) + i2[0:1, :]])
        best_s, best_id = _topk_rows(jnp.concatenate(pair_s, axis=0), topk,
                                     payload=jnp.concatenate(pair_id, axis=0))
        e = jnp.exp(best_s - best_s[0:1, :])
        gate = e / jnp.sum(e, axis=0, keepdims=True)
        ids_ref[hd * topk:(hd + 1) * topk, :] = best_id.astype(jnp.int32)
        gates_ref[hd * topk:(hd + 1) * topk, :] = gate


def _peer_route(x, g, w_pq, sk2d, *, heads, nkeys, half, topk, tm):
    t, d = x.shape
    dq = w_pq.shape[1]
    return pl.pallas_call(
        functools.partial(_peer_route_kernel, heads=heads, nkeys=nkeys, half=half, topk=topk),
        grid=(t // tm,),
        in_specs=[
            pl.BlockSpec((tm, d), lambda i: (i, 0)),
            pl.BlockSpec((1, d), lambda i: (0, 0)),
            pl.BlockSpec((d, dq), lambda i: (0, 0)),
            pl.BlockSpec((heads * 2 * nkeys, half), lambda i: (0, 0)),
        ],
        out_specs=[
            pl.BlockSpec((tm, d), lambda i: (i, 0)),
            pl.BlockSpec((heads * topk, tm), lambda i: (0, i)),
            pl.BlockSpec((heads * topk, tm), lambda i: (0, i)),
        ],
        out_shape=[
            jax.ShapeDtypeStruct((t, d), BF16),
            jax.ShapeDtypeStruct((heads * topk, t), jnp.int32),
            jax.ShapeDtypeStruct((heads * topk, t), F32),
        ],
        compiler_params=_params(("parallel",)),
    )(x, g, w_pq, sk2d)


def _pack_expert_table(u_emb, v_emb):
    def pack(w):
        bits = lax.bitcast_convert_type(w.astype(BF16), jnp.uint16).astype(jnp.uint32)
        half = w.shape[1] // 2
        return bits[:, :half] | (bits[:, half:] << 16)

    return lax.bitcast_convert_type(jnp.concatenate([pack(u_emb), pack(v_emb)], axis=1), jnp.int32)


def _unpack_words(w):
    lo = lax.bitcast_convert_type(w << 16, F32)
    hi = lax.bitcast_convert_type(w & jnp.int32(-65536), F32)
    return lo, hi


def _expert_mix(words, h, gate, d):
    half = d // 2
    u_lo, u_hi = _unpack_words(words[:, :half])
    act = jnp.sum(u_lo * h[:, :half] + u_hi * h[:, half:], axis=-1, keepdims=True)
    w = gate * _gelu_exact(act)
    v_lo, v_hi = _unpack_words(words[:, half:])
    return jnp.concatenate([jnp.sum(v_lo * w, axis=0, keepdims=True),
                            jnp.sum(v_hi * w, axis=0, keepdims=True)], axis=1)


def _peer_mix_kernel(ids_hbm, uv_hbm, rows_ref, x_ref, h_ref, gates_ref, o_ref, ids_smem, buf, ids_sems, row_sems,
                     *, td, ts, picks, d, n_steps):
    step = pl.program_id(0)
    par = lax.rem(step, 2)
    ahead = MIX_SLOTS - 1
    assert td % MIX_SLOTS == 0 and td > ahead and (td + ts) % SUBLANES == 0

    def ids_copy(s, row):
        return pltpu.make_async_copy(ids_hbm.at[s], ids_smem.at[row], ids_sems.at[row])

    def issue(row, tok, slot):
        for j in range(picks):
            eid = ids_smem[row, tok * picks + j]
            pltpu.make_async_copy(uv_hbm.at[pl.ds(eid, 1), :], buf.at[slot, pl.ds(j, 1), :],
                                  row_sems.at[slot]).start()

    def wait_rows(slot):
        pltpu.make_async_copy(uv_hbm.at[pl.ds(0, picks), :], buf.at[slot], row_sems.at[slot]).wait()

    @pl.when(step == 0)
    def _():
        first = ids_copy(0, 0)
        first.start()
        first.wait()
        for u in range(ahead):
            issue(0, u, u)

    @pl.when(step + 1 < n_steps)
    def _():
        ids_copy(step + 1, 1 - par).start()

    loaded = {}

    def finish(pos, words):
        g, lane = divmod(pos, SUBLANES)
        if g not in loaded:
            loaded[g] = (gates_ref[g], h_ref[g * SUBLANES:(g + 1) * SUBLANES, :].astype(F32))
        gates, h8 = loaded[g]
        row = _expert_mix(words, h8[lane:lane + 1, :], gates[:, lane:lane + 1], d)
        o_ref[pos:pos + 1, :] = x_ref[pos:pos + 1, :] + row

    staged_done = 0
    for u in range(td):
        nxt = u + ahead
        if nxt < td:
            issue(par, nxt, nxt % MIX_SLOTS)
        else:
            @pl.when(step + 1 < n_steps)
            def _():
                if nxt == td:
                    ids_copy(step + 1, 1 - par).wait()
                issue(1 - par, nxt - td, nxt % MIX_SLOTS)

        staged_upto = (u + 1) * ts // td
        for k in range(staged_done, staged_upto):
            finish(td + k, rows_ref[k * picks:(k + 1) * picks, :])
        staged_done = staged_upto
        wait_rows(u % MIX_SLOTS)
        finish(u, buf[u % MIX_SLOTS])


def _peer_mix(ids_direct, uv, staged, x, h, gates3, *, td, ts):
    t, d = x.shape
    picks = gates3.shape[1]
    n_steps = t // (td + ts)
    assert staged.shape[0] == n_steps * ts * picks and ids_direct.shape == (n_steps, td * picks)
    return pl.pallas_call(
        functools.partial(_peer_mix_kernel, td=td, ts=ts, picks=picks, d=d, n_steps=n_steps),
        grid=(n_steps,),
        input_output_aliases={3: 0},
        in_specs=[
            pl.BlockSpec(memory_space=pl.ANY),
            pl.BlockSpec(memory_space=pl.ANY),
            pl.BlockSpec((ts * picks, d), lambda i: (i, 0)),
            pl.BlockSpec((td + ts, d), lambda i: (i, 0)),
            pl.BlockSpec((td + ts, d), lambda i: (i, 0)),
            pl.BlockSpec(((td + ts) // SUBLANES, picks, SUBLANES), lambda i: (i, 0, 0)),
        ],
        out_specs=pl.BlockSpec((td + ts, d), lambda i: (i, 0)),
        out_shape=jax.ShapeDtypeStruct((t, d), F32),
        scratch_shapes=[
            pltpu.SMEM((2, td * picks), jnp.int32),
            pltpu.VMEM((MIX_SLOTS, picks, d), jnp.int32),
            pltpu.SemaphoreType.DMA((2,)),
            pltpu.SemaphoreType.DMA((MIX_SLOTS,)),
        ],
        compiler_params=_params(("arbitrary",)),
    )(ids_direct, uv, staged, x, h, gates3)


def _sc_gather_rows(table, idx, *, chunk):
    n_rows = idx.shape[0]
    d = table.shape[1]
    workers = SC_CORES * SC_SUBCORES
    assert n_rows % (workers * 2 * chunk) == 0 and chunk % SUBLANES == 0 and chunk <= LANES
    rows_per_worker = n_rows // workers
    n_pairs = rows_per_worker // (2 * chunk)
    mesh = plsc.VectorSubcoreMesh(core_axis_name="c", subcore_axis_name="s")

    @functools.partial(
        pl.kernel, mesh=mesh,
        out_type=jax.ShapeDtypeStruct((n_rows, d), table.dtype),
        scratch_types=[
            pltpu.VMEM((chunk,), jnp.int32), pltpu.VMEM((chunk,), jnp.int32),
            pltpu.VMEM((chunk, d), table.dtype), pltpu.VMEM((chunk, d), table.dtype),
            pltpu.SemaphoreType.DMA, pltpu.SemaphoreType.DMA,
            pltpu.SemaphoreType.DMA, pltpu.SemaphoreType.DMA,
        ],
    )
    def gather_kernel(table_hbm, idx_hbm, out_hbm, idx0, idx1, rows0, rows1, gsem0, gsem1, wsem0, wsem1):
        idx_v, rows_v, gsem, wsem = (idx0, idx1), (rows0, rows1), (gsem0, gsem1), (wsem0, wsem1)
        worker = lax.axis_index("s") * SC_CORES + lax.axis_index("c")
        base = worker * rows_per_worker

        def out_rows(c):
            return pl.ds(pl.multiple_of(base + c * chunk, SUBLANES), chunk)

        def load_idx(slot, c):
            pltpu.sync_copy(idx_hbm.at[out_rows(c)], idx_v[slot])

        def gather(slot):
            return pltpu.make_async_copy(table_hbm.at[idx_v[slot]], rows_v[slot], gsem[slot])

        def writeout(slot, c):
            return pltpu.make_async_copy(rows_v[slot], out_hbm.at[out_rows(c)], wsem[slot])

        load_idx(0, 0)
        gather(0).start()

        @pl.loop(0, n_pairs)
        def _(p):
            c0 = 2 * p

            @pl.when(p > 0)
            def _():
                writeout(1, c0 - 1).wait()

            load_idx(1, c0 + 1)
            gather(1).start()
            gather(0).wait()
            writeout(0, c0).start()

            @pl.when(p + 1 < n_pairs)
            def _():
                load_idx(0, c0 + 2)
                writeout(0, c0).wait()
                gather(0).start()

            gather(1).wait()
            writeout(1, c0 + 1).start()

        writeout(0, 2 * n_pairs - 2).wait()
        writeout(1, 2 * n_pairs - 1).wait()

    return gather_kernel(table, idx)


def _pad_cols(w, width):
    return jnp.pad(w, ((0, 0), (0, width - w.shape[1])))


def _prepare_layer(g_norm_mix, w_in, b_gate, g_cq, w_uq, g_ckv, w_ukv, g_qn, g_qr, g_kn, g_kr,
                   w_a2, b_a, g_gla_out, w_out, g_norm_ffn, w_pq, sub_keys, u_emb, v_emb):
    d = w_in.shape[0]
    q_rank, kv_rank = g_cq.shape[0], g_ckv.shape[0]
    nope, rope = g_qn.shape[0], g_qr.shape[0]
    mla_heads = w_uq.shape[1] // (nope + rope)
    mla_v = w_ukv.shape[1] // mla_heads - nope
    gate_rank, gla_dk_all = w_a2.shape
    gla_dv = g_gla_out.shape[0]
    gla_heads = d // gla_dv
    gla_dk = gla_dk_all // gla_heads
    peer_heads, _, nkeys, half = sub_keys.shape
    assert nope == LANES and mla_v == LANES and rope <= LANES and rope % 2 == 0
    assert mla_heads * mla_v == d and gla_heads * gla_dv == d
    assert rope + gate_rank <= LANES and nkeys == LANES and half == LANES

    widths = (q_rank, kv_rank, rope, gla_dk_all, gla_dk_all, d, gate_rank, d, d, d)
    offs = [0]
    for wd in widths:
        offs.append(offs[-1] + wd)
    assert offs[-1] == w_in.shape[1]
    seg = lambda i: w_in[:, offs[i]:offs[i + 1]]
    w_main = jnp.concatenate([seg(5), seg(7), seg(8), seg(9), seg(3), seg(4), seg(0), seg(1)], axis=1).astype(BF16)
    w_small = _pad_cols(jnp.concatenate([seg(2), seg(6)], axis=1), LANES).astype(BF16)
    cq_off = 4 * d + 2 * gla_dk_all
    assert cq_off % q_rank == 0 and (cq_off + q_rank) % kv_rank == 0
    inv_freq = ROPE_THETA ** (-jnp.arange(0, rope, 2, dtype=F32) / rope)
    scale = (nope + rope) ** -0.5
    return dict(
        dims=dict(q_rank=q_rank, kv_rank=kv_rank, nope=nope, rope=rope, mla_heads=mla_heads, mla_v=mla_v,
                  gla_heads=gla_heads, gla_dk=gla_dk, gla_dv=gla_dv, peer_heads=peer_heads, nkeys=nkeys, half=half,
                  v_blk=0, og_blk=d // gla_dv, ga_blk=2, gb_blk=3, q_blk=4 * d // gla_dk,
                  k_blk=4 * d // gla_dk + gla_heads, cq_blk=cq_off // q_rank, ckv_blk=(cq_off + q_rank) // kv_rank),
        g_norm_mix=g_norm_mix[None, :], w_main=w_main, w_small=w_small,
        invf=_pad_cols(jnp.concatenate([inv_freq, inv_freq])[None, :], LANES),
        gq=_pad_cols(jnp.concatenate([g_qn, g_qr])[None, :] * scale, nope + LANES),
        gkr=_pad_cols(g_kr[None, :], LANES), g_cq=g_cq[None, :], g_ckv=g_ckv[None, :], g_kn=g_kn[None, :],
        w_uq=jnp.pad(w_uq.reshape(q_rank, mla_heads, nope + rope),
                     ((0, 0), (0, 0), (0, LANES - rope))).reshape(q_rank, -1).astype(BF16),
        w_ukv=w_ukv.astype(BF16),
        w2p=jnp.zeros((LANES, gla_dk_all), F32).at[rope:rope + gate_rank].set(w_a2).astype(BF16),
        b_a=b_a[None, :], g_on=g_gla_out[None, :], b_gate=b_gate, w_out=w_out.astype(BF16),
        g_norm_ffn=g_norm_ffn[None, :], w_pq=w_pq.astype(BF16),
        sk2d=sub_keys.reshape(peer_heads * 2 * nkeys, half).astype(BF16),
        uv=_pack_expert_table(u_emb, v_emb),
    )


def _mixers_and_route(x2, pos, w, *, batch, seq, tiles):
    t = x2.shape[0]
    dm = w["dims"]
    p, small = _in_proj(x2, w["g_norm_mix"], w["w_main"], w["w_small"], tm=tiles["in_tm"], tn=tiles["in_tn"])
    q = _mla_q(p, dm["cq_blk"], w["g_cq"], w["w_uq"], w["gq"], pos, w["invf"], heads=dm["mla_heads"],
               rank=dm["q_rank"], nope=dm["nope"], rope=dm["rope"], tm=tiles["mla_tm"])
    k, v = _mla_kv(p, dm["ckv_blk"], w["g_ckv"], w["w_ukv"], w["g_kn"], small, w["gkr"], pos, w["invf"],
                   heads=dm["mla_heads"], rank=dm["kv_rank"], nope=dm["nope"], rope=dm["rope"], dv=dm["mla_v"],
                   tm=tiles["mla_tm"])
    o_mla = _mla_attn(q, k, v, batch=batch, seq=seq, heads=dm["mla_heads"], dk=dm["nope"] + LANES, dv=dm["mla_v"],
                      tq=tiles["attn_tq"], tk=tiles["attn_tk"])
    o_gla = _gla(p, small, w["w2p"], w["b_a"], w["g_on"], batch=batch, seq=seq, heads=dm["gla_heads"],
                 dk=dm["gla_dk"], dv=dm["gla_dv"], q_blk=dm["q_blk"], k_blk=dm["k_blk"], v_blk=dm["v_blk"],
                 og_blk=dm["og_blk"])
    x2 = _out_proj(x2, p, o_mla, o_gla, w["b_gate"], w["w_out"], ga_blk=dm["ga_blk"], gb_blk=dm["gb_blk"],
                   tm=tiles["out_tm"], tn=tiles["out_tn"])
    h2, ids_t, gates_t = _peer_route(x2, w["g_norm_ffn"], w["w_pq"], w["sk2d"], heads=dm["peer_heads"],
                                     nkeys=dm["nkeys"], half=dm["half"], topk=PEER_TOPK, tm=tiles["route_tm"])
    picks = dm["peer_heads"] * PEER_TOPK
    gates3 = gates_t.reshape(picks, t // SUBLANES, SUBLANES).transpose(1, 0, 2)
    return x2, h2, ids_t.T, gates3


def _layer(x2, pos, w, *, batch, seq, tiles):
    t, d = x2.shape
    groups = tiles["groups"]
    tg, bg = t // groups, batch // groups
    routed = [_mixers_and_route(x2[g * tg:(g + 1) * tg], pos[g * tg:(g + 1) * tg], w,
                                batch=bg, seq=seq, tiles=tiles) for g in range(groups)]
    td, ts = tiles["mix_td"], tiles["mix_ts"]
    out = []
    for xg, hg, ids_tok, gates3 in routed:
        picks = ids_tok.shape[1]
        ids3 = ids_tok.reshape(tg // (td + ts), td + ts, picks)
        staged = _sc_gather_rows(w["uv"], ids3[:, td:].reshape(-1), chunk=tiles["sc_chunk"])
        out.append(_peer_mix(ids3[:, :td].reshape(-1, td * picks), w["uv"], staged, xg, hg, gates3, td=td, ts=ts))
    return jnp.concatenate(out, axis=0)


_TILES = dict(in_tm=2048, in_tn=512, mla_tm=2048, attn_tq=1024, attn_tk=1024,
              out_tm=1024, out_tn=512, route_tm=256, groups=8, mix_td=12, mix_ts=20, sc_chunk=16)


def kernel(x, positions, g_norm_mix, w_in, b_gate, g_cq, w_uq, g_ckv, w_ukv, g_qn, g_qr, g_kn, g_kr,
           w_a2, b_a, g_gla_out, w_out, g_norm_ffn, w_pq, sub_keys, u_emb, v_emb, tiles=None):
    tiles = _TILES if tiles is None else tiles
    batch, seq, d = x.shape
    x2 = x.reshape(batch * seq, d)
    pos = positions.reshape(batch * seq, 1)
    for l in range(g_norm_mix.shape[0]):
        w = _prepare_layer(g_norm_mix[l], w_in[l], b_gate[l], g_cq[l], w_uq[l], g_ckv[l], w_ukv[l], g_qn[l],
                           g_qr[l], g_kn[l], g_kr[l], w_a2[l], b_a[l], g_gla_out[l], w_out[l], g_norm_ffn[l],
                           w_pq[l], sub_keys[l], u_emb[l], v_emb[l])
        x2 = _layer(x2, pos, w, batch=batch, seq=seq, tiles=tiles)
    return x2.reshape(batch, seq, d)
```

```python
import functools

import jax
import jax.numpy as jnp
from jax import lax
from jax.experimental import pallas as pl
from jax.experimental.pallas import tpu as pltpu
from jax.experimental.pallas import tpu_sc as plsc

EPS = 1e-6
ROPE_THETA = 10000.0
GLA_TAU = 16.0
GLA_CHUNK = 64
PEER_TOPK = 16

LANES = 128
SUBLANES = 8
VMEM_LIMIT_BYTES = 56 * 1024 * 1024
MIX_SLOTS = 4
SC_CORES = 2
SC_SUBCORES = 16

F32 = jnp.float32
BF16 = jnp.bfloat16
NEG_INF = float("-inf")


def _params(semantics):
    return pltpu.CompilerParams(dimension_semantics=semantics, vmem_limit_bytes=VMEM_LIMIT_BYTES)


def _rms(x, gain, n=None):
    ss = jnp.sum(x * x, axis=-1, keepdims=True)
    n = x.shape[-1] if n is None else n
    return x * lax.rsqrt(ss * (1.0 / n) + EPS) * gain


def _gelu_exact(x):
    return 0.5 * x * (1.0 + lax.erf(x * (0.5 ** 0.5)))


def _dot(a, b):
    return jnp.dot(a, b, preferred_element_type=F32)


def _dot_nt(a, b):
    return lax.dot_general(a, b, (((1,), (1,)), ((), ())), preferred_element_type=F32)


def _dot_tn(a, b):
    return lax.dot_general(a, b, (((0,), (0,)), ((), ())), preferred_element_type=F32)


def _in_proj_kernel(x_ref, g_ref, w_ref, ws_ref, p_ref, ps_ref, h_scr):
    @pl.when(pl.program_id(1) == 0)
    def _():
        h = _rms(x_ref[...], g_ref[...]).astype(BF16)
        h_scr[...] = h
        ps_ref[...] = _dot(h, ws_ref[...])

    p_ref[...] = _dot(h_scr[...], w_ref[...]).astype(p_ref.dtype)


def _in_proj(x, g, w_main, w_small, *, tm, tn):
    t, d = x.shape
    n = w_main.shape[1]
    x_mode = pl.Buffered(1) if t == tm else None
    return pl.pallas_call(
        _in_proj_kernel,
        grid=(t // tm, n // tn),
        in_specs=[
            pl.BlockSpec((tm, d), lambda i, j: (i, 0), pipeline_mode=x_mode),
            pl.BlockSpec((1, d), lambda i, j: (0, 0)),
            pl.BlockSpec((d, tn), lambda i, j: (0, j)),
            pl.BlockSpec((d, LANES), lambda i, j: (0, 0)),
        ],
        out_specs=[
            pl.BlockSpec((tm, tn), lambda i, j: (i, j)),
            pl.BlockSpec((tm, LANES), lambda i, j: (i, 0)),
        ],
        out_shape=[
            jax.ShapeDtypeStruct((t, n), BF16),
            jax.ShapeDtypeStruct((t, LANES), F32),
        ],
        scratch_shapes=[pltpu.VMEM((tm, d), BF16)],
        compiler_params=_params(("parallel", "arbitrary")),
    )(x, g, w_main, w_small)


def _rope_tables(pos_ref, invf_ref, rope):
    ang = pos_ref[...].astype(F32) * invf_ref[...]
    cos, sin = jnp.cos(ang), jnp.sin(ang)
    lane = lax.broadcasted_iota(jnp.int32, ang.shape, 1)
    half = rope // 2
    c = jnp.where(lane < rope, cos, 0.0)
    s_lo = jnp.where(lane < half, -sin, 0.0)
    s_hi = jnp.where(lane < half, 0.0, jnp.where(lane < rope, sin, 0.0))
    return c, s_lo, s_hi


def _apply_rope(pe, c, s_lo, s_hi, rope):
    half = rope // 2
    from_hi = pltpu.roll(pe, LANES - half, 1)
    from_lo = pltpu.roll(pe, half, 1)
    return pe * c + from_hi * s_lo + from_lo * s_hi


def _mla_q_kernel(cq_ref, gcq_ref, w_ref, gq_ref, pos_ref, invf_ref, q_ref,
                  h_scr, c_scr, slo_scr, shi_scr, *, nope, rope):
    @pl.when(pl.program_id(1) == 0)
    def _():
        h_scr[...] = _rms(cq_ref[...].astype(F32), gcq_ref[...]).astype(BF16)
        c, s_lo, s_hi = _rope_tables(pos_ref, invf_ref, rope)
        c_scr[...] = c
        slo_scr[...] = s_lo
        shi_scr[...] = s_hi

    y = _dot(h_scr[...], w_ref[...])
    g = gq_ref[...]
    qn = _rms(y[:, :nope], g[:, :nope])
    pe = _rms(y[:, nope:], g[:, nope:], n=rope)
    pe = _apply_rope(pe, c_scr[...], slo_scr[...], shi_scr[...], rope)
    q_ref[:, :nope] = qn.astype(q_ref.dtype)
    q_ref[:, nope:] = pe.astype(q_ref.dtype)


def _mla_q(p, cq_blk, g_cq, w_uq_p, gq, pos, invf, *, heads, rank, nope, rope, tm):
    t = p.shape[0]
    hw = nope + LANES
    return pl.pallas_call(
        functools.partial(_mla_q_kernel, nope=nope, rope=rope),
        grid=(t // tm, heads),
        in_specs=[
            pl.BlockSpec((tm, rank), lambda i, j: (i, cq_blk)),
            pl.BlockSpec((1, rank), lambda i, j: (0, 0)),
            pl.BlockSpec((rank, hw), lambda i, j: (0, j)),
            pl.BlockSpec((1, hw), lambda i, j: (0, 0)),
            pl.BlockSpec((tm, 1), lambda i, j: (i, 0)),
            pl.BlockSpec((1, LANES), lambda i, j: (0, 0)),
        ],
        out_specs=pl.BlockSpec((tm, hw), lambda i, j: (i, j)),
        out_shape=jax.ShapeDtypeStruct((t, heads * hw), BF16),
        scratch_shapes=[
            pltpu.VMEM((tm, rank), BF16),
            pltpu.VMEM((tm, LANES), F32),
            pltpu.VMEM((tm, LANES), F32),
            pltpu.VMEM((tm, LANES), F32),
        ],
        compiler_params=_params(("parallel", "arbitrary")),
    )(p, g_cq, w_uq_p, gq, pos, invf)


def _mla_kv_kernel(ckv_ref, gckv_ref, w_ref, gkn_ref, small_ref, gkr_ref, pos_ref, invf_ref,
                   k_ref, v_ref, h_scr, kpe_scr, *, nope, rope):
    @pl.when(pl.program_id(1) == 0)
    def _():
        h_scr[...] = _rms(ckv_ref[...].astype(F32), gckv_ref[...]).astype(BF16)
        c, s_lo, s_hi = _rope_tables(pos_ref, invf_ref, rope)
        sm = small_ref[...]
        lane = lax.broadcasted_iota(jnp.int32, sm.shape, 1)
        pe = _rms(jnp.where(lane < rope, sm, 0.0), gkr_ref[...], n=rope)
        kpe_scr[...] = _apply_rope(pe, c, s_lo, s_hi, rope).astype(BF16)

    y = _dot(h_scr[...], w_ref[...])
    k_ref[:, :nope] = _rms(y[:, :nope], gkn_ref[...]).astype(k_ref.dtype)
    k_ref[:, nope:] = kpe_scr[...]
    v_ref[...] = y[:, nope:].astype(v_ref.dtype)


def _mla_kv(p, ckv_blk, g_ckv, w_ukv, g_kn, small, gkr, pos, invf, *, heads, rank, nope, rope, dv, tm):
    t = p.shape[0]
    kw = nope + LANES
    return pl.pallas_call(
        functools.partial(_mla_kv_kernel, nope=nope, rope=rope),
        grid=(t // tm, heads),
        in_specs=[
            pl.BlockSpec((tm, rank), lambda i, j: (i, ckv_blk)),
            pl.BlockSpec((1, rank), lambda i, j: (0, 0)),
            pl.BlockSpec((rank, nope + dv), lambda i, j: (0, j)),
            pl.BlockSpec((1, nope), lambda i, j: (0, 0)),
            pl.BlockSpec((tm, LANES), lambda i, j: (i, 0)),
            pl.BlockSpec((1, LANES), lambda i, j: (0, 0)),
            pl.BlockSpec((tm, 1), lambda i, j: (i, 0)),
            pl.BlockSpec((1, LANES), lambda i, j: (0, 0)),
        ],
        out_specs=[
            pl.BlockSpec((tm, kw), lambda i, j: (i, j)),
            pl.BlockSpec((tm, dv), lambda i, j: (i, j)),
        ],
        out_shape=[
            jax.ShapeDtypeStruct((t, heads * kw), BF16),
            jax.ShapeDtypeStruct((t, heads * dv), BF16),
        ],
        scratch_shapes=[pltpu.VMEM((tm, rank), BF16), pltpu.VMEM((tm, LANES), BF16)],
        compiler_params=_params(("parallel", "arbitrary")),
    )(p, g_ckv, w_ukv, g_kn, small, gkr, pos, invf)


def _attn_kernel(q_ref, k_ref, v_ref, o_ref, *, tq, tk):
    qi = pl.program_id(2)
    q = q_ref[...]
    dv = v_ref.shape[1]
    row = qi * tq + lax.broadcasted_iota(jnp.int32, (tq, tk), 0)
    col0 = lax.broadcasted_iota(jnp.int32, (tq, tk), 1)

    def body(kb, carry, masked):
        m, l, acc = carry
        k0 = pl.multiple_of(kb * tk, tk)
        s = _dot_nt(q, k_ref[pl.ds(k0, tk), :])
        if masked:
            s = jnp.where(col0 + k0 <= row, s, NEG_INF)
        m_new = jnp.maximum(m, jnp.max(s, axis=-1, keepdims=True))
        alpha = jnp.exp(m - m_new)
        pr = jnp.exp(s - m_new)
        l = alpha * l + jnp.sum(pr, axis=-1, keepdims=True)
        acc = alpha * acc + _dot(pr.astype(BF16), v_ref[pl.ds(k0, tk), :])
        return m_new, l, acc

    init = (jnp.full((tq, 1), NEG_INF, F32), jnp.zeros((tq, 1), F32), jnp.zeros((tq, dv), F32))
    n_below = (qi * tq) // tk
    nkb = ((qi + 1) * tq + tk - 1) // tk
    carry = lax.fori_loop(0, n_below, functools.partial(body, masked=False), init)
    _, l, acc = lax.fori_loop(n_below, nkb, functools.partial(body, masked=True), carry)
    o_ref[...] = (acc / l).astype(o_ref.dtype)


def _mla_attn(q, k, v, *, batch, seq, heads, dk, dv, tq, tk):
    t = q.shape[0]
    nq = seq // tq
    return pl.pallas_call(
        functools.partial(_attn_kernel, tq=tq, tk=tk),
        grid=(batch, heads, nq),
        in_specs=[
            pl.BlockSpec((tq, dk), lambda b, h, i: (b * nq + i, h)),
            pl.BlockSpec((seq, dk), lambda b, h, i: (b, h)),
            pl.BlockSpec((seq, dv), lambda b, h, i: (b, h)),
        ],
        out_specs=pl.BlockSpec((tq, dv), lambda b, h, i: (b * nq + i, h)),
        out_shape=jax.ShapeDtypeStruct((t, heads * dv), BF16),
        compiler_params=_params(("parallel", "parallel", "arbitrary")),
    )(q, k, v)


def _gla_kernel(q_ref, k_ref, v_ref, og_ref, small_ref, w2_ref, ba_ref, gon_ref, o_ref, st_scr,
                *, seq, dk, dv, chunk):
    c = chunk
    st_scr[...] = jnp.zeros_like(st_scr)
    r_i = lax.broadcasted_iota(jnp.int32, (c, c), 0)
    c_i = lax.broadcasted_iota(jnp.int32, (c, c), 1)
    tri = jnp.where(c_i <= r_i, 1.0, 0.0).astype(BF16)
    row_id = lax.broadcasted_iota(jnp.int32, (c, 1), 0)
    w2 = w2_ref[...]
    ba = ba_ref[...]
    gon = gon_ref[...]
    q_scale = dk ** -0.5

    def chunk_step(ci, carry):
        r0 = pl.multiple_of(ci * c, c)
        qc = q_ref[pl.ds(r0, c), :].astype(F32) * q_scale
        kc = k_ref[pl.ds(r0, c), :].astype(F32)
        vc = v_ref[pl.ds(r0, c), :]
        z = _dot(small_ref[pl.ds(r0, c), :].astype(BF16), w2) + ba
        la = jax.nn.log_sigmoid(z) * (1.0 / GLA_TAU)
        hi = la.astype(BF16)
        r1 = la - hi.astype(F32)
        mid = r1.astype(BF16)
        lo = (r1 - mid.astype(F32)).astype(BF16)
        b = _dot(tri, hi) + _dot(tri, mid) + _dot(tri, lo)

        st = st_scr[...]
        inter = _dot_nt((qc * jnp.exp(b)).astype(BF16), st.astype(BF16))

        att = jnp.zeros((c, c), F32)
        for j in range(c):
            lo_r = (j // SUBLANES) * SUBLANES
            d = b[lo_r:, :] - b[j:j + 1, :]
            e = jnp.exp(jnp.where(row_id[lo_r:, :] >= j, d, NEG_INF))
            col = jnp.sum(qc[lo_r:, :] * kc[j:j + 1, :] * e, axis=-1, keepdims=True)
            if lo_r:
                col = jnp.concatenate([jnp.zeros((lo_r, 1), F32), col], axis=0)
            att = jnp.where(c_i == j, col, att)
        o = inter + _dot(att.astype(BF16), vc)

        b_last = b[c - 1:c, :]
        k_dec = (kc * jnp.exp(b_last - b)).astype(BF16)
        st_scr[...] = st * jnp.exp(b_last) + _dot_tn(vc, k_dec)

        og = og_ref[pl.ds(r0, c), :].astype(F32)
        out = _rms(o, gon) * (og * jax.nn.sigmoid(og))
        o_ref[pl.ds(r0, c), :] = out.astype(o_ref.dtype)
        return carry

    lax.fori_loop(0, seq // c, chunk_step, 0)


def _gla(p, small, w2p, b_a, g_on, *, batch, seq, heads, dk, dv, q_blk, k_blk, v_blk, og_blk):
    t = p.shape[0]
    return pl.pallas_call(
        functools.partial(_gla_kernel, seq=seq, dk=dk, dv=dv, chunk=GLA_CHUNK),
        grid=(batch, heads),
        in_specs=[
            pl.BlockSpec((seq, dk), lambda b, h: (b, q_blk + h)),
            pl.BlockSpec((seq, dk), lambda b, h: (b, k_blk + h)),
            pl.BlockSpec((seq, dv), lambda b, h: (b, v_blk + h)),
            pl.BlockSpec((seq, dv), lambda b, h: (b, og_blk + h)),
            pl.BlockSpec((seq, LANES), lambda b, h: (b, 0)),
            pl.BlockSpec((LANES, dk), lambda b, h: (0, h)),
            pl.BlockSpec((1, dk), lambda b, h: (0, h)),
            pl.BlockSpec((1, dv), lambda b, h: (0, 0)),
        ],
        out_specs=pl.BlockSpec((seq, dv), lambda b, h: (b, h)),
        out_shape=jax.ShapeDtypeStruct((t, heads * dv), BF16),
        scratch_shapes=[pltpu.VMEM((dv, dk), F32)],
        compiler_params=_params(("parallel", "parallel")),
    )(p, p, p, p, small, w2p, b_a, g_on)


def _out_proj_kernel(x_ref, ga_ref, gb_ref, oa_ref, ob_ref, bg_ref, w_ref, o_ref, m_scr):
    @pl.when(pl.program_id(1) == 0)
    def _():
        bg = bg_ref[...]
        sa = jax.nn.sigmoid(ga_ref[...].astype(F32) + bg[0:1, :])
        sb = jax.nn.sigmoid(gb_ref[...].astype(F32) + bg[1:2, :])
        m_scr[...] = (sa * oa_ref[...].astype(F32) + sb * ob_ref[...].astype(F32)).astype(BF16)

    o_ref[...] = x_ref[...] + _dot(m_scr[...], w_ref[...])


def _out_proj(x, p, o_mla, o_gla, b_gate, w_out, *, ga_blk, gb_blk, tm, tn):
    t, d = x.shape
    return pl.pallas_call(
        _out_proj_kernel,
        grid=(t // tm, d // tn),
        in_specs=[
            pl.BlockSpec((tm, tn), lambda i, j: (i, j)),
            pl.BlockSpec((tm, d), lambda i, j: (i, ga_blk)),
            pl.BlockSpec((tm, d), lambda i, j: (i, gb_blk)),
            pl.BlockSpec((tm, d), lambda i, j: (i, 0)),
            pl.BlockSpec((tm, d), lambda i, j: (i, 0)),
            pl.BlockSpec((2, d), lambda i, j: (0, 0)),
            pl.BlockSpec((d, tn), lambda i, j: (0, j)),
        ],
        out_specs=pl.BlockSpec((tm, tn), lambda i, j: (i, j)),
        out_shape=jax.ShapeDtypeStruct((t, d), F32),
        scratch_shapes=[pltpu.VMEM((tm, d), BF16)],
        compiler_params=_params(("parallel", "arbitrary")),
    )(x, p, p, o_mla, o_gla, b_gate, w_out)


def _topk_rows(s, k, payload=None):
    n = s.shape[0]
    iota = lax.broadcasted_iota(jnp.int32, s.shape, 0).astype(F32)
    kiota = lax.broadcasted_iota(jnp.int32, (k, s.shape[1]), 0)
    vals = jnp.zeros((k, s.shape[1]), F32)
    picks = jnp.zeros((k, s.shape[1]), F32)
    for r in range(k):
        m = jnp.max(s, axis=0, keepdims=True)
        idx = jnp.min(jnp.where(s == m, iota, float(n)), axis=0, keepdims=True)
        hit = iota == idx
        if payload is None:
            pick = idx
        else:
            pick = jnp.sum(jnp.where(hit, payload, 0.0), axis=0, keepdims=True)
        vals = jnp.where(kiota == r, m, vals)
        picks = jnp.where(kiota == r, pick, picks)
        s = jnp.where(hit, NEG_INF, s)
    return vals, picks


def _peer_route_kernel(x_ref, g_ref, w_ref, sk_ref, h_ref, ids_ref, gates_ref, *, heads, nkeys, half, topk):
    h = _rms(x_ref[...], g_ref[...]).astype(BF16)
    h_ref[...] = h
    qf = _dot(h, w_ref[...])
    for hd in range(heads):
        tops = []
        for part in range(2):
            o = (hd * 2 + part) * half
            qh = qf[:, o:o + half].astype(BF16)
            keys = sk_ref[(hd * 2 + part) * nkeys:(hd * 2 + part + 1) * nkeys, :]
            tops.append(_topk_rows(_dot_nt(keys, qh), topk))
        (s1, i1), (s2, i2) = tops
        assert topk == 2 * SUBLANES
        hs = SUBLANES
        pair_s = [s1[0:1, :] + s2] + [s1[a:a + 1, :] + s2[:hs, :] for a in range(1, hs)] + [s1[hs:, :] + s2[0:1, :]]
        pair_id = ([i1[0:1, :] * float(nkeys) + i2]
                   + [i1[a:a + 1, :] * float(nkeys) + i2[:hs, :] for a in range(1, hs)]
                   + [i1[hs:, :] * float(nkeys) + i2[0:1, :]])
        best_s, best_id = _topk_rows(jnp.concatenate(pair_s, axis=0), topk,
                                     payload=jnp.concatenate(pair_id, axis=0))
        e = jnp.exp(best_s - best_s[0:1, :])
        gate = e / jnp.sum(e, axis=0, keepdims=True)
        ids_ref[hd * topk:(hd + 1) * topk, :] = best_id.astype(jnp.int32)
        gates_ref[hd * topk:(hd + 1) * topk, :] = gate


def _peer_route(x, g, w_pq, sk2d, *, heads, nkeys, half, topk, tm):
    t, d = x.shape
    dq = w_pq.shape[1]
    return pl.pallas_call(
        functools.partial(_peer_route_kernel, heads=heads, nkeys=nkeys, half=half, topk=topk),
        grid=(t // tm,),
        in_specs=[
            pl.BlockSpec((tm, d), lambda i: (i, 0)),
            pl.BlockSpec((1, d), lambda i: (0, 0)),
            pl.BlockSpec((d, dq), lambda i: (0, 0)),
            pl.BlockSpec((heads * 2 * nkeys, half), lambda i: (0, 0)),
        ],
        out_specs=[
            pl.BlockSpec((tm, d), lambda i: (i, 0)),
            pl.BlockSpec((heads * topk, tm), lambda i: (0, i)),
            pl.BlockSpec((heads * topk, tm), lambda i: (0, i)),
        ],
        out_shape=[
            jax.ShapeDtypeStruct((t, d), BF16),
            jax.ShapeDtypeStruct((heads * topk, t), jnp.int32),
            jax.ShapeDtypeStruct((heads * topk, t), F32),
        ],
        compiler_params=_params(("parallel",)),
    )(x, g, w_pq, sk2d)


def _pack_expert_table(u_emb, v_emb):
    def pack(w):
        bits = lax.bitcast_convert_type(w.astype(BF16), jnp.uint16).astype(jnp.uint32)
        half = w.shape[1] // 2
        return bits[:, :half] | (bits[:, half:] << 16)

    return lax.bitcast_convert_type(jnp.concatenate([pack(u_emb), pack(v_emb)], axis=1), jnp.int32)


def _unpack_words(w):
    lo = lax.bitcast_convert_type(w << 16, F32)
    hi = lax.bitcast_convert_type(w & jnp.int32(-65536), F32)
    return lo, hi


def _expert_mix(words, h, gate, d):
    half = d // 2
    u_lo, u_hi = _unpack_words(words[:, :half])
    act = jnp.sum(u_lo * h[:, :half] + u_hi * h[:, half:], axis=-1, keepdims=True)
    w = gate * _gelu_exact(act)
    v_lo, v_hi = _unpack_words(words[:, half:])
    return jnp.concatenate([jnp.sum(v_lo * w, axis=0, keepdims=True),
                            jnp.sum(v_hi * w, axis=0, keepdims=True)], axis=1)


def _peer_mix_kernel(ids_hbm, uv_hbm, rows_ref, x_ref, h_ref, gates_ref, o_ref, ids_smem, buf, ids_sems, row_sems,
                     *, td, ts, picks, d, n_steps):
    step = pl.program_id(0)
    par = lax.rem(step, 2)
    ahead = MIX_SLOTS - 1
    assert td % MIX_SLOTS == 0 and td > ahead and (td + ts) % SUBLANES == 0

    def ids_copy(s, row):
        return pltpu.make_async_copy(ids_hbm.at[s], ids_smem.at[row], ids_sems.at[row])

    def issue(row, tok, slot):
        for j in range(picks):
            eid = ids_smem[row, tok * picks + j]
            pltpu.make_async_copy(uv_hbm.at[pl.ds(eid, 1), :], buf.at[slot, pl.ds(j, 1), :],
                                  row_sems.at[slot]).start()

    def wait_rows(slot):
        pltpu.make_async_copy(uv_hbm.at[pl.ds(0, picks), :], buf.at[slot], row_sems.at[slot]).wait()

    @pl.when(step == 0)
    def _():
        first = ids_copy(0, 0)
        first.start()
        first.wait()
        for u in range(ahead):
            issue(0, u, u)

    @pl.when(step + 1 < n_steps)
    def _():
        ids_copy(step + 1, 1 - par).start()

    loaded = {}

    def finish(pos, words):
        g, lane = divmod(pos, SUBLANES)
        if g not in loaded:
            loaded[g] = (gates_ref[g], h_ref[g * SUBLANES:(g + 1) * SUBLANES, :].astype(F32))
        gates, h8 = loaded[g]
        row = _expert_mix(words, h8[lane:lane + 1, :], gates[:, lane:lane + 1], d)
        o_ref[pos:pos + 1, :] = x_ref[pos:pos + 1, :] + row

    staged_done = 0
    for u in range(td):
        nxt = u + ahead
        if nxt < td:
            issue(par, nxt, nxt % MIX_SLOTS)
        else:
            @pl.when(step + 1 < n_steps)
            def _():
                if nxt == td:
                    ids_copy(step + 1, 1 - par).wait()
                issue(1 - par, nxt - td, nxt % MIX_SLOTS)

        staged_upto = (u + 1) * ts // td
        for k in range(staged_done, staged_upto):
            finish(td + k, rows_ref[k * picks:(k + 1) * picks, :])
        staged_done = staged_upto
        wait_rows(u % MIX_SLOTS)
        finish(u, buf[u % MIX_SLOTS])


def _peer_mix(ids_direct, uv, staged, x, h, gates3, *, td, ts):
    t, d = x.shape
    picks = gates3.shape[1]
    n_steps = t // (td + ts)
    assert staged.shape[0] == n_steps * ts * picks and ids_direct.shape == (n_steps, td * picks)
    return pl.pallas_call(
        functools.partial(_peer_mix_kernel, td=td, ts=ts, picks=picks, d=d, n_steps=n_steps),
        grid=(n_steps,),
        input_output_aliases={3: 0},
        in_specs=[
            pl.BlockSpec(memory_space=pl.ANY),
            pl.BlockSpec(memory_space=pl.ANY),
            pl.BlockSpec((ts * picks, d), lambda i: (i, 0)),
            pl.BlockSpec((td + ts, d), lambda i: (i, 0)),
            pl.BlockSpec((td + ts, d), lambda i: (i, 0)),
            pl.BlockSpec(((td + ts) // SUBLANES, picks, SUBLANES), lambda i: (i, 0, 0)),
        ],
        out_specs=pl.BlockSpec((td + ts, d), lambda i: (i, 0)),
        out_shape=jax.ShapeDtypeStruct((t, d), F32),
        scratch_shapes=[
            pltpu.SMEM((2, td * picks), jnp.int32),
            pltpu.VMEM((MIX_SLOTS, picks, d), jnp.int32),
            pltpu.SemaphoreType.DMA((2,)),
            pltpu.SemaphoreType.DMA((MIX_SLOTS,)),
        ],
        compiler_params=_params(("arbitrary",)),
    )(ids_direct, uv, staged, x, h, gates3)


def _sc_gather_rows(table, idx, *, chunk):
    n_rows = idx.shape[0]
    d = table.shape[1]
    workers = SC_CORES * SC_SUBCORES
    assert n_rows % (workers * 2 * chunk) == 0 and chunk % SUBLANES == 0 and chunk <= LANES
    rows_per_worker = n_rows // workers
    n_pairs = rows_per_worker // (2 * chunk)
    mesh = plsc.VectorSubcoreMesh(core_axis_name="c", subcore_axis_name="s")

    @functools.partial(
        pl.kernel, mesh=mesh,
        out_type=jax.ShapeDtypeStruct((n_rows, d), table.dtype),
        scratch_types=[
            pltpu.VMEM((chunk,), jnp.int32), pltpu.VMEM((chunk,), jnp.int32),
            pltpu.VMEM((chunk, d), table.dtype), pltpu.VMEM((chunk, d), table.dtype),
            pltpu.SemaphoreType.DMA, pltpu.SemaphoreType.DMA,
            pltpu.SemaphoreType.DMA, pltpu.SemaphoreType.DMA,
        ],
    )
    def gather_kernel(table_hbm, idx_hbm, out_hbm, idx0, idx1, rows0, rows1, gsem0, gsem1, wsem0, wsem1):
        idx_v, rows_v, gsem, wsem = (idx0, idx1), (rows0, rows1), (gsem0, gsem1), (wsem0, wsem1)
        worker = lax.axis_index("s") * SC_CORES + lax.axis_index("c")
        base = worker * rows_per_worker

        def out_rows(c):
            return pl.ds(pl.multiple_of(base + c * chunk, SUBLANES), chunk)

        def load_idx(slot, c):
            pltpu.sync_copy(idx_hbm.at[out_rows(c)], idx_v[slot])

        def gather(slot):
            return pltpu.make_async_copy(table_hbm.at[idx_v[slot]], rows_v[slot], gsem[slot])

        def writeout(slot, c):
            return pltpu.make_async_copy(rows_v[slot], out_hbm.at[out_rows(c)], wsem[slot])

        load_idx(0, 0)
        gather(0).start()

        @pl.loop(0, n_pairs)
        def _(p):
            c0 = 2 * p

            @pl.when(p > 0)
            def _():
                writeout(1, c0 - 1).wait()

            load_idx(1, c0 + 1)
            gather(1).start()
            gather(0).wait()
            writeout(0, c0).start()

            @pl.when(p + 1 < n_pairs)
            def _():
                load_idx(0, c0 + 2)
                writeout(0, c0).wait()
                gather(0).start()

            gather(1).wait()
            writeout(1, c0 + 1).start()

        writeout(0, 2 * n_pairs - 2).wait()
        writeout(1, 2 * n_pairs - 1).wait()

    return gather_kernel(table, idx)


def _pad_cols(w, width):
    return jnp.pad(w, ((0, 0), (0, width - w.shape[1])))


def _prepare_layer(g_norm_mix, w_in, b_gate, g_cq, w_uq, g_ckv, w_ukv, g_qn, g_qr, g_kn, g_kr,
                   w_a2, b_a, g_gla_out, w_out, g_norm_ffn, w_pq, sub_keys, u_emb, v_emb):
    d = w_in.shape[0]
    q_rank, kv_rank = g_cq.shape[0], g_ckv.shape[0]
    nope, rope = g_qn.shape[0], g_qr.shape[0]
    mla_heads = w_uq.shape[1] // (nope + rope)
    mla_v = w_ukv.shape[1] // mla_heads - nope
    gate_rank, gla_dk_all = w_a2.shape
    gla_dv = g_gla_out.shape[0]
    gla_heads = d // gla_dv
    gla_dk = gla_dk_all // gla_heads
    peer_heads, _, nkeys, half = sub_keys.shape
    assert nope == LANES and mla_v == LANES and rope <= LANES and rope % 2 == 0
    assert mla_heads * mla_v == d and gla_heads * gla_dv == d
    assert rope + gate_rank <= LANES and nkeys == LANES and half == LANES

    widths = (q_rank, kv_rank, rope, gla_dk_all, gla_dk_all, d, gate_rank, d, d, d)
    offs = [0]
    for wd in widths:
        offs.append(offs[-1] + wd)
    assert offs[-1] == w_in.shape[1]
    seg = lambda i: w_in[:, offs[i]:offs[i + 1]]
    w_main = jnp.concatenate([seg(5), seg(7), seg(8), seg(9), seg(3), seg(4), seg(0), seg(1)], axis=1).astype(BF16)
    w_small = _pad_cols(jnp.concatenate([seg(2), seg(6)], axis=1), LANES).astype(BF16)
    cq_off = 4 * d + 2 * gla_dk_all
    assert cq_off % q_rank == 0 and (cq_off + q_rank) % kv_rank == 0
    inv_freq = ROPE_THETA ** (-jnp.arange(0, rope, 2, dtype=F32) / rope)
    scale = (nope + rope) ** -0.5
    return dict(
        dims=dict(q_rank=q_rank, kv_rank=kv_rank, nope=nope, rope=rope, mla_heads=mla_heads, mla_v=mla_v,
                  gla_heads=gla_heads, gla_dk=gla_dk, gla_dv=gla_dv, peer_heads=peer_heads, nkeys=nkeys, half=half,
                  v_blk=0, og_blk=d // gla_dv, ga_blk=2, gb_blk=3, q_blk=4 * d // gla_dk,
                  k_blk=4 * d // gla_dk + gla_heads, cq_blk=cq_off // q_rank, ckv_blk=(cq_off + q_rank) // kv_rank),
        g_norm_mix=g_norm_mix[None, :], w_main=w_main, w_small=w_small,
        invf=_pad_cols(jnp.concatenate([inv_freq, inv_freq])[None, :], LANES),
        gq=_pad_cols(jnp.concatenate([g_qn, g_qr])[None, :] * scale, nope + LANES),
        gkr=_pad_cols(g_kr[None, :], LANES), g_cq=g_cq[None, :], g_ckv=g_ckv[None, :], g_kn=g_kn[None, :],
        w_uq=jnp.pad(w_uq.reshape(q_rank, mla_heads, nope + rope),
                     ((0, 0), (0, 0), (0, LANES - rope))).reshape(q_rank, -1).astype(BF16),
        w_ukv=w_ukv.astype(BF16),
        w2p=jnp.zeros((LANES, gla_dk_all), F32).at[rope:rope + gate_rank].set(w_a2).astype(BF16),
        b_a=b_a[None, :], g_on=g_gla_out[None, :], b_gate=b_gate, w_out=w_out.astype(BF16),
        g_norm_ffn=g_norm_ffn[None, :], w_pq=w_pq.astype(BF16),
        sk2d=sub_keys.reshape(peer_heads * 2 * nkeys, half).astype(BF16),
        uv=_pack_expert_table(u_emb, v_emb),
    )


def _mixers_and_route(x2, pos, w, *, batch, seq, tiles):
    t = x2.shape[0]
    dm = w["dims"]
    p, small = _in_proj(x2, w["g_norm_mix"], w["w_main"], w["w_small"], tm=tiles["in_tm"], tn=tiles["in_tn"])
    q = _mla_q(p, dm["cq_blk"], w["g_cq"], w["w_uq"], w["gq"], pos, w["invf"], heads=dm["mla_heads"],
               rank=dm["q_rank"], nope=dm["nope"], rope=dm["rope"], tm=tiles["mla_tm"])
    k, v = _mla_kv(p, dm["ckv_blk"], w["g_ckv"], w["w_ukv"], w["g_kn"], small, w["gkr"], pos, w["invf"],
                   heads=dm["mla_heads"], rank=dm["kv_rank"], nope=dm["nope"], rope=dm["rope"], dv=dm["mla_v"],
                   tm=tiles["mla_tm"])
    o_mla = _mla_attn(q, k, v, batch=batch, seq=seq, heads=dm["mla_heads"], dk=dm["nope"] + LANES, dv=dm["mla_v"],
                      tq=tiles["attn_tq"], tk=tiles["attn_tk"])
    o_gla = _gla(p, small, w["w2p"], w["b_a"], w["g_on"], batch=batch, seq=seq, heads=dm["gla_heads"],
                 dk=dm["gla_dk"], dv=dm["gla_dv"], q_blk=dm["q_blk"], k_blk=dm["k_blk"], v_blk=dm["v_blk"],
                 og_blk=dm["og_blk"])
    x2 = _out_proj(x2, p, o_mla, o_gla, w["b_gate"], w["w_out"], ga_blk=dm["ga_blk"], gb_blk=dm["gb_blk"],
                   tm=tiles["out_tm"], tn=tiles["out_tn"])
    h2, ids_t, gates_t = _peer_route(x2, w["g_norm_ffn"], w["w_pq"], w["sk2d"], heads=dm["peer_heads"],
                                     nkeys=dm["nkeys"], half=dm["half"], topk=PEER_TOPK, tm=tiles["route_tm"])
    picks = dm["peer_heads"] * PEER_TOPK
    gates3 = gates_t.reshape(picks, t // SUBLANES, SUBLANES).transpose(1, 0, 2)
    return x2, h2, ids_t.T, gates3


def _layer(x2, pos, w, *, batch, seq, tiles):
    t, d = x2.shape
    splits = tiles["mix_split"]
    groups = len(splits)
    tg, bg = t // groups, batch // groups
    routed = [_mixers_and_route(x2[g * tg:(g + 1) * tg], pos[g * tg:(g + 1) * tg], w,
                                batch=bg, seq=seq, tiles=tiles) for g in range(groups)]
    out = []
    for (td, ts), (xg, hg, ids_tok, gates3) in zip(splits, routed):
        picks = ids_tok.shape[1]
        ids3 = ids_tok.reshape(tg // (td + ts), td + ts, picks)
        staged = _sc_gather_rows(w["uv"], ids3[:, td:].reshape(-1), chunk=tiles["sc_chunk"])
        out.append(_peer_mix(ids3[:, :td].reshape(-1, td * picks), w["uv"], staged, xg, hg, gates3, td=td, ts=ts))
    return jnp.concatenate(out, axis=0)


_TILES = dict(in_tm=2048, in_tn=512, mla_tm=2048, attn_tq=1024, attn_tk=1024,
              out_tm=1024, out_tn=512, route_tm=256, mix_split=((12, 20),) * 4 + ((4, 12),) * 4, sc_chunk=16)


def kernel(x, positions, g_norm_mix, w_in, b_gate, g_cq, w_uq, g_ckv, w_ukv, g_qn, g_qr, g_kn, g_kr,
           w_a2, b_a, g_gla_out, w_out, g_norm_ffn, w_pq, sub_keys, u_emb, v_emb, tiles=None):
    tiles = _TILES if tiles is None else tiles
    batch, seq, d = x.shape
    x2 = x.reshape(batch * seq, d)
    pos = positions.reshape(batch * seq, 1)
    for l in range(g_norm_mix.shape[0]):
        w = _prepare_layer(g_norm_mix[l], w_in[l], b_gate[l], g_cq[l], w_uq[l], g_ckv[l], w_ukv[l], g_qn[l],
                           g_qr[l], g_kn[l], g_kr[l], w_a2[l], b_a[l], g_gla_out[l], w_out[l], g_norm_ffn[l],
                           w_pq[l], sub_keys[l], u_emb[l], v_emb[l])
        x2 = _layer(x2, pos, w, batch=batch, seq=seq, tiles=tiles)
    return x2.reshape(batch, seq, d)
```

```python
import functools

import jax
import jax.numpy as jnp
from jax import lax
from jax.experimental import pallas as pl
from jax.experimental.pallas import tpu as pltpu
from jax.experimental.pallas import tpu_sc as plsc

EPS = 1e-6
ROPE_THETA = 10000.0
GLA_TAU = 16.0
GLA_CHUNK = 64
PEER_TOPK = 16

LANES = 128
SUBLANES = 8
VMEM_LIMIT_BYTES = 56 * 1024 * 1024
MIX_SLOTS = 4
SC_CORES = 2
SC_SUBCORES = 16

F32 = jnp.float32
BF16 = jnp.bfloat16
NEG_INF = float("-inf")


def _params(semantics):
    return pltpu.CompilerParams(dimension_semantics=semantics, vmem_limit_bytes=VMEM_LIMIT_BYTES)


def _rms(x, gain, n=None):
    ss = jnp.sum(x * x, axis=-1, keepdims=True)
    n = x.shape[-1] if n is None else n
    return x * lax.rsqrt(ss * (1.0 / n) + EPS) * gain


def _gelu_exact(x):
    return 0.5 * x * (1.0 + lax.erf(x * (0.5 ** 0.5)))


def _dot(a, b):
    return jnp.dot(a, b, preferred_element_type=F32)


def _dot_nt(a, b):
    return lax.dot_general(a, b, (((1,), (1,)), ((), ())), preferred_element_type=F32)


def _dot_tn(a, b):
    return lax.dot_general(a, b, (((0,), (0,)), ((), ())), preferred_element_type=F32)


def _in_proj_kernel(x_ref, g_ref, w_ref, ws_ref, p_ref, ps_ref, h_scr):
    @pl.when(pl.program_id(1) == 0)
    def _():
        h = _rms(x_ref[...], g_ref[...]).astype(BF16)
        h_scr[...] = h
        ps_ref[...] = _dot(h, ws_ref[...])

    p_ref[...] = _dot(h_scr[...], w_ref[...]).astype(p_ref.dtype)


def _in_proj(x, g, w_main, w_small, *, tm, tn):
    t, d = x.shape
    n = w_main.shape[1]
    x_mode = pl.Buffered(1) if t == tm else None
    return pl.pallas_call(
        _in_proj_kernel,
        grid=(t // tm, n // tn),
        in_specs=[
            pl.BlockSpec((tm, d), lambda i, j: (i, 0), pipeline_mode=x_mode),
            pl.BlockSpec((1, d), lambda i, j: (0, 0)),
            pl.BlockSpec((d, tn), lambda i, j: (0, j)),
            pl.BlockSpec((d, LANES), lambda i, j: (0, 0)),
        ],
        out_specs=[
            pl.BlockSpec((tm, tn), lambda i, j: (i, j)),
            pl.BlockSpec((tm, LANES), lambda i, j: (i, 0)),
        ],
        out_shape=[
            jax.ShapeDtypeStruct((t, n), BF16),
            jax.ShapeDtypeStruct((t, LANES), F32),
        ],
        scratch_shapes=[pltpu.VMEM((tm, d), BF16)],
        compiler_params=_params(("parallel", "arbitrary")),
    )(x, g, w_main, w_small)


def _rope_tables(pos_ref, invf_ref, rope):
    ang = pos_ref[...].astype(F32) * invf_ref[...]
    cos, sin = jnp.cos(ang), jnp.sin(ang)
    lane = lax.broadcasted_iota(jnp.int32, ang.shape, 1)
    half = rope // 2
    c = jnp.where(lane < rope, cos, 0.0)
    s_lo = jnp.where(lane < half, -sin, 0.0)
    s_hi = jnp.where(lane < half, 0.0, jnp.where(lane < rope, sin, 0.0))
    return c, s_lo, s_hi


def _apply_rope(pe, c, s_lo, s_hi, rope):
    half = rope // 2
    from_hi = pltpu.roll(pe, LANES - half, 1)
    from_lo = pltpu.roll(pe, half, 1)
    return pe * c + from_hi * s_lo + from_lo * s_hi


def _mla_q_kernel(cq_ref, gcq_ref, w_ref, gq_ref, pos_ref, invf_ref, q_ref,
                  h_scr, c_scr, slo_scr, shi_scr, *, nope, rope):
    @pl.when(pl.program_id(1) == 0)
    def _():
        h_scr[...] = _rms(cq_ref[...].astype(F32), gcq_ref[...]).astype(BF16)
        c, s_lo, s_hi = _rope_tables(pos_ref, invf_ref, rope)
        c_scr[...] = c
        slo_scr[...] = s_lo
        shi_scr[...] = s_hi

    y = _dot(h_scr[...], w_ref[...])
    g = gq_ref[...]
    qn = _rms(y[:, :nope], g[:, :nope])
    pe = _rms(y[:, nope:], g[:, nope:], n=rope)
    pe = _apply_rope(pe, c_scr[...], slo_scr[...], shi_scr[...], rope)
    q_ref[:, :nope] = qn.astype(q_ref.dtype)
    q_ref[:, nope:] = pe.astype(q_ref.dtype)


def _mla_q(p, cq_blk, g_cq, w_uq_p, gq, pos, invf, *, heads, rank, nope, rope, tm):
    t = p.shape[0]
    hw = nope + LANES
    return pl.pallas_call(
        functools.partial(_mla_q_kernel, nope=nope, rope=rope),
        grid=(t // tm, heads),
        in_specs=[
            pl.BlockSpec((tm, rank), lambda i, j: (i, cq_blk)),
            pl.BlockSpec((1, rank), lambda i, j: (0, 0)),
            pl.BlockSpec((rank, hw), lambda i, j: (0, j)),
            pl.BlockSpec((1, hw), lambda i, j: (0, 0)),
            pl.BlockSpec((tm, 1), lambda i, j: (i, 0)),
            pl.BlockSpec((1, LANES), lambda i, j: (0, 0)),
        ],
        out_specs=pl.BlockSpec((tm, hw), lambda i, j: (i, j)),
        out_shape=jax.ShapeDtypeStruct((t, heads * hw), BF16),
        scratch_shapes=[
            pltpu.VMEM((tm, rank), BF16),
            pltpu.VMEM((tm, LANES), F32),
            pltpu.VMEM((tm, LANES), F32),
            pltpu.VMEM((tm, LANES), F32),
        ],
        compiler_params=_params(("parallel", "arbitrary")),
    )(p, g_cq, w_uq_p, gq, pos, invf)


def _mla_kv_kernel(ckv_ref, gckv_ref, w_ref, gkn_ref, small_ref, gkr_ref, pos_ref, invf_ref,
                   k_ref, v_ref, h_scr, kpe_scr, *, nope, rope):
    @pl.when(pl.program_id(1) == 0)
    def _():
        h_scr[...] = _rms(ckv_ref[...].astype(F32), gckv_ref[...]).astype(BF16)
        c, s_lo, s_hi = _rope_tables(pos_ref, invf_ref, rope)
        sm = small_ref[...]
        lane = lax.broadcasted_iota(jnp.int32, sm.shape, 1)
        pe = _rms(jnp.where(lane < rope, sm, 0.0), gkr_ref[...], n=rope)
        kpe_scr[...] = _apply_rope(pe, c, s_lo, s_hi, rope).astype(BF16)

    y = _dot(h_scr[...], w_ref[...])
    k_ref[:, :nope] = _rms(y[:, :nope], gkn_ref[...]).astype(k_ref.dtype)
    k_ref[:, nope:] = kpe_scr[...]
    v_ref[...] = y[:, nope:].astype(v_ref.dtype)


def _mla_kv(p, ckv_blk, g_ckv, w_ukv, g_kn, small, gkr, pos, invf, *, heads, rank, nope, rope, dv, tm):
    t = p.shape[0]
    kw = nope + LANES
    return pl.pallas_call(
        functools.partial(_mla_kv_kernel, nope=nope, rope=rope),
        grid=(t // tm, heads),
        in_specs=[
            pl.BlockSpec((tm, rank), lambda i, j: (i, ckv_blk)),
            pl.BlockSpec((1, rank), lambda i, j: (0, 0)),
            pl.BlockSpec((rank, nope + dv), lambda i, j: (0, j)),
            pl.BlockSpec((1, nope), lambda i, j: (0, 0)),
            pl.BlockSpec((tm, LANES), lambda i, j: (i, 0)),
            pl.BlockSpec((1, LANES), lambda i, j: (0, 0)),
            pl.BlockSpec((tm, 1), lambda i, j: (i, 0)),
            pl.BlockSpec((1, LANES), lambda i, j: (0, 0)),
        ],
        out_specs=[
            pl.BlockSpec((tm, kw), lambda i, j: (i, j)),
            pl.BlockSpec((tm, dv), lambda i, j: (i, j)),
        ],
        out_shape=[
            jax.ShapeDtypeStruct((t, heads * kw), BF16),
            jax.ShapeDtypeStruct((t, heads * dv), BF16),
        ],
        scratch_shapes=[pltpu.VMEM((tm, rank), BF16), pltpu.VMEM((tm, LANES), BF16)],
        compiler_params=_params(("parallel", "arbitrary")),
    )(p, g_ckv, w_ukv, g_kn, small, gkr, pos, invf)


def _attn_kernel(q_ref, k_ref, v_ref, o_ref, *, tq, tk):
    qi = pl.program_id(2)
    q = q_ref[...]
    dv = v_ref.shape[1]
    row = qi * tq + lax.broadcasted_iota(jnp.int32, (tq, tk), 0)
    col0 = lax.broadcasted_iota(jnp.int32, (tq, tk), 1)

    def body(kb, carry, masked):
        m, l, acc = carry
        k0 = pl.multiple_of(kb * tk, tk)
        s = _dot_nt(q, k_ref[pl.ds(k0, tk), :])
        if masked:
            s = jnp.where(col0 + k0 <= row, s, NEG_INF)
        m_new = jnp.maximum(m, jnp.max(s, axis=-1, keepdims=True))
        alpha = jnp.exp(m - m_new)
        pr = jnp.exp(s - m_new)
        l = alpha * l + jnp.sum(pr, axis=-1, keepdims=True)
        acc = alpha * acc + _dot(pr.astype(BF16), v_ref[pl.ds(k0, tk), :])
        return m_new, l, acc

    init = (jnp.full((tq, 1), NEG_INF, F32), jnp.zeros((tq, 1), F32), jnp.zeros((tq, dv), F32))
    n_below = (qi * tq) // tk
    nkb = ((qi + 1) * tq + tk - 1) // tk
    carry = lax.fori_loop(0, n_below, functools.partial(body, masked=False), init)
    _, l, acc = lax.fori_loop(n_below, nkb, functools.partial(body, masked=True), carry)
    o_ref[...] = (acc / l).astype(o_ref.dtype)


def _mla_attn(q, k, v, *, batch, seq, heads, dk, dv, tq, tk):
    t = q.shape[0]
    nq = seq // tq
    return pl.pallas_call(
        functools.partial(_attn_kernel, tq=tq, tk=tk),
        grid=(batch, heads, nq),
        in_specs=[
            pl.BlockSpec((tq, dk), lambda b, h, i: (b * nq + i, h)),
            pl.BlockSpec((seq, dk), lambda b, h, i: (b, h)),
            pl.BlockSpec((seq, dv), lambda b, h, i: (b, h)),
        ],
        out_specs=pl.BlockSpec((tq, dv), lambda b, h, i: (b * nq + i, h)),
        out_shape=jax.ShapeDtypeStruct((t, heads * dv), BF16),
        compiler_params=_params(("parallel", "parallel", "arbitrary")),
    )(q, k, v)


def _gla_kernel(q_ref, k_ref, v_ref, og_ref, small_ref, w2_ref, ba_ref, gon_ref, o_ref, st_scr,
                *, seq, dk, dv, chunk):
    c = chunk
    st_scr[...] = jnp.zeros_like(st_scr)
    r_i = lax.broadcasted_iota(jnp.int32, (c, c), 0)
    c_i = lax.broadcasted_iota(jnp.int32, (c, c), 1)
    tri = jnp.where(c_i <= r_i, 1.0, 0.0).astype(BF16)
    row_id = lax.broadcasted_iota(jnp.int32, (c, 1), 0)
    w2 = w2_ref[...]
    ba = ba_ref[...]
    gon = gon_ref[...]
    q_scale = dk ** -0.5

    def chunk_step(ci, carry):
        r0 = pl.multiple_of(ci * c, c)
        qc = q_ref[pl.ds(r0, c), :].astype(F32) * q_scale
        kc = k_ref[pl.ds(r0, c), :].astype(F32)
        vc = v_ref[pl.ds(r0, c), :]
        z = _dot(small_ref[pl.ds(r0, c), :].astype(BF16), w2) + ba
        la = jax.nn.log_sigmoid(z) * (1.0 / GLA_TAU)
        hi = la.astype(BF16)
        r1 = la - hi.astype(F32)
        mid = r1.astype(BF16)
        lo = (r1 - mid.astype(F32)).astype(BF16)
        b = _dot(tri, hi) + _dot(tri, mid) + _dot(tri, lo)

        st = st_scr[...]
        inter = _dot_nt((qc * jnp.exp(b)).astype(BF16), st.astype(BF16))

        att = jnp.zeros((c, c), F32)
        for j in range(c):
            lo_r = (j // SUBLANES) * SUBLANES
            d = b[lo_r:, :] - b[j:j + 1, :]
            e = jnp.exp(jnp.where(row_id[lo_r:, :] >= j, d, NEG_INF))
            col = jnp.sum(qc[lo_r:, :] * kc[j:j + 1, :] * e, axis=-1, keepdims=True)
            if lo_r:
                col = jnp.concatenate([jnp.zeros((lo_r, 1), F32), col], axis=0)
            att = jnp.where(c_i == j, col, att)
        o = inter + _dot(att.astype(BF16), vc)

        b_last = b[c - 1:c, :]
        k_dec = (kc * jnp.exp(b_last - b)).astype(BF16)
        st_scr[...] = st * jnp.exp(b_last) + _dot_tn(vc, k_dec)

        og = og_ref[pl.ds(r0, c), :].astype(F32)
        out = _rms(o, gon) * (og * jax.nn.sigmoid(og))
        o_ref[pl.ds(r0, c), :] = out.astype(o_ref.dtype)
        return carry

    lax.fori_loop(0, seq // c, chunk_step, 0)


def _gla(p, small, w2p, b_a, g_on, *, batch, seq, heads, dk, dv, q_blk, k_blk, v_blk, og_blk):
    t = p.shape[0]
    return pl.pallas_call(
        functools.partial(_gla_kernel, seq=seq, dk=dk, dv=dv, chunk=GLA_CHUNK),
        grid=(batch, heads),
        in_specs=[
            pl.BlockSpec((seq, dk), lambda b, h: (b, q_blk + h)),
            pl.BlockSpec((seq, dk), lambda b, h: (b, k_blk + h)),
            pl.BlockSpec((seq, dv), lambda b, h: (b, v_blk + h)),
            pl.BlockSpec((seq, dv), lambda b, h: (b, og_blk + h)),
            pl.BlockSpec((seq, LANES), lambda b, h: (b, 0)),
            pl.BlockSpec((LANES, dk), lambda b, h: (0, h)),
            pl.BlockSpec((1, dk), lambda b, h: (0, h)),
            pl.BlockSpec((1, dv), lambda b, h: (0, 0)),
        ],
        out_specs=pl.BlockSpec((seq, dv), lambda b, h: (b, h)),
        out_shape=jax.ShapeDtypeStruct((t, heads * dv), BF16),
        scratch_shapes=[pltpu.VMEM((dv, dk), F32)],
        compiler_params=_params(("parallel", "parallel")),
    )(p, p, p, p, small, w2p, b_a, g_on)


def _out_proj_kernel(x_ref, ga_ref, gb_ref, oa_ref, ob_ref, bg_ref, w_ref, o_ref, m_scr):
    @pl.when(pl.program_id(1) == 0)
    def _():
        bg = bg_ref[...]
        sa = jax.nn.sigmoid(ga_ref[...].astype(F32) + bg[0:1, :])
        sb = jax.nn.sigmoid(gb_ref[...].astype(F32) + bg[1:2, :])
        m_scr[...] = (sa * oa_ref[...].astype(F32) + sb * ob_ref[...].astype(F32)).astype(BF16)

    o_ref[...] = x_ref[...] + _dot(m_scr[...], w_ref[...])


def _out_proj(x, p, o_mla, o_gla, b_gate, w_out, *, ga_blk, gb_blk, tm, tn):
    t, d = x.shape
    return pl.pallas_call(
        _out_proj_kernel,
        grid=(t // tm, d // tn),
        in_specs=[
            pl.BlockSpec((tm, tn), lambda i, j: (i, j)),
            pl.BlockSpec((tm, d), lambda i, j: (i, ga_blk)),
            pl.BlockSpec((tm, d), lambda i, j: (i, gb_blk)),
            pl.BlockSpec((tm, d), lambda i, j: (i, 0)),
            pl.BlockSpec((tm, d), lambda i, j: (i, 0)),
            pl.BlockSpec((2, d), lambda i, j: (0, 0)),
            pl.BlockSpec((d, tn), lambda i, j: (0, j)),
        ],
        out_specs=pl.BlockSpec((tm, tn), lambda i, j: (i, j)),
        out_shape=jax.ShapeDtypeStruct((t, d), F32),
        scratch_shapes=[pltpu.VMEM((tm, d), BF16)],
        compiler_params=_params(("parallel", "arbitrary")),
    )(x, p, p, o_mla, o_gla, b_gate, w_out)


def _topk_rows(s, k, payload=None):
    n = s.shape[0]
    iota = lax.broadcasted_iota(jnp.int32, s.shape, 0).astype(F32)
    kiota = lax.broadcasted_iota(jnp.int32, (k, s.shape[1]), 0)
    vals = jnp.zeros((k, s.shape[1]), F32)
    picks = jnp.zeros((k, s.shape[1]), F32)
    for r in range(k):
        m = jnp.max(s, axis=0, keepdims=True)
        idx = jnp.min(jnp.where(s == m, iota, float(n)), axis=0, keepdims=True)
        hit = iota == idx
        if payload is None:
            pick = idx
        else:
            pick = jnp.sum(jnp.where(hit, payload, 0.0), axis=0, keepdims=True)
        vals = jnp.where(kiota == r, m, vals)
        picks = jnp.where(kiota == r, pick, picks)
        s = jnp.where(hit, NEG_INF, s)
    return vals, picks


def _peer_route_kernel(x_ref, g_ref, w_ref, sk_ref, h_ref, ids_ref, gates_ref, *, heads, nkeys, half, topk):
    h = _rms(x_ref[...], g_ref[...]).astype(BF16)
    h_ref[...] = h
    qf = _dot(h, w_ref[...])
    for hd in range(heads):
        tops = []
        for part in range(2):
            o = (hd * 2 + part) * half
            qh = qf[:, o:o + half].astype(BF16)
            keys = sk_ref[(hd * 2 + part) * nkeys:(hd * 2 + part + 1) * nkeys, :]
            tops.append(_topk_rows(_dot_nt(keys, qh), topk))
        (s1, i1), (s2, i2) = tops
        assert topk == 2 * SUBLANES
        hs = SUBLANES
        pair_s = [s1[0:1, :] + s2] + [s1[a:a + 1, :] + s2[:hs, :] for a in range(1, hs)] + [s1[hs:, :] + s2[0:1, :]]
        pair_id = ([i1[0:1, :] * float(nkeys) + i2]
                   + [i1[a:a + 1, :] * float(nkeys) + i2[:hs, :] for a in range(1, hs)]
                   + [i1[hs:, :] * float(nkeys) + i2[0:1, :]])
        best_s, best_id = _topk_rows(jnp.concatenate(pair_s, axis=0), topk,
                                     payload=jnp.concatenate(pair_id, axis=0))
        e = jnp.exp(best_s - best_s[0:1, :])
        gate = e / jnp.sum(e, axis=0, keepdims=True)
        ids_ref[hd * topk:(hd + 1) * topk, :] = best_id.astype(jnp.int32)
        gates_ref[hd * topk:(hd + 1) * topk, :] = gate


def _peer_route(x, g, w_pq, sk2d, *, heads, nkeys, half, topk, tm):
    t, d = x.shape
    dq = w_pq.shape[1]
    return pl.pallas_call(
        functools.partial(_peer_route_kernel, heads=heads, nkeys=nkeys, half=half, topk=topk),
        grid=(t // tm,),
        in_specs=[
            pl.BlockSpec((tm, d), lambda i: (i, 0)),
            pl.BlockSpec((1, d), lambda i: (0, 0)),
            pl.BlockSpec((d, dq), lambda i: (0, 0)),
            pl.BlockSpec((heads * 2 * nkeys, half), lambda i: (0, 0)),
        ],
        out_specs=[
            pl.BlockSpec((tm, d), lambda i: (i, 0)),
            pl.BlockSpec((heads * topk, tm), lambda i: (0, i)),
            pl.BlockSpec((heads * topk, tm), lambda i: (0, i)),
        ],
        out_shape=[
            jax.ShapeDtypeStruct((t, d), BF16),
            jax.ShapeDtypeStruct((heads * topk, t), jnp.int32),
            jax.ShapeDtypeStruct((heads * topk, t), F32),
        ],
        compiler_params=_params(("parallel",)),
    )(x, g, w_pq, sk2d)


def _pack_expert_table(u_emb, v_emb):
    def pack(w):
        bits = lax.bitcast_convert_type(w.astype(BF16), jnp.uint16).astype(jnp.uint32)
        half = w.shape[1] // 2
        return bits[:, :half] | (bits[:, half:] << 16)

    return lax.bitcast_convert_type(jnp.concatenate([pack(u_emb), pack(v_emb)], axis=1), jnp.int32)


def _unpack_words(w):
    lo = lax.bitcast_convert_type(w << 16, F32)
    hi = lax.bitcast_convert_type(w & jnp.int32(-65536), F32)
    return lo, hi


def _expert_mix(words, h, gate, d):
    half = d // 2
    u_lo, u_hi = _unpack_words(words[:, :half])
    act = jnp.sum(u_lo * h[:, :half] + u_hi * h[:, half:], axis=-1, keepdims=True)
    w = gate * _gelu_exact(act)
    v_lo, v_hi = _unpack_words(words[:, half:])
    return jnp.concatenate([jnp.sum(v_lo * w, axis=0, keepdims=True),
                            jnp.sum(v_hi * w, axis=0, keepdims=True)], axis=1)


def _peer_mix_kernel(ids_hbm, uv_hbm, rows_ref, x_ref, h_ref, gates_ref, o_ref, ids_smem, buf, ids_sems, row_sems,
                     *, td, ts, picks, d, n_steps):
    step = pl.program_id(0)
    par = lax.rem(step, 2)
    ahead = MIX_SLOTS - 1
    assert td % MIX_SLOTS == 0 and td > ahead and (td + ts) % SUBLANES == 0

    def ids_copy(s, row):
        return pltpu.make_async_copy(ids_hbm.at[s], ids_smem.at[row], ids_sems.at[row])

    def issue(row, tok, slot):
        for j in range(picks):
            eid = ids_smem[row, tok * picks + j]
            pltpu.make_async_copy(uv_hbm.at[pl.ds(eid, 1), :], buf.at[slot, pl.ds(j, 1), :],
                                  row_sems.at[slot]).start()

    def wait_rows(slot):
        pltpu.make_async_copy(uv_hbm.at[pl.ds(0, picks), :], buf.at[slot], row_sems.at[slot]).wait()

    @pl.when(step == 0)
    def _():
        first = ids_copy(0, 0)
        first.start()
        first.wait()
        for u in range(ahead):
            issue(0, u, u)

    @pl.when(step + 1 < n_steps)
    def _():
        ids_copy(step + 1, 1 - par).start()

    loaded = {}

    def finish(pos, words):
        g, lane = divmod(pos, SUBLANES)
        if g not in loaded:
            loaded[g] = (gates_ref[g], h_ref[g * SUBLANES:(g + 1) * SUBLANES, :].astype(F32))
        gates, h8 = loaded[g]
        row = _expert_mix(words, h8[lane:lane + 1, :], gates[:, lane:lane + 1], d)
        o_ref[pos:pos + 1, :] = x_ref[pos:pos + 1, :] + row

    staged_done = 0
    for u in range(td):
        nxt = u + ahead
        if nxt < td:
            issue(par, nxt, nxt % MIX_SLOTS)
        else:
            @pl.when(step + 1 < n_steps)
            def _():
                if nxt == td:
                    ids_copy(step + 1, 1 - par).wait()
                issue(1 - par, nxt - td, nxt % MIX_SLOTS)

        staged_upto = (u + 1) * ts // td
        for k in range(staged_done, staged_upto):
            finish(td + k, rows_ref[k * picks:(k + 1) * picks, :])
        staged_done = staged_upto
        wait_rows(u % MIX_SLOTS)
        finish(u, buf[u % MIX_SLOTS])


def _peer_mix(ids_direct, uv, staged, x, h, gates3, *, td, ts):
    t, d = x.shape
    picks = gates3.shape[1]
    n_steps = t // (td + ts)
    assert staged.shape[0] == n_steps * ts * picks and ids_direct.shape == (n_steps, td * picks)
    return pl.pallas_call(
        functools.partial(_peer_mix_kernel, td=td, ts=ts, picks=picks, d=d, n_steps=n_steps),
        grid=(n_steps,),
        input_output_aliases={3: 0},
        in_specs=[
            pl.BlockSpec(memory_space=pl.ANY),
            pl.BlockSpec(memory_space=pl.ANY),
            pl.BlockSpec((ts * picks, d), lambda i: (i, 0)),
            pl.BlockSpec((td + ts, d), lambda i: (i, 0)),
            pl.BlockSpec((td + ts, d), lambda i: (i, 0)),
            pl.BlockSpec(((td + ts) // SUBLANES, picks, SUBLANES), lambda i: (i, 0, 0)),
        ],
        out_specs=pl.BlockSpec((td + ts, d), lambda i: (i, 0)),
        out_shape=jax.ShapeDtypeStruct((t, d), F32),
        scratch_shapes=[
            pltpu.SMEM((2, td * picks), jnp.int32),
            pltpu.VMEM((MIX_SLOTS, picks, d), jnp.int32),
            pltpu.SemaphoreType.DMA((2,)),
            pltpu.SemaphoreType.DMA((MIX_SLOTS,)),
        ],
        compiler_params=_params(("arbitrary",)),
    )(ids_direct, uv, staged, x, h, gates3)


def _sc_gather_rows(table, idx, *, chunk):
    n_rows = idx.shape[0]
    d = table.shape[1]
    workers = SC_CORES * SC_SUBCORES
    assert n_rows % (workers * 2 * chunk) == 0 and chunk % SUBLANES == 0 and chunk <= LANES
    rows_per_worker = n_rows // workers
    n_pairs = rows_per_worker // (2 * chunk)
    mesh = plsc.VectorSubcoreMesh(core_axis_name="c", subcore_axis_name="s")

    @functools.partial(
        pl.kernel, mesh=mesh,
        out_type=jax.ShapeDtypeStruct((n_rows, d), table.dtype),
        scratch_types=[
            pltpu.VMEM((chunk,), jnp.int32), pltpu.VMEM((chunk,), jnp.int32),
            pltpu.VMEM((chunk, d), table.dtype), pltpu.VMEM((chunk, d), table.dtype),
            pltpu.SemaphoreType.DMA, pltpu.SemaphoreType.DMA,
            pltpu.SemaphoreType.DMA, pltpu.SemaphoreType.DMA,
        ],
    )
    def gather_kernel(table_hbm, idx_hbm, out_hbm, idx0, idx1, rows0, rows1, gsem0, gsem1, wsem0, wsem1):
        idx_v, rows_v, gsem, wsem = (idx0, idx1), (rows0, rows1), (gsem0, gsem1), (wsem0, wsem1)
        worker = lax.axis_index("s") * SC_CORES + lax.axis_index("c")
        base = worker * rows_per_worker

        def out_rows(c):
            return pl.ds(pl.multiple_of(base + c * chunk, SUBLANES), chunk)

        def load_idx(slot, c):
            pltpu.sync_copy(idx_hbm.at[out_rows(c)], idx_v[slot])

        def gather(slot):
            return pltpu.make_async_copy(table_hbm.at[idx_v[slot]], rows_v[slot], gsem[slot])

        def writeout(slot, c):
            return pltpu.make_async_copy(rows_v[slot], out_hbm.at[out_rows(c)], wsem[slot])

        load_idx(0, 0)
        gather(0).start()

        @pl.loop(0, n_pairs)
        def _(p):
            c0 = 2 * p

            @pl.when(p > 0)
            def _():
                writeout(1, c0 - 1).wait()

            load_idx(1, c0 + 1)
            gather(1).start()
            gather(0).wait()
            writeout(0, c0).start()

            @pl.when(p + 1 < n_pairs)
            def _():
                load_idx(0, c0 + 2)
                writeout(0, c0).wait()
                gather(0).start()

            gather(1).wait()
            writeout(1, c0 + 1).start()

        writeout(0, 2 * n_pairs - 2).wait()
        writeout(1, 2 * n_pairs - 1).wait()

    return gather_kernel(table, idx)


def _pad_cols(w, width):
    return jnp.pad(w, ((0, 0), (0, width - w.shape[1])))


def _prepare_layer(g_norm_mix, w_in, b_gate, g_cq, w_uq, g_ckv, w_ukv, g_qn, g_qr, g_kn, g_kr,
                   w_a2, b_a, g_gla_out, w_out, g_norm_ffn, w_pq, sub_keys, u_emb, v_emb):
    d = w_in.shape[0]
    q_rank, kv_rank = g_cq.shape[0], g_ckv.shape[0]
    nope, rope = g_qn.shape[0], g_qr.shape[0]
    mla_heads = w_uq.shape[1] // (nope + rope)
    mla_v = w_ukv.shape[1] // mla_heads - nope
    gate_rank, gla_dk_all = w_a2.shape
    gla_dv = g_gla_out.shape[0]
    gla_heads = d // gla_dv
    gla_dk = gla_dk_all // gla_heads
    peer_heads, _, nkeys, half = sub_keys.shape
    assert nope == LANES and mla_v == LANES and rope <= LANES and rope % 2 == 0
    assert mla_heads * mla_v == d and gla_heads * gla_dv == d
    assert rope + gate_rank <= LANES and nkeys == LANES and half == LANES

    widths = (q_rank, kv_rank, rope, gla_dk_all, gla_dk_all, d, gate_rank, d, d, d)
    offs = [0]
    for wd in widths:
        offs.append(offs[-1] + wd)
    assert offs[-1] == w_in.shape[1]
    seg = lambda i: w_in[:, offs[i]:offs[i + 1]]
    w_main = jnp.concatenate([seg(5), seg(7), seg(8), seg(9), seg(3), seg(4), seg(0), seg(1)], axis=1).astype(BF16)
    w_small = _pad_cols(jnp.concatenate([seg(2), seg(6)], axis=1), LANES).astype(BF16)
    cq_off = 4 * d + 2 * gla_dk_all
    assert cq_off % q_rank == 0 and (cq_off + q_rank) % kv_rank == 0
    inv_freq = ROPE_THETA ** (-jnp.arange(0, rope, 2, dtype=F32) / rope)
    scale = (nope + rope) ** -0.5
    return dict(
        dims=dict(q_rank=q_rank, kv_rank=kv_rank, nope=nope, rope=rope, mla_heads=mla_heads, mla_v=mla_v,
                  gla_heads=gla_heads, gla_dk=gla_dk, gla_dv=gla_dv, peer_heads=peer_heads, nkeys=nkeys, half=half,
                  v_blk=0, og_blk=d // gla_dv, ga_blk=2, gb_blk=3, q_blk=4 * d // gla_dk,
                  k_blk=4 * d // gla_dk + gla_heads, cq_blk=cq_off // q_rank, ckv_blk=(cq_off + q_rank) // kv_rank),
        g_norm_mix=g_norm_mix[None, :], w_main=w_main, w_small=w_small,
        invf=_pad_cols(jnp.concatenate([inv_freq, inv_freq])[None, :], LANES),
        gq=_pad_cols(jnp.concatenate([g_qn, g_qr])[None, :] * scale, nope + LANES),
        gkr=_pad_cols(g_kr[None, :], LANES), g_cq=g_cq[None, :], g_ckv=g_ckv[None, :], g_kn=g_kn[None, :],
        w_uq=jnp.pad(w_uq.reshape(q_rank, mla_heads, nope + rope),
                     ((0, 0), (0, 0), (0, LANES - rope))).reshape(q_rank, -1).astype(BF16),
        w_ukv=w_ukv.astype(BF16),
        w2p=jnp.zeros((LANES, gla_dk_all), F32).at[rope:rope + gate_rank].set(w_a2).astype(BF16),
        b_a=b_a[None, :], g_on=g_gla_out[None, :], b_gate=b_gate, w_out=w_out.astype(BF16),
        g_norm_ffn=g_norm_ffn[None, :], w_pq=w_pq.astype(BF16),
        sk2d=sub_keys.reshape(peer_heads * 2 * nkeys, half).astype(BF16),
        uv=_pack_expert_table(u_emb, v_emb),
    )


def _mixers_and_route(x2, pos, w, *, batch, seq, tiles):
    t = x2.shape[0]
    dm = w["dims"]
    p, small = _in_proj(x2, w["g_norm_mix"], w["w_main"], w["w_small"], tm=tiles["in_tm"], tn=tiles["in_tn"])
    q = _mla_q(p, dm["cq_blk"], w["g_cq"], w["w_uq"], w["gq"], pos, w["invf"], heads=dm["mla_heads"],
               rank=dm["q_rank"], nope=dm["nope"], rope=dm["rope"], tm=tiles["mla_tm"])
    k, v = _mla_kv(p, dm["ckv_blk"], w["g_ckv"], w["w_ukv"], w["g_kn"], small, w["gkr"], pos, w["invf"],
                   heads=dm["mla_heads"], rank=dm["kv_rank"], nope=dm["nope"], rope=dm["rope"], dv=dm["mla_v"],
                   tm=tiles["mla_tm"])
    o_mla = _mla_attn(q, k, v, batch=batch, seq=seq, heads=dm["mla_heads"], dk=dm["nope"] + LANES, dv=dm["mla_v"],
                      tq=tiles["attn_tq"], tk=tiles["attn_tk"])
    o_gla = _gla(p, small, w["w2p"], w["b_a"], w["g_on"], batch=batch, seq=seq, heads=dm["gla_heads"],
                 dk=dm["gla_dk"], dv=dm["gla_dv"], q_blk=dm["q_blk"], k_blk=dm["k_blk"], v_blk=dm["v_blk"],
                 og_blk=dm["og_blk"])
    x2 = _out_proj(x2, p, o_mla, o_gla, w["b_gate"], w["w_out"], ga_blk=dm["ga_blk"], gb_blk=dm["gb_blk"],
                   tm=tiles["out_tm"], tn=tiles["out_tn"])
    h2, ids_t, gates_t = _peer_route(x2, w["g_norm_ffn"], w["w_pq"], w["sk2d"], heads=dm["peer_heads"],
                                     nkeys=dm["nkeys"], half=dm["half"], topk=PEER_TOPK, tm=tiles["route_tm"])
    picks = dm["peer_heads"] * PEER_TOPK
    gates3 = gates_t.reshape(picks, t // SUBLANES, SUBLANES).transpose(1, 0, 2)
    return x2, h2, ids_t.T, gates3


def _layer(x2, pos, w, *, batch, seq, tiles):
    t, d = x2.shape
    splits = tiles["mix_split"]
    groups = len(splits)
    tg, bg = t // groups, batch // groups
    routed = [_mixers_and_route(x2[g * tg:(g + 1) * tg], pos[g * tg:(g + 1) * tg], w,
                                batch=bg, seq=seq, tiles=tiles) for g in range(groups)]
    out = []
    for (td, ts), (xg, hg, ids_tok, gates3) in zip(splits, routed):
        picks = ids_tok.shape[1]
        ids3 = ids_tok.reshape(tg // (td + ts), td + ts, picks)
        staged = _sc_gather_rows(w["uv"], ids3[:, td:].reshape(-1), chunk=tiles["sc_chunk"])
        out.append(_peer_mix(ids3[:, :td].reshape(-1, td * picks), w["uv"], staged, xg, hg, gates3, td=td, ts=ts))
    return jnp.concatenate(out, axis=0)


_TILES = dict(in_tm=2048, in_tn=512, mla_tm=2048, attn_tq=1024, attn_tk=1024,
              out_tm=1024, out_tn=512, route_tm=256, mix_split=((4, 12),) * 8, sc_chunk=16)


def kernel(x, positions, g_norm_mix, w_in, b_gate, g_cq, w_uq, g_ckv, w_ukv, g_qn, g_qr, g_kn, g_kr,
           w_a2, b_a, g_gla_out, w_out, g_norm_ffn, w_pq, sub_keys, u_emb, v_emb, tiles=None):
    tiles = _TILES if tiles is None else tiles
    batch, seq, d = x.shape
    x2 = x.reshape(batch * seq, d)
    pos = positions.reshape(batch * seq, 1)
    for l in range(g_norm_mix.shape[0]):
        w = _prepare_layer(g_norm_mix[l], w_in[l], b_gate[l], g_cq[l], w_uq[l], g_ckv[l], w_ukv[l], g_qn[l],
                           g_qr[l], g_kn[l], g_kr[l], w_a2[l], b_a[l], g_gla_out[l], w_out[l], g_norm_ffn[l],
                           w_pq[l], sub_keys[l], u_emb[l], v_emb[l])
        x2 = _layer(x2, pos, w, batch=batch, seq=seq, tiles=tiles)
    return x2.reshape(batch, seq, d)
```

```python
import functools

import jax
import jax.numpy as jnp
from jax import lax
from jax.experimental import pallas as pl
from jax.experimental.pallas import tpu as pltpu
from jax.experimental.pallas import tpu_sc as plsc

EPS = 1e-6
ROPE_THETA = 10000.0
GLA_TAU = 16.0
GLA_CHUNK = 64
PEER_TOPK = 16

LANES = 128
SUBLANES = 8
VMEM_LIMIT_BYTES = 56 * 1024 * 1024
MIX_SLOTS = 4
SC_CORES = 2
SC_SUBCORES = 16

F32 = jnp.float32
BF16 = jnp.bfloat16
NEG_INF = float("-inf")


def _params(semantics):
    return pltpu.CompilerParams(dimension_semantics=semantics, vmem_limit_bytes=VMEM_LIMIT_BYTES)


def _rms(x, gain, n=None):
    ss = jnp.sum(x * x, axis=-1, keepdims=True)
    n = x.shape[-1] if n is None else n
    return x * lax.rsqrt(ss * (1.0 / n) + EPS) * gain


def _gelu_exact(x):
    return 0.5 * x * (1.0 + lax.erf(x * (0.5 ** 0.5)))


def _dot(a, b):
    return jnp.dot(a, b, preferred_element_type=F32)


def _dot_nt(a, b):
    return lax.dot_general(a, b, (((1,), (1,)), ((), ())), preferred_element_type=F32)


def _dot_tn(a, b):
    return lax.dot_general(a, b, (((0,), (0,)), ((), ())), preferred_element_type=F32)


def _in_proj_kernel(x_ref, g_ref, w_ref, ws_ref, p_ref, ps_ref, h_scr):
    @pl.when(pl.program_id(1) == 0)
    def _():
        h = _rms(x_ref[...], g_ref[...]).astype(BF16)
        h_scr[...] = h
        ps_ref[...] = _dot(h, ws_ref[...])

    p_ref[...] = _dot(h_scr[...], w_ref[...]).astype(p_ref.dtype)


def _in_proj(x, g, w_main, w_small, *, row0, t, tm, tn):
    d = x.shape[1]
    n = w_main.shape[1]
    assert row0 % tm == 0
    b0 = row0 // tm
    x_mode = pl.Buffered(1) if t == tm else None
    return pl.pallas_call(
        _in_proj_kernel,
        grid=(t // tm, n // tn),
        in_specs=[
            pl.BlockSpec((tm, d), lambda i, j: (i + b0, 0), pipeline_mode=x_mode),
            pl.BlockSpec((1, d), lambda i, j: (0, 0)),
            pl.BlockSpec((d, tn), lambda i, j: (0, j)),
            pl.BlockSpec((d, LANES), lambda i, j: (0, 0)),
        ],
        out_specs=[
            pl.BlockSpec((tm, tn), lambda i, j: (i, j)),
            pl.BlockSpec((tm, LANES), lambda i, j: (i, 0)),
        ],
        out_shape=[
            jax.ShapeDtypeStruct((t, n), BF16),
            jax.ShapeDtypeStruct((t, LANES), F32),
        ],
        scratch_shapes=[pltpu.VMEM((tm, d), BF16)],
        compiler_params=_params(("parallel", "arbitrary")),
    )(x, g, w_main, w_small)


def _rope_tables(pos_ref, invf_ref, rope):
    ang = pos_ref[...].astype(F32) * invf_ref[...]
    cos, sin = jnp.cos(ang), jnp.sin(ang)
    lane = lax.broadcasted_iota(jnp.int32, ang.shape, 1)
    half = rope // 2
    c = jnp.where(lane < rope, cos, 0.0)
    s_lo = jnp.where(lane < half, -sin, 0.0)
    s_hi = jnp.where(lane < half, 0.0, jnp.where(lane < rope, sin, 0.0))
    return c, s_lo, s_hi


def _apply_rope(pe, c, s_lo, s_hi, rope):
    half = rope // 2
    from_hi = pltpu.roll(pe, LANES - half, 1)
    from_lo = pltpu.roll(pe, half, 1)
    return pe * c + from_hi * s_lo + from_lo * s_hi


def _mla_q_kernel(cq_ref, gcq_ref, w_ref, gq_ref, pos_ref, invf_ref, q_ref,
                  h_scr, c_scr, slo_scr, shi_scr, *, nope, rope):
    @pl.when(pl.program_id(1) == 0)
    def _():
        h_scr[...] = _rms(cq_ref[...].astype(F32), gcq_ref[...]).astype(BF16)
        c, s_lo, s_hi = _rope_tables(pos_ref, invf_ref, rope)
        c_scr[...] = c
        slo_scr[...] = s_lo
        shi_scr[...] = s_hi

    y = _dot(h_scr[...], w_ref[...])
    g = gq_ref[...]
    qn = _rms(y[:, :nope], g[:, :nope])
    pe = _rms(y[:, nope:], g[:, nope:], n=rope)
    pe = _apply_rope(pe, c_scr[...], slo_scr[...], shi_scr[...], rope)
    q_ref[:, :nope] = qn.astype(q_ref.dtype)
    q_ref[:, nope:] = pe.astype(q_ref.dtype)


def _mla_q(p, cq_blk, g_cq, w_uq_p, gq, pos, invf, *, heads, rank, nope, rope, tm):
    t = p.shape[0]
    hw = nope + LANES
    return pl.pallas_call(
        functools.partial(_mla_q_kernel, nope=nope, rope=rope),
        grid=(t // tm, heads),
        in_specs=[
            pl.BlockSpec((tm, rank), lambda i, j: (i, cq_blk)),
            pl.BlockSpec((1, rank), lambda i, j: (0, 0)),
            pl.BlockSpec((rank, hw), lambda i, j: (0, j)),
            pl.BlockSpec((1, hw), lambda i, j: (0, 0)),
            pl.BlockSpec((tm, 1), lambda i, j: (i, 0)),
            pl.BlockSpec((1, LANES), lambda i, j: (0, 0)),
        ],
        out_specs=pl.BlockSpec((tm, hw), lambda i, j: (i, j)),
        out_shape=jax.ShapeDtypeStruct((t, heads * hw), BF16),
        scratch_shapes=[
            pltpu.VMEM((tm, rank), BF16),
            pltpu.VMEM((tm, LANES), F32),
            pltpu.VMEM((tm, LANES), F32),
            pltpu.VMEM((tm, LANES), F32),
        ],
        compiler_params=_params(("parallel", "arbitrary")),
    )(p, g_cq, w_uq_p, gq, pos, invf)


def _mla_kv_kernel(ckv_ref, gckv_ref, w_ref, gkn_ref, small_ref, gkr_ref, pos_ref, invf_ref,
                   k_ref, v_ref, h_scr, kpe_scr, *, nope, rope):
    @pl.when(pl.program_id(1) == 0)
    def _():
        h_scr[...] = _rms(ckv_ref[...].astype(F32), gckv_ref[...]).astype(BF16)
        c, s_lo, s_hi = _rope_tables(pos_ref, invf_ref, rope)
        sm = small_ref[...]
        lane = lax.broadcasted_iota(jnp.int32, sm.shape, 1)
        pe = _rms(jnp.where(lane < rope, sm, 0.0), gkr_ref[...], n=rope)
        kpe_scr[...] = _apply_rope(pe, c, s_lo, s_hi, rope).astype(BF16)

    y = _dot(h_scr[...], w_ref[...])
    k_ref[:, :nope] = _rms(y[:, :nope], gkn_ref[...]).astype(k_ref.dtype)
    k_ref[:, nope:] = kpe_scr[...]
    v_ref[...] = y[:, nope:].astype(v_ref.dtype)


def _mla_kv(p, ckv_blk, g_ckv, w_ukv, g_kn, small, gkr, pos, invf, *, heads, rank, nope, rope, dv, tm):
    t = p.shape[0]
    kw = nope + LANES
    return pl.pallas_call(
        functools.partial(_mla_kv_kernel, nope=nope, rope=rope),
        grid=(t // tm, heads),
        in_specs=[
            pl.BlockSpec((tm, rank), lambda i, j: (i, ckv_blk)),
            pl.BlockSpec((1, rank), lambda i, j: (0, 0)),
            pl.BlockSpec((rank, nope + dv), lambda i, j: (0, j)),
            pl.BlockSpec((1, nope), lambda i, j: (0, 0)),
            pl.BlockSpec((tm, LANES), lambda i, j: (i, 0)),
            pl.BlockSpec((1, LANES), lambda i, j: (0, 0)),
            pl.BlockSpec((tm, 1), lambda i, j: (i, 0)),
            pl.BlockSpec((1, LANES), lambda i, j: (0, 0)),
        ],
        out_specs=[
            pl.BlockSpec((tm, kw), lambda i, j: (i, j)),
            pl.BlockSpec((tm, dv), lambda i, j: (i, j)),
        ],
        out_shape=[
            jax.ShapeDtypeStruct((t, heads * kw), BF16),
            jax.ShapeDtypeStruct((t, heads * dv), BF16),
        ],
        scratch_shapes=[pltpu.VMEM((tm, rank), BF16), pltpu.VMEM((tm, LANES), BF16)],
        compiler_params=_params(("parallel", "arbitrary")),
    )(p, g_ckv, w_ukv, g_kn, small, gkr, pos, invf)


def _attn_kernel(q_ref, k_ref, v_ref, o_ref, *, tq, tk):
    qi = pl.program_id(2)
    q = q_ref[...]
    dv = v_ref.shape[1]
    row = qi * tq + lax.broadcasted_iota(jnp.int32, (tq, tk), 0)
    col0 = lax.broadcasted_iota(jnp.int32, (tq, tk), 1)

    def body(kb, carry, masked):
        m, l, acc = carry
        k0 = pl.multiple_of(kb * tk, tk)
        s = _dot_nt(q, k_ref[pl.ds(k0, tk), :])
        if masked:
            s = jnp.where(col0 + k0 <= row, s, NEG_INF)
        m_new = jnp.maximum(m, jnp.max(s, axis=-1, keepdims=True))
        alpha = jnp.exp(m - m_new)
        pr = jnp.exp(s - m_new)
        l = alpha * l + jnp.sum(pr, axis=-1, keepdims=True)
        acc = alpha * acc + _dot(pr.astype(BF16), v_ref[pl.ds(k0, tk), :])
        return m_new, l, acc

    init = (jnp.full((tq, 1), NEG_INF, F32), jnp.zeros((tq, 1), F32), jnp.zeros((tq, dv), F32))
    n_below = (qi * tq) // tk
    nkb = ((qi + 1) * tq + tk - 1) // tk
    carry = lax.fori_loop(0, n_below, functools.partial(body, masked=False), init)
    _, l, acc = lax.fori_loop(n_below, nkb, functools.partial(body, masked=True), carry)
    o_ref[...] = (acc / l).astype(o_ref.dtype)


def _mla_attn(q, k, v, *, batch, seq, heads, dk, dv, tq, tk):
    t = q.shape[0]
    nq = seq // tq
    return pl.pallas_call(
        functools.partial(_attn_kernel, tq=tq, tk=tk),
        grid=(batch, heads, nq),
        in_specs=[
            pl.BlockSpec((tq, dk), lambda b, h, i: (b * nq + i, h)),
            pl.BlockSpec((seq, dk), lambda b, h, i: (b, h)),
            pl.BlockSpec((seq, dv), lambda b, h, i: (b, h)),
        ],
        out_specs=pl.BlockSpec((tq, dv), lambda b, h, i: (b * nq + i, h)),
        out_shape=jax.ShapeDtypeStruct((t, heads * dv), BF16),
        compiler_params=_params(("parallel", "parallel", "arbitrary")),
    )(q, k, v)


def _gla_kernel(q_ref, k_ref, v_ref, og_ref, small_ref, w2_ref, ba_ref, gon_ref, o_ref, st_scr,
                *, seq, dk, dv, chunk):
    c = chunk
    st_scr[...] = jnp.zeros_like(st_scr)
    r_i = lax.broadcasted_iota(jnp.int32, (c, c), 0)
    c_i = lax.broadcasted_iota(jnp.int32, (c, c), 1)
    tri = jnp.where(c_i <= r_i, 1.0, 0.0).astype(BF16)
    row_id = lax.broadcasted_iota(jnp.int32, (c, 1), 0)
    w2 = w2_ref[...]
    ba = ba_ref[...]
    gon = gon_ref[...]
    q_scale = dk ** -0.5

    def chunk_step(ci, carry):
        r0 = pl.multiple_of(ci * c, c)
        qc = q_ref[pl.ds(r0, c), :].astype(F32) * q_scale
        kc = k_ref[pl.ds(r0, c), :].astype(F32)
        vc = v_ref[pl.ds(r0, c), :]
        z = _dot(small_ref[pl.ds(r0, c), :].astype(BF16), w2) + ba
        la = jax.nn.log_sigmoid(z) * (1.0 / GLA_TAU)
        hi = la.astype(BF16)
        r1 = la - hi.astype(F32)
        mid = r1.astype(BF16)
        lo = (r1 - mid.astype(F32)).astype(BF16)
        b = _dot(tri, hi) + _dot(tri, mid) + _dot(tri, lo)

        st = st_scr[...]
        inter = _dot_nt((qc * jnp.exp(b)).astype(BF16), st.astype(BF16))

        att = jnp.zeros((c, c), F32)
        for j in range(c):
            lo_r = (j // SUBLANES) * SUBLANES
            d = b[lo_r:, :] - b[j:j + 1, :]
            head = jnp.where(row_id[lo_r:lo_r + SUBLANES, :] >= j, d[:SUBLANES, :], NEG_INF)
            e = jnp.exp(jnp.concatenate([head, d[SUBLANES:, :]], axis=0) if lo_r + SUBLANES < c else head)
            col = jnp.sum(qc[lo_r:, :] * kc[j:j + 1, :] * e, axis=-1, keepdims=True)
            if lo_r:
                col = jnp.concatenate([jnp.zeros((lo_r, 1), F32), col], axis=0)
            att = jnp.where(c_i == j, col, att)
        o = inter + _dot(att.astype(BF16), vc)

        b_last = b[c - 1:c, :]
        k_dec = (kc * jnp.exp(b_last - b)).astype(BF16)
        st_scr[...] = st * jnp.exp(b_last) + _dot_tn(vc, k_dec)

        og = og_ref[pl.ds(r0, c), :].astype(F32)
        out = _rms(o, gon) * (og * jax.nn.sigmoid(og))
        o_ref[pl.ds(r0, c), :] = out.astype(o_ref.dtype)
        return carry

    lax.fori_loop(0, seq // c, chunk_step, 0)


def _gla(p, small, w2p, b_a, g_on, *, batch, seq, heads, dk, dv, q_blk, k_blk, v_blk, og_blk):
    t = p.shape[0]
    return pl.pallas_call(
        functools.partial(_gla_kernel, seq=seq, dk=dk, dv=dv, chunk=GLA_CHUNK),
        grid=(batch, heads),
        in_specs=[
            pl.BlockSpec((seq, dk), lambda b, h: (b, q_blk + h)),
            pl.BlockSpec((seq, dk), lambda b, h: (b, k_blk + h)),
            pl.BlockSpec((seq, dv), lambda b, h: (b, v_blk + h)),
            pl.BlockSpec((seq, dv), lambda b, h: (b, og_blk + h)),
            pl.BlockSpec((seq, LANES), lambda b, h: (b, 0)),
            pl.BlockSpec((LANES, dk), lambda b, h: (0, h)),
            pl.BlockSpec((1, dk), lambda b, h: (0, h)),
            pl.BlockSpec((1, dv), lambda b, h: (0, 0)),
        ],
        out_specs=pl.BlockSpec((seq, dv), lambda b, h: (b, h)),
        out_shape=jax.ShapeDtypeStruct((t, heads * dv), BF16),
        scratch_shapes=[pltpu.VMEM((dv, dk), F32)],
        compiler_params=_params(("parallel", "parallel")),
    )(p, p, p, p, small, w2p, b_a, g_on)


def _out_proj_kernel(x_ref, ga_ref, gb_ref, oa_ref, ob_ref, bg_ref, w_ref, o_ref, m_scr):
    @pl.when(pl.program_id(1) == 0)
    def _():
        bg = bg_ref[...]
        sa = jax.nn.sigmoid(ga_ref[...].astype(F32) + bg[0:1, :])
        sb = jax.nn.sigmoid(gb_ref[...].astype(F32) + bg[1:2, :])
        m_scr[...] = (sa * oa_ref[...].astype(F32) + sb * ob_ref[...].astype(F32)).astype(BF16)

    o_ref[...] = x_ref[...] + _dot(m_scr[...], w_ref[...])


def _out_proj(x, p, o_mla, o_gla, b_gate, w_out, *, row0, ga_blk, gb_blk, tm, tn):
    t, d = p.shape[0], x.shape[1]
    assert row0 % tm == 0
    b0 = row0 // tm
    return pl.pallas_call(
        _out_proj_kernel,
        grid=(t // tm, d // tn),
        in_specs=[
            pl.BlockSpec((tm, tn), lambda i, j: (i + b0, j)),
            pl.BlockSpec((tm, d), lambda i, j: (i, ga_blk)),
            pl.BlockSpec((tm, d), lambda i, j: (i, gb_blk)),
            pl.BlockSpec((tm, d), lambda i, j: (i, 0)),
            pl.BlockSpec((tm, d), lambda i, j: (i, 0)),
            pl.BlockSpec((2, d), lambda i, j: (0, 0)),
            pl.BlockSpec((d, tn), lambda i, j: (0, j)),
        ],
        out_specs=pl.BlockSpec((tm, tn), lambda i, j: (i, j)),
        out_shape=jax.ShapeDtypeStruct((t, d), F32),
        scratch_shapes=[pltpu.VMEM((tm, d), BF16)],
        compiler_params=_params(("parallel", "arbitrary")),
    )(x, p, p, o_mla, o_gla, b_gate, w_out)


def _topk_rows(s, k, payload=None):
    n = s.shape[0]
    iota = lax.broadcasted_iota(jnp.int32, s.shape, 0).astype(F32)
    kiota = lax.broadcasted_iota(jnp.int32, (k, s.shape[1]), 0)
    vals = jnp.zeros((k, s.shape[1]), F32)
    picks = jnp.zeros((k, s.shape[1]), F32)
    for r in range(k):
        m = jnp.max(s, axis=0, keepdims=True)
        idx = jnp.min(jnp.where(s == m, iota, float(n)), axis=0, keepdims=True)
        hit = iota == idx
        if payload is None:
            pick = idx
        else:
            pick = jnp.sum(jnp.where(hit, payload, 0.0), axis=0, keepdims=True)
        vals = jnp.where(kiota == r, m, vals)
        picks = jnp.where(kiota == r, pick, picks)
        s = jnp.where(hit, NEG_INF, s)
    return vals, picks


def _peer_route_kernel(x_ref, g_ref, w_ref, sk_ref, h_ref, ids_ref, gates_ref, *, heads, nkeys, half, topk):
    h = _rms(x_ref[...], g_ref[...]).astype(BF16)
    h_ref[...] = h
    qf = _dot(h, w_ref[...])
    for hd in range(heads):
        tops = []
        for part in range(2):
            o = (hd * 2 + part) * half
            qh = qf[:, o:o + half].astype(BF16)
            keys = sk_ref[(hd * 2 + part) * nkeys:(hd * 2 + part + 1) * nkeys, :]
            tops.append(_topk_rows(_dot_nt(keys, qh), topk))
        (s1, i1), (s2, i2) = tops
        assert topk == 2 * SUBLANES
        hs = SUBLANES
        pair_s = [s1[0:1, :] + s2] + [s1[a:a + 1, :] + s2[:hs, :] for a in range(1, hs)] + [s1[hs:, :] + s2[0:1, :]]
        pair_id = ([i1[0:1, :] * float(nkeys) + i2]
                   + [i1[a:a + 1, :] * float(nkeys) + i2[:hs, :] for a in range(1, hs)]
                   + [i1[hs:, :] * float(nkeys) + i2[0:1, :]])
        best_s, best_id = _topk_rows(jnp.concatenate(pair_s, axis=0), topk,
                                     payload=jnp.concatenate(pair_id, axis=0))
        e = jnp.exp(best_s - best_s[0:1, :])
        gate = e / jnp.sum(e, axis=0, keepdims=True)
        ids_ref[hd * topk:(hd + 1) * topk, :] = best_id.astype(jnp.int32)
        gates_ref[hd * topk:(hd + 1) * topk, :] = gate


def _peer_route(x, g, w_pq, sk2d, *, heads, nkeys, half, topk, tm):
    t, d = x.shape
    dq = w_pq.shape[1]
    return pl.pallas_call(
        functools.partial(_peer_route_kernel, heads=heads, nkeys=nkeys, half=half, topk=topk),
        grid=(t // tm,),
        in_specs=[
            pl.BlockSpec((tm, d), lambda i: (i, 0)),
            pl.BlockSpec((1, d), lambda i: (0, 0)),
            pl.BlockSpec((d, dq), lambda i: (0, 0)),
            pl.BlockSpec((heads * 2 * nkeys, half), lambda i: (0, 0)),
        ],
        out_specs=[
            pl.BlockSpec((tm, d), lambda i: (i, 0)),
            pl.BlockSpec((heads * topk, tm), lambda i: (0, i)),
            pl.BlockSpec((heads * topk, tm), lambda i: (0, i)),
        ],
        out_shape=[
            jax.ShapeDtypeStruct((t, d), BF16),
            jax.ShapeDtypeStruct((heads * topk, t), jnp.int32),
            jax.ShapeDtypeStruct((heads * topk, t), F32),
        ],
        compiler_params=_params(("parallel",)),
    )(x, g, w_pq, sk2d)


def _pack_expert_table(u_emb, v_emb):
    def pack(w):
        bits = lax.bitcast_convert_type(w.astype(BF16).astype(F32), jnp.uint32)
        half = w.shape[1] // 2
        return (bits[:, :half] >> 16) | (bits[:, half:] & jnp.uint32(0xFFFF0000))

    return lax.bitcast_convert_type(jnp.concatenate([pack(u_emb), pack(v_emb)], axis=1), jnp.int32)


def _unpack_words(w):
    lo = lax.bitcast_convert_type(w << 16, F32)
    hi = lax.bitcast_convert_type(w & jnp.int32(-65536), F32)
    return lo, hi


def _expert_mix(words, h, gate, d):
    half = d // 2
    u_lo, u_hi = _unpack_words(words[:, :half])
    act = jnp.sum(u_lo * h[:, :half] + u_hi * h[:, half:], axis=-1, keepdims=True)
    w = gate * _gelu_exact(act)
    v_lo, v_hi = _unpack_words(words[:, half:])
    return jnp.concatenate([jnp.sum(v_lo * w, axis=0, keepdims=True),
                            jnp.sum(v_hi * w, axis=0, keepdims=True)], axis=1)


def _peer_mix_kernel(ids_hbm, uv_hbm, rows_ref, x_ref, h_ref, gates_ref, o_ref, ids_smem, buf, ids_sems, row_sems,
                     *, td, ts, picks, d, n_steps):
    step = pl.program_id(0)
    par = lax.rem(step, 2)
    ahead = MIX_SLOTS - 1
    assert td % MIX_SLOTS == 0 and td > ahead and (td + ts) % SUBLANES == 0

    def ids_copy(s, row):
        return pltpu.make_async_copy(ids_hbm.at[s], ids_smem.at[row], ids_sems.at[row])

    def issue(row, tok, slot):
        for j in range(picks):
            eid = ids_smem[row, tok * picks + j]
            pltpu.make_async_copy(uv_hbm.at[pl.ds(eid, 1), :], buf.at[slot, pl.ds(j, 1), :],
                                  row_sems.at[slot]).start()

    def wait_rows(slot):
        pltpu.make_async_copy(uv_hbm.at[pl.ds(0, picks), :], buf.at[slot], row_sems.at[slot]).wait()

    @pl.when(step == 0)
    def _():
        first = ids_copy(0, 0)
        first.start()
        first.wait()
        for u in range(ahead):
            issue(0, u, u)

    @pl.when(step + 1 < n_steps)
    def _():
        ids_copy(step + 1, 1 - par).start()

    loaded = {}

    def finish(pos, words):
        g, lane = divmod(pos, SUBLANES)
        if g not in loaded:
            loaded[g] = (gates_ref[g], h_ref[g * SUBLANES:(g + 1) * SUBLANES, :].astype(F32))
        gates, h8 = loaded[g]
        row = _expert_mix(words, h8[lane:lane + 1, :], gates[:, lane:lane + 1], d)
        o_ref[pos:pos + 1, :] = x_ref[pos:pos + 1, :] + row

    staged_done = 0
    for u in range(td):
        nxt = u + ahead
        if nxt < td:
            issue(par, nxt, nxt % MIX_SLOTS)
        else:
            @pl.when(step + 1 < n_steps)
            def _():
                if nxt == td:
                    ids_copy(step + 1, 1 - par).wait()
                issue(1 - par, nxt - td, nxt % MIX_SLOTS)

        staged_upto = (u + 1) * ts // td
        for k in range(staged_done, staged_upto):
            finish(td + k, rows_ref[k * picks:(k + 1) * picks, :])
        staged_done = staged_upto
        wait_rows(u % MIX_SLOTS)
        finish(u, buf[u % MIX_SLOTS])


def _peer_mix(ids_direct, uv, staged, x, h, gates3, *, td, ts):
    t, d = x.shape
    picks = gates3.shape[1]
    n_steps = t // (td + ts)
    assert staged.shape[0] == n_steps * ts * picks and ids_direct.shape == (n_steps, td * picks)
    return pl.pallas_call(
        functools.partial(_peer_mix_kernel, td=td, ts=ts, picks=picks, d=d, n_steps=n_steps),
        grid=(n_steps,),
        input_output_aliases={3: 0},
        in_specs=[
            pl.BlockSpec(memory_space=pl.ANY),
            pl.BlockSpec(memory_space=pl.ANY),
            pl.BlockSpec((ts * picks, d), lambda i: (i, 0)),
            pl.BlockSpec((td + ts, d), lambda i: (i, 0)),
            pl.BlockSpec((td + ts, d), lambda i: (i, 0)),
            pl.BlockSpec(((td + ts) // SUBLANES, picks, SUBLANES), lambda i: (i, 0, 0)),
        ],
        out_specs=pl.BlockSpec((td + ts, d), lambda i: (i, 0)),
        out_shape=jax.ShapeDtypeStruct((t, d), F32),
        scratch_shapes=[
            pltpu.SMEM((2, td * picks), jnp.int32),
            pltpu.VMEM((MIX_SLOTS, picks, d), jnp.int32),
            pltpu.SemaphoreType.DMA((2,)),
            pltpu.SemaphoreType.DMA((MIX_SLOTS,)),
        ],
        compiler_params=_params(("arbitrary",)),
    )(ids_direct, uv, staged, x, h, gates3)


def _sc_gather_rows(table, idx, *, chunk):
    n_rows = idx.shape[0]
    d = table.shape[1]
    workers = SC_CORES * SC_SUBCORES
    assert n_rows % (workers * 2 * chunk) == 0 and chunk % SUBLANES == 0 and chunk <= LANES
    rows_per_worker = n_rows // workers
    n_pairs = rows_per_worker // (2 * chunk)
    mesh = plsc.VectorSubcoreMesh(core_axis_name="c", subcore_axis_name="s")

    @functools.partial(
        pl.kernel, mesh=mesh,
        out_type=jax.ShapeDtypeStruct((n_rows, d), table.dtype),
        scratch_types=[
            pltpu.VMEM((chunk,), jnp.int32), pltpu.VMEM((chunk,), jnp.int32),
            pltpu.VMEM((chunk, d), table.dtype), pltpu.VMEM((chunk, d), table.dtype),
            pltpu.SemaphoreType.DMA, pltpu.SemaphoreType.DMA,
            pltpu.SemaphoreType.DMA, pltpu.SemaphoreType.DMA,
        ],
    )
    def gather_kernel(table_hbm, idx_hbm, out_hbm, idx0, idx1, rows0, rows1, gsem0, gsem1, wsem0, wsem1):
        idx_v, rows_v, gsem, wsem = (idx0, idx1), (rows0, rows1), (gsem0, gsem1), (wsem0, wsem1)
        worker = lax.axis_index("s") * SC_CORES + lax.axis_index("c")
        base = worker * rows_per_worker

        def out_rows(c):
            return pl.ds(pl.multiple_of(base + c * chunk, SUBLANES), chunk)

        def load_idx(slot, c):
            pltpu.sync_copy(idx_hbm.at[out_rows(c)], idx_v[slot])

        def gather(slot):
            return pltpu.make_async_copy(table_hbm.at[idx_v[slot]], rows_v[slot], gsem[slot])

        def writeout(slot, c):
            return pltpu.make_async_copy(rows_v[slot], out_hbm.at[out_rows(c)], wsem[slot])

        load_idx(0, 0)
        gather(0).start()

        @pl.loop(0, n_pairs)
        def _(p):
            c0 = 2 * p

            @pl.when(p > 0)
            def _():
                writeout(1, c0 - 1).wait()

            load_idx(1, c0 + 1)
            gather(1).start()
            gather(0).wait()
            writeout(0, c0).start()

            @pl.when(p + 1 < n_pairs)
            def _():
                load_idx(0, c0 + 2)
                writeout(0, c0).wait()
                gather(0).start()

            gather(1).wait()
            writeout(1, c0 + 1).start()

        writeout(0, 2 * n_pairs - 2).wait()
        writeout(1, 2 * n_pairs - 1).wait()

    return gather_kernel(table, idx)


def _pad_cols(w, width):
    return jnp.pad(w, ((0, 0), (0, width - w.shape[1])))


def _prepare_layer(g_norm_mix, w_in, b_gate, g_cq, w_uq, g_ckv, w_ukv, g_qn, g_qr, g_kn, g_kr,
                   w_a2, b_a, g_gla_out, w_out, g_norm_ffn, w_pq, sub_keys, u_emb, v_emb):
    d = w_in.shape[0]
    q_rank, kv_rank = g_cq.shape[0], g_ckv.shape[0]
    nope, rope = g_qn.shape[0], g_qr.shape[0]
    mla_heads = w_uq.shape[1] // (nope + rope)
    mla_v = w_ukv.shape[1] // mla_heads - nope
    gate_rank, gla_dk_all = w_a2.shape
    gla_dv = g_gla_out.shape[0]
    gla_heads = d // gla_dv
    gla_dk = gla_dk_all // gla_heads
    peer_heads, _, nkeys, half = sub_keys.shape
    assert nope == LANES and mla_v == LANES and rope <= LANES and rope % 2 == 0
    assert mla_heads * mla_v == d and gla_heads * gla_dv == d
    assert rope + gate_rank <= LANES and nkeys == LANES and half == LANES

    widths = (q_rank, kv_rank, rope, gla_dk_all, gla_dk_all, d, gate_rank, d, d, d)
    offs = [0]
    for wd in widths:
        offs.append(offs[-1] + wd)
    assert offs[-1] == w_in.shape[1]
    seg = lambda i: w_in[:, offs[i]:offs[i + 1]]
    w_main = jnp.concatenate([seg(5), seg(7), seg(8), seg(9), seg(3), seg(4), seg(0), seg(1)], axis=1).astype(BF16)
    w_small = _pad_cols(jnp.concatenate([seg(2), seg(6)], axis=1), LANES).astype(BF16)
    cq_off = 4 * d + 2 * gla_dk_all
    assert cq_off % q_rank == 0 and (cq_off + q_rank) % kv_rank == 0
    inv_freq = ROPE_THETA ** (-jnp.arange(0, rope, 2, dtype=F32) / rope)
    scale = (nope + rope) ** -0.5
    return dict(
        dims=dict(q_rank=q_rank, kv_rank=kv_rank, nope=nope, rope=rope, mla_heads=mla_heads, mla_v=mla_v,
                  gla_heads=gla_heads, gla_dk=gla_dk, gla_dv=gla_dv, peer_heads=peer_heads, nkeys=nkeys, half=half,
                  v_blk=0, og_blk=d // gla_dv, ga_blk=2, gb_blk=3, q_blk=4 * d // gla_dk,
                  k_blk=4 * d // gla_dk + gla_heads, cq_blk=cq_off // q_rank, ckv_blk=(cq_off + q_rank) // kv_rank),
        g_norm_mix=g_norm_mix[None, :], w_main=w_main, w_small=w_small,
        invf=_pad_cols(jnp.concatenate([inv_freq, inv_freq])[None, :], LANES),
        gq=_pad_cols(jnp.concatenate([g_qn, g_qr])[None, :] * scale, nope + LANES),
        gkr=_pad_cols(g_kr[None, :], LANES), g_cq=g_cq[None, :], g_ckv=g_ckv[None, :], g_kn=g_kn[None, :],
        w_uq=jnp.pad(w_uq.reshape(q_rank, mla_heads, nope + rope),
                     ((0, 0), (0, 0), (0, LANES - rope))).reshape(q_rank, -1).astype(BF16),
        w_ukv=w_ukv.astype(BF16),
        w2p=jnp.zeros((LANES, gla_dk_all), F32).at[rope:rope + gate_rank].set(w_a2).astype(BF16),
        b_a=b_a[None, :], g_on=g_gla_out[None, :], b_gate=b_gate, w_out=w_out.astype(BF16),
        g_norm_ffn=g_norm_ffn[None, :], w_pq=w_pq.astype(BF16),
        sk2d=sub_keys.reshape(peer_heads * 2 * nkeys, half).astype(BF16),
        uv=_pack_expert_table(u_emb, v_emb),
    )


def _mixers_and_route(x_all, row0, pos, w, *, batch, seq, tiles):
    t = batch * seq
    dm = w["dims"]
    p, small = _in_proj(x_all, w["g_norm_mix"], w["w_main"], w["w_small"], row0=row0, t=t,
                        tm=tiles["in_tm"], tn=tiles["in_tn"])
    q = _mla_q(p, dm["cq_blk"], w["g_cq"], w["w_uq"], w["gq"], pos, w["invf"], heads=dm["mla_heads"],
               rank=dm["q_rank"], nope=dm["nope"], rope=dm["rope"], tm=tiles["mla_tm"])
    k, v = _mla_kv(p, dm["ckv_blk"], w["g_ckv"], w["w_ukv"], w["g_kn"], small, w["gkr"], pos, w["invf"],
                   heads=dm["mla_heads"], rank=dm["kv_rank"], nope=dm["nope"], rope=dm["rope"], dv=dm["mla_v"],
                   tm=tiles["mla_tm"])
    o_mla = _mla_attn(q, k, v, batch=batch, seq=seq, heads=dm["mla_heads"], dk=dm["nope"] + LANES, dv=dm["mla_v"],
                      tq=tiles["attn_tq"], tk=tiles["attn_tk"])
    o_gla = _gla(p, small, w["w2p"], w["b_a"], w["g_on"], batch=batch, seq=seq, heads=dm["gla_heads"],
                 dk=dm["gla_dk"], dv=dm["gla_dv"], q_blk=dm["q_blk"], k_blk=dm["k_blk"], v_blk=dm["v_blk"],
                 og_blk=dm["og_blk"])
    x2 = _out_proj(x_all, p, o_mla, o_gla, w["b_gate"], w["w_out"], row0=row0, ga_blk=dm["ga_blk"],
                   gb_blk=dm["gb_blk"], tm=tiles["out_tm"], tn=tiles["out_tn"])
    h2, ids_t, gates_t = _peer_route(x2, w["g_norm_ffn"], w["w_pq"], w["sk2d"], heads=dm["peer_heads"],
                                     nkeys=dm["nkeys"], half=dm["half"], topk=PEER_TOPK, tm=tiles["route_tm"])
    picks = dm["peer_heads"] * PEER_TOPK
    gates3 = gates_t.reshape(picks, t // SUBLANES, SUBLANES).transpose(1, 0, 2)
    return x2, h2, ids_t.T, gates3


def _layer(x2, pos, w, *, batch, seq, tiles):
    t, d = x2.shape
    splits = tiles["mix_split"]
    groups = len(splits)
    tg, bg = t // groups, batch // groups
    routed = [_mixers_and_route(x2, g * tg, pos[g * tg:(g + 1) * tg], w, batch=bg, seq=seq, tiles=tiles)
              for g in range(groups)]
    out = []
    for (td, ts), (xg, hg, ids_tok, gates3) in zip(splits, routed):
        picks = ids_tok.shape[1]
        ids3 = ids_tok.reshape(tg // (td + ts), td + ts, picks)
        staged = _sc_gather_rows(w["uv"], ids3[:, td:].reshape(-1), chunk=tiles["sc_chunk"])
        out.append(_peer_mix(ids3[:, :td].reshape(-1, td * picks), w["uv"], staged, xg, hg, gates3, td=td, ts=ts))
    return jnp.concatenate(out, axis=0)


_TILES = dict(in_tm=2048, in_tn=512, mla_tm=2048, attn_tq=1024, attn_tk=1024,
              out_tm=1024, out_tn=512, route_tm=256, mix_split=((4, 12),) * 8, sc_chunk=16)


def kernel(x, positions, g_norm_mix, w_in, b_gate, g_cq, w_uq, g_ckv, w_ukv, g_qn, g_qr, g_kn, g_kr,
           w_a2, b_a, g_gla_out, w_out, g_norm_ffn, w_pq, sub_keys, u_emb, v_emb, tiles=None):
    tiles = _TILES if tiles is None else tiles
    batch, seq, d = x.shape
    x2 = x.reshape(batch * seq, d)
    pos = positions.reshape(batch * seq, 1)
    for l in range(g_norm_mix.shape[0]):
        w = _prepare_layer(g_norm_mix[l], w_in[l], b_gate[l], g_cq[l], w_uq[l], g_ckv[l], w_ukv[l], g_qn[l],
                           g_qr[l], g_kn[l], g_kr[l], w_a2[l], b_a[l], g_gla_out[l], w_out[l], g_norm_ffn[l],
                           w_pq[l], sub_keys[l], u_emb[l], v_emb[l])
        x2 = _layer(x2, pos, w, batch=batch, seq=seq, tiles=tiles)
    return x2.reshape(batch, seq, d)
```

```python
import functools

import jax
import jax.numpy as jnp
from jax import lax
from jax.experimental import pallas as pl
from jax.experimental.pallas import tpu as pltpu
from jax.experimental.pallas import tpu_sc as plsc

EPS = 1e-6
ROPE_THETA = 10000.0
GLA_TAU = 16.0
GLA_CHUNK = 64
PEER_TOPK = 16

LANES = 128
SUBLANES = 8
VMEM_LIMIT_BYTES = 56 * 1024 * 1024
MIX_SLOTS = 4
SC_CORES = 2
SC_SUBCORES = 16

F32 = jnp.float32
BF16 = jnp.bfloat16
NEG_INF = float("-inf")


def _params(semantics):
    return pltpu.CompilerParams(dimension_semantics=semantics, vmem_limit_bytes=VMEM_LIMIT_BYTES)


def _rms(x, gain, n=None):
    ss = jnp.sum(x * x, axis=-1, keepdims=True)
    n = x.shape[-1] if n is None else n
    return x * lax.rsqrt(ss * (1.0 / n) + EPS) * gain


def _gelu_exact(x):
    return 0.5 * x * (1.0 + lax.erf(x * (0.5 ** 0.5)))


def _dot(a, b):
    return jnp.dot(a, b, preferred_element_type=F32)


def _dot_nt(a, b):
    return lax.dot_general(a, b, (((1,), (1,)), ((), ())), preferred_element_type=F32)


def _dot_tn(a, b):
    return lax.dot_general(a, b, (((0,), (0,)), ((), ())), preferred_element_type=F32)


def _in_proj_kernel(x_ref, g_ref, w_ref, ws_ref, p_ref, ps_ref, h_scr):
    @pl.when(pl.program_id(1) == 0)
    def _():
        h = _rms(x_ref[...], g_ref[...]).astype(BF16)
        h_scr[...] = h
        ps_ref[...] = _dot(h, ws_ref[...])

    p_ref[...] = _dot(h_scr[...], w_ref[...]).astype(p_ref.dtype)


def _in_proj(x, g, w_main, w_small, *, row0, t, tm, tn):
    d = x.shape[1]
    n = w_main.shape[1]
    assert row0 % tm == 0
    b0 = row0 // tm
    x_mode = pl.Buffered(1) if t == tm else None
    return pl.pallas_call(
        _in_proj_kernel,
        grid=(t // tm, n // tn),
        in_specs=[
            pl.BlockSpec((tm, d), lambda i, j: (i + b0, 0), pipeline_mode=x_mode),
            pl.BlockSpec((1, d), lambda i, j: (0, 0)),
            pl.BlockSpec((d, tn), lambda i, j: (0, j)),
            pl.BlockSpec((d, LANES), lambda i, j: (0, 0)),
        ],
        out_specs=[
            pl.BlockSpec((tm, tn), lambda i, j: (i, j)),
            pl.BlockSpec((tm, LANES), lambda i, j: (i, 0)),
        ],
        out_shape=[
            jax.ShapeDtypeStruct((t, n), BF16),
            jax.ShapeDtypeStruct((t, LANES), F32),
        ],
        scratch_shapes=[pltpu.VMEM((tm, d), BF16)],
        compiler_params=_params(("parallel", "arbitrary")),
    )(x, g, w_main, w_small)


def _rope_tables(pos_ref, invf_ref, rope):
    ang = pos_ref[...].astype(F32) * invf_ref[...]
    cos, sin = jnp.cos(ang), jnp.sin(ang)
    lane = lax.broadcasted_iota(jnp.int32, ang.shape, 1)
    half = rope // 2
    c = jnp.where(lane < rope, cos, 0.0)
    s_lo = jnp.where(lane < half, -sin, 0.0)
    s_hi = jnp.where(lane < half, 0.0, jnp.where(lane < rope, sin, 0.0))
    return c, s_lo, s_hi


def _apply_rope(pe, c, s_lo, s_hi, rope):
    half = rope // 2
    from_hi = pltpu.roll(pe, LANES - half, 1)
    from_lo = pltpu.roll(pe, half, 1)
    return pe * c + from_hi * s_lo + from_lo * s_hi


def _mla_q_kernel(cq_ref, gcq_ref, w_ref, gq_ref, pos_ref, invf_ref, q_ref,
                  h_scr, c_scr, slo_scr, shi_scr, *, nope, rope):
    @pl.when(pl.program_id(1) == 0)
    def _():
        h_scr[...] = _rms(cq_ref[...].astype(F32), gcq_ref[...]).astype(BF16)
        c, s_lo, s_hi = _rope_tables(pos_ref, invf_ref, rope)
        c_scr[...] = c
        slo_scr[...] = s_lo
        shi_scr[...] = s_hi

    y = _dot(h_scr[...], w_ref[...])
    g = gq_ref[...]
    qn = _rms(y[:, :nope], g[:, :nope])
    pe = _rms(y[:, nope:], g[:, nope:], n=rope)
    pe = _apply_rope(pe, c_scr[...], slo_scr[...], shi_scr[...], rope)
    q_ref[:, :nope] = qn.astype(q_ref.dtype)
    q_ref[:, nope:] = pe.astype(q_ref.dtype)


def _mla_q(p, cq_blk, g_cq, w_uq_p, gq, pos, invf, *, heads, rank, nope, rope, tm):
    t = p.shape[0]
    hw = nope + LANES
    return pl.pallas_call(
        functools.partial(_mla_q_kernel, nope=nope, rope=rope),
        grid=(t // tm, heads),
        in_specs=[
            pl.BlockSpec((tm, rank), lambda i, j: (i, cq_blk)),
            pl.BlockSpec((1, rank), lambda i, j: (0, 0)),
            pl.BlockSpec((rank, hw), lambda i, j: (0, j)),
            pl.BlockSpec((1, hw), lambda i, j: (0, 0)),
            pl.BlockSpec((tm, 1), lambda i, j: (i, 0)),
            pl.BlockSpec((1, LANES), lambda i, j: (0, 0)),
        ],
        out_specs=pl.BlockSpec((tm, hw), lambda i, j: (i, j)),
        out_shape=jax.ShapeDtypeStruct((t, heads * hw), BF16),
        scratch_shapes=[
            pltpu.VMEM((tm, rank), BF16),
            pltpu.VMEM((tm, LANES), F32),
            pltpu.VMEM((tm, LANES), F32),
            pltpu.VMEM((tm, LANES), F32),
        ],
        compiler_params=_params(("parallel", "arbitrary")),
    )(p, g_cq, w_uq_p, gq, pos, invf)


def _mla_kv_kernel(ckv_ref, gckv_ref, w_ref, gkn_ref, small_ref, gkr_ref, pos_ref, invf_ref,
                   k_ref, v_ref, h_scr, kpe_scr, *, nope, rope):
    @pl.when(pl.program_id(1) == 0)
    def _():
        h_scr[...] = _rms(ckv_ref[...].astype(F32), gckv_ref[...]).astype(BF16)
        c, s_lo, s_hi = _rope_tables(pos_ref, invf_ref, rope)
        sm = small_ref[...]
        lane = lax.broadcasted_iota(jnp.int32, sm.shape, 1)
        pe = _rms(jnp.where(lane < rope, sm, 0.0), gkr_ref[...], n=rope)
        kpe_scr[...] = _apply_rope(pe, c, s_lo, s_hi, rope).astype(BF16)

    y = _dot(h_scr[...], w_ref[...])
    k_ref[:, :nope] = _rms(y[:, :nope], gkn_ref[...]).astype(k_ref.dtype)
    k_ref[:, nope:] = kpe_scr[...]
    v_ref[...] = y[:, nope:].astype(v_ref.dtype)


def _mla_kv(p, ckv_blk, g_ckv, w_ukv, g_kn, small, gkr, pos, invf, *, heads, rank, nope, rope, dv, tm):
    t = p.shape[0]
    kw = nope + LANES
    return pl.pallas_call(
        functools.partial(_mla_kv_kernel, nope=nope, rope=rope),
        grid=(t // tm, heads),
        in_specs=[
            pl.BlockSpec((tm, rank), lambda i, j: (i, ckv_blk)),
            pl.BlockSpec((1, rank), lambda i, j: (0, 0)),
            pl.BlockSpec((rank, nope + dv), lambda i, j: (0, j)),
            pl.BlockSpec((1, nope), lambda i, j: (0, 0)),
            pl.BlockSpec((tm, LANES), lambda i, j: (i, 0)),
            pl.BlockSpec((1, LANES), lambda i, j: (0, 0)),
            pl.BlockSpec((tm, 1), lambda i, j: (i, 0)),
            pl.BlockSpec((1, LANES), lambda i, j: (0, 0)),
        ],
        out_specs=[
            pl.BlockSpec((tm, kw), lambda i, j: (i, j)),
            pl.BlockSpec((tm, dv), lambda i, j: (i, j)),
        ],
        out_shape=[
            jax.ShapeDtypeStruct((t, heads * kw), BF16),
            jax.ShapeDtypeStruct((t, heads * dv), BF16),
        ],
        scratch_shapes=[pltpu.VMEM((tm, rank), BF16), pltpu.VMEM((tm, LANES), BF16)],
        compiler_params=_params(("parallel", "arbitrary")),
    )(p, g_ckv, w_ukv, g_kn, small, gkr, pos, invf)


def _attn_kernel(q_ref, k_ref, v_ref, o_ref, *, tq, tk):
    qi = pl.program_id(2)
    q = q_ref[...]
    dv = v_ref.shape[1]
    row = qi * tq + lax.broadcasted_iota(jnp.int32, (tq, tk), 0)
    col0 = lax.broadcasted_iota(jnp.int32, (tq, tk), 1)

    def body(kb, carry, masked):
        m, l, acc = carry
        k0 = pl.multiple_of(kb * tk, tk)
        s = _dot_nt(q, k_ref[pl.ds(k0, tk), :])
        if masked:
            s = jnp.where(col0 + k0 <= row, s, NEG_INF)
        m_new = jnp.maximum(m, jnp.max(s, axis=-1, keepdims=True))
        alpha = jnp.exp(m - m_new)
        pr = jnp.exp(s - m_new)
        l = alpha * l + jnp.sum(pr, axis=-1, keepdims=True)
        acc = alpha * acc + _dot(pr.astype(BF16), v_ref[pl.ds(k0, tk), :])
        return m_new, l, acc

    init = (jnp.full((tq, 1), NEG_INF, F32), jnp.zeros((tq, 1), F32), jnp.zeros((tq, dv), F32))
    n_below = (qi * tq) // tk
    nkb = ((qi + 1) * tq + tk - 1) // tk
    carry = lax.fori_loop(0, n_below, functools.partial(body, masked=False), init)
    _, l, acc = lax.fori_loop(n_below, nkb, functools.partial(body, masked=True), carry)
    o_ref[...] = (acc / l).astype(o_ref.dtype)


def _mla_attn(q, k, v, *, batch, seq, heads, dk, dv, tq, tk):
    t = q.shape[0]
    nq = seq // tq
    return pl.pallas_call(
        functools.partial(_attn_kernel, tq=tq, tk=tk),
        grid=(batch, heads, nq),
        in_specs=[
            pl.BlockSpec((tq, dk), lambda b, h, i: (b * nq + i, h)),
            pl.BlockSpec((seq, dk), lambda b, h, i: (b, h)),
            pl.BlockSpec((seq, dv), lambda b, h, i: (b, h)),
        ],
        out_specs=pl.BlockSpec((tq, dv), lambda b, h, i: (b * nq + i, h)),
        out_shape=jax.ShapeDtypeStruct((t, heads * dv), BF16),
        compiler_params=_params(("parallel", "parallel", "arbitrary")),
    )(q, k, v)


def _gla_kernel(q_ref, k_ref, v_ref, og_ref, small_ref, w2_ref, ba_ref, gon_ref, o_ref, st_scr,
                *, seq, dk, dv, chunk):
    c = chunk
    st_scr[...] = jnp.zeros_like(st_scr)
    r_i = lax.broadcasted_iota(jnp.int32, (c, c), 0)
    c_i = lax.broadcasted_iota(jnp.int32, (c, c), 1)
    tri = jnp.where(c_i <= r_i, 1.0, 0.0).astype(BF16)
    row_id = lax.broadcasted_iota(jnp.int32, (c, 1), 0)
    w2 = w2_ref[...]
    ba = ba_ref[...]
    gon = gon_ref[...]
    q_scale = dk ** -0.5

    def chunk_step(ci, carry):
        r0 = pl.multiple_of(ci * c, c)
        qc = q_ref[pl.ds(r0, c), :].astype(F32) * q_scale
        kc = k_ref[pl.ds(r0, c), :].astype(F32)
        vc = v_ref[pl.ds(r0, c), :]
        z = _dot(small_ref[pl.ds(r0, c), :].astype(BF16), w2) + ba
        la = jax.nn.log_sigmoid(z) * (1.0 / GLA_TAU)
        hi = la.astype(BF16)
        r1 = la - hi.astype(F32)
        mid = r1.astype(BF16)
        lo = (r1 - mid.astype(F32)).astype(BF16)
        b = _dot(tri, hi) + _dot(tri, mid) + _dot(tri, lo)

        st = st_scr[...]
        inter = _dot_nt((qc * jnp.exp(b)).astype(BF16), st.astype(BF16))

        att = jnp.zeros((c, c), F32)
        for j in range(c):
            lo_r = (j // SUBLANES) * SUBLANES
            d = b[lo_r:, :] - b[j:j + 1, :]
            head = jnp.where(row_id[lo_r:lo_r + SUBLANES, :] >= j, d[:SUBLANES, :], NEG_INF)
            e = jnp.exp(jnp.concatenate([head, d[SUBLANES:, :]], axis=0) if lo_r + SUBLANES < c else head)
            col = jnp.sum(qc[lo_r:, :] * kc[j:j + 1, :] * e, axis=-1, keepdims=True)
            if lo_r:
                col = jnp.concatenate([jnp.zeros((lo_r, 1), F32), col], axis=0)
            att = jnp.where(c_i == j, col, att)
        o = inter + _dot(att.astype(BF16), vc)

        b_last = b[c - 1:c, :]
        k_dec = (kc * jnp.exp(b_last - b)).astype(BF16)
        st_scr[...] = st * jnp.exp(b_last) + _dot_tn(vc, k_dec)

        og = og_ref[pl.ds(r0, c), :].astype(F32)
        out = _rms(o, gon) * (og * jax.nn.sigmoid(og))
        o_ref[pl.ds(r0, c), :] = out.astype(o_ref.dtype)
        return carry

    lax.fori_loop(0, seq // c, chunk_step, 0)


def _gla(p, small, w2p, b_a, g_on, *, batch, seq, heads, dk, dv, q_blk, k_blk, v_blk, og_blk):
    t = p.shape[0]
    return pl.pallas_call(
        functools.partial(_gla_kernel, seq=seq, dk=dk, dv=dv, chunk=GLA_CHUNK),
        grid=(batch, heads),
        in_specs=[
            pl.BlockSpec((seq, dk), lambda b, h: (b, q_blk + h)),
            pl.BlockSpec((seq, dk), lambda b, h: (b, k_blk + h)),
            pl.BlockSpec((seq, dv), lambda b, h: (b, v_blk + h)),
            pl.BlockSpec((seq, dv), lambda b, h: (b, og_blk + h)),
            pl.BlockSpec((seq, LANES), lambda b, h: (b, 0)),
            pl.BlockSpec((LANES, dk), lambda b, h: (0, h)),
            pl.BlockSpec((1, dk), lambda b, h: (0, h)),
            pl.BlockSpec((1, dv), lambda b, h: (0, 0)),
        ],
        out_specs=pl.BlockSpec((seq, dv), lambda b, h: (b, h)),
        out_shape=jax.ShapeDtypeStruct((t, heads * dv), BF16),
        scratch_shapes=[pltpu.VMEM((dv, dk), F32)],
        compiler_params=_params(("parallel", "parallel")),
    )(p, p, p, p, small, w2p, b_a, g_on)


def _out_proj_kernel(x_ref, ga_ref, gb_ref, oa_ref, ob_ref, bg_ref, w_ref, o_ref, m_scr):
    @pl.when(pl.program_id(1) == 0)
    def _():
        bg = bg_ref[...]
        sa = jax.nn.sigmoid(ga_ref[...].astype(F32) + bg[0:1, :])
        sb = jax.nn.sigmoid(gb_ref[...].astype(F32) + bg[1:2, :])
        m_scr[...] = (sa * oa_ref[...].astype(F32) + sb * ob_ref[...].astype(F32)).astype(BF16)

    o_ref[...] = x_ref[...] + _dot(m_scr[...], w_ref[...])


def _out_proj(x, p, o_mla, o_gla, b_gate, w_out, *, row0, ga_blk, gb_blk, tm, tn):
    t, d = p.shape[0], x.shape[1]
    assert row0 % tm == 0
    b0 = row0 // tm
    return pl.pallas_call(
        _out_proj_kernel,
        grid=(t // tm, d // tn),
        in_specs=[
            pl.BlockSpec((tm, tn), lambda i, j: (i + b0, j)),
            pl.BlockSpec((tm, d), lambda i, j: (i, ga_blk)),
            pl.BlockSpec((tm, d), lambda i, j: (i, gb_blk)),
            pl.BlockSpec((tm, d), lambda i, j: (i, 0)),
            pl.BlockSpec((tm, d), lambda i, j: (i, 0)),
            pl.BlockSpec((2, d), lambda i, j: (0, 0)),
            pl.BlockSpec((d, tn), lambda i, j: (0, j)),
        ],
        out_specs=pl.BlockSpec((tm, tn), lambda i, j: (i, j)),
        out_shape=jax.ShapeDtypeStruct((t, d), F32),
        scratch_shapes=[pltpu.VMEM((tm, d), BF16)],
        compiler_params=_params(("parallel", "arbitrary")),
    )(x, p, p, o_mla, o_gla, b_gate, w_out)


def _topk_rows(s, k, payload=None):
    n = s.shape[0]
    iota = lax.broadcasted_iota(jnp.int32, s.shape, 0).astype(F32)
    kiota = lax.broadcasted_iota(jnp.int32, (k, s.shape[1]), 0)
    vals = jnp.zeros((k, s.shape[1]), F32)
    picks = jnp.zeros((k, s.shape[1]), F32)
    for r in range(k):
        m = jnp.max(s, axis=0, keepdims=True)
        idx = jnp.min(jnp.where(s == m, iota, float(n)), axis=0, keepdims=True)
        hit = iota == idx
        if payload is None:
            pick = idx
        else:
            pick = jnp.sum(jnp.where(hit, payload, 0.0), axis=0, keepdims=True)
        vals = jnp.where(kiota == r, m, vals)
        picks = jnp.where(kiota == r, pick, picks)
        s = jnp.where(hit, NEG_INF, s)
    return vals, picks


def _peer_route_kernel(x_ref, g_ref, w_ref, sk_ref, h_ref, ids_ref, gates_ref, *, heads, nkeys, half, topk):
    h = _rms(x_ref[...], g_ref[...]).astype(BF16)
    h_ref[...] = h
    qf = _dot(h, w_ref[...])
    for hd in range(heads):
        tops = []
        for part in range(2):
            o = (hd * 2 + part) * half
            qh = qf[:, o:o + half].astype(BF16)
            keys = sk_ref[(hd * 2 + part) * nkeys:(hd * 2 + part + 1) * nkeys, :]
            tops.append(_topk_rows(_dot_nt(keys, qh), topk))
        (s1, i1), (s2, i2) = tops
        assert topk == 2 * SUBLANES
        hs = SUBLANES
        pair_s = [s1[0:1, :] + s2] + [s1[a:a + 1, :] + s2[:hs, :] for a in range(1, hs)] + [s1[hs:, :] + s2[0:1, :]]
        pair_id = ([i1[0:1, :] * float(nkeys) + i2]
                   + [i1[a:a + 1, :] * float(nkeys) + i2[:hs, :] for a in range(1, hs)]
                   + [i1[hs:, :] * float(nkeys) + i2[0:1, :]])
        best_s, best_id = _topk_rows(jnp.concatenate(pair_s, axis=0), topk,
                                     payload=jnp.concatenate(pair_id, axis=0))
        e = jnp.exp(best_s - best_s[0:1, :])
        gate = e / jnp.sum(e, axis=0, keepdims=True)
        ids_ref[hd * topk:(hd + 1) * topk, :] = best_id.astype(jnp.int32)
        gates_ref[hd * topk:(hd + 1) * topk, :] = gate


def _peer_route(x, g, w_pq, sk2d, *, heads, nkeys, half, topk, tm):
    t, d = x.shape
    dq = w_pq.shape[1]
    return pl.pallas_call(
        functools.partial(_peer_route_kernel, heads=heads, nkeys=nkeys, half=half, topk=topk),
        grid=(t // tm,),
        in_specs=[
            pl.BlockSpec((tm, d), lambda i: (i, 0)),
            pl.BlockSpec((1, d), lambda i: (0, 0)),
            pl.BlockSpec((d, dq), lambda i: (0, 0)),
            pl.BlockSpec((heads * 2 * nkeys, half), lambda i: (0, 0)),
        ],
        out_specs=[
            pl.BlockSpec((tm, d), lambda i: (i, 0)),
            pl.BlockSpec((heads * topk, tm), lambda i: (0, i)),
            pl.BlockSpec((heads * topk, tm), lambda i: (0, i)),
        ],
        out_shape=[
            jax.ShapeDtypeStruct((t, d), BF16),
            jax.ShapeDtypeStruct((heads * topk, t), jnp.int32),
            jax.ShapeDtypeStruct((heads * topk, t), F32),
        ],
        compiler_params=_params(("parallel",)),
    )(x, g, w_pq, sk2d)


def _pack_expert_table(u_emb, v_emb):
    n, d = u_emb.shape
    half = d // 2
    tm = 512
    assert n % tm == 0

    def pack(w):
        bits = lax.bitcast_convert_type(w.astype(BF16).astype(F32), jnp.int32)
        return ((bits[:, :half] >> 16) & jnp.int32(0xFFFF)) | (bits[:, half:] & jnp.int32(-65536))

    def pack_kernel(u_ref, v_ref, o_ref):
        o_ref[:, :half] = pack(u_ref[...])
        o_ref[:, half:] = pack(v_ref[...])

    return pl.pallas_call(
        pack_kernel,
        grid=(n // tm,),
        in_specs=[pl.BlockSpec((tm, d), lambda i: (i, 0)), pl.BlockSpec((tm, d), lambda i: (i, 0))],
        out_specs=pl.BlockSpec((tm, d), lambda i: (i, 0)),
        out_shape=jax.ShapeDtypeStruct((n, d), jnp.int32),
        compiler_params=_params(("parallel",)),
    )(u_emb, v_emb)


def _unpack_words(w):
    lo = lax.bitcast_convert_type(w << 16, F32)
    hi = lax.bitcast_convert_type(w & jnp.int32(-65536), F32)
    return lo, hi


def _expert_mix(words, h, gate, d):
    half = d // 2
    u_lo, u_hi = _unpack_words(words[:, :half])
    act = jnp.sum(u_lo * h[:, :half] + u_hi * h[:, half:], axis=-1, keepdims=True)
    w = gate * _gelu_exact(act)
    v_lo, v_hi = _unpack_words(words[:, half:])
    return jnp.concatenate([jnp.sum(v_lo * w, axis=0, keepdims=True),
                            jnp.sum(v_hi * w, axis=0, keepdims=True)], axis=1)


def _peer_mix_kernel(ids_hbm, uv_hbm, rows_ref, x_ref, h_ref, gates_ref, o_ref, ids_smem, buf, ids_sems, row_sems,
                     *, td, ts, picks, d, n_steps):
    step = pl.program_id(0)
    par = lax.rem(step, 2)
    ahead = MIX_SLOTS - 1
    assert td % MIX_SLOTS == 0 and td > ahead and (td + ts) % SUBLANES == 0

    def ids_copy(s, row):
        return pltpu.make_async_copy(ids_hbm.at[s], ids_smem.at[row], ids_sems.at[row])

    def issue(row, tok, slot):
        for j in range(picks):
            eid = ids_smem[row, tok * picks + j]
            pltpu.make_async_copy(uv_hbm.at[pl.ds(eid, 1), :], buf.at[slot, pl.ds(j, 1), :],
                                  row_sems.at[slot]).start()

    def wait_rows(slot):
        pltpu.make_async_copy(uv_hbm.at[pl.ds(0, picks), :], buf.at[slot], row_sems.at[slot]).wait()

    @pl.when(step == 0)
    def _():
        first = ids_copy(0, 0)
        first.start()
        first.wait()
        for u in range(ahead):
            issue(0, u, u)

    @pl.when(step + 1 < n_steps)
    def _():
        ids_copy(step + 1, 1 - par).start()

    loaded = {}

    def finish(pos, words):
        g, lane = divmod(pos, SUBLANES)
        if g not in loaded:
            loaded[g] = (gates_ref[g], h_ref[g * SUBLANES:(g + 1) * SUBLANES, :].astype(F32))
        gates, h8 = loaded[g]
        row = _expert_mix(words, h8[lane:lane + 1, :], gates[:, lane:lane + 1], d)
        o_ref[pos:pos + 1, :] = x_ref[pos:pos + 1, :] + row

    staged_done = 0
    for u in range(td):
        nxt = u + ahead
        if nxt < td:
            issue(par, nxt, nxt % MIX_SLOTS)
        else:
            @pl.when(step + 1 < n_steps)
            def _():
                if nxt == td:
                    ids_copy(step + 1, 1 - par).wait()
                issue(1 - par, nxt - td, nxt % MIX_SLOTS)

        staged_upto = (u + 1) * ts // td
        for k in range(staged_done, staged_upto):
            finish(td + k, rows_ref[k * picks:(k + 1) * picks, :])
        staged_done = staged_upto
        wait_rows(u % MIX_SLOTS)
        finish(u, buf[u % MIX_SLOTS])


def _peer_mix(ids_direct, uv, staged, x, h, gates3, *, td, ts):
    t, d = x.shape
    picks = gates3.shape[1]
    n_steps = t // (td + ts)
    assert staged.shape[0] == n_steps * ts * picks and ids_direct.shape == (n_steps, td * picks)
    return pl.pallas_call(
        functools.partial(_peer_mix_kernel, td=td, ts=ts, picks=picks, d=d, n_steps=n_steps),
        grid=(n_steps,),
        input_output_aliases={3: 0},
        in_specs=[
            pl.BlockSpec(memory_space=pl.ANY),
            pl.BlockSpec(memory_space=pl.ANY),
            pl.BlockSpec((ts * picks, d), lambda i: (i, 0)),
            pl.BlockSpec((td + ts, d), lambda i: (i, 0)),
            pl.BlockSpec((td + ts, d), lambda i: (i, 0)),
            pl.BlockSpec(((td + ts) // SUBLANES, picks, SUBLANES), lambda i: (i, 0, 0)),
        ],
        out_specs=pl.BlockSpec((td + ts, d), lambda i: (i, 0)),
        out_shape=jax.ShapeDtypeStruct((t, d), F32),
        scratch_shapes=[
            pltpu.SMEM((2, td * picks), jnp.int32),
            pltpu.VMEM((MIX_SLOTS, picks, d), jnp.int32),
            pltpu.SemaphoreType.DMA((2,)),
            pltpu.SemaphoreType.DMA((MIX_SLOTS,)),
        ],
        compiler_params=_params(("arbitrary",)),
    )(ids_direct, uv, staged, x, h, gates3)


def _sc_gather_rows(table, idx, *, chunk):
    n_rows = idx.shape[0]
    d = table.shape[1]
    workers = SC_CORES * SC_SUBCORES
    assert n_rows % (workers * 2 * chunk) == 0 and chunk % SUBLANES == 0 and chunk <= LANES
    rows_per_worker = n_rows // workers
    n_pairs = rows_per_worker // (2 * chunk)
    mesh = plsc.VectorSubcoreMesh(core_axis_name="c", subcore_axis_name="s")

    @functools.partial(
        pl.kernel, mesh=mesh,
        out_type=jax.ShapeDtypeStruct((n_rows, d), table.dtype),
        scratch_types=[
            pltpu.VMEM((chunk,), jnp.int32), pltpu.VMEM((chunk,), jnp.int32),
            pltpu.VMEM((chunk, d), table.dtype), pltpu.VMEM((chunk, d), table.dtype),
            pltpu.SemaphoreType.DMA, pltpu.SemaphoreType.DMA,
            pltpu.SemaphoreType.DMA, pltpu.SemaphoreType.DMA,
        ],
    )
    def gather_kernel(table_hbm, idx_hbm, out_hbm, idx0, idx1, rows0, rows1, gsem0, gsem1, wsem0, wsem1):
        idx_v, rows_v, gsem, wsem = (idx0, idx1), (rows0, rows1), (gsem0, gsem1), (wsem0, wsem1)
        worker = lax.axis_index("s") * SC_CORES + lax.axis_index("c")
        base = worker * rows_per_worker

        def out_rows(c):
            return pl.ds(pl.multiple_of(base + c * chunk, SUBLANES), chunk)

        def load_idx(slot, c):
            pltpu.sync_copy(idx_hbm.at[out_rows(c)], idx_v[slot])

        def gather(slot):
            return pltpu.make_async_copy(table_hbm.at[idx_v[slot]], rows_v[slot], gsem[slot])

        def writeout(slot, c):
            return pltpu.make_async_copy(rows_v[slot], out_hbm.at[out_rows(c)], wsem[slot])

        load_idx(0, 0)
        gather(0).start()

        @pl.loop(0, n_pairs)
        def _(p):
            c0 = 2 * p

            @pl.when(p > 0)
            def _():
                writeout(1, c0 - 1).wait()

            load_idx(1, c0 + 1)
            gather(1).start()
            gather(0).wait()
            writeout(0, c0).start()

            @pl.when(p + 1 < n_pairs)
            def _():
                load_idx(0, c0 + 2)
                writeout(0, c0).wait()
                gather(0).start()

            gather(1).wait()
            writeout(1, c0 + 1).start()

        writeout(0, 2 * n_pairs - 2).wait()
        writeout(1, 2 * n_pairs - 1).wait()

    return gather_kernel(table, idx)


def _pad_cols(w, width):
    return jnp.pad(w, ((0, 0), (0, width - w.shape[1])))


def _prepare_layer(g_norm_mix, w_in, b_gate, g_cq, w_uq, g_ckv, w_ukv, g_qn, g_qr, g_kn, g_kr,
                   w_a2, b_a, g_gla_out, w_out, g_norm_ffn, w_pq, sub_keys, u_emb, v_emb):
    d = w_in.shape[0]
    q_rank, kv_rank = g_cq.shape[0], g_ckv.shape[0]
    nope, rope = g_qn.shape[0], g_qr.shape[0]
    mla_heads = w_uq.shape[1] // (nope + rope)
    mla_v = w_ukv.shape[1] // mla_heads - nope
    gate_rank, gla_dk_all = w_a2.shape
    gla_dv = g_gla_out.shape[0]
    gla_heads = d // gla_dv
    gla_dk = gla_dk_all // gla_heads
    peer_heads, _, nkeys, half = sub_keys.shape
    assert nope == LANES and mla_v == LANES and rope <= LANES and rope % 2 == 0
    assert mla_heads * mla_v == d and gla_heads * gla_dv == d
    assert rope + gate_rank <= LANES and nkeys == LANES and half == LANES

    widths = (q_rank, kv_rank, rope, gla_dk_all, gla_dk_all, d, gate_rank, d, d, d)
    offs = [0]
    for wd in widths:
        offs.append(offs[-1] + wd)
    assert offs[-1] == w_in.shape[1]
    seg = lambda i: w_in[:, offs[i]:offs[i + 1]]
    w_main = jnp.concatenate([seg(5), seg(7), seg(8), seg(9), seg(3), seg(4), seg(0), seg(1)], axis=1).astype(BF16)
    w_small = _pad_cols(jnp.concatenate([seg(2), seg(6)], axis=1), LANES).astype(BF16)
    cq_off = 4 * d + 2 * gla_dk_all
    assert cq_off % q_rank == 0 and (cq_off + q_rank) % kv_rank == 0
    inv_freq = ROPE_THETA ** (-jnp.arange(0, rope, 2, dtype=F32) / rope)
    scale = (nope + rope) ** -0.5
    return dict(
        dims=dict(q_rank=q_rank, kv_rank=kv_rank, nope=nope, rope=rope, mla_heads=mla_heads, mla_v=mla_v,
                  gla_heads=gla_heads, gla_dk=gla_dk, gla_dv=gla_dv, peer_heads=peer_heads, nkeys=nkeys, half=half,
                  v_blk=0, og_blk=d // gla_dv, ga_blk=2, gb_blk=3, q_blk=4 * d // gla_dk,
                  k_blk=4 * d // gla_dk + gla_heads, cq_blk=cq_off // q_rank, ckv_blk=(cq_off + q_rank) // kv_rank),
        g_norm_mix=g_norm_mix[None, :], w_main=w_main, w_small=w_small,
        invf=_pad_cols(jnp.concatenate([inv_freq, inv_freq])[None, :], LANES),
        gq=_pad_cols(jnp.concatenate([g_qn, g_qr])[None, :] * scale, nope + LANES),
        gkr=_pad_cols(g_kr[None, :], LANES), g_cq=g_cq[None, :], g_ckv=g_ckv[None, :], g_kn=g_kn[None, :],
        w_uq=jnp.pad(w_uq.reshape(q_rank, mla_heads, nope + rope),
                     ((0, 0), (0, 0), (0, LANES - rope))).reshape(q_rank, -1).astype(BF16),
        w_ukv=w_ukv.astype(BF16),
        w2p=jnp.zeros((LANES, gla_dk_all), F32).at[rope:rope + gate_rank].set(w_a2).astype(BF16),
        b_a=b_a[None, :], g_on=g_gla_out[None, :], b_gate=b_gate, w_out=w_out.astype(BF16),
        g_norm_ffn=g_norm_ffn[None, :], w_pq=w_pq.astype(BF16),
        sk2d=sub_keys.reshape(peer_heads * 2 * nkeys, half).astype(BF16),
        uv=_pack_expert_table(u_emb, v_emb),
    )


def _mixers_and_route(x_all, row0, pos, w, *, batch, seq, tiles):
    t = batch * seq
    dm = w["dims"]
    p, small = _in_proj(x_all, w["g_norm_mix"], w["w_main"], w["w_small"], row0=row0, t=t,
                        tm=tiles["in_tm"], tn=tiles["in_tn"])
    q = _mla_q(p, dm["cq_blk"], w["g_cq"], w["w_uq"], w["gq"], pos, w["invf"], heads=dm["mla_heads"],
               rank=dm["q_rank"], nope=dm["nope"], rope=dm["rope"], tm=tiles["mla_tm"])
    k, v = _mla_kv(p, dm["ckv_blk"], w["g_ckv"], w["w_ukv"], w["g_kn"], small, w["gkr"], pos, w["invf"],
                   heads=dm["mla_heads"], rank=dm["kv_rank"], nope=dm["nope"], rope=dm["rope"], dv=dm["mla_v"],
                   tm=tiles["mla_tm"])
    o_mla = _mla_attn(q, k, v, batch=batch, seq=seq, heads=dm["mla_heads"], dk=dm["nope"] + LANES, dv=dm["mla_v"],
                      tq=tiles["attn_tq"], tk=tiles["attn_tk"])
    o_gla = _gla(p, small, w["w2p"], w["b_a"], w["g_on"], batch=batch, seq=seq, heads=dm["gla_heads"],
                 dk=dm["gla_dk"], dv=dm["gla_dv"], q_blk=dm["q_blk"], k_blk=dm["k_blk"], v_blk=dm["v_blk"],
                 og_blk=dm["og_blk"])
    x2 = _out_proj(x_all, p, o_mla, o_gla, w["b_gate"], w["w_out"], row0=row0, ga_blk=dm["ga_blk"],
                   gb_blk=dm["gb_blk"], tm=tiles["out_tm"], tn=tiles["out_tn"])
    h2, ids_t, gates_t = _peer_route(x2, w["g_norm_ffn"], w["w_pq"], w["sk2d"], heads=dm["peer_heads"],
                                     nkeys=dm["nkeys"], half=dm["half"], topk=PEER_TOPK, tm=tiles["route_tm"])
    picks = dm["peer_heads"] * PEER_TOPK
    gates3 = gates_t.reshape(picks, t // SUBLANES, SUBLANES).transpose(1, 0, 2)
    return x2, h2, ids_t.T, gates3


def _layer(x2, pos, w, *, batch, seq, tiles):
    t, d = x2.shape
    splits = tiles["mix_split"]
    groups = len(splits)
    tg, bg = t // groups, batch // groups
    routed = [_mixers_and_route(x2, g * tg, pos[g * tg:(g + 1) * tg], w, batch=bg, seq=seq, tiles=tiles)
              for g in range(groups)]
    out = []
    for (td, ts), (xg, hg, ids_tok, gates3) in zip(splits, routed):
        picks = ids_tok.shape[1]
        ids3 = ids_tok.reshape(tg // (td + ts), td + ts, picks)
        staged = _sc_gather_rows(w["uv"], ids3[:, td:].reshape(-1), chunk=tiles["sc_chunk"])
        out.append(_peer_mix(ids3[:, :td].reshape(-1, td * picks), w["uv"], staged, xg, hg, gates3, td=td, ts=ts))
    return jnp.concatenate(out, axis=0)


_TILES = dict(in_tm=2048, in_tn=512, mla_tm=2048, attn_tq=1024, attn_tk=1024,
              out_tm=1024, out_tn=512, route_tm=256, mix_split=((4, 12),) * 8, sc_chunk=16)


def kernel(x, positions, g_norm_mix, w_in, b_gate, g_cq, w_uq, g_ckv, w_ukv, g_qn, g_qr, g_kn, g_kr,
           w_a2, b_a, g_gla_out, w_out, g_norm_ffn, w_pq, sub_keys, u_emb, v_emb, tiles=None):
    tiles = _TILES if tiles is None else tiles
    batch, seq, d = x.shape
    x2 = x.reshape(batch * seq, d)
    pos = positions.reshape(batch * seq, 1)
    for l in range(g_norm_mix.shape[0]):
        w = _prepare_layer(g_norm_mix[l], w_in[l], b_gate[l], g_cq[l], w_uq[l], g_ckv[l], w_ukv[l], g_qn[l],
                           g_qr[l], g_kn[l], g_kr[l], w_a2[l], b_a[l], g_gla_out[l], w_out[l], g_norm_ffn[l],
                           w_pq[l], sub_keys[l], u_emb[l], v_emb[l])
        x2 = _layer(x2, pos, w, batch=batch, seq=seq, tiles=tiles)
    return x2.reshape(batch, seq, d)
```

```python
import functools

import jax
import jax.numpy as jnp
from jax import lax
from jax.experimental import pallas as pl
from jax.experimental.pallas import tpu as pltpu
from jax.experimental.pallas import tpu_sc as plsc

EPS = 1e-6
ROPE_THETA = 10000.0
GLA_TAU = 16.0
GLA_CHUNK = 64
PEER_TOPK = 16

LANES = 128
SUBLANES = 8
VMEM_LIMIT_BYTES = 56 * 1024 * 1024
MIX_SLOTS = 4
SC_CORES = 2
SC_SUBCORES = 16

F32 = jnp.float32
BF16 = jnp.bfloat16
NEG_INF = float("-inf")


def _params(semantics):
    return pltpu.CompilerParams(dimension_semantics=semantics, vmem_limit_bytes=VMEM_LIMIT_BYTES)


def _rms(x, gain, n=None):
    ss = jnp.sum(x * x, axis=-1, keepdims=True)
    n = x.shape[-1] if n is None else n
    return x * lax.rsqrt(ss * (1.0 / n) + EPS) * gain


def _gelu_exact(x):
    return 0.5 * x * (1.0 + lax.erf(x * (0.5 ** 0.5)))


def _dot(a, b):
    return jnp.dot(a, b, preferred_element_type=F32)


def _dot_nt(a, b):
    return lax.dot_general(a, b, (((1,), (1,)), ((), ())), preferred_element_type=F32)


def _dot_tn(a, b):
    return lax.dot_general(a, b, (((0,), (0,)), ((), ())), preferred_element_type=F32)


def _in_proj_kernel(x_ref, g_ref, w_ref, ws_ref, p_ref, ps_ref, h_scr):
    @pl.when(pl.program_id(1) == 0)
    def _():
        h = _rms(x_ref[...], g_ref[...]).astype(BF16)
        h_scr[...] = h
        ps_ref[...] = _dot(h, ws_ref[...])

    p_ref[...] = _dot(h_scr[...], w_ref[...]).astype(p_ref.dtype)


def _in_proj(x, g, w_main, w_small, *, row0, t, tm, tn):
    d = x.shape[1]
    n = w_main.shape[1]
    assert row0 % tm == 0
    b0 = row0 // tm
    x_mode = pl.Buffered(1) if t == tm else None
    return pl.pallas_call(
        _in_proj_kernel,
        grid=(t // tm, n // tn),
        in_specs=[
            pl.BlockSpec((tm, d), lambda i, j: (i + b0, 0), pipeline_mode=x_mode),
            pl.BlockSpec((1, d), lambda i, j: (0, 0)),
            pl.BlockSpec((d, tn), lambda i, j: (0, j)),
            pl.BlockSpec((d, LANES), lambda i, j: (0, 0)),
        ],
        out_specs=[
            pl.BlockSpec((tm, tn), lambda i, j: (i, j)),
            pl.BlockSpec((tm, LANES), lambda i, j: (i, 0)),
        ],
        out_shape=[
            jax.ShapeDtypeStruct((t, n), BF16),
            jax.ShapeDtypeStruct((t, LANES), F32),
        ],
        scratch_shapes=[pltpu.VMEM((tm, d), BF16)],
        compiler_params=_params(("parallel", "arbitrary")),
    )(x, g, w_main, w_small)


def _rope_tables(pos_ref, invf_ref, rope):
    ang = pos_ref[...].astype(F32) * invf_ref[...]
    cos, sin = jnp.cos(ang), jnp.sin(ang)
    lane = lax.broadcasted_iota(jnp.int32, ang.shape, 1)
    half = rope // 2
    c = jnp.where(lane < rope, cos, 0.0)
    s_lo = jnp.where(lane < half, -sin, 0.0)
    s_hi = jnp.where(lane < half, 0.0, jnp.where(lane < rope, sin, 0.0))
    return c, s_lo, s_hi


def _apply_rope(pe, c, s_lo, s_hi, rope):
    half = rope // 2
    from_hi = pltpu.roll(pe, LANES - half, 1)
    from_lo = pltpu.roll(pe, half, 1)
    return pe * c + from_hi * s_lo + from_lo * s_hi


def _mla_q_kernel(cq_ref, gcq_ref, w_ref, gq_ref, pos_ref, invf_ref, q_ref,
                  h_scr, c_scr, slo_scr, shi_scr, *, nope, rope):
    @pl.when(pl.program_id(1) == 0)
    def _():
        h_scr[...] = _rms(cq_ref[...].astype(F32), gcq_ref[...]).astype(BF16)
        c, s_lo, s_hi = _rope_tables(pos_ref, invf_ref, rope)
        c_scr[...] = c
        slo_scr[...] = s_lo
        shi_scr[...] = s_hi

    y = _dot(h_scr[...], w_ref[...])
    g = gq_ref[...]
    qn = _rms(y[:, :nope], g[:, :nope])
    pe = _rms(y[:, nope:], g[:, nope:], n=rope)
    pe = _apply_rope(pe, c_scr[...], slo_scr[...], shi_scr[...], rope)
    q_ref[:, :nope] = qn.astype(q_ref.dtype)
    q_ref[:, nope:] = pe.astype(q_ref.dtype)


def _mla_q(p, cq_blk, g_cq, w_uq_p, gq, pos, invf, *, heads, rank, nope, rope, tm):
    t = p.shape[0]
    hw = nope + LANES
    return pl.pallas_call(
        functools.partial(_mla_q_kernel, nope=nope, rope=rope),
        grid=(t // tm, heads),
        in_specs=[
            pl.BlockSpec((tm, rank), lambda i, j: (i, cq_blk)),
            pl.BlockSpec((1, rank), lambda i, j: (0, 0)),
            pl.BlockSpec((rank, hw), lambda i, j: (0, j)),
            pl.BlockSpec((1, hw), lambda i, j: (0, 0)),
            pl.BlockSpec((tm, 1), lambda i, j: (i, 0)),
            pl.BlockSpec((1, LANES), lambda i, j: (0, 0)),
        ],
        out_specs=pl.BlockSpec((tm, hw), lambda i, j: (i, j)),
        out_shape=jax.ShapeDtypeStruct((t, heads * hw), BF16),
        scratch_shapes=[
            pltpu.VMEM((tm, rank), BF16),
            pltpu.VMEM((tm, LANES), F32),
            pltpu.VMEM((tm, LANES), F32),
            pltpu.VMEM((tm, LANES), F32),
        ],
        compiler_params=_params(("parallel", "arbitrary")),
    )(p, g_cq, w_uq_p, gq, pos, invf)


def _mla_kv_kernel(ckv_ref, gckv_ref, w_ref, gkn_ref, small_ref, gkr_ref, pos_ref, invf_ref,
                   k_ref, v_ref, h_scr, kpe_scr, *, nope, rope):
    @pl.when(pl.program_id(1) == 0)
    def _():
        h_scr[...] = _rms(ckv_ref[...].astype(F32), gckv_ref[...]).astype(BF16)
        c, s_lo, s_hi = _rope_tables(pos_ref, invf_ref, rope)
        sm = small_ref[...]
        lane = lax.broadcasted_iota(jnp.int32, sm.shape, 1)
        pe = _rms(jnp.where(lane < rope, sm, 0.0), gkr_ref[...], n=rope)
        kpe_scr[...] = _apply_rope(pe, c, s_lo, s_hi, rope).astype(BF16)

    y = _dot(h_scr[...], w_ref[...])
    k_ref[:, :nope] = _rms(y[:, :nope], gkn_ref[...]).astype(k_ref.dtype)
    k_ref[:, nope:] = kpe_scr[...]
    v_ref[...] = y[:, nope:].astype(v_ref.dtype)


def _mla_kv(p, ckv_blk, g_ckv, w_ukv, g_kn, small, gkr, pos, invf, *, heads, rank, nope, rope, dv, tm):
    t = p.shape[0]
    kw = nope + LANES
    return pl.pallas_call(
        functools.partial(_mla_kv_kernel, nope=nope, rope=rope),
        grid=(t // tm, heads),
        in_specs=[
            pl.BlockSpec((tm, rank), lambda i, j: (i, ckv_blk)),
            pl.BlockSpec((1, rank), lambda i, j: (0, 0)),
            pl.BlockSpec((rank, nope + dv), lambda i, j: (0, j)),
            pl.BlockSpec((1, nope), lambda i, j: (0, 0)),
            pl.BlockSpec((tm, LANES), lambda i, j: (i, 0)),
            pl.BlockSpec((1, LANES), lambda i, j: (0, 0)),
            pl.BlockSpec((tm, 1), lambda i, j: (i, 0)),
            pl.BlockSpec((1, LANES), lambda i, j: (0, 0)),
        ],
        out_specs=[
            pl.BlockSpec((tm, kw), lambda i, j: (i, j)),
            pl.BlockSpec((tm, dv), lambda i, j: (i, j)),
        ],
        out_shape=[
            jax.ShapeDtypeStruct((t, heads * kw), BF16),
            jax.ShapeDtypeStruct((t, heads * dv), BF16),
        ],
        scratch_shapes=[pltpu.VMEM((tm, rank), BF16), pltpu.VMEM((tm, LANES), BF16)],
        compiler_params=_params(("parallel", "arbitrary")),
    )(p, g_ckv, w_ukv, g_kn, small, gkr, pos, invf)


def _attn_kernel(q_ref, k_ref, v_ref, o_ref, *, tq, tk):
    qi = pl.program_id(2)
    q = q_ref[...]
    dv = v_ref.shape[1]
    row = qi * tq + lax.broadcasted_iota(jnp.int32, (tq, tk), 0)
    col0 = lax.broadcasted_iota(jnp.int32, (tq, tk), 1)

    def body(kb, carry, masked):
        m, l, acc = carry
        k0 = pl.multiple_of(kb * tk, tk)
        s = _dot_nt(q, k_ref[pl.ds(k0, tk), :])
        if masked:
            s = jnp.where(col0 + k0 <= row, s, NEG_INF)
        m_new = jnp.maximum(m, jnp.max(s, axis=-1, keepdims=True))
        alpha = jnp.exp(m - m_new)
        pr = jnp.exp(s - m_new)
        l = alpha * l + jnp.sum(pr, axis=-1, keepdims=True)
        acc = alpha * acc + _dot(pr.astype(BF16), v_ref[pl.ds(k0, tk), :])
        return m_new, l, acc

    init = (jnp.full((tq, 1), NEG_INF, F32), jnp.zeros((tq, 1), F32), jnp.zeros((tq, dv), F32))
    n_below = (qi * tq) // tk
    nkb = ((qi + 1) * tq + tk - 1) // tk
    carry = lax.fori_loop(0, n_below, functools.partial(body, masked=False), init)
    _, l, acc = lax.fori_loop(n_below, nkb, functools.partial(body, masked=True), carry)
    o_ref[...] = (acc / l).astype(o_ref.dtype)


def _mla_attn(q, k, v, *, batch, seq, heads, dk, dv, tq, tk):
    t = q.shape[0]
    nq = seq // tq
    return pl.pallas_call(
        functools.partial(_attn_kernel, tq=tq, tk=tk),
        grid=(batch, heads, nq),
        in_specs=[
            pl.BlockSpec((tq, dk), lambda b, h, i: (b * nq + i, h)),
            pl.BlockSpec((seq, dk), lambda b, h, i: (b, h)),
            pl.BlockSpec((seq, dv), lambda b, h, i: (b, h)),
        ],
        out_specs=pl.BlockSpec((tq, dv), lambda b, h, i: (b * nq + i, h)),
        out_shape=jax.ShapeDtypeStruct((t, heads * dv), BF16),
        compiler_params=_params(("parallel", "parallel", "arbitrary")),
    )(q, k, v)


def _gla_kernel(q_ref, k_ref, v_ref, og_ref, small_ref, w2_ref, ba_ref, gon_ref, o_ref, st_scr,
                *, seq, dk, dv, chunk):
    c = chunk
    st_scr[...] = jnp.zeros_like(st_scr)
    r_i = lax.broadcasted_iota(jnp.int32, (c, c), 0)
    c_i = lax.broadcasted_iota(jnp.int32, (c, c), 1)
    tri = jnp.where(c_i <= r_i, 1.0, 0.0).astype(BF16)
    row_id = lax.broadcasted_iota(jnp.int32, (c, 1), 0)
    w2 = w2_ref[...]
    ba = ba_ref[...]
    gon = gon_ref[...]
    q_scale = dk ** -0.5

    def chunk_step(ci, carry):
        r0 = pl.multiple_of(ci * c, c)
        qc = q_ref[pl.ds(r0, c), :].astype(F32) * q_scale
        kc = k_ref[pl.ds(r0, c), :].astype(F32)
        vc = v_ref[pl.ds(r0, c), :]
        z = _dot(small_ref[pl.ds(r0, c), :].astype(BF16), w2) + ba
        la = jax.nn.log_sigmoid(z) * (1.0 / GLA_TAU)
        hi = la.astype(BF16)
        r1 = la - hi.astype(F32)
        mid = r1.astype(BF16)
        lo = (r1 - mid.astype(F32)).astype(BF16)
        b = _dot(tri, hi) + _dot(tri, mid) + _dot(tri, lo)

        st = st_scr[...]
        inter = _dot_nt((qc * jnp.exp(b)).astype(BF16), st.astype(BF16))

        att = jnp.zeros((c, c), F32)
        for j in range(c):
            lo_r = (j // SUBLANES) * SUBLANES
            d = b[lo_r:, :] - b[j:j + 1, :]
            head = jnp.where(row_id[lo_r:lo_r + SUBLANES, :] >= j, d[:SUBLANES, :], NEG_INF)
            e = jnp.exp(jnp.concatenate([head, d[SUBLANES:, :]], axis=0) if lo_r + SUBLANES < c else head)
            col = jnp.sum(qc[lo_r:, :] * kc[j:j + 1, :] * e, axis=-1, keepdims=True)
            if lo_r:
                col = jnp.concatenate([jnp.zeros((lo_r, 1), F32), col], axis=0)
            att = jnp.where(c_i == j, col, att)
        o = inter + _dot(att.astype(BF16), vc)

        b_last = b[c - 1:c, :]
        k_dec = (kc * jnp.exp(b_last - b)).astype(BF16)
        st_scr[...] = st * jnp.exp(b_last) + _dot_tn(vc, k_dec)

        og = og_ref[pl.ds(r0, c), :].astype(F32)
        out = _rms(o, gon) * (og * jax.nn.sigmoid(og))
        o_ref[pl.ds(r0, c), :] = out.astype(o_ref.dtype)
        return carry

    lax.fori_loop(0, seq // c, chunk_step, 0)


def _gla(p, small, w2p, b_a, g_on, *, batch, seq, heads, dk, dv, q_blk, k_blk, v_blk, og_blk):
    t = p.shape[0]
    return pl.pallas_call(
        functools.partial(_gla_kernel, seq=seq, dk=dk, dv=dv, chunk=GLA_CHUNK),
        grid=(batch, heads),
        in_specs=[
            pl.BlockSpec((seq, dk), lambda b, h: (b, q_blk + h)),
            pl.BlockSpec((seq, dk), lambda b, h: (b, k_blk + h)),
            pl.BlockSpec((seq, dv), lambda b, h: (b, v_blk + h)),
            pl.BlockSpec((seq, dv), lambda b, h: (b, og_blk + h)),
            pl.BlockSpec((seq, LANES), lambda b, h: (b, 0)),
            pl.BlockSpec((LANES, dk), lambda b, h: (0, h)),
            pl.BlockSpec((1, dk), lambda b, h: (0, h)),
            pl.BlockSpec((1, dv), lambda b, h: (0, 0)),
        ],
        out_specs=pl.BlockSpec((seq, dv), lambda b, h: (b, h)),
        out_shape=jax.ShapeDtypeStruct((t, heads * dv), BF16),
        scratch_shapes=[pltpu.VMEM((dv, dk), F32)],
        compiler_params=_params(("parallel", "parallel")),
    )(p, p, p, p, small, w2p, b_a, g_on)


def _out_proj_kernel(x_ref, ga_ref, gb_ref, oa_ref, ob_ref, bg_ref, w_ref, o_ref, m_scr):
    @pl.when(pl.program_id(1) == 0)
    def _():
        bg = bg_ref[...]
        sa = jax.nn.sigmoid(ga_ref[...].astype(F32) + bg[0:1, :])
        sb = jax.nn.sigmoid(gb_ref[...].astype(F32) + bg[1:2, :])
        m_scr[...] = (sa * oa_ref[...].astype(F32) + sb * ob_ref[...].astype(F32)).astype(BF16)

    o_ref[...] = x_ref[...] + _dot(m_scr[...], w_ref[...])


def _out_proj(x, p, o_mla, o_gla, b_gate, w_out, *, row0, ga_blk, gb_blk, tm, tn):
    t, d = p.shape[0], x.shape[1]
    assert row0 % tm == 0
    b0 = row0 // tm
    return pl.pallas_call(
        _out_proj_kernel,
        grid=(t // tm, d // tn),
        in_specs=[
            pl.BlockSpec((tm, tn), lambda i, j: (i + b0, j)),
            pl.BlockSpec((tm, d), lambda i, j: (i, ga_blk)),
            pl.BlockSpec((tm, d), lambda i, j: (i, gb_blk)),
            pl.BlockSpec((tm, d), lambda i, j: (i, 0)),
            pl.BlockSpec((tm, d), lambda i, j: (i, 0)),
            pl.BlockSpec((2, d), lambda i, j: (0, 0)),
            pl.BlockSpec((d, tn), lambda i, j: (0, j)),
        ],
        out_specs=pl.BlockSpec((tm, tn), lambda i, j: (i, j)),
        out_shape=jax.ShapeDtypeStruct((t, d), F32),
        scratch_shapes=[pltpu.VMEM((tm, d), BF16)],
        compiler_params=_params(("parallel", "arbitrary")),
    )(x, p, p, o_mla, o_gla, b_gate, w_out)


def _topk_rows(s, k, payload=None):
    n = s.shape[0]
    iota = lax.broadcasted_iota(jnp.int32, s.shape, 0).astype(F32)
    kiota = lax.broadcasted_iota(jnp.int32, (k, s.shape[1]), 0)
    vals = jnp.zeros((k, s.shape[1]), F32)
    picks = jnp.zeros((k, s.shape[1]), F32)
    for r in range(k):
        m = jnp.max(s, axis=0, keepdims=True)
        idx = jnp.min(jnp.where(s == m, iota, float(n)), axis=0, keepdims=True)
        hit = iota == idx
        if payload is None:
            pick = idx
        else:
            pick = jnp.sum(jnp.where(hit, payload, 0.0), axis=0, keepdims=True)
        vals = jnp.where(kiota == r, m, vals)
        picks = jnp.where(kiota == r, pick, picks)
        s = jnp.where(hit, NEG_INF, s)
    return vals, picks


def _peer_route_kernel(x_ref, g_ref, w_ref, sk_ref, h_ref, ids_ref, gates_ref, *, heads, nkeys, half, topk):
    h = _rms(x_ref[...], g_ref[...]).astype(BF16)
    h_ref[...] = h
    qf = _dot(h, w_ref[...])
    for hd in range(heads):
        tops = []
        for part in range(2):
            o = (hd * 2 + part) * half
            qh = qf[:, o:o + half].astype(BF16)
            keys = sk_ref[(hd * 2 + part) * nkeys:(hd * 2 + part + 1) * nkeys, :]
            tops.append(_topk_rows(_dot_nt(keys, qh), topk))
        (s1, i1), (s2, i2) = tops
        assert topk == 2 * SUBLANES
        hs = SUBLANES
        pair_s = [s1[0:1, :] + s2] + [s1[a:a + 1, :] + s2[:hs, :] for a in range(1, hs)] + [s1[hs:, :] + s2[0:1, :]]
        pair_id = ([i1[0:1, :] * float(nkeys) + i2]
                   + [i1[a:a + 1, :] * float(nkeys) + i2[:hs, :] for a in range(1, hs)]
                   + [i1[hs:, :] * float(nkeys) + i2[0:1, :]])
        best_s, best_id = _topk_rows(jnp.concatenate(pair_s, axis=0), topk,
                                     payload=jnp.concatenate(pair_id, axis=0))
        e = jnp.exp(best_s - best_s[0:1, :])
        gate = e / jnp.sum(e, axis=0, keepdims=True)
        ids_ref[hd * topk:(hd + 1) * topk, :] = best_id.astype(jnp.int32)
        gates_ref[hd * topk:(hd + 1) * topk, :] = gate


def _peer_route(x, g, w_pq, sk2d, *, heads, nkeys, half, topk, tm):
    t, d = x.shape
    dq = w_pq.shape[1]
    return pl.pallas_call(
        functools.partial(_peer_route_kernel, heads=heads, nkeys=nkeys, half=half, topk=topk),
        grid=(t // tm,),
        in_specs=[
            pl.BlockSpec((tm, d), lambda i: (i, 0)),
            pl.BlockSpec((1, d), lambda i: (0, 0)),
            pl.BlockSpec((d, dq), lambda i: (0, 0)),
            pl.BlockSpec((heads * 2 * nkeys, half), lambda i: (0, 0)),
        ],
        out_specs=[
            pl.BlockSpec((tm, d), lambda i: (i, 0)),
            pl.BlockSpec((heads * topk, tm), lambda i: (0, i)),
            pl.BlockSpec((heads * topk, tm), lambda i: (0, i)),
        ],
        out_shape=[
            jax.ShapeDtypeStruct((t, d), BF16),
            jax.ShapeDtypeStruct((heads * topk, t), jnp.int32),
            jax.ShapeDtypeStruct((heads * topk, t), F32),
        ],
        compiler_params=_params(("parallel",)),
    )(x, g, w_pq, sk2d)


def _pack_expert_table(u_emb, v_emb):
    n, d = u_emb.shape
    half = d // 2
    tm = 512
    assert n % tm == 0

    def pack(w):
        bits = lax.bitcast_convert_type(w.astype(BF16).astype(F32), jnp.int32)
        return ((bits[:, :half] >> 16) & jnp.int32(0xFFFF)) | (bits[:, half:] & jnp.int32(-65536))

    def pack_kernel(u_ref, v_ref, o_ref):
        o_ref[:, :half] = pack(u_ref[...])
        o_ref[:, half:] = pack(v_ref[...])

    return pl.pallas_call(
        pack_kernel,
        grid=(n // tm,),
        in_specs=[pl.BlockSpec((tm, d), lambda i: (i, 0)), pl.BlockSpec((tm, d), lambda i: (i, 0))],
        out_specs=pl.BlockSpec((tm, d), lambda i: (i, 0)),
        out_shape=jax.ShapeDtypeStruct((n, d), jnp.int32),
        compiler_params=_params(("parallel",)),
    )(u_emb, v_emb)


def _unpack_words(w):
    lo = lax.bitcast_convert_type(w << 16, F32)
    hi = lax.bitcast_convert_type(w & jnp.int32(-65536), F32)
    return lo, hi


def _expert_mix(words, h, gate, d):
    half = d // 2
    u_lo, u_hi = _unpack_words(words[:, :half])
    act = jnp.sum(u_lo * h[:, :half] + u_hi * h[:, half:], axis=-1, keepdims=True)
    w = gate * _gelu_exact(act)
    v_lo, v_hi = _unpack_words(words[:, half:])
    return jnp.concatenate([jnp.sum(v_lo * w, axis=0, keepdims=True),
                            jnp.sum(v_hi * w, axis=0, keepdims=True)], axis=1)


def _peer_mix_kernel(ids_hbm, uv_hbm, rows_ref, x_ref, h_ref, gates_ref, o_ref, ids_smem, buf, ids_sems, row_sems,
                     *, td, ts, picks, d, n_steps):
    step = pl.program_id(0)
    par = lax.rem(step, 2)
    ahead = MIX_SLOTS - 1
    assert td % MIX_SLOTS == 0 and td > ahead and (td + ts) % SUBLANES == 0

    def ids_copy(s, row):
        return pltpu.make_async_copy(ids_hbm.at[s], ids_smem.at[row], ids_sems.at[row])

    def issue(row, tok, slot):
        for j in range(picks):
            eid = ids_smem[row, tok * picks + j]
            pltpu.make_async_copy(uv_hbm.at[pl.ds(eid, 1), :], buf.at[slot, pl.ds(j, 1), :],
                                  row_sems.at[slot]).start()

    def wait_rows(slot):
        pltpu.make_async_copy(uv_hbm.at[pl.ds(0, picks), :], buf.at[slot], row_sems.at[slot]).wait()

    @pl.when(step == 0)
    def _():
        first = ids_copy(0, 0)
        first.start()
        first.wait()
        for u in range(ahead):
            issue(0, u, u)

    @pl.when(step + 1 < n_steps)
    def _():
        ids_copy(step + 1, 1 - par).start()

    loaded = {}

    def finish(pos, words):
        g, lane = divmod(pos, SUBLANES)
        if g not in loaded:
            loaded[g] = (gates_ref[g], h_ref[g * SUBLANES:(g + 1) * SUBLANES, :].astype(F32))
        gates, h8 = loaded[g]
        row = _expert_mix(words, h8[lane:lane + 1, :], gates[:, lane:lane + 1], d)
        o_ref[pos:pos + 1, :] = x_ref[pos:pos + 1, :] + row

    staged_done = 0
    for u in range(td):
        nxt = u + ahead
        if nxt < td:
            issue(par, nxt, nxt % MIX_SLOTS)
        else:
            @pl.when(step + 1 < n_steps)
            def _():
                if nxt == td:
                    ids_copy(step + 1, 1 - par).wait()
                issue(1 - par, nxt - td, nxt % MIX_SLOTS)

        staged_upto = (u + 1) * ts // td
        for k in range(staged_done, staged_upto):
            finish(td + k, rows_ref[k * picks:(k + 1) * picks, :])
        staged_done = staged_upto
        wait_rows(u % MIX_SLOTS)
        finish(u, buf[u % MIX_SLOTS])


def _peer_mix(ids_direct, uv, staged, x, h, gates3, *, td, ts):
    t, d = x.shape
    picks = gates3.shape[1]
    n_steps = t // (td + ts)
    assert staged.shape[0] == n_steps * ts * picks and ids_direct.shape == (n_steps, td * picks)
    return pl.pallas_call(
        functools.partial(_peer_mix_kernel, td=td, ts=ts, picks=picks, d=d, n_steps=n_steps),
        grid=(n_steps,),
        input_output_aliases={3: 0},
        in_specs=[
            pl.BlockSpec(memory_space=pl.ANY),
            pl.BlockSpec(memory_space=pl.ANY),
            pl.BlockSpec((ts * picks, d), lambda i: (i, 0)),
            pl.BlockSpec((td + ts, d), lambda i: (i, 0)),
            pl.BlockSpec((td + ts, d), lambda i: (i, 0)),
            pl.BlockSpec(((td + ts) // SUBLANES, picks, SUBLANES), lambda i: (i, 0, 0)),
        ],
        out_specs=pl.BlockSpec((td + ts, d), lambda i: (i, 0)),
        out_shape=jax.ShapeDtypeStruct((t, d), F32),
        scratch_shapes=[
            pltpu.SMEM((2, td * picks), jnp.int32),
            pltpu.VMEM((MIX_SLOTS, picks, d), jnp.int32),
            pltpu.SemaphoreType.DMA((2,)),
            pltpu.SemaphoreType.DMA((MIX_SLOTS,)),
        ],
        compiler_params=_params(("arbitrary",)),
    )(ids_direct, uv, staged, x, h, gates3)


def _sc_gather_rows(table, idx, *, chunk):
    n_rows = idx.shape[0]
    d = table.shape[1]
    workers = SC_CORES * SC_SUBCORES
    assert n_rows % (workers * 2 * chunk) == 0 and chunk % SUBLANES == 0 and chunk <= LANES
    rows_per_worker = n_rows // workers
    n_pairs = rows_per_worker // (2 * chunk)
    mesh = plsc.VectorSubcoreMesh(core_axis_name="c", subcore_axis_name="s")

    @functools.partial(
        pl.kernel, mesh=mesh,
        out_type=jax.ShapeDtypeStruct((n_rows, d), table.dtype),
        scratch_types=[
            pltpu.VMEM((chunk,), jnp.int32), pltpu.VMEM((chunk,), jnp.int32),
            pltpu.VMEM((chunk, d), table.dtype), pltpu.VMEM((chunk, d), table.dtype),
            pltpu.SemaphoreType.DMA, pltpu.SemaphoreType.DMA,
            pltpu.SemaphoreType.DMA, pltpu.SemaphoreType.DMA,
        ],
    )
    def gather_kernel(table_hbm, idx_hbm, out_hbm, idx0, idx1, rows0, rows1, gsem0, gsem1, wsem0, wsem1):
        idx_v, rows_v, gsem, wsem = (idx0, idx1), (rows0, rows1), (gsem0, gsem1), (wsem0, wsem1)
        worker = lax.axis_index("s") * SC_CORES + lax.axis_index("c")
        base = worker * rows_per_worker

        def out_rows(c):
            return pl.ds(pl.multiple_of(base + c * chunk, SUBLANES), chunk)

        def load_idx(slot, c):
            pltpu.sync_copy(idx_hbm.at[out_rows(c)], idx_v[slot])

        def gather(slot):
            return pltpu.make_async_copy(table_hbm.at[idx_v[slot]], rows_v[slot], gsem[slot])

        def writeout(slot, c):
            return pltpu.make_async_copy(rows_v[slot], out_hbm.at[out_rows(c)], wsem[slot])

        load_idx(0, 0)
        gather(0).start()

        @pl.loop(0, n_pairs)
        def _(p):
            c0 = 2 * p

            @pl.when(p > 0)
            def _():
                writeout(1, c0 - 1).wait()

            load_idx(1, c0 + 1)
            gather(1).start()
            gather(0).wait()
            writeout(0, c0).start()

            @pl.when(p + 1 < n_pairs)
            def _():
                load_idx(0, c0 + 2)
                writeout(0, c0).wait()
                gather(0).start()

            gather(1).wait()
            writeout(1, c0 + 1).start()

        writeout(0, 2 * n_pairs - 2).wait()
        writeout(1, 2 * n_pairs - 1).wait()

    return gather_kernel(table, idx)


def _pad_cols(w, width):
    return jnp.pad(w, ((0, 0), (0, width - w.shape[1])))


def _prepare_layer(g_norm_mix, w_in, b_gate, g_cq, w_uq, g_ckv, w_ukv, g_qn, g_qr, g_kn, g_kr,
                   w_a2, b_a, g_gla_out, w_out, g_norm_ffn, w_pq, sub_keys, u_emb, v_emb):
    d = w_in.shape[0]
    q_rank, kv_rank = g_cq.shape[0], g_ckv.shape[0]
    nope, rope = g_qn.shape[0], g_qr.shape[0]
    mla_heads = w_uq.shape[1] // (nope + rope)
    mla_v = w_ukv.shape[1] // mla_heads - nope
    gate_rank, gla_dk_all = w_a2.shape
    gla_dv = g_gla_out.shape[0]
    gla_heads = d // gla_dv
    gla_dk = gla_dk_all // gla_heads
    peer_heads, _, nkeys, half = sub_keys.shape
    assert nope == LANES and mla_v == LANES and rope <= LANES and rope % 2 == 0
    assert mla_heads * mla_v == d and gla_heads * gla_dv == d
    assert rope + gate_rank <= LANES and nkeys == LANES and half == LANES

    widths = (q_rank, kv_rank, rope, gla_dk_all, gla_dk_all, d, gate_rank, d, d, d)
    offs = [0]
    for wd in widths:
        offs.append(offs[-1] + wd)
    assert offs[-1] == w_in.shape[1]
    seg = lambda i: w_in[:, offs[i]:offs[i + 1]]
    w_main = jnp.concatenate([seg(5), seg(7), seg(8), seg(9), seg(3), seg(4), seg(0), seg(1)], axis=1).astype(BF16)
    w_small = _pad_cols(jnp.concatenate([seg(2), seg(6)], axis=1), LANES).astype(BF16)
    cq_off = 4 * d + 2 * gla_dk_all
    assert cq_off % q_rank == 0 and (cq_off + q_rank) % kv_rank == 0
    inv_freq = ROPE_THETA ** (-jnp.arange(0, rope, 2, dtype=F32) / rope)
    scale = (nope + rope) ** -0.5
    return dict(
        dims=dict(q_rank=q_rank, kv_rank=kv_rank, nope=nope, rope=rope, mla_heads=mla_heads, mla_v=mla_v,
                  gla_heads=gla_heads, gla_dk=gla_dk, gla_dv=gla_dv, peer_heads=peer_heads, nkeys=nkeys, half=half,
                  v_blk=0, og_blk=d // gla_dv, ga_blk=2, gb_blk=3, q_blk=4 * d // gla_dk,
                  k_blk=4 * d // gla_dk + gla_heads, cq_blk=cq_off // q_rank, ckv_blk=(cq_off + q_rank) // kv_rank),
        g_norm_mix=g_norm_mix[None, :], w_main=w_main, w_small=w_small,
        invf=_pad_cols(jnp.concatenate([inv_freq, inv_freq])[None, :], LANES),
        gq=_pad_cols(jnp.concatenate([g_qn, g_qr])[None, :] * scale, nope + LANES),
        gkr=_pad_cols(g_kr[None, :], LANES), g_cq=g_cq[None, :], g_ckv=g_ckv[None, :], g_kn=g_kn[None, :],
        w_uq=jnp.pad(w_uq.reshape(q_rank, mla_heads, nope + rope),
                     ((0, 0), (0, 0), (0, LANES - rope))).reshape(q_rank, -1).astype(BF16),
        w_ukv=w_ukv.astype(BF16),
        w2p=jnp.zeros((LANES, gla_dk_all), F32).at[rope:rope + gate_rank].set(w_a2).astype(BF16),
        b_a=b_a[None, :], g_on=g_gla_out[None, :], b_gate=b_gate, w_out=w_out.astype(BF16),
        g_norm_ffn=g_norm_ffn[None, :], w_pq=w_pq.astype(BF16),
        sk2d=sub_keys.reshape(peer_heads * 2 * nkeys, half).astype(BF16),
        uv=_pack_expert_table(u_emb, v_emb),
    )


def _mixers_and_route(x_all, row0, pos, w, *, batch, seq, tiles):
    t = batch * seq
    dm = w["dims"]
    p, small = _in_proj(x_all, w["g_norm_mix"], w["w_main"], w["w_small"], row0=row0, t=t,
                        tm=tiles["in_tm"], tn=tiles["in_tn"])
    q = _mla_q(p, dm["cq_blk"], w["g_cq"], w["w_uq"], w["gq"], pos, w["invf"], heads=dm["mla_heads"],
               rank=dm["q_rank"], nope=dm["nope"], rope=dm["rope"], tm=tiles["mla_tm"])
    k, v = _mla_kv(p, dm["ckv_blk"], w["g_ckv"], w["w_ukv"], w["g_kn"], small, w["gkr"], pos, w["invf"],
                   heads=dm["mla_heads"], rank=dm["kv_rank"], nope=dm["nope"], rope=dm["rope"], dv=dm["mla_v"],
                   tm=tiles["mla_tm"])
    o_mla = _mla_attn(q, k, v, batch=batch, seq=seq, heads=dm["mla_heads"], dk=dm["nope"] + LANES, dv=dm["mla_v"],
                      tq=tiles["attn_tq"], tk=tiles["attn_tk"])
    o_gla = _gla(p, small, w["w2p"], w["b_a"], w["g_on"], batch=batch, seq=seq, heads=dm["gla_heads"],
                 dk=dm["gla_dk"], dv=dm["gla_dv"], q_blk=dm["q_blk"], k_blk=dm["k_blk"], v_blk=dm["v_blk"],
                 og_blk=dm["og_blk"])
    x2 = _out_proj(x_all, p, o_mla, o_gla, w["b_gate"], w["w_out"], row0=row0, ga_blk=dm["ga_blk"],
                   gb_blk=dm["gb_blk"], tm=tiles["out_tm"], tn=tiles["out_tn"])
    h2, ids_t, gates_t = _peer_route(x2, w["g_norm_ffn"], w["w_pq"], w["sk2d"], heads=dm["peer_heads"],
                                     nkeys=dm["nkeys"], half=dm["half"], topk=PEER_TOPK, tm=tiles["route_tm"])
    picks = dm["peer_heads"] * PEER_TOPK
    gates3 = gates_t.reshape(picks, t // SUBLANES, SUBLANES).transpose(1, 0, 2)
    return x2, h2, ids_t.T, gates3


def _layer(x2, pos, w, *, batch, seq, tiles):
    t, d = x2.shape
    splits = tiles["mix_split"]
    groups = len(splits)
    tg, bg = t // groups, batch // groups
    routed = [_mixers_and_route(x2, g * tg, pos[g * tg:(g + 1) * tg], w, batch=bg, seq=seq, tiles=tiles)
              for g in range(groups)]
    out = []
    for (td, ts), (xg, hg, ids_tok, gates3) in zip(splits, routed):
        picks = ids_tok.shape[1]
        ids3 = ids_tok.reshape(tg // (td + ts), td + ts, picks)
        staged = _sc_gather_rows(w["uv"], ids3[:, td:].reshape(-1), chunk=tiles["sc_chunk"])
        out.append(_peer_mix(ids3[:, :td].reshape(-1, td * picks), w["uv"], staged, xg, hg, gates3, td=td, ts=ts))
    return jnp.concatenate(out, axis=0)


_TILES = dict(in_tm=2048, in_tn=512, mla_tm=2048, attn_tq=1024, attn_tk=1024,
              out_tm=1024, out_tn=512, route_tm=256, mix_split=((4, 12),) * 5 + ((12, 20),) * 3, sc_chunk=16)


def kernel(x, positions, g_norm_mix, w_in, b_gate, g_cq, w_uq, g_ckv, w_ukv, g_qn, g_qr, g_kn, g_kr,
           w_a2, b_a, g_gla_out, w_out, g_norm_ffn, w_pq, sub_keys, u_emb, v_emb, tiles=None):
    tiles = _TILES if tiles is None else tiles
    batch, seq, d = x.shape
    x2 = x.reshape(batch * seq, d)
    pos = positions.reshape(batch * seq, 1)
    for l in range(g_norm_mix.shape[0]):
        w = _prepare_layer(g_norm_mix[l], w_in[l], b_gate[l], g_cq[l], w_uq[l], g_ckv[l], w_ukv[l], g_qn[l],
                           g_qr[l], g_kn[l], g_kr[l], w_a2[l], b_a[l], g_gla_out[l], w_out[l], g_norm_ffn[l],
                           w_pq[l], sub_keys[l], u_emb[l], v_emb[l])
        x2 = _layer(x2, pos, w, batch=batch, seq=seq, tiles=tiles)
    return x2.reshape(batch, seq, d)
```

```python
import functools

import jax
import jax.numpy as jnp
from jax import lax
from jax.experimental import pallas as pl
from jax.experimental.pallas import tpu as pltpu
from jax.experimental.pallas import tpu_sc as plsc

EPS = 1e-6
ROPE_THETA = 10000.0
GLA_TAU = 16.0
GLA_CHUNK = 64
PEER_TOPK = 16

LANES = 128
SUBLANES = 8
VMEM_LIMIT_BYTES = 56 * 1024 * 1024
MIX_SLOTS = 4
SC_CORES = 2
SC_SUBCORES = 16

F32 = jnp.float32
BF16 = jnp.bfloat16
NEG_INF = float("-inf")


def _params(semantics):
    return pltpu.CompilerParams(dimension_semantics=semantics, vmem_limit_bytes=VMEM_LIMIT_BYTES)


def _rms(x, gain, n=None):
    ss = jnp.sum(x * x, axis=-1, keepdims=True)
    n = x.shape[-1] if n is None else n
    return x * lax.rsqrt(ss * (1.0 / n) + EPS) * gain


def _gelu_exact(x):
    return 0.5 * x * (1.0 + lax.erf(x * (0.5 ** 0.5)))


def _dot(a, b):
    return jnp.dot(a, b, preferred_element_type=F32)


def _dot_nt(a, b):
    return lax.dot_general(a, b, (((1,), (1,)), ((), ())), preferred_element_type=F32)


def _dot_tn(a, b):
    return lax.dot_general(a, b, (((0,), (0,)), ((), ())), preferred_element_type=F32)


def _in_proj_kernel(x_ref, g_ref, w_ref, ws_ref, p_ref, ps_ref, h_scr):
    @pl.when(pl.program_id(1) == 0)
    def _():
        h = _rms(x_ref[...], g_ref[...]).astype(BF16)
        h_scr[...] = h
        ps_ref[...] = _dot(h, ws_ref[...])

    p_ref[...] = _dot(h_scr[...], w_ref[...]).astype(p_ref.dtype)


def _in_proj(x, g, w_main, w_small, *, row0, t, tm, tn):
    d = x.shape[1]
    n = w_main.shape[1]
    assert row0 % tm == 0
    b0 = row0 // tm
    x_mode = pl.Buffered(1) if t == tm else None
    return pl.pallas_call(
        _in_proj_kernel,
        grid=(t // tm, n // tn),
        in_specs=[
            pl.BlockSpec((tm, d), lambda i, j: (i + b0, 0), pipeline_mode=x_mode),
            pl.BlockSpec((1, d), lambda i, j: (0, 0)),
            pl.BlockSpec((d, tn), lambda i, j: (0, j)),
            pl.BlockSpec((d, LANES), lambda i, j: (0, 0)),
        ],
        out_specs=[
            pl.BlockSpec((tm, tn), lambda i, j: (i, j)),
            pl.BlockSpec((tm, LANES), lambda i, j: (i, 0)),
        ],
        out_shape=[
            jax.ShapeDtypeStruct((t, n), BF16),
            jax.ShapeDtypeStruct((t, LANES), F32),
        ],
        scratch_shapes=[pltpu.VMEM((tm, d), BF16)],
        compiler_params=_params(("parallel", "arbitrary")),
    )(x, g, w_main, w_small)


def _rope_tables(pos_ref, invf_ref, rope):
    ang = pos_ref[...].astype(F32) * invf_ref[...]
    cos, sin = jnp.cos(ang), jnp.sin(ang)
    lane = lax.broadcasted_iota(jnp.int32, ang.shape, 1)
    half = rope // 2
    c = jnp.where(lane < rope, cos, 0.0)
    s_lo = jnp.where(lane < half, -sin, 0.0)
    s_hi = jnp.where(lane < half, 0.0, jnp.where(lane < rope, sin, 0.0))
    return c, s_lo, s_hi


def _apply_rope(pe, c, s_lo, s_hi, rope):
    half = rope // 2
    from_hi = pltpu.roll(pe, LANES - half, 1)
    from_lo = pltpu.roll(pe, half, 1)
    return pe * c + from_hi * s_lo + from_lo * s_hi


def _mla_q_kernel(cq_ref, gcq_ref, w_ref, gq_ref, pos_ref, invf_ref, q_ref,
                  h_scr, c_scr, slo_scr, shi_scr, *, nope, rope):
    @pl.when(pl.program_id(1) == 0)
    def _():
        h_scr[...] = _rms(cq_ref[...].astype(F32), gcq_ref[...]).astype(BF16)
        c, s_lo, s_hi = _rope_tables(pos_ref, invf_ref, rope)
        c_scr[...] = c
        slo_scr[...] = s_lo
        shi_scr[...] = s_hi

    y = _dot(h_scr[...], w_ref[...])
    g = gq_ref[...]
    qn = _rms(y[:, :nope], g[:, :nope])
    pe = _rms(y[:, nope:], g[:, nope:], n=rope)
    pe = _apply_rope(pe, c_scr[...], slo_scr[...], shi_scr[...], rope)
    q_ref[:, :nope] = qn.astype(q_ref.dtype)
    q_ref[:, nope:] = pe.astype(q_ref.dtype)


def _mla_q(p, cq_blk, g_cq, w_uq_p, gq, pos, invf, *, heads, rank, nope, rope, tm):
    t = p.shape[0]
    hw = nope + LANES
    return pl.pallas_call(
        functools.partial(_mla_q_kernel, nope=nope, rope=rope),
        grid=(t // tm, heads),
        in_specs=[
            pl.BlockSpec((tm, rank), lambda i, j: (i, cq_blk)),
            pl.BlockSpec((1, rank), lambda i, j: (0, 0)),
            pl.BlockSpec((rank, hw), lambda i, j: (0, j)),
            pl.BlockSpec((1, hw), lambda i, j: (0, 0)),
            pl.BlockSpec((tm, 1), lambda i, j: (i, 0)),
            pl.BlockSpec((1, LANES), lambda i, j: (0, 0)),
        ],
        out_specs=pl.BlockSpec((tm, hw), lambda i, j: (i, j)),
        out_shape=jax.ShapeDtypeStruct((t, heads * hw), BF16),
        scratch_shapes=[
            pltpu.VMEM((tm, rank), BF16),
            pltpu.VMEM((tm, LANES), F32),
            pltpu.VMEM((tm, LANES), F32),
            pltpu.VMEM((tm, LANES), F32),
        ],
        compiler_params=_params(("parallel", "arbitrary")),
    )(p, g_cq, w_uq_p, gq, pos, invf)


def _mla_kv_kernel(ckv_ref, gckv_ref, w_ref, gkn_ref, small_ref, gkr_ref, pos_ref, invf_ref,
                   k_ref, v_ref, h_scr, kpe_scr, *, nope, rope):
    @pl.when(pl.program_id(1) == 0)
    def _():
        h_scr[...] = _rms(ckv_ref[...].astype(F32), gckv_ref[...]).astype(BF16)
        c, s_lo, s_hi = _rope_tables(pos_ref, invf_ref, rope)
        sm = small_ref[...]
        lane = lax.broadcasted_iota(jnp.int32, sm.shape, 1)
        pe = _rms(jnp.where(lane < rope, sm, 0.0), gkr_ref[...], n=rope)
        kpe_scr[...] = _apply_rope(pe, c, s_lo, s_hi, rope).astype(BF16)

    y = _dot(h_scr[...], w_ref[...])
    k_ref[:, :nope] = _rms(y[:, :nope], gkn_ref[...]).astype(k_ref.dtype)
    k_ref[:, nope:] = kpe_scr[...]
    v_ref[...] = y[:, nope:].astype(v_ref.dtype)


def _mla_kv(p, ckv_blk, g_ckv, w_ukv, g_kn, small, gkr, pos, invf, *, heads, rank, nope, rope, dv, tm):
    t = p.shape[0]
    kw = nope + LANES
    return pl.pallas_call(
        functools.partial(_mla_kv_kernel, nope=nope, rope=rope),
        grid=(t // tm, heads),
        in_specs=[
            pl.BlockSpec((tm, rank), lambda i, j: (i, ckv_blk)),
            pl.BlockSpec((1, rank), lambda i, j: (0, 0)),
            pl.BlockSpec((rank, nope + dv), lambda i, j: (0, j)),
            pl.BlockSpec((1, nope), lambda i, j: (0, 0)),
            pl.BlockSpec((tm, LANES), lambda i, j: (i, 0)),
            pl.BlockSpec((1, LANES), lambda i, j: (0, 0)),
            pl.BlockSpec((tm, 1), lambda i, j: (i, 0)),
            pl.BlockSpec((1, LANES), lambda i, j: (0, 0)),
        ],
        out_specs=[
            pl.BlockSpec((tm, kw), lambda i, j: (i, j)),
            pl.BlockSpec((tm, dv), lambda i, j: (i, j)),
        ],
        out_shape=[
            jax.ShapeDtypeStruct((t, heads * kw), BF16),
            jax.ShapeDtypeStruct((t, heads * dv), BF16),
        ],
        scratch_shapes=[pltpu.VMEM((tm, rank), BF16), pltpu.VMEM((tm, LANES), BF16)],
        compiler_params=_params(("parallel", "arbitrary")),
    )(p, g_ckv, w_ukv, g_kn, small, gkr, pos, invf)


def _attn_kernel(q_ref, k_ref, v_ref, o_ref, *, tq, tk):
    qi = pl.program_id(2)
    q = q_ref[...]
    dv = v_ref.shape[1]
    row = qi * tq + lax.broadcasted_iota(jnp.int32, (tq, tk), 0)
    col0 = lax.broadcasted_iota(jnp.int32, (tq, tk), 1)

    def body(kb, carry, masked):
        m, l, acc = carry
        k0 = pl.multiple_of(kb * tk, tk)
        s = _dot_nt(q, k_ref[pl.ds(k0, tk), :])
        if masked:
            s = jnp.where(col0 + k0 <= row, s, NEG_INF)
        m_new = jnp.maximum(m, jnp.max(s, axis=-1, keepdims=True))
        alpha = jnp.exp(m - m_new)
        pr = jnp.exp(s - m_new)
        l = alpha * l + jnp.sum(pr, axis=-1, keepdims=True)
        acc = alpha * acc + _dot(pr.astype(BF16), v_ref[pl.ds(k0, tk), :])
        return m_new, l, acc

    init = (jnp.full((tq, 1), NEG_INF, F32), jnp.zeros((tq, 1), F32), jnp.zeros((tq, dv), F32))
    n_below = (qi * tq) // tk
    nkb = ((qi + 1) * tq + tk - 1) // tk
    carry = lax.fori_loop(0, n_below, functools.partial(body, masked=False), init)
    _, l, acc = lax.fori_loop(n_below, nkb, functools.partial(body, masked=True), carry)
    o_ref[...] = (acc / l).astype(o_ref.dtype)


def _mla_attn(q, k, v, *, batch, seq, heads, dk, dv, tq, tk):
    t = q.shape[0]
    nq = seq // tq
    return pl.pallas_call(
        functools.partial(_attn_kernel, tq=tq, tk=tk),
        grid=(batch, heads, nq),
        in_specs=[
            pl.BlockSpec((tq, dk), lambda b, h, i: (b * nq + i, h)),
            pl.BlockSpec((seq, dk), lambda b, h, i: (b, h)),
            pl.BlockSpec((seq, dv), lambda b, h, i: (b, h)),
        ],
        out_specs=pl.BlockSpec((tq, dv), lambda b, h, i: (b * nq + i, h)),
        out_shape=jax.ShapeDtypeStruct((t, heads * dv), BF16),
        compiler_params=_params(("parallel", "parallel", "arbitrary")),
    )(q, k, v)


def _gla_kernel(q_ref, k_ref, v_ref, og_ref, small_ref, w2_ref, ba_ref, gon_ref, o_ref, st_scr,
                *, seq, dk, dv, chunk):
    c = chunk
    st_scr[...] = jnp.zeros_like(st_scr)
    r_i = lax.broadcasted_iota(jnp.int32, (c, c), 0)
    c_i = lax.broadcasted_iota(jnp.int32, (c, c), 1)
    tri = jnp.where(c_i <= r_i, 1.0, 0.0).astype(BF16)
    row_id = lax.broadcasted_iota(jnp.int32, (c, 1), 0)
    w2 = w2_ref[...]
    ba = ba_ref[...]
    gon = gon_ref[...]
    q_scale = dk ** -0.5

    def chunk_step(ci, carry):
        r0 = pl.multiple_of(ci * c, c)
        qc = q_ref[pl.ds(r0, c), :].astype(F32) * q_scale
        kc = k_ref[pl.ds(r0, c), :].astype(F32)
        vc = v_ref[pl.ds(r0, c), :]
        z = _dot(small_ref[pl.ds(r0, c), :].astype(BF16), w2) + ba
        la = jax.nn.log_sigmoid(z) * (1.0 / GLA_TAU)
        hi = la.astype(BF16)
        r1 = la - hi.astype(F32)
        mid = r1.astype(BF16)
        lo = (r1 - mid.astype(F32)).astype(BF16)
        b = _dot(tri, hi) + _dot(tri, mid) + _dot(tri, lo)

        st = st_scr[...]
        inter = _dot_nt((qc * jnp.exp(b)).astype(BF16), st.astype(BF16))

        att = jnp.zeros((c, c), F32)
        for j in range(c):
            lo_r = (j // SUBLANES) * SUBLANES
            d = b[lo_r:, :] - b[j:j + 1, :]
            head = jnp.where(row_id[lo_r:lo_r + SUBLANES, :] >= j, d[:SUBLANES, :], NEG_INF)
            e = jnp.exp(jnp.concatenate([head, d[SUBLANES:, :]], axis=0) if lo_r + SUBLANES < c else head)
            col = jnp.sum(qc[lo_r:, :] * kc[j:j + 1, :] * e, axis=-1, keepdims=True)
            if lo_r:
                col = jnp.concatenate([jnp.zeros((lo_r, 1), F32), col], axis=0)
            att = jnp.where(c_i == j, col, att)
        o = inter + _dot(att.astype(BF16), vc)

        b_last = b[c - 1:c, :]
        k_dec = (kc * jnp.exp(b_last - b)).astype(BF16)
        st_scr[...] = st * jnp.exp(b_last) + _dot_tn(vc, k_dec)

        og = og_ref[pl.ds(r0, c), :].astype(F32)
        out = _rms(o, gon) * (og * jax.nn.sigmoid(og))
        o_ref[pl.ds(r0, c), :] = out.astype(o_ref.dtype)
        return carry

    lax.fori_loop(0, seq // c, chunk_step, 0)


def _gla(p, small, w2p, b_a, g_on, *, batch, seq, heads, dk, dv, q_blk, k_blk, v_blk, og_blk):
    t = p.shape[0]
    return pl.pallas_call(
        functools.partial(_gla_kernel, seq=seq, dk=dk, dv=dv, chunk=GLA_CHUNK),
        grid=(batch, heads),
        in_specs=[
            pl.BlockSpec((seq, dk), lambda b, h: (b, q_blk + h)),
            pl.BlockSpec((seq, dk), lambda b, h: (b, k_blk + h)),
            pl.BlockSpec((seq, dv), lambda b, h: (b, v_blk + h)),
            pl.BlockSpec((seq, dv), lambda b, h: (b, og_blk + h)),
            pl.BlockSpec((seq, LANES), lambda b, h: (b, 0)),
            pl.BlockSpec((LANES, dk), lambda b, h: (0, h)),
            pl.BlockSpec((1, dk), lambda b, h: (0, h)),
            pl.BlockSpec((1, dv), lambda b, h: (0, 0)),
        ],
        out_specs=pl.BlockSpec((seq, dv), lambda b, h: (b, h)),
        out_shape=jax.ShapeDtypeStruct((t, heads * dv), BF16),
        scratch_shapes=[pltpu.VMEM((dv, dk), F32)],
        compiler_params=_params(("parallel", "parallel")),
    )(p, p, p, p, small, w2p, b_a, g_on)


def _out_proj_kernel(x_ref, ga_ref, gb_ref, oa_ref, ob_ref, bg_ref, w_ref, o_ref, m_scr):
    @pl.when(pl.program_id(1) == 0)
    def _():
        bg = bg_ref[...]
        sa = jax.nn.sigmoid(ga_ref[...].astype(F32) + bg[0:1, :])
        sb = jax.nn.sigmoid(gb_ref[...].astype(F32) + bg[1:2, :])
        m_scr[...] = (sa * oa_ref[...].astype(F32) + sb * ob_ref[...].astype(F32)).astype(BF16)

    o_ref[...] = x_ref[...] + _dot(m_scr[...], w_ref[...])


def _out_proj(x, p, o_mla, o_gla, b_gate, w_out, *, row0, ga_blk, gb_blk, tm, tn):
    t, d = p.shape[0], x.shape[1]
    assert row0 % tm == 0
    b0 = row0 // tm
    return pl.pallas_call(
        _out_proj_kernel,
        grid=(t // tm, d // tn),
        in_specs=[
            pl.BlockSpec((tm, tn), lambda i, j: (i + b0, j)),
            pl.BlockSpec((tm, d), lambda i, j: (i, ga_blk)),
            pl.BlockSpec((tm, d), lambda i, j: (i, gb_blk)),
            pl.BlockSpec((tm, d), lambda i, j: (i, 0)),
            pl.BlockSpec((tm, d), lambda i, j: (i, 0)),
            pl.BlockSpec((2, d), lambda i, j: (0, 0)),
            pl.BlockSpec((d, tn), lambda i, j: (0, j)),
        ],
        out_specs=pl.BlockSpec((tm, tn), lambda i, j: (i, j)),
        out_shape=jax.ShapeDtypeStruct((t, d), F32),
        scratch_shapes=[pltpu.VMEM((tm, d), BF16)],
        compiler_params=_params(("parallel", "arbitrary")),
    )(x, p, p, o_mla, o_gla, b_gate, w_out)


def _topk_rows(s, k, payload=None):
    n = s.shape[0]
    iota = lax.broadcasted_iota(jnp.int32, s.shape, 0).astype(F32)
    kiota = lax.broadcasted_iota(jnp.int32, (k, s.shape[1]), 0)
    vals = jnp.zeros((k, s.shape[1]), F32)
    picks = jnp.zeros((k, s.shape[1]), F32)
    for r in range(k):
        m = jnp.max(s, axis=0, keepdims=True)
        idx = jnp.min(jnp.where(s == m, iota, float(n)), axis=0, keepdims=True)
        hit = iota == idx
        if payload is None:
            pick = idx
        else:
            pick = jnp.sum(jnp.where(hit, payload, 0.0), axis=0, keepdims=True)
        vals = jnp.where(kiota == r, m, vals)
        picks = jnp.where(kiota == r, pick, picks)
        s = jnp.where(hit, NEG_INF, s)
    return vals, picks


def _peer_route_kernel(x_ref, g_ref, w_ref, sk_ref, h_ref, ids_ref, gates_ref, *, heads, nkeys, half, topk):
    h = _rms(x_ref[...], g_ref[...]).astype(BF16)
    h_ref[...] = h
    qf = _dot(h, w_ref[...])
    for hd in range(heads):
        tops = []
        for part in range(2):
            o = (hd * 2 + part) * half
            qh = qf[:, o:o + half].astype(BF16)
            keys = sk_ref[(hd * 2 + part) * nkeys:(hd * 2 + part + 1) * nkeys, :]
            tops.append(_topk_rows(_dot_nt(keys, qh), topk))
        (s1, i1), (s2, i2) = tops
        assert topk == 2 * SUBLANES
        hs = SUBLANES
        pair_s = [s1[0:1, :] + s2] + [s1[a:a + 1, :] + s2[:hs, :] for a in range(1, hs)] + [s1[hs:, :] + s2[0:1, :]]
        pair_id = ([i1[0:1, :] * float(nkeys) + i2]
                   + [i1[a:a + 1, :] * float(nkeys) + i2[:hs, :] for a in range(1, hs)]
                   + [i1[hs:, :] * float(nkeys) + i2[0:1, :]])
        best_s, best_id = _topk_rows(jnp.concatenate(pair_s, axis=0), topk,
                                     payload=jnp.concatenate(pair_id, axis=0))
        e = jnp.exp(best_s - best_s[0:1, :])
        gate = e / jnp.sum(e, axis=0, keepdims=True)
        ids_ref[hd * topk:(hd + 1) * topk, :] = best_id.astype(jnp.int32)
        gates_ref[hd * topk:(hd + 1) * topk, :] = gate


def _peer_route(x, g, w_pq, sk2d, *, heads, nkeys, half, topk, tm):
    t, d = x.shape
    dq = w_pq.shape[1]
    return pl.pallas_call(
        functools.partial(_peer_route_kernel, heads=heads, nkeys=nkeys, half=half, topk=topk),
        grid=(t // tm,),
        in_specs=[
            pl.BlockSpec((tm, d), lambda i: (i, 0)),
            pl.BlockSpec((1, d), lambda i: (0, 0)),
            pl.BlockSpec((d, dq), lambda i: (0, 0)),
            pl.BlockSpec((heads * 2 * nkeys, half), lambda i: (0, 0)),
        ],
        out_specs=[
            pl.BlockSpec((tm, d), lambda i: (i, 0)),
            pl.BlockSpec((heads * topk, tm), lambda i: (0, i)),
            pl.BlockSpec((heads * topk, tm), lambda i: (0, i)),
        ],
        out_shape=[
            jax.ShapeDtypeStruct((t, d), BF16),
            jax.ShapeDtypeStruct((heads * topk, t), jnp.int32),
            jax.ShapeDtypeStruct((heads * topk, t), F32),
        ],
        compiler_params=_params(("parallel",)),
    )(x, g, w_pq, sk2d)


def _pack_expert_table(u_emb, v_emb):
    n, d = u_emb.shape
    half = d // 2
    tm = 512
    assert n % tm == 0

    def pack(w):
        bits = lax.bitcast_convert_type(w.astype(BF16).astype(F32), jnp.int32)
        return ((bits[:, :half] >> 16) & jnp.int32(0xFFFF)) | (bits[:, half:] & jnp.int32(-65536))

    def pack_kernel(u_ref, v_ref, o_ref):
        o_ref[:, :half] = pack(u_ref[...])
        o_ref[:, half:] = pack(v_ref[...])

    return pl.pallas_call(
        pack_kernel,
        grid=(n // tm,),
        in_specs=[pl.BlockSpec((tm, d), lambda i: (i, 0)), pl.BlockSpec((tm, d), lambda i: (i, 0))],
        out_specs=pl.BlockSpec((tm, d), lambda i: (i, 0)),
        out_shape=jax.ShapeDtypeStruct((n, d), jnp.int32),
        compiler_params=_params(("parallel",)),
    )(u_emb, v_emb)


def _unpack_words(w):
    lo = lax.bitcast_convert_type(w << 16, F32)
    hi = lax.bitcast_convert_type(w & jnp.int32(-65536), F32)
    return lo, hi


def _expert_mix(words, h, gate, d):
    half = d // 2
    u_lo, u_hi = _unpack_words(words[:, :half])
    act = jnp.sum(u_lo * h[:, :half] + u_hi * h[:, half:], axis=-1, keepdims=True)
    w = gate * _gelu_exact(act)
    v_lo, v_hi = _unpack_words(words[:, half:])
    return jnp.concatenate([jnp.sum(v_lo * w, axis=0, keepdims=True),
                            jnp.sum(v_hi * w, axis=0, keepdims=True)], axis=1)


def _peer_mix_kernel(ids_hbm, uv_hbm, rows_ref, x_ref, h_ref, gates_ref, o_ref, ids_smem, buf, ids_sems, row_sems,
                     *, td, ts, picks, d, n_steps):
    step = pl.program_id(0)
    par = lax.rem(step, 2)
    ahead = MIX_SLOTS - 1
    assert td % MIX_SLOTS == 0 and td > ahead and (td + ts) % SUBLANES == 0

    def ids_copy(s, row):
        return pltpu.make_async_copy(ids_hbm.at[s], ids_smem.at[row], ids_sems.at[row])

    def issue(row, tok, slot):
        for j in range(picks):
            eid = ids_smem[row, tok * picks + j]
            pltpu.make_async_copy(uv_hbm.at[pl.ds(eid, 1), :], buf.at[slot, pl.ds(j, 1), :],
                                  row_sems.at[slot]).start()

    def wait_rows(slot):
        pltpu.make_async_copy(uv_hbm.at[pl.ds(0, picks), :], buf.at[slot], row_sems.at[slot]).wait()

    @pl.when(step == 0)
    def _():
        first = ids_copy(0, 0)
        first.start()
        first.wait()
        for u in range(ahead):
            issue(0, u, u)

    @pl.when(step + 1 < n_steps)
    def _():
        ids_copy(step + 1, 1 - par).start()

    loaded = {}

    def finish(pos, words):
        g, lane = divmod(pos, SUBLANES)
        if g not in loaded:
            loaded[g] = (gates_ref[g], h_ref[g * SUBLANES:(g + 1) * SUBLANES, :].astype(F32))
        gates, h8 = loaded[g]
        row = _expert_mix(words, h8[lane:lane + 1, :], gates[:, lane:lane + 1], d)
        o_ref[pos:pos + 1, :] = x_ref[pos:pos + 1, :] + row

    staged_done = 0
    for u in range(td):
        nxt = u + ahead
        if nxt < td:
            issue(par, nxt, nxt % MIX_SLOTS)
        else:
            @pl.when(step + 1 < n_steps)
            def _():
                if nxt == td:
                    ids_copy(step + 1, 1 - par).wait()
                issue(1 - par, nxt - td, nxt % MIX_SLOTS)

        staged_upto = (u + 1) * ts // td
        for k in range(staged_done, staged_upto):
            finish(td + k, rows_ref[k * picks:(k + 1) * picks, :])
        staged_done = staged_upto
        wait_rows(u % MIX_SLOTS)
        finish(u, buf[u % MIX_SLOTS])


def _peer_mix(ids_direct, uv, staged, x, h, gates3, *, td, ts):
    t, d = x.shape
    picks = gates3.shape[1]
    n_steps = t // (td + ts)
    assert staged.shape[0] == n_steps * ts * picks and ids_direct.shape == (n_steps, td * picks)
    return pl.pallas_call(
        functools.partial(_peer_mix_kernel, td=td, ts=ts, picks=picks, d=d, n_steps=n_steps),
        grid=(n_steps,),
        input_output_aliases={3: 0},
        in_specs=[
            pl.BlockSpec(memory_space=pl.ANY),
            pl.BlockSpec(memory_space=pl.ANY),
            pl.BlockSpec((ts * picks, d), lambda i: (i, 0)),
            pl.BlockSpec((td + ts, d), lambda i: (i, 0)),
            pl.BlockSpec((td + ts, d), lambda i: (i, 0)),
            pl.BlockSpec(((td + ts) // SUBLANES, picks, SUBLANES), lambda i: (i, 0, 0)),
        ],
        out_specs=pl.BlockSpec((td + ts, d), lambda i: (i, 0)),
        out_shape=jax.ShapeDtypeStruct((t, d), F32),
        scratch_shapes=[
            pltpu.SMEM((2, td * picks), jnp.int32),
            pltpu.VMEM((MIX_SLOTS, picks, d), jnp.int32),
            pltpu.SemaphoreType.DMA((2,)),
            pltpu.SemaphoreType.DMA((MIX_SLOTS,)),
        ],
        compiler_params=_params(("arbitrary",)),
    )(ids_direct, uv, staged, x, h, gates3)


def _sc_gather_rows(table, idx, *, chunk):
    n_rows = idx.shape[0]
    d = table.shape[1]
    workers = SC_CORES * SC_SUBCORES
    assert n_rows % (workers * 2 * chunk) == 0 and chunk % SUBLANES == 0 and chunk <= LANES
    rows_per_worker = n_rows // workers
    n_pairs = rows_per_worker // (2 * chunk)
    mesh = plsc.VectorSubcoreMesh(core_axis_name="c", subcore_axis_name="s")

    @functools.partial(
        pl.kernel, mesh=mesh,
        out_type=jax.ShapeDtypeStruct((n_rows, d), table.dtype),
        scratch_types=[
            pltpu.VMEM((chunk,), jnp.int32), pltpu.VMEM((chunk,), jnp.int32),
            pltpu.VMEM((chunk, d), table.dtype), pltpu.VMEM((chunk, d), table.dtype),
            pltpu.SemaphoreType.DMA, pltpu.SemaphoreType.DMA,
            pltpu.SemaphoreType.DMA, pltpu.SemaphoreType.DMA,
        ],
    )
    def gather_kernel(table_hbm, idx_hbm, out_hbm, idx0, idx1, rows0, rows1, gsem0, gsem1, wsem0, wsem1):
        idx_v, rows_v, gsem, wsem = (idx0, idx1), (rows0, rows1), (gsem0, gsem1), (wsem0, wsem1)
        worker = lax.axis_index("s") * SC_CORES + lax.axis_index("c")
        base = worker * rows_per_worker

        def out_rows(c):
            return pl.ds(pl.multiple_of(base + c * chunk, SUBLANES), chunk)

        def load_idx(slot, c):
            pltpu.sync_copy(idx_hbm.at[out_rows(c)], idx_v[slot])

        def gather(slot):
            return pltpu.make_async_copy(table_hbm.at[idx_v[slot]], rows_v[slot], gsem[slot])

        def writeout(slot, c):
            return pltpu.make_async_copy(rows_v[slot], out_hbm.at[out_rows(c)], wsem[slot])

        load_idx(0, 0)
        gather(0).start()

        @pl.loop(0, n_pairs)
        def _(p):
            c0 = 2 * p

            @pl.when(p > 0)
            def _():
                writeout(1, c0 - 1).wait()

            load_idx(1, c0 + 1)
            gather(1).start()
            gather(0).wait()
            writeout(0, c0).start()

            @pl.when(p + 1 < n_pairs)
            def _():
                load_idx(0, c0 + 2)
                writeout(0, c0).wait()
                gather(0).start()

            gather(1).wait()
            writeout(1, c0 + 1).start()

        writeout(0, 2 * n_pairs - 2).wait()
        writeout(1, 2 * n_pairs - 1).wait()

    return gather_kernel(table, idx)


def _pad_cols(w, width):
    return jnp.pad(w, ((0, 0), (0, width - w.shape[1])))


def _prepare_layer(g_norm_mix, w_in, b_gate, g_cq, w_uq, g_ckv, w_ukv, g_qn, g_qr, g_kn, g_kr,
                   w_a2, b_a, g_gla_out, w_out, g_norm_ffn, w_pq, sub_keys, u_emb, v_emb):
    d = w_in.shape[0]
    q_rank, kv_rank = g_cq.shape[0], g_ckv.shape[0]
    nope, rope = g_qn.shape[0], g_qr.shape[0]
    mla_heads = w_uq.shape[1] // (nope + rope)
    mla_v = w_ukv.shape[1] // mla_heads - nope
    gate_rank, gla_dk_all = w_a2.shape
    gla_dv = g_gla_out.shape[0]
    gla_heads = d // gla_dv
    gla_dk = gla_dk_all // gla_heads
    peer_heads, _, nkeys, half = sub_keys.shape
    assert nope == LANES and mla_v == LANES and rope <= LANES and rope % 2 == 0
    assert mla_heads * mla_v == d and gla_heads * gla_dv == d
    assert rope + gate_rank <= LANES and nkeys == LANES and half == LANES

    widths = (q_rank, kv_rank, rope, gla_dk_all, gla_dk_all, d, gate_rank, d, d, d)
    offs = [0]
    for wd in widths:
        offs.append(offs[-1] + wd)
    assert offs[-1] == w_in.shape[1]
    seg = lambda i: w_in[:, offs[i]:offs[i + 1]]
    w_main = jnp.concatenate([seg(5), seg(7), seg(8), seg(9), seg(3), seg(4), seg(0), seg(1)], axis=1).astype(BF16)
    w_small = _pad_cols(jnp.concatenate([seg(2), seg(6)], axis=1), LANES).astype(BF16)
    cq_off = 4 * d + 2 * gla_dk_all
    assert cq_off % q_rank == 0 and (cq_off + q_rank) % kv_rank == 0
    inv_freq = ROPE_THETA ** (-jnp.arange(0, rope, 2, dtype=F32) / rope)
    scale = (nope + rope) ** -0.5
    return dict(
        dims=dict(q_rank=q_rank, kv_rank=kv_rank, nope=nope, rope=rope, mla_heads=mla_heads, mla_v=mla_v,
                  gla_heads=gla_heads, gla_dk=gla_dk, gla_dv=gla_dv, peer_heads=peer_heads, nkeys=nkeys, half=half,
                  v_blk=0, og_blk=d // gla_dv, ga_blk=2, gb_blk=3, q_blk=4 * d // gla_dk,
                  k_blk=4 * d // gla_dk + gla_heads, cq_blk=cq_off // q_rank, ckv_blk=(cq_off + q_rank) // kv_rank),
        g_norm_mix=g_norm_mix[None, :], w_main=w_main, w_small=w_small,
        invf=_pad_cols(jnp.concatenate([inv_freq, inv_freq])[None, :], LANES),
        gq=_pad_cols(jnp.concatenate([g_qn, g_qr])[None, :] * scale, nope + LANES),
        gkr=_pad_cols(g_kr[None, :], LANES), g_cq=g_cq[None, :], g_ckv=g_ckv[None, :], g_kn=g_kn[None, :],
        w_uq=jnp.pad(w_uq.reshape(q_rank, mla_heads, nope + rope),
                     ((0, 0), (0, 0), (0, LANES - rope))).reshape(q_rank, -1).astype(BF16),
        w_ukv=w_ukv.astype(BF16),
        w2p=jnp.zeros((LANES, gla_dk_all), F32).at[rope:rope + gate_rank].set(w_a2).astype(BF16),
        b_a=b_a[None, :], g_on=g_gla_out[None, :], b_gate=b_gate, w_out=w_out.astype(BF16),
        g_norm_ffn=g_norm_ffn[None, :], w_pq=w_pq.astype(BF16),
        sk2d=sub_keys.reshape(peer_heads * 2 * nkeys, half).astype(BF16),
        uv=_pack_expert_table(u_emb, v_emb),
    )


def _mixers_and_route(x_all, row0, pos, w, *, batch, seq, tiles):
    t = batch * seq
    dm = w["dims"]
    p, small = _in_proj(x_all, w["g_norm_mix"], w["w_main"], w["w_small"], row0=row0, t=t,
                        tm=tiles["in_tm"], tn=tiles["in_tn"])
    q = _mla_q(p, dm["cq_blk"], w["g_cq"], w["w_uq"], w["gq"], pos, w["invf"], heads=dm["mla_heads"],
               rank=dm["q_rank"], nope=dm["nope"], rope=dm["rope"], tm=tiles["mla_tm"])
    k, v = _mla_kv(p, dm["ckv_blk"], w["g_ckv"], w["w_ukv"], w["g_kn"], small, w["gkr"], pos, w["invf"],
                   heads=dm["mla_heads"], rank=dm["kv_rank"], nope=dm["nope"], rope=dm["rope"], dv=dm["mla_v"],
                   tm=tiles["mla_tm"])
    o_mla = _mla_attn(q, k, v, batch=batch, seq=seq, heads=dm["mla_heads"], dk=dm["nope"] + LANES, dv=dm["mla_v"],
                      tq=tiles["attn_tq"], tk=tiles["attn_tk"])
    o_gla = _gla(p, small, w["w2p"], w["b_a"], w["g_on"], batch=batch, seq=seq, heads=dm["gla_heads"],
                 dk=dm["gla_dk"], dv=dm["gla_dv"], q_blk=dm["q_blk"], k_blk=dm["k_blk"], v_blk=dm["v_blk"],
                 og_blk=dm["og_blk"])
    x2 = _out_proj(x_all, p, o_mla, o_gla, w["b_gate"], w["w_out"], row0=row0, ga_blk=dm["ga_blk"],
                   gb_blk=dm["gb_blk"], tm=tiles["out_tm"], tn=tiles["out_tn"])
    h2, ids_t, gates_t = _peer_route(x2, w["g_norm_ffn"], w["w_pq"], w["sk2d"], heads=dm["peer_heads"],
                                     nkeys=dm["nkeys"], half=dm["half"], topk=PEER_TOPK, tm=tiles["route_tm"])
    picks = dm["peer_heads"] * PEER_TOPK
    gates3 = gates_t.reshape(picks, t // SUBLANES, SUBLANES).transpose(1, 0, 2)
    return x2, h2, ids_t.T, gates3


def _layer(x2, pos, w, *, batch, seq, tiles):
    t, d = x2.shape
    splits = tiles["mix_split"]
    groups = len(splits)
    tg, bg = t // groups, batch // groups
    routed = [_mixers_and_route(x2, g * tg, pos[g * tg:(g + 1) * tg], w, batch=bg, seq=seq, tiles=tiles)
              for g in range(groups)]
    out = []
    for (td, ts), (xg, hg, ids_tok, gates3) in zip(splits, routed):
        picks = ids_tok.shape[1]
        ids3 = ids_tok.reshape(tg // (td + ts), td + ts, picks)
        staged = _sc_gather_rows(w["uv"], ids3[:, td:].reshape(-1), chunk=tiles["sc_chunk"])
        out.append(_peer_mix(ids3[:, :td].reshape(-1, td * picks), w["uv"], staged, xg, hg, gates3, td=td, ts=ts))
    return jnp.concatenate(out, axis=0)


_TILES = dict(in_tm=2048, in_tn=512, mla_tm=2048, attn_tq=1024, attn_tk=1024,
              out_tm=1024, out_tn=512, route_tm=256, mix_split=((4, 12),) * 5 + ((12, 20),) * 2 + ((4, 12),), sc_chunk=16)


def kernel(x, positions, g_norm_mix, w_in, b_gate, g_cq, w_uq, g_ckv, w_ukv, g_qn, g_qr, g_kn, g_kr,
           w_a2, b_a, g_gla_out, w_out, g_norm_ffn, w_pq, sub_keys, u_emb, v_emb, tiles=None):
    tiles = _TILES if tiles is None else tiles
    batch, seq, d = x.shape
    x2 = x.reshape(batch * seq, d)
    pos = positions.reshape(batch * seq, 1)
    for l in range(g_norm_mix.shape[0]):
        w = _prepare_layer(g_norm_mix[l], w_in[l], b_gate[l], g_cq[l], w_uq[l], g_ckv[l], w_ukv[l], g_qn[l],
                           g_qr[l], g_kn[l], g_kr[l], w_a2[l], b_a[l], g_gla_out[l], w_out[l], g_norm_ffn[l],
                           w_pq[l], sub_keys[l], u_emb[l], v_emb[l])
        x2 = _layer(x2, pos, w, batch=batch, seq=seq, tiles=tiles)
    return x2.reshape(batch, seq, d)
```

```python
import functools

import jax
import jax.numpy as jnp
from jax import lax
from jax.experimental import pallas as pl
from jax.experimental.pallas import tpu as pltpu
from jax.experimental.pallas import tpu_sc as plsc

EPS = 1e-6
ROPE_THETA = 10000.0
GLA_TAU = 16.0
GLA_CHUNK = 64
PEER_TOPK = 16

LANES = 128
SUBLANES = 8
VMEM_LIMIT_BYTES = 56 * 1024 * 1024
MIX_SLOTS = 4
SC_CORES = 2
SC_SUBCORES = 16

F32 = jnp.float32
BF16 = jnp.bfloat16
NEG_INF = float("-inf")


def _params(semantics):
    return pltpu.CompilerParams(dimension_semantics=semantics, vmem_limit_bytes=VMEM_LIMIT_BYTES)


def _rms(x, gain, n=None):
    ss = jnp.sum(x * x, axis=-1, keepdims=True)
    n = x.shape[-1] if n is None else n
    return x * lax.rsqrt(ss * (1.0 / n) + EPS) * gain


def _gelu_exact(x):
    return 0.5 * x * (1.0 + lax.erf(x * (0.5 ** 0.5)))


def _dot(a, b):
    return jnp.dot(a, b, preferred_element_type=F32)


def _dot_nt(a, b):
    return lax.dot_general(a, b, (((1,), (1,)), ((), ())), preferred_element_type=F32)


def _dot_tn(a, b):
    return lax.dot_general(a, b, (((0,), (0,)), ((), ())), preferred_element_type=F32)


def _in_proj_kernel(x_ref, g_ref, w_ref, ws_ref, p_ref, ps_ref, h_scr):
    @pl.when(pl.program_id(1) == 0)
    def _():
        h = _rms(x_ref[...], g_ref[...]).astype(BF16)
        h_scr[...] = h
        ps_ref[...] = _dot(h, ws_ref[...])

    p_ref[...] = _dot(h_scr[...], w_ref[...]).astype(p_ref.dtype)


def _in_proj(x, g, w_main, w_small, *, row0, t, tm, tn):
    d = x.shape[1]
    n = w_main.shape[1]
    assert row0 % tm == 0
    b0 = row0 // tm
    x_mode = pl.Buffered(1) if t == tm else None
    return pl.pallas_call(
        _in_proj_kernel,
        grid=(t // tm, n // tn),
        in_specs=[
            pl.BlockSpec((tm, d), lambda i, j: (i + b0, 0), pipeline_mode=x_mode),
            pl.BlockSpec((1, d), lambda i, j: (0, 0)),
            pl.BlockSpec((d, tn), lambda i, j: (0, j)),
            pl.BlockSpec((d, LANES), lambda i, j: (0, 0)),
        ],
        out_specs=[
            pl.BlockSpec((tm, tn), lambda i, j: (i, j)),
            pl.BlockSpec((tm, LANES), lambda i, j: (i, 0)),
        ],
        out_shape=[
            jax.ShapeDtypeStruct((t, n), BF16),
            jax.ShapeDtypeStruct((t, LANES), F32),
        ],
        scratch_shapes=[pltpu.VMEM((tm, d), BF16)],
        compiler_params=_params(("parallel", "arbitrary")),
    )(x, g, w_main, w_small)


def _rope_tables(pos_ref, invf_ref, rope):
    ang = pos_ref[...].astype(F32) * invf_ref[...]
    cos, sin = jnp.cos(ang), jnp.sin(ang)
    lane = lax.broadcasted_iota(jnp.int32, ang.shape, 1)
    half = rope // 2
    c = jnp.where(lane < rope, cos, 0.0)
    s_lo = jnp.where(lane < half, -sin, 0.0)
    s_hi = jnp.where(lane < half, 0.0, jnp.where(lane < rope, sin, 0.0))
    return c, s_lo, s_hi


def _apply_rope(pe, c, s_lo, s_hi, rope):
    half = rope // 2
    from_hi = pltpu.roll(pe, LANES - half, 1)
    from_lo = pltpu.roll(pe, half, 1)
    return pe * c + from_hi * s_lo + from_lo * s_hi


def _mla_q_kernel(cq_ref, gcq_ref, w_ref, gq_ref, pos_ref, invf_ref, q_ref,
                  h_scr, c_scr, slo_scr, shi_scr, *, nope, rope):
    @pl.when(pl.program_id(1) == 0)
    def _():
        h_scr[...] = _rms(cq_ref[...].astype(F32), gcq_ref[...]).astype(BF16)
        c, s_lo, s_hi = _rope_tables(pos_ref, invf_ref, rope)
        c_scr[...] = c
        slo_scr[...] = s_lo
        shi_scr[...] = s_hi

    y = _dot(h_scr[...], w_ref[...])
    g = gq_ref[...]
    qn = _rms(y[:, :nope], g[:, :nope])
    pe = _rms(y[:, nope:], g[:, nope:], n=rope)
    pe = _apply_rope(pe, c_scr[...], slo_scr[...], shi_scr[...], rope)
    q_ref[:, :nope] = qn.astype(q_ref.dtype)
    q_ref[:, nope:] = pe.astype(q_ref.dtype)


def _mla_q(p, cq_blk, g_cq, w_uq_p, gq, pos, invf, *, heads, rank, nope, rope, tm):
    t = p.shape[0]
    hw = nope + LANES
    return pl.pallas_call(
        functools.partial(_mla_q_kernel, nope=nope, rope=rope),
        grid=(t // tm, heads),
        in_specs=[
            pl.BlockSpec((tm, rank), lambda i, j: (i, cq_blk)),
            pl.BlockSpec((1, rank), lambda i, j: (0, 0)),
            pl.BlockSpec((rank, hw), lambda i, j: (0, j)),
            pl.BlockSpec((1, hw), lambda i, j: (0, 0)),
            pl.BlockSpec((tm, 1), lambda i, j: (i, 0)),
            pl.BlockSpec((1, LANES), lambda i, j: (0, 0)),
        ],
        out_specs=pl.BlockSpec((tm, hw), lambda i, j: (i, j)),
        out_shape=jax.ShapeDtypeStruct((t, heads * hw), BF16),
        scratch_shapes=[
            pltpu.VMEM((tm, rank), BF16),
            pltpu.VMEM((tm, LANES), F32),
            pltpu.VMEM((tm, LANES), F32),
            pltpu.VMEM((tm, LANES), F32),
        ],
        compiler_params=_params(("parallel", "arbitrary")),
    )(p, g_cq, w_uq_p, gq, pos, invf)


def _mla_kv_kernel(ckv_ref, gckv_ref, w_ref, gkn_ref, small_ref, gkr_ref, pos_ref, invf_ref,
                   k_ref, v_ref, h_scr, kpe_scr, *, nope, rope):
    @pl.when(pl.program_id(1) == 0)
    def _():
        h_scr[...] = _rms(ckv_ref[...].astype(F32), gckv_ref[...]).astype(BF16)
        c, s_lo, s_hi = _rope_tables(pos_ref, invf_ref, rope)
        sm = small_ref[...]
        lane = lax.broadcasted_iota(jnp.int32, sm.shape, 1)
        pe = _rms(jnp.where(lane < rope, sm, 0.0), gkr_ref[...], n=rope)
        kpe_scr[...] = _apply_rope(pe, c, s_lo, s_hi, rope).astype(BF16)

    y = _dot(h_scr[...], w_ref[...])
    k_ref[:, :nope] = _rms(y[:, :nope], gkn_ref[...]).astype(k_ref.dtype)
    k_ref[:, nope:] = kpe_scr[...]
    v_ref[...] = y[:, nope:].astype(v_ref.dtype)


def _mla_kv(p, ckv_blk, g_ckv, w_ukv, g_kn, small, gkr, pos, invf, *, heads, rank, nope, rope, dv, tm):
    t = p.shape[0]
    kw = nope + LANES
    return pl.pallas_call(
        functools.partial(_mla_kv_kernel, nope=nope, rope=rope),
        grid=(t // tm, heads),
        in_specs=[
            pl.BlockSpec((tm, rank), lambda i, j: (i, ckv_blk)),
            pl.BlockSpec((1, rank), lambda i, j: (0, 0)),
            pl.BlockSpec((rank, nope + dv), lambda i, j: (0, j)),
            pl.BlockSpec((1, nope), lambda i, j: (0, 0)),
            pl.BlockSpec((tm, LANES), lambda i, j: (i, 0)),
            pl.BlockSpec((1, LANES), lambda i, j: (0, 0)),
            pl.BlockSpec((tm, 1), lambda i, j: (i, 0)),
            pl.BlockSpec((1, LANES), lambda i, j: (0, 0)),
        ],
        out_specs=[
            pl.BlockSpec((tm, kw), lambda i, j: (i, j)),
            pl.BlockSpec((tm, dv), lambda i, j: (i, j)),
        ],
        out_shape=[
            jax.ShapeDtypeStruct((t, heads * kw), BF16),
            jax.ShapeDtypeStruct((t, heads * dv), BF16),
        ],
        scratch_shapes=[pltpu.VMEM((tm, rank), BF16), pltpu.VMEM((tm, LANES), BF16)],
        compiler_params=_params(("parallel", "arbitrary")),
    )(p, g_ckv, w_ukv, g_kn, small, gkr, pos, invf)


def _attn_kernel(q_ref, k_ref, v_ref, o_ref, *, tq, tk):
    qi = pl.program_id(2)
    q = q_ref[...]
    dv = v_ref.shape[1]
    row = qi * tq + lax.broadcasted_iota(jnp.int32, (tq, tk), 0)
    col0 = lax.broadcasted_iota(jnp.int32, (tq, tk), 1)

    def body(kb, carry, masked):
        m, l, acc = carry
        k0 = pl.multiple_of(kb * tk, tk)
        s = _dot_nt(q, k_ref[pl.ds(k0, tk), :])
        if masked:
            s = jnp.where(col0 + k0 <= row, s, NEG_INF)
        m_new = jnp.maximum(m, jnp.max(s, axis=-1, keepdims=True))
        alpha = jnp.exp(m - m_new)
        pr = jnp.exp(s - m_new)
        l = alpha * l + jnp.sum(pr, axis=-1, keepdims=True)
        acc = alpha * acc + _dot(pr.astype(BF16), v_ref[pl.ds(k0, tk), :])
        return m_new, l, acc

    init = (jnp.full((tq, 1), NEG_INF, F32), jnp.zeros((tq, 1), F32), jnp.zeros((tq, dv), F32))
    n_below = (qi * tq) // tk
    nkb = ((qi + 1) * tq + tk - 1) // tk
    carry = lax.fori_loop(0, n_below, functools.partial(body, masked=False), init)
    _, l, acc = lax.fori_loop(n_below, nkb, functools.partial(body, masked=True), carry)
    o_ref[...] = (acc / l).astype(o_ref.dtype)


def _mla_attn(q, k, v, *, batch, seq, heads, dk, dv, tq, tk):
    t = q.shape[0]
    nq = seq // tq
    return pl.pallas_call(
        functools.partial(_attn_kernel, tq=tq, tk=tk),
        grid=(batch, heads, nq),
        in_specs=[
            pl.BlockSpec((tq, dk), lambda b, h, i: (b * nq + i, h)),
            pl.BlockSpec((seq, dk), lambda b, h, i: (b, h)),
            pl.BlockSpec((seq, dv), lambda b, h, i: (b, h)),
        ],
        out_specs=pl.BlockSpec((tq, dv), lambda b, h, i: (b * nq + i, h)),
        out_shape=jax.ShapeDtypeStruct((t, heads * dv), BF16),
        compiler_params=_params(("parallel", "parallel", "arbitrary")),
    )(q, k, v)


def _gla_kernel(q_ref, k_ref, v_ref, og_ref, small_ref, w2_ref, ba_ref, gon_ref, o_ref, st_scr,
                *, seq, dk, dv, chunk):
    c = chunk
    st_scr[...] = jnp.zeros_like(st_scr)
    r_i = lax.broadcasted_iota(jnp.int32, (c, c), 0)
    c_i = lax.broadcasted_iota(jnp.int32, (c, c), 1)
    tri = jnp.where(c_i <= r_i, 1.0, 0.0).astype(BF16)
    row_id = lax.broadcasted_iota(jnp.int32, (c, 1), 0)
    w2 = w2_ref[...]
    ba = ba_ref[...]
    gon = gon_ref[...]
    q_scale = dk ** -0.5

    def chunk_step(ci, carry):
        r0 = pl.multiple_of(ci * c, c)
        qc = q_ref[pl.ds(r0, c), :].astype(F32) * q_scale
        kc = k_ref[pl.ds(r0, c), :].astype(F32)
        vc = v_ref[pl.ds(r0, c), :]
        z = _dot(small_ref[pl.ds(r0, c), :].astype(BF16), w2) + ba
        la = jax.nn.log_sigmoid(z) * (1.0 / GLA_TAU)
        hi = la.astype(BF16)
        r1 = la - hi.astype(F32)
        mid = r1.astype(BF16)
        lo = (r1 - mid.astype(F32)).astype(BF16)
        b = _dot(tri, hi) + _dot(tri, mid) + _dot(tri, lo)

        st = st_scr[...]
        inter = _dot_nt((qc * jnp.exp(b)).astype(BF16), st.astype(BF16))

        att = jnp.zeros((c, c), F32)
        for j in range(c):
            lo_r = (j // SUBLANES) * SUBLANES
            d = b[lo_r:, :] - b[j:j + 1, :]
            head = jnp.where(row_id[lo_r:lo_r + SUBLANES, :] >= j, d[:SUBLANES, :], NEG_INF)
            e = jnp.exp(jnp.concatenate([head, d[SUBLANES:, :]], axis=0) if lo_r + SUBLANES < c else head)
            col = jnp.sum(qc[lo_r:, :] * kc[j:j + 1, :] * e, axis=-1, keepdims=True)
            if lo_r:
                col = jnp.concatenate([jnp.zeros((lo_r, 1), F32), col], axis=0)
            att = jnp.where(c_i == j, col, att)
        o = inter + _dot(att.astype(BF16), vc)

        b_last = b[c - 1:c, :]
        k_dec = (kc * jnp.exp(b_last - b)).astype(BF16)
        st_scr[...] = st * jnp.exp(b_last) + _dot_tn(vc, k_dec)

        og = og_ref[pl.ds(r0, c), :].astype(F32)
        out = _rms(o, gon) * (og * jax.nn.sigmoid(og))
        o_ref[pl.ds(r0, c), :] = out.astype(o_ref.dtype)
        return carry

    lax.fori_loop(0, seq // c, chunk_step, 0)


def _gla(p, small, w2p, b_a, g_on, *, batch, seq, heads, dk, dv, q_blk, k_blk, v_blk, og_blk):
    t = p.shape[0]
    return pl.pallas_call(
        functools.partial(_gla_kernel, seq=seq, dk=dk, dv=dv, chunk=GLA_CHUNK),
        grid=(batch, heads),
        in_specs=[
            pl.BlockSpec((seq, dk), lambda b, h: (b, q_blk + h)),
            pl.BlockSpec((seq, dk), lambda b, h: (b, k_blk + h)),
            pl.BlockSpec((seq, dv), lambda b, h: (b, v_blk + h)),
            pl.BlockSpec((seq, dv), lambda b, h: (b, og_blk + h)),
            pl.BlockSpec((seq, LANES), lambda b, h: (b, 0)),
            pl.BlockSpec((LANES, dk), lambda b, h: (0, h)),
            pl.BlockSpec((1, dk), lambda b, h: (0, h)),
            pl.BlockSpec((1, dv), lambda b, h: (0, 0)),
        ],
        out_specs=pl.BlockSpec((seq, dv), lambda b, h: (b, h)),
        out_shape=jax.ShapeDtypeStruct((t, heads * dv), BF16),
        scratch_shapes=[pltpu.VMEM((dv, dk), F32)],
        compiler_params=_params(("parallel", "parallel")),
    )(p, p, p, p, small, w2p, b_a, g_on)


def _out_proj_kernel(x_ref, ga_ref, gb_ref, oa_ref, ob_ref, bg_ref, w_ref, o_ref, m_scr):
    @pl.when(pl.program_id(1) == 0)
    def _():
        bg = bg_ref[...]
        sa = jax.nn.sigmoid(ga_ref[...].astype(F32) + bg[0:1, :])
        sb = jax.nn.sigmoid(gb_ref[...].astype(F32) + bg[1:2, :])
        m_scr[...] = (sa * oa_ref[...].astype(F32) + sb * ob_ref[...].astype(F32)).astype(BF16)

    o_ref[...] = x_ref[...] + _dot(m_scr[...], w_ref[...])


def _out_proj(x, p, o_mla, o_gla, b_gate, w_out, *, row0, ga_blk, gb_blk, tm, tn):
    t, d = p.shape[0], x.shape[1]
    assert row0 % tm == 0
    b0 = row0 // tm
    return pl.pallas_call(
        _out_proj_kernel,
        grid=(t // tm, d // tn),
        in_specs=[
            pl.BlockSpec((tm, tn), lambda i, j: (i + b0, j)),
            pl.BlockSpec((tm, d), lambda i, j: (i, ga_blk)),
            pl.BlockSpec((tm, d), lambda i, j: (i, gb_blk)),
            pl.BlockSpec((tm, d), lambda i, j: (i, 0)),
            pl.BlockSpec((tm, d), lambda i, j: (i, 0)),
            pl.BlockSpec((2, d), lambda i, j: (0, 0)),
            pl.BlockSpec((d, tn), lambda i, j: (0, j)),
        ],
        out_specs=pl.BlockSpec((tm, tn), lambda i, j: (i, j)),
        out_shape=jax.ShapeDtypeStruct((t, d), F32),
        scratch_shapes=[pltpu.VMEM((tm, d), BF16)],
        compiler_params=_params(("parallel", "arbitrary")),
    )(x, p, p, o_mla, o_gla, b_gate, w_out)


def _topk_rows(s, k, payload=None):
    n = s.shape[0]
    iota = lax.broadcasted_iota(jnp.int32, s.shape, 0).astype(F32)
    kiota = lax.broadcasted_iota(jnp.int32, (k, s.shape[1]), 0)
    vals = jnp.zeros((k, s.shape[1]), F32)
    picks = jnp.zeros((k, s.shape[1]), F32)
    for r in range(k):
        m = jnp.max(s, axis=0, keepdims=True)
        idx = jnp.min(jnp.where(s == m, iota, float(n)), axis=0, keepdims=True)
        hit = iota == idx
        if payload is None:
            pick = idx
        else:
            pick = jnp.sum(jnp.where(hit, payload, 0.0), axis=0, keepdims=True)
        vals = jnp.where(kiota == r, m, vals)
        picks = jnp.where(kiota == r, pick, picks)
        s = jnp.where(hit, NEG_INF, s)
    return vals, picks


def _peer_route_kernel(x_ref, g_ref, w_ref, sk_ref, h_ref, ids_ref, gates_ref, *, heads, nkeys, half, topk):
    h = _rms(x_ref[...], g_ref[...]).astype(BF16)
    h_ref[...] = h
    qf = _dot(h, w_ref[...])
    for hd in range(heads):
        tops = []
        for part in range(2):
            o = (hd * 2 + part) * half
            qh = qf[:, o:o + half].astype(BF16)
            keys = sk_ref[(hd * 2 + part) * nkeys:(hd * 2 + part + 1) * nkeys, :]
            tops.append(_topk_rows(_dot_nt(keys, qh), topk))
        (s1, i1), (s2, i2) = tops
        assert topk == 2 * SUBLANES
        hs = SUBLANES
        pair_s = [s1[0:1, :] + s2] + [s1[a:a + 1, :] + s2[:hs, :] for a in range(1, hs)] + [s1[hs:, :] + s2[0:1, :]]
        pair_id = ([i1[0:1, :] * float(nkeys) + i2]
                   + [i1[a:a + 1, :] * float(nkeys) + i2[:hs, :] for a in range(1, hs)]
                   + [i1[hs:, :] * float(nkeys) + i2[0:1, :]])
        best_s, best_id = _topk_rows(jnp.concatenate(pair_s, axis=0), topk,
                                     payload=jnp.concatenate(pair_id, axis=0))
        e = jnp.exp(best_s - best_s[0:1, :])
        gate = e / jnp.sum(e, axis=0, keepdims=True)
        ids_ref[hd * topk:(hd + 1) * topk, :] = best_id.astype(jnp.int32)
        gates_ref[hd * topk:(hd + 1) * topk, :] = gate


def _peer_route(x, g, w_pq, sk2d, *, heads, nkeys, half, topk, tm):
    t, d = x.shape
    dq = w_pq.shape[1]
    return pl.pallas_call(
        functools.partial(_peer_route_kernel, heads=heads, nkeys=nkeys, half=half, topk=topk),
        grid=(t // tm,),
        in_specs=[
            pl.BlockSpec((tm, d), lambda i: (i, 0)),
            pl.BlockSpec((1, d), lambda i: (0, 0)),
            pl.BlockSpec((d, dq), lambda i: (0, 0)),
            pl.BlockSpec((heads * 2 * nkeys, half), lambda i: (0, 0)),
        ],
        out_specs=[
            pl.BlockSpec((tm, d), lambda i: (i, 0)),
            pl.BlockSpec((heads * topk, tm), lambda i: (0, i)),
            pl.BlockSpec((heads * topk, tm), lambda i: (0, i)),
        ],
        out_shape=[
            jax.ShapeDtypeStruct((t, d), BF16),
            jax.ShapeDtypeStruct((heads * topk, t), jnp.int32),
            jax.ShapeDtypeStruct((heads * topk, t), F32),
        ],
        compiler_params=_params(("parallel",)),
    )(x, g, w_pq, sk2d)


def _pack_expert_table(u_emb, v_emb):
    n, d = u_emb.shape
    half = d // 2
    tm = 512
    assert n % tm == 0

    def pack(w):
        bits = lax.bitcast_convert_type(w.astype(BF16).astype(F32), jnp.int32)
        return ((bits[:, :half] >> 16) & jnp.int32(0xFFFF)) | (bits[:, half:] & jnp.int32(-65536))

    def pack_kernel(u_ref, v_ref, o_ref):
        o_ref[:, :half] = pack(u_ref[...])
        o_ref[:, half:] = pack(v_ref[...])

    return pl.pallas_call(
        pack_kernel,
        grid=(n // tm,),
        in_specs=[pl.BlockSpec((tm, d), lambda i: (i, 0)), pl.BlockSpec((tm, d), lambda i: (i, 0))],
        out_specs=pl.BlockSpec((tm, d), lambda i: (i, 0)),
        out_shape=jax.ShapeDtypeStruct((n, d), jnp.int32),
        compiler_params=_params(("parallel",)),
    )(u_emb, v_emb)


def _unpack_words(w):
    lo = lax.bitcast_convert_type(w << 16, F32)
    hi = lax.bitcast_convert_type(w & jnp.int32(-65536), F32)
    return lo, hi


def _expert_mix(words, h, gate, d):
    half = d // 2
    u_lo, u_hi = _unpack_words(words[:, :half])
    act = jnp.sum(u_lo * h[:, :half] + u_hi * h[:, half:], axis=-1, keepdims=True)
    w = gate * _gelu_exact(act)
    v_lo, v_hi = _unpack_words(words[:, half:])
    return jnp.concatenate([jnp.sum(v_lo * w, axis=0, keepdims=True),
                            jnp.sum(v_hi * w, axis=0, keepdims=True)], axis=1)


def _peer_mix_kernel(ids_hbm, uv_hbm, rows_ref, x_ref, h_ref, gates_ref, o_ref, ids_smem, buf, ids_sems, row_sems,
                     *, td, ts, picks, d, n_steps):
    step = pl.program_id(0)
    par = lax.rem(step, 2)
    ahead = MIX_SLOTS - 1
    assert td % MIX_SLOTS == 0 and td > ahead and (td + ts) % SUBLANES == 0

    def ids_copy(s, row):
        return pltpu.make_async_copy(ids_hbm.at[s], ids_smem.at[row], ids_sems.at[row])

    def issue(row, tok, slot):
        for j in range(picks):
            eid = ids_smem[row, tok * picks + j]
            pltpu.make_async_copy(uv_hbm.at[pl.ds(eid, 1), :], buf.at[slot, pl.ds(j, 1), :],
                                  row_sems.at[slot]).start()

    def wait_rows(slot):
        pltpu.make_async_copy(uv_hbm.at[pl.ds(0, picks), :], buf.at[slot], row_sems.at[slot]).wait()

    @pl.when(step == 0)
    def _():
        first = ids_copy(0, 0)
        first.start()
        first.wait()
        for u in range(ahead):
            issue(0, u, u)

    @pl.when(step + 1 < n_steps)
    def _():
        ids_copy(step + 1, 1 - par).start()

    loaded = {}

    def finish(pos, words):
        g, lane = divmod(pos, SUBLANES)
        if g not in loaded:
            loaded[g] = (gates_ref[g], h_ref[g * SUBLANES:(g + 1) * SUBLANES, :].astype(F32))
        gates, h8 = loaded[g]
        row = _expert_mix(words, h8[lane:lane + 1, :], gates[:, lane:lane + 1], d)
        o_ref[pos:pos + 1, :] = x_ref[pos:pos + 1, :] + row

    staged_done = 0
    for u in range(td):
        nxt = u + ahead
        if nxt < td:
            issue(par, nxt, nxt % MIX_SLOTS)
        else:
            @pl.when(step + 1 < n_steps)
            def _():
                if nxt == td:
                    ids_copy(step + 1, 1 - par).wait()
                issue(1 - par, nxt - td, nxt % MIX_SLOTS)

        staged_upto = (u + 1) * ts // td
        for k in range(staged_done, staged_upto):
            finish(td + k, rows_ref[k * picks:(k + 1) * picks, :])
        staged_done = staged_upto
        wait_rows(u % MIX_SLOTS)
        finish(u, buf[u % MIX_SLOTS])


def _peer_mix(ids_direct, uv, staged, x, h, gates3, *, td, ts):
    t, d = x.shape
    picks = gates3.shape[1]
    n_steps = t // (td + ts)
    assert staged.shape[0] == n_steps * ts * picks and ids_direct.shape == (n_steps, td * picks)
    return pl.pallas_call(
        functools.partial(_peer_mix_kernel, td=td, ts=ts, picks=picks, d=d, n_steps=n_steps),
        grid=(n_steps,),
        input_output_aliases={3: 0},
        in_specs=[
            pl.BlockSpec(memory_space=pl.ANY),
            pl.BlockSpec(memory_space=pl.ANY),
            pl.BlockSpec((ts * picks, d), lambda i: (i, 0)),
            pl.BlockSpec((td + ts, d), lambda i: (i, 0)),
            pl.BlockSpec((td + ts, d), lambda i: (i, 0)),
            pl.BlockSpec(((td + ts) // SUBLANES, picks, SUBLANES), lambda i: (i, 0, 0)),
        ],
        out_specs=pl.BlockSpec((td + ts, d), lambda i: (i, 0)),
        out_shape=jax.ShapeDtypeStruct((t, d), F32),
        scratch_shapes=[
            pltpu.SMEM((2, td * picks), jnp.int32),
            pltpu.VMEM((MIX_SLOTS, picks, d), jnp.int32),
            pltpu.SemaphoreType.DMA((2,)),
            pltpu.SemaphoreType.DMA((MIX_SLOTS,)),
        ],
        compiler_params=_params(("arbitrary",)),
    )(ids_direct, uv, staged, x, h, gates3)


def _sc_gather_rows(table, idx, *, chunk):
    n_rows = idx.shape[0]
    d = table.shape[1]
    workers = SC_CORES * SC_SUBCORES
    assert n_rows % (workers * 2 * chunk) == 0 and chunk % SUBLANES == 0 and chunk <= LANES
    rows_per_worker = n_rows // workers
    n_pairs = rows_per_worker // (2 * chunk)
    mesh = plsc.VectorSubcoreMesh(core_axis_name="c", subcore_axis_name="s")

    @functools.partial(
        pl.kernel, mesh=mesh,
        out_type=jax.ShapeDtypeStruct((n_rows, d), table.dtype),
        scratch_types=[
            pltpu.VMEM((chunk,), jnp.int32), pltpu.VMEM((chunk,), jnp.int32),
            pltpu.VMEM((chunk, d), table.dtype), pltpu.VMEM((chunk, d), table.dtype),
            pltpu.SemaphoreType.DMA, pltpu.SemaphoreType.DMA,
            pltpu.SemaphoreType.DMA, pltpu.SemaphoreType.DMA,
        ],
    )
    def gather_kernel(table_hbm, idx_hbm, out_hbm, idx0, idx1, rows0, rows1, gsem0, gsem1, wsem0, wsem1):
        idx_v, rows_v, gsem, wsem = (idx0, idx1), (rows0, rows1), (gsem0, gsem1), (wsem0, wsem1)
        worker = lax.axis_index("s") * SC_CORES + lax.axis_index("c")
        base = worker * rows_per_worker

        def out_rows(c):
            return pl.ds(pl.multiple_of(base + c * chunk, SUBLANES), chunk)

        def load_idx(slot, c):
            pltpu.sync_copy(idx_hbm.at[out_rows(c)], idx_v[slot])

        def gather(slot):
            return pltpu.make_async_copy(table_hbm.at[idx_v[slot]], rows_v[slot], gsem[slot])

        def writeout(slot, c):
            return pltpu.make_async_copy(rows_v[slot], out_hbm.at[out_rows(c)], wsem[slot])

        load_idx(0, 0)
        gather(0).start()

        @pl.loop(0, n_pairs)
        def _(p):
            c0 = 2 * p

            @pl.when(p > 0)
            def _():
                writeout(1, c0 - 1).wait()

            load_idx(1, c0 + 1)
            gather(1).start()
            gather(0).wait()
            writeout(0, c0).start()

            @pl.when(p + 1 < n_pairs)
            def _():
                load_idx(0, c0 + 2)
                writeout(0, c0).wait()
                gather(0).start()

            gather(1).wait()
            writeout(1, c0 + 1).start()

        writeout(0, 2 * n_pairs - 2).wait()
        writeout(1, 2 * n_pairs - 1).wait()

    return gather_kernel(table, idx)


def _pad_cols(w, width):
    return jnp.pad(w, ((0, 0), (0, width - w.shape[1])))


def _prepare_layer(g_norm_mix, w_in, b_gate, g_cq, w_uq, g_ckv, w_ukv, g_qn, g_qr, g_kn, g_kr,
                   w_a2, b_a, g_gla_out, w_out, g_norm_ffn, w_pq, sub_keys, u_emb, v_emb):
    d = w_in.shape[0]
    q_rank, kv_rank = g_cq.shape[0], g_ckv.shape[0]
    nope, rope = g_qn.shape[0], g_qr.shape[0]
    mla_heads = w_uq.shape[1] // (nope + rope)
    mla_v = w_ukv.shape[1] // mla_heads - nope
    gate_rank, gla_dk_all = w_a2.shape
    gla_dv = g_gla_out.shape[0]
    gla_heads = d // gla_dv
    gla_dk = gla_dk_all // gla_heads
    peer_heads, _, nkeys, half = sub_keys.shape
    assert nope == LANES and mla_v == LANES and rope <= LANES and rope % 2 == 0
    assert mla_heads * mla_v == d and gla_heads * gla_dv == d
    assert rope + gate_rank <= LANES and nkeys == LANES and half == LANES

    widths = (q_rank, kv_rank, rope, gla_dk_all, gla_dk_all, d, gate_rank, d, d, d)
    offs = [0]
    for wd in widths:
        offs.append(offs[-1] + wd)
    assert offs[-1] == w_in.shape[1]
    seg = lambda i: w_in[:, offs[i]:offs[i + 1]]
    w_main = jnp.concatenate([seg(5), seg(7), seg(8), seg(9), seg(3), seg(4), seg(0), seg(1)], axis=1).astype(BF16)
    w_small = _pad_cols(jnp.concatenate([seg(2), seg(6)], axis=1), LANES).astype(BF16)
    cq_off = 4 * d + 2 * gla_dk_all
    assert cq_off % q_rank == 0 and (cq_off + q_rank) % kv_rank == 0
    inv_freq = ROPE_THETA ** (-jnp.arange(0, rope, 2, dtype=F32) / rope)
    scale = (nope + rope) ** -0.5
    return dict(
        dims=dict(q_rank=q_rank, kv_rank=kv_rank, nope=nope, rope=rope, mla_heads=mla_heads, mla_v=mla_v,
                  gla_heads=gla_heads, gla_dk=gla_dk, gla_dv=gla_dv, peer_heads=peer_heads, nkeys=nkeys, half=half,
                  v_blk=0, og_blk=d // gla_dv, ga_blk=2, gb_blk=3, q_blk=4 * d // gla_dk,
                  k_blk=4 * d // gla_dk + gla_heads, cq_blk=cq_off // q_rank, ckv_blk=(cq_off + q_rank) // kv_rank),
        g_norm_mix=g_norm_mix[None, :], w_main=w_main, w_small=w_small,
        invf=_pad_cols(jnp.concatenate([inv_freq, inv_freq])[None, :], LANES),
        gq=_pad_cols(jnp.concatenate([g_qn, g_qr])[None, :] * scale, nope + LANES),
        gkr=_pad_cols(g_kr[None, :], LANES), g_cq=g_cq[None, :], g_ckv=g_ckv[None, :], g_kn=g_kn[None, :],
        w_uq=jnp.pad(w_uq.reshape(q_rank, mla_heads, nope + rope),
                     ((0, 0), (0, 0), (0, LANES - rope))).reshape(q_rank, -1).astype(BF16),
        w_ukv=w_ukv.astype(BF16),
        w2p=jnp.zeros((LANES, gla_dk_all), F32).at[rope:rope + gate_rank].set(w_a2).astype(BF16),
        b_a=b_a[None, :], g_on=g_gla_out[None, :], b_gate=b_gate, w_out=w_out.astype(BF16),
        g_norm_ffn=g_norm_ffn[None, :], w_pq=w_pq.astype(BF16),
        sk2d=sub_keys.reshape(peer_heads * 2 * nkeys, half).astype(BF16),
        uv=_pack_expert_table(u_emb, v_emb),
    )


def _mixers_and_route(x_all, row0, pos, w, *, batch, seq, tiles):
    t = batch * seq
    dm = w["dims"]
    p, small = _in_proj(x_all, w["g_norm_mix"], w["w_main"], w["w_small"], row0=row0, t=t,
                        tm=tiles["in_tm"], tn=tiles["in_tn"])
    q = _mla_q(p, dm["cq_blk"], w["g_cq"], w["w_uq"], w["gq"], pos, w["invf"], heads=dm["mla_heads"],
               rank=dm["q_rank"], nope=dm["nope"], rope=dm["rope"], tm=tiles["mla_tm"])
    k, v = _mla_kv(p, dm["ckv_blk"], w["g_ckv"], w["w_ukv"], w["g_kn"], small, w["gkr"], pos, w["invf"],
                   heads=dm["mla_heads"], rank=dm["kv_rank"], nope=dm["nope"], rope=dm["rope"], dv=dm["mla_v"],
                   tm=tiles["mla_tm"])
    o_mla = _mla_attn(q, k, v, batch=batch, seq=seq, heads=dm["mla_heads"], dk=dm["nope"] + LANES, dv=dm["mla_v"],
                      tq=tiles["attn_tq"], tk=tiles["attn_tk"])
    o_gla = _gla(p, small, w["w2p"], w["b_a"], w["g_on"], batch=batch, seq=seq, heads=dm["gla_heads"],
                 dk=dm["gla_dk"], dv=dm["gla_dv"], q_blk=dm["q_blk"], k_blk=dm["k_blk"], v_blk=dm["v_blk"],
                 og_blk=dm["og_blk"])
    x2 = _out_proj(x_all, p, o_mla, o_gla, w["b_gate"], w["w_out"], row0=row0, ga_blk=dm["ga_blk"],
                   gb_blk=dm["gb_blk"], tm=tiles["out_tm"], tn=tiles["out_tn"])
    h2, ids_t, gates_t = _peer_route(x2, w["g_norm_ffn"], w["w_pq"], w["sk2d"], heads=dm["peer_heads"],
                                     nkeys=dm["nkeys"], half=dm["half"], topk=PEER_TOPK, tm=tiles["route_tm"])
    picks = dm["peer_heads"] * PEER_TOPK
    gates3 = gates_t.reshape(picks, t // SUBLANES, SUBLANES).transpose(1, 0, 2)
    return x2, h2, ids_t.T, gates3


def _layer(x2, pos, w, *, batch, seq, tiles):
    t, d = x2.shape
    splits = tiles["mix_split"]
    groups = len(splits)
    tg, bg = t // groups, batch // groups
    routed = [_mixers_and_route(x2, g * tg, pos[g * tg:(g + 1) * tg], w, batch=bg, seq=seq, tiles=tiles)
              for g in range(groups)]
    out = []
    for (td, ts), (xg, hg, ids_tok, gates3) in zip(splits, routed):
        picks = ids_tok.shape[1]
        ids3 = ids_tok.reshape(tg // (td + ts), td + ts, picks)
        staged = _sc_gather_rows(w["uv"], ids3[:, td:].reshape(-1), chunk=tiles["sc_chunk"])
        out.append(_peer_mix(ids3[:, :td].reshape(-1, td * picks), w["uv"], staged, xg, hg, gates3, td=td, ts=ts))
    return jnp.concatenate(out, axis=0)


_TILES = dict(in_tm=2048, in_tn=512, mla_tm=2048, attn_tq=1024, attn_tk=1024,
              out_tm=1024, out_tn=512, route_tm=256, mix_split=((4, 12),) * 4 + ((12, 20),) + ((20, 12),) * 2 + ((4, 12),), sc_chunk=16)


def kernel(x, positions, g_norm_mix, w_in, b_gate, g_cq, w_uq, g_ckv, w_ukv, g_qn, g_qr, g_kn, g_kr,
           w_a2, b_a, g_gla_out, w_out, g_norm_ffn, w_pq, sub_keys, u_emb, v_emb, tiles=None):
    tiles = _TILES if tiles is None else tiles
    batch, seq, d = x.shape
    x2 = x.reshape(batch * seq, d)
    pos = positions.reshape(batch * seq, 1)
    for l in range(g_norm_mix.shape[0]):
        w = _prepare_layer(g_norm_mix[l], w_in[l], b_gate[l], g_cq[l], w_uq[l], g_ckv[l], w_ukv[l], g_qn[l],
                           g_qr[l], g_kn[l], g_kr[l], w_a2[l], b_a[l], g_gla_out[l], w_out[l], g_norm_ffn[l],
                           w_pq[l], sub_keys[l], u_emb[l], v_emb[l])
        x2 = _layer(x2, pos, w, batch=batch, seq=seq, tiles=tiles)
    return x2.reshape(batch, seq, d)
```

```python
import functools

import jax
import jax.numpy as jnp
from jax import lax
from jax.experimental import pallas as pl
from jax.experimental.pallas import tpu as pltpu
from jax.experimental.pallas import tpu_sc as plsc

EPS = 1e-6
ROPE_THETA = 10000.0
GLA_TAU = 16.0
GLA_CHUNK = 64
PEER_TOPK = 16

LANES = 128
SUBLANES = 8
VMEM_LIMIT_BYTES = 56 * 1024 * 1024
MIX_SLOTS = 4
SC_CORES = 2
SC_SUBCORES = 16

F32 = jnp.float32
BF16 = jnp.bfloat16
NEG_INF = float("-inf")


def _params(semantics):
    return pltpu.CompilerParams(dimension_semantics=semantics, vmem_limit_bytes=VMEM_LIMIT_BYTES)


def _rms(x, gain, n=None):
    ss = jnp.sum(x * x, axis=-1, keepdims=True)
    n = x.shape[-1] if n is None else n
    return x * lax.rsqrt(ss * (1.0 / n) + EPS) * gain


def _gelu_exact(x):
    return 0.5 * x * (1.0 + lax.erf(x * (0.5 ** 0.5)))


def _dot(a, b):
    return jnp.dot(a, b, preferred_element_type=F32)


def _dot_nt(a, b):
    return lax.dot_general(a, b, (((1,), (1,)), ((), ())), preferred_element_type=F32)


def _dot_tn(a, b):
    return lax.dot_general(a, b, (((0,), (0,)), ((), ())), preferred_element_type=F32)


def _in_proj_kernel(x_ref, g_ref, w_ref, ws_ref, p_ref, ps_ref, h_scr):
    @pl.when(pl.program_id(1) == 0)
    def _():
        h = _rms(x_ref[...], g_ref[...]).astype(BF16)
        h_scr[...] = h
        ps_ref[...] = _dot(h, ws_ref[...])

    p_ref[...] = _dot(h_scr[...], w_ref[...]).astype(p_ref.dtype)


def _in_proj(x, g, w_main, w_small, *, row0, t, tm, tn):
    d = x.shape[1]
    n = w_main.shape[1]
    assert row0 % tm == 0
    b0 = row0 // tm
    x_mode = pl.Buffered(1) if t == tm else None
    return pl.pallas_call(
        _in_proj_kernel,
        grid=(t // tm, n // tn),
        in_specs=[
            pl.BlockSpec((tm, d), lambda i, j: (i + b0, 0), pipeline_mode=x_mode),
            pl.BlockSpec((1, d), lambda i, j: (0, 0)),
            pl.BlockSpec((d, tn), lambda i, j: (0, j)),
            pl.BlockSpec((d, LANES), lambda i, j: (0, 0)),
        ],
        out_specs=[
            pl.BlockSpec((tm, tn), lambda i, j: (i, j)),
            pl.BlockSpec((tm, LANES), lambda i, j: (i, 0)),
        ],
        out_shape=[
            jax.ShapeDtypeStruct((t, n), BF16),
            jax.ShapeDtypeStruct((t, LANES), F32),
        ],
        scratch_shapes=[pltpu.VMEM((tm, d), BF16)],
        compiler_params=_params(("parallel", "arbitrary")),
    )(x, g, w_main, w_small)


def _rope_tables(pos_ref, invf_ref, rope):
    ang = pos_ref[...].astype(F32) * invf_ref[...]
    cos, sin = jnp.cos(ang), jnp.sin(ang)
    lane = lax.broadcasted_iota(jnp.int32, ang.shape, 1)
    half = rope // 2
    c = jnp.where(lane < rope, cos, 0.0)
    s_lo = jnp.where(lane < half, -sin, 0.0)
    s_hi = jnp.where(lane < half, 0.0, jnp.where(lane < rope, sin, 0.0))
    return c, s_lo, s_hi


def _apply_rope(pe, c, s_lo, s_hi, rope):
    half = rope // 2
    from_hi = pltpu.roll(pe, LANES - half, 1)
    from_lo = pltpu.roll(pe, half, 1)
    return pe * c + from_hi * s_lo + from_lo * s_hi


def _mla_q_kernel(cq_ref, gcq_ref, w_ref, gq_ref, pos_ref, invf_ref, q_ref,
                  h_scr, c_scr, slo_scr, shi_scr, *, nope, rope):
    @pl.when(pl.program_id(1) == 0)
    def _():
        h_scr[...] = _rms(cq_ref[...].astype(F32), gcq_ref[...]).astype(BF16)
        c, s_lo, s_hi = _rope_tables(pos_ref, invf_ref, rope)
        c_scr[...] = c
        slo_scr[...] = s_lo
        shi_scr[...] = s_hi

    y = _dot(h_scr[...], w_ref[...])
    g = gq_ref[...]
    qn = _rms(y[:, :nope], g[:, :nope])
    pe = _rms(y[:, nope:], g[:, nope:], n=rope)
    pe = _apply_rope(pe, c_scr[...], slo_scr[...], shi_scr[...], rope)
    q_ref[:, :nope] = qn.astype(q_ref.dtype)
    q_ref[:, nope:] = pe.astype(q_ref.dtype)


def _mla_q(p, cq_blk, g_cq, w_uq_p, gq, pos, invf, *, heads, rank, nope, rope, tm):
    t = p.shape[0]
    hw = nope + LANES
    return pl.pallas_call(
        functools.partial(_mla_q_kernel, nope=nope, rope=rope),
        grid=(t // tm, heads),
        in_specs=[
            pl.BlockSpec((tm, rank), lambda i, j: (i, cq_blk)),
            pl.BlockSpec((1, rank), lambda i, j: (0, 0)),
            pl.BlockSpec((rank, hw), lambda i, j: (0, j)),
            pl.BlockSpec((1, hw), lambda i, j: (0, 0)),
            pl.BlockSpec((tm, 1), lambda i, j: (i, 0)),
            pl.BlockSpec((1, LANES), lambda i, j: (0, 0)),
        ],
        out_specs=pl.BlockSpec((tm, hw), lambda i, j: (i, j)),
        out_shape=jax.ShapeDtypeStruct((t, heads * hw), BF16),
        scratch_shapes=[
            pltpu.VMEM((tm, rank), BF16),
            pltpu.VMEM((tm, LANES), F32),
            pltpu.VMEM((tm, LANES), F32),
            pltpu.VMEM((tm, LANES), F32),
        ],
        compiler_params=_params(("parallel", "arbitrary")),
    )(p, g_cq, w_uq_p, gq, pos, invf)


def _mla_kv_kernel(ckv_ref, gckv_ref, w_ref, gkn_ref, small_ref, gkr_ref, pos_ref, invf_ref,
                   k_ref, v_ref, h_scr, kpe_scr, *, nope, rope):
    @pl.when(pl.program_id(1) == 0)
    def _():
        h_scr[...] = _rms(ckv_ref[...].astype(F32), gckv_ref[...]).astype(BF16)
        c, s_lo, s_hi = _rope_tables(pos_ref, invf_ref, rope)
        sm = small_ref[...]
        lane = lax.broadcasted_iota(jnp.int32, sm.shape, 1)
        pe = _rms(jnp.where(lane < rope, sm, 0.0), gkr_ref[...], n=rope)
        kpe_scr[...] = _apply_rope(pe, c, s_lo, s_hi, rope).astype(BF16)

    y = _dot(h_scr[...], w_ref[...])
    k_ref[:, :nope] = _rms(y[:, :nope], gkn_ref[...]).astype(k_ref.dtype)
    k_ref[:, nope:] = kpe_scr[...]
    v_ref[...] = y[:, nope:].astype(v_ref.dtype)


def _mla_kv(p, ckv_blk, g_ckv, w_ukv, g_kn, small, gkr, pos, invf, *, heads, rank, nope, rope, dv, tm):
    t = p.shape[0]
    kw = nope + LANES
    return pl.pallas_call(
        functools.partial(_mla_kv_kernel, nope=nope, rope=rope),
        grid=(t // tm, heads),
        in_specs=[
            pl.BlockSpec((tm, rank), lambda i, j: (i, ckv_blk)),
            pl.BlockSpec((1, rank), lambda i, j: (0, 0)),
            pl.BlockSpec((rank, nope + dv), lambda i, j: (0, j)),
            pl.BlockSpec((1, nope), lambda i, j: (0, 0)),
            pl.BlockSpec((tm, LANES), lambda i, j: (i, 0)),
            pl.BlockSpec((1, LANES), lambda i, j: (0, 0)),
            pl.BlockSpec((tm, 1), lambda i, j: (i, 0)),
            pl.BlockSpec((1, LANES), lambda i, j: (0, 0)),
        ],
        out_specs=[
            pl.BlockSpec((tm, kw), lambda i, j: (i, j)),
            pl.BlockSpec((tm, dv), lambda i, j: (i, j)),
        ],
        out_shape=[
            jax.ShapeDtypeStruct((t, heads * kw), BF16),
            jax.ShapeDtypeStruct((t, heads * dv), BF16),
        ],
        scratch_shapes=[pltpu.VMEM((tm, rank), BF16), pltpu.VMEM((tm, LANES), BF16)],
        compiler_params=_params(("parallel", "arbitrary")),
    )(p, g_ckv, w_ukv, g_kn, small, gkr, pos, invf)


def _attn_kernel(q_ref, k_ref, v_ref, o_ref, *, tq, tk):
    qi = pl.program_id(2)
    q = q_ref[...]
    dv = v_ref.shape[1]
    row = qi * tq + lax.broadcasted_iota(jnp.int32, (tq, tk), 0)
    col0 = lax.broadcasted_iota(jnp.int32, (tq, tk), 1)

    def body(kb, carry, masked):
        m, l, acc = carry
        k0 = pl.multiple_of(kb * tk, tk)
        s = _dot_nt(q, k_ref[pl.ds(k0, tk), :])
        if masked:
            s = jnp.where(col0 + k0 <= row, s, NEG_INF)
        m_new = jnp.maximum(m, jnp.max(s, axis=-1, keepdims=True))
        alpha = jnp.exp(m - m_new)
        pr = jnp.exp(s - m_new)
        l = alpha * l + jnp.sum(pr, axis=-1, keepdims=True)
        acc = alpha * acc + _dot(pr.astype(BF16), v_ref[pl.ds(k0, tk), :])
        return m_new, l, acc

    init = (jnp.full((tq, 1), NEG_INF, F32), jnp.zeros((tq, 1), F32), jnp.zeros((tq, dv), F32))
    n_below = (qi * tq) // tk
    nkb = ((qi + 1) * tq + tk - 1) // tk
    carry = lax.fori_loop(0, n_below, functools.partial(body, masked=False), init)
    _, l, acc = lax.fori_loop(n_below, nkb, functools.partial(body, masked=True), carry)
    o_ref[...] = (acc / l).astype(o_ref.dtype)


def _mla_attn(q, k, v, *, batch, seq, heads, dk, dv, tq, tk):
    t = q.shape[0]
    nq = seq // tq
    return pl.pallas_call(
        functools.partial(_attn_kernel, tq=tq, tk=tk),
        grid=(batch, heads, nq),
        in_specs=[
            pl.BlockSpec((tq, dk), lambda b, h, i: (b * nq + i, h)),
            pl.BlockSpec((seq, dk), lambda b, h, i: (b, h)),
            pl.BlockSpec((seq, dv), lambda b, h, i: (b, h)),
        ],
        out_specs=pl.BlockSpec((tq, dv), lambda b, h, i: (b * nq + i, h)),
        out_shape=jax.ShapeDtypeStruct((t, heads * dv), BF16),
        compiler_params=_params(("parallel", "parallel", "arbitrary")),
    )(q, k, v)


def _gla_kernel(q_ref, k_ref, v_ref, og_ref, small_ref, w2_ref, ba_ref, gon_ref, o_ref, st_scr,
                *, seq, dk, dv, chunk):
    c = chunk
    st_scr[...] = jnp.zeros_like(st_scr)
    r_i = lax.broadcasted_iota(jnp.int32, (c, c), 0)
    c_i = lax.broadcasted_iota(jnp.int32, (c, c), 1)
    tri = jnp.where(c_i <= r_i, 1.0, 0.0).astype(BF16)
    row_id = lax.broadcasted_iota(jnp.int32, (c, 1), 0)
    w2 = w2_ref[...]
    ba = ba_ref[...]
    gon = gon_ref[...]
    q_scale = dk ** -0.5

    def chunk_step(ci, carry):
        r0 = pl.multiple_of(ci * c, c)
        qc = q_ref[pl.ds(r0, c), :].astype(F32) * q_scale
        kc = k_ref[pl.ds(r0, c), :].astype(F32)
        vc = v_ref[pl.ds(r0, c), :]
        z = _dot(small_ref[pl.ds(r0, c), :].astype(BF16), w2) + ba
        la = jax.nn.log_sigmoid(z) * (1.0 / GLA_TAU)
        hi = la.astype(BF16)
        r1 = la - hi.astype(F32)
        mid = r1.astype(BF16)
        lo = (r1 - mid.astype(F32)).astype(BF16)
        b = _dot(tri, hi) + _dot(tri, mid) + _dot(tri, lo)

        st = st_scr[...]
        inter = _dot_nt((qc * jnp.exp(b)).astype(BF16), st.astype(BF16))

        att = jnp.zeros((c, c), F32)
        for j in range(c):
            lo_r = (j // SUBLANES) * SUBLANES
            d = b[lo_r:, :] - b[j:j + 1, :]
            head = jnp.where(row_id[lo_r:lo_r + SUBLANES, :] >= j, d[:SUBLANES, :], NEG_INF)
            e = jnp.exp(jnp.concatenate([head, d[SUBLANES:, :]], axis=0) if lo_r + SUBLANES < c else head)
            col = jnp.sum(qc[lo_r:, :] * kc[j:j + 1, :] * e, axis=-1, keepdims=True)
            if lo_r:
                col = jnp.concatenate([jnp.zeros((lo_r, 1), F32), col], axis=0)
            att = jnp.where(c_i == j, col, att)
        o = inter + _dot(att.astype(BF16), vc)

        b_last = b[c - 1:c, :]
        k_dec = (kc * jnp.exp(b_last - b)).astype(BF16)
        st_scr[...] = st * jnp.exp(b_last) + _dot_tn(vc, k_dec)

        og = og_ref[pl.ds(r0, c), :].astype(F32)
        out = _rms(o, gon) * (og * jax.nn.sigmoid(og))
        o_ref[pl.ds(r0, c), :] = out.astype(o_ref.dtype)
        return carry

    lax.fori_loop(0, seq // c, chunk_step, 0)


def _gla(p, small, w2p, b_a, g_on, *, batch, seq, heads, dk, dv, q_blk, k_blk, v_blk, og_blk):
    t = p.shape[0]
    return pl.pallas_call(
        functools.partial(_gla_kernel, seq=seq, dk=dk, dv=dv, chunk=GLA_CHUNK),
        grid=(batch, heads),
        in_specs=[
            pl.BlockSpec((seq, dk), lambda b, h: (b, q_blk + h)),
            pl.BlockSpec((seq, dk), lambda b, h: (b, k_blk + h)),
            pl.BlockSpec((seq, dv), lambda b, h: (b, v_blk + h)),
            pl.BlockSpec((seq, dv), lambda b, h: (b, og_blk + h)),
            pl.BlockSpec((seq, LANES), lambda b, h: (b, 0)),
            pl.BlockSpec((LANES, dk), lambda b, h: (0, h)),
            pl.BlockSpec((1, dk), lambda b, h: (0, h)),
            pl.BlockSpec((1, dv), lambda b, h: (0, 0)),
        ],
        out_specs=pl.BlockSpec((seq, dv), lambda b, h: (b, h)),
        out_shape=jax.ShapeDtypeStruct((t, heads * dv), BF16),
        scratch_shapes=[pltpu.VMEM((dv, dk), F32)],
        compiler_params=_params(("parallel", "parallel")),
    )(p, p, p, p, small, w2p, b_a, g_on)


def _out_proj_kernel(x_ref, ga_ref, gb_ref, oa_ref, ob_ref, bg_ref, w_ref, o_ref, m_scr):
    @pl.when(pl.program_id(1) == 0)
    def _():
        bg = bg_ref[...]
        sa = jax.nn.sigmoid(ga_ref[...].astype(F32) + bg[0:1, :])
        sb = jax.nn.sigmoid(gb_ref[...].astype(F32) + bg[1:2, :])
        m_scr[...] = (sa * oa_ref[...].astype(F32) + sb * ob_ref[...].astype(F32)).astype(BF16)

    o_ref[...] = x_ref[...] + _dot(m_scr[...], w_ref[...])


def _out_proj(x, p, o_mla, o_gla, b_gate, w_out, *, row0, ga_blk, gb_blk, tm, tn):
    t, d = p.shape[0], x.shape[1]
    assert row0 % tm == 0
    b0 = row0 // tm
    return pl.pallas_call(
        _out_proj_kernel,
        grid=(t // tm, d // tn),
        in_specs=[
            pl.BlockSpec((tm, tn), lambda i, j: (i + b0, j)),
            pl.BlockSpec((tm, d), lambda i, j: (i, ga_blk)),
            pl.BlockSpec((tm, d), lambda i, j: (i, gb_blk)),
            pl.BlockSpec((tm, d), lambda i, j: (i, 0)),
            pl.BlockSpec((tm, d), lambda i, j: (i, 0)),
            pl.BlockSpec((2, d), lambda i, j: (0, 0)),
            pl.BlockSpec((d, tn), lambda i, j: (0, j)),
        ],
        out_specs=pl.BlockSpec((tm, tn), lambda i, j: (i, j)),
        out_shape=jax.ShapeDtypeStruct((t, d), F32),
        scratch_shapes=[pltpu.VMEM((tm, d), BF16)],
        compiler_params=_params(("parallel", "arbitrary")),
    )(x, p, p, o_mla, o_gla, b_gate, w_out)


def _topk_rows(s, k, payload=None):
    n = s.shape[0]
    iota = lax.broadcasted_iota(jnp.int32, s.shape, 0).astype(F32)
    kiota = lax.broadcasted_iota(jnp.int32, (k, s.shape[1]), 0)
    vals = jnp.zeros((k, s.shape[1]), F32)
    picks = jnp.zeros((k, s.shape[1]), F32)
    for r in range(k):
        m = jnp.max(s, axis=0, keepdims=True)
        idx = jnp.min(jnp.where(s == m, iota, float(n)), axis=0, keepdims=True)
        hit = iota == idx
        if payload is None:
            pick = idx
        else:
            pick = jnp.sum(jnp.where(hit, payload, 0.0), axis=0, keepdims=True)
        vals = jnp.where(kiota == r, m, vals)
        picks = jnp.where(kiota == r, pick, picks)
        s = jnp.where(hit, NEG_INF, s)
    return vals, picks


def _peer_route_kernel(x_ref, g_ref, w_ref, sk_ref, h_ref, ids_ref, gates_ref, *, heads, nkeys, half, topk):
    h = _rms(x_ref[...], g_ref[...]).astype(BF16)
    h_ref[...] = h
    qf = _dot(h, w_ref[...])
    for hd in range(heads):
        tops = []
        for part in range(2):
            o = (hd * 2 + part) * half
            qh = qf[:, o:o + half].astype(BF16)
            keys = sk_ref[(hd * 2 + part) * nkeys:(hd * 2 + part + 1) * nkeys, :]
            tops.append(_topk_rows(_dot_nt(keys, qh), topk))
        (s1, i1), (s2, i2) = tops
        assert topk == 2 * SUBLANES
        hs = SUBLANES
        pair_s = [s1[0:1, :] + s2] + [s1[a:a + 1, :] + s2[:hs, :] for a in range(1, hs)] + [s1[hs:, :] + s2[0:1, :]]
        pair_id = ([i1[0:1, :] * float(nkeys) + i2]
                   + [i1[a:a + 1, :] * float(nkeys) + i2[:hs, :] for a in range(1, hs)]
                   + [i1[hs:, :] * float(nkeys) + i2[0:1, :]])
        best_s, best_id = _topk_rows(jnp.concatenate(pair_s, axis=0), topk,
                                     payload=jnp.concatenate(pair_id, axis=0))
        e = jnp.exp(best_s - best_s[0:1, :])
        gate = e / jnp.sum(e, axis=0, keepdims=True)
        ids_ref[hd * topk:(hd + 1) * topk, :] = best_id.astype(jnp.int32)
        gates_ref[hd * topk:(hd + 1) * topk, :] = gate


def _peer_route(x, g, w_pq, sk2d, *, heads, nkeys, half, topk, tm):
    t, d = x.shape
    dq = w_pq.shape[1]
    return pl.pallas_call(
        functools.partial(_peer_route_kernel, heads=heads, nkeys=nkeys, half=half, topk=topk),
        grid=(t // tm,),
        in_specs=[
            pl.BlockSpec((tm, d), lambda i: (i, 0)),
            pl.BlockSpec((1, d), lambda i: (0, 0)),
            pl.BlockSpec((d, dq), lambda i: (0, 0)),
            pl.BlockSpec((heads * 2 * nkeys, half), lambda i: (0, 0)),
        ],
        out_specs=[
            pl.BlockSpec((tm, d), lambda i: (i, 0)),
            pl.BlockSpec((heads * topk, tm), lambda i: (0, i)),
            pl.BlockSpec((heads * topk, tm), lambda i: (0, i)),
        ],
        out_shape=[
            jax.ShapeDtypeStruct((t, d), BF16),
            jax.ShapeDtypeStruct((heads * topk, t), jnp.int32),
            jax.ShapeDtypeStruct((heads * topk, t), F32),
        ],
        compiler_params=_params(("parallel",)),
    )(x, g, w_pq, sk2d)


def _pack_expert_table(u_emb, v_emb):
    n, d = u_emb.shape
    half = d // 2
    tm = 512
    assert n % tm == 0

    def pack(w):
        bits = lax.bitcast_convert_type(w.astype(BF16).astype(F32), jnp.int32)
        return ((bits[:, :half] >> 16) & jnp.int32(0xFFFF)) | (bits[:, half:] & jnp.int32(-65536))

    def pack_kernel(u_ref, v_ref, o_ref):
        o_ref[:, :half] = pack(u_ref[...])
        o_ref[:, half:] = pack(v_ref[...])

    return pl.pallas_call(
        pack_kernel,
        grid=(n // tm,),
        in_specs=[pl.BlockSpec((tm, d), lambda i: (i, 0)), pl.BlockSpec((tm, d), lambda i: (i, 0))],
        out_specs=pl.BlockSpec((tm, d), lambda i: (i, 0)),
        out_shape=jax.ShapeDtypeStruct((n, d), jnp.int32),
        compiler_params=_params(("parallel",)),
    )(u_emb, v_emb)


def _unpack_words(w):
    lo = lax.bitcast_convert_type(w << 16, F32)
    hi = lax.bitcast_convert_type(w & jnp.int32(-65536), F32)
    return lo, hi


def _expert_mix(words, h, gate, d):
    half = d // 2
    u_lo, u_hi = _unpack_words(words[:, :half])
    act = jnp.sum(u_lo * h[:, :half] + u_hi * h[:, half:], axis=-1, keepdims=True)
    w = gate * _gelu_exact(act)
    v_lo, v_hi = _unpack_words(words[:, half:])
    return jnp.concatenate([jnp.sum(v_lo * w, axis=0, keepdims=True),
                            jnp.sum(v_hi * w, axis=0, keepdims=True)], axis=1)


def _peer_mix_kernel(ids_hbm, uv_hbm, rows_ref, x_ref, h_ref, gates_ref, o_ref, ids_smem, buf, ids_sems, row_sems,
                     *, td, ts, picks, d, n_steps):
    step = pl.program_id(0)
    par = lax.rem(step, 2)
    ahead = MIX_SLOTS - 1
    assert td % MIX_SLOTS == 0 and td > ahead and (td + ts) % SUBLANES == 0

    def ids_copy(s, row):
        return pltpu.make_async_copy(ids_hbm.at[s], ids_smem.at[row], ids_sems.at[row])

    def issue(row, tok, slot):
        for j in range(picks):
            eid = ids_smem[row, tok * picks + j]
            pltpu.make_async_copy(uv_hbm.at[pl.ds(eid, 1), :], buf.at[slot, pl.ds(j, 1), :],
                                  row_sems.at[slot]).start()

    def wait_rows(slot):
        pltpu.make_async_copy(uv_hbm.at[pl.ds(0, picks), :], buf.at[slot], row_sems.at[slot]).wait()

    @pl.when(step == 0)
    def _():
        first = ids_copy(0, 0)
        first.start()
        first.wait()
        for u in range(ahead):
            issue(0, u, u)

    @pl.when(step + 1 < n_steps)
    def _():
        ids_copy(step + 1, 1 - par).start()

    loaded = {}

    def finish(pos, words):
        g, lane = divmod(pos, SUBLANES)
        if g not in loaded:
            loaded[g] = (gates_ref[g], h_ref[g * SUBLANES:(g + 1) * SUBLANES, :].astype(F32))
        gates, h8 = loaded[g]
        row = _expert_mix(words, h8[lane:lane + 1, :], gates[:, lane:lane + 1], d)
        o_ref[pos:pos + 1, :] = x_ref[pos:pos + 1, :] + row

    staged_done = 0
    for u in range(td):
        nxt = u + ahead
        if nxt < td:
            issue(par, nxt, nxt % MIX_SLOTS)
        else:
            @pl.when(step + 1 < n_steps)
            def _():
                if nxt == td:
                    ids_copy(step + 1, 1 - par).wait()
                issue(1 - par, nxt - td, nxt % MIX_SLOTS)

        staged_upto = (u + 1) * ts // td
        for k in range(staged_done, staged_upto):
            finish(td + k, rows_ref[k * picks:(k + 1) * picks, :])
        staged_done = staged_upto
        wait_rows(u % MIX_SLOTS)
        finish(u, buf[u % MIX_SLOTS])


def _peer_mix(ids_direct, uv, staged, x, h, gates3, *, td, ts):
    t, d = x.shape
    picks = gates3.shape[1]
    n_steps = t // (td + ts)
    assert staged.shape[0] == n_steps * ts * picks and ids_direct.shape == (n_steps, td * picks)
    return pl.pallas_call(
        functools.partial(_peer_mix_kernel, td=td, ts=ts, picks=picks, d=d, n_steps=n_steps),
        grid=(n_steps,),
        input_output_aliases={3: 0},
        in_specs=[
            pl.BlockSpec(memory_space=pl.ANY),
            pl.BlockSpec(memory_space=pl.ANY),
            pl.BlockSpec((ts * picks, d), lambda i: (i, 0)),
            pl.BlockSpec((td + ts, d), lambda i: (i, 0)),
            pl.BlockSpec((td + ts, d), lambda i: (i, 0)),
            pl.BlockSpec(((td + ts) // SUBLANES, picks, SUBLANES), lambda i: (i, 0, 0)),
        ],
        out_specs=pl.BlockSpec((td + ts, d), lambda i: (i, 0)),
        out_shape=jax.ShapeDtypeStruct((t, d), F32),
        scratch_shapes=[
            pltpu.SMEM((2, td * picks), jnp.int32),
            pltpu.VMEM((MIX_SLOTS, picks, d), jnp.int32),
            pltpu.SemaphoreType.DMA((2,)),
            pltpu.SemaphoreType.DMA((MIX_SLOTS,)),
        ],
        compiler_params=_params(("arbitrary",)),
    )(ids_direct, uv, staged, x, h, gates3)


def _sc_gather_rows(table, idx, *, chunk):
    n_rows = idx.shape[0]
    d = table.shape[1]
    workers = SC_CORES * SC_SUBCORES
    assert n_rows % (workers * 2 * chunk) == 0 and chunk % SUBLANES == 0 and chunk <= LANES
    rows_per_worker = n_rows // workers
    n_pairs = rows_per_worker // (2 * chunk)
    mesh = plsc.VectorSubcoreMesh(core_axis_name="c", subcore_axis_name="s")

    @functools.partial(
        pl.kernel, mesh=mesh,
        out_type=jax.ShapeDtypeStruct((n_rows, d), table.dtype),
        scratch_types=[
            pltpu.VMEM((chunk,), jnp.int32), pltpu.VMEM((chunk,), jnp.int32),
            pltpu.VMEM((chunk, d), table.dtype), pltpu.VMEM((chunk, d), table.dtype),
            pltpu.SemaphoreType.DMA, pltpu.SemaphoreType.DMA,
            pltpu.SemaphoreType.DMA, pltpu.SemaphoreType.DMA,
        ],
    )
    def gather_kernel(table_hbm, idx_hbm, out_hbm, idx0, idx1, rows0, rows1, gsem0, gsem1, wsem0, wsem1):
        idx_v, rows_v, gsem, wsem = (idx0, idx1), (rows0, rows1), (gsem0, gsem1), (wsem0, wsem1)
        worker = lax.axis_index("s") * SC_CORES + lax.axis_index("c")
        base = worker * rows_per_worker

        def out_rows(c):
            return pl.ds(pl.multiple_of(base + c * chunk, SUBLANES), chunk)

        def load_idx(slot, c):
            pltpu.sync_copy(idx_hbm.at[out_rows(c)], idx_v[slot])

        def gather(slot):
            return pltpu.make_async_copy(table_hbm.at[idx_v[slot]], rows_v[slot], gsem[slot])

        def writeout(slot, c):
            return pltpu.make_async_copy(rows_v[slot], out_hbm.at[out_rows(c)], wsem[slot])

        load_idx(0, 0)
        gather(0).start()

        @pl.loop(0, n_pairs)
        def _(p):
            c0 = 2 * p

            @pl.when(p > 0)
            def _():
                writeout(1, c0 - 1).wait()

            load_idx(1, c0 + 1)
            gather(1).start()
            gather(0).wait()
            writeout(0, c0).start()

            @pl.when(p + 1 < n_pairs)
            def _():
                load_idx(0, c0 + 2)
                writeout(0, c0).wait()
                gather(0).start()

            gather(1).wait()
            writeout(1, c0 + 1).start()

        writeout(0, 2 * n_pairs - 2).wait()
        writeout(1, 2 * n_pairs - 1).wait()

    return gather_kernel(table, idx)


def _pad_cols(w, width):
    return jnp.pad(w, ((0, 0), (0, width - w.shape[1])))


def _prepare_layer(g_norm_mix, w_in, b_gate, g_cq, w_uq, g_ckv, w_ukv, g_qn, g_qr, g_kn, g_kr,
                   w_a2, b_a, g_gla_out, w_out, g_norm_ffn, w_pq, sub_keys, u_emb, v_emb):
    d = w_in.shape[0]
    q_rank, kv_rank = g_cq.shape[0], g_ckv.shape[0]
    nope, rope = g_qn.shape[0], g_qr.shape[0]
    mla_heads = w_uq.shape[1] // (nope + rope)
    mla_v = w_ukv.shape[1] // mla_heads - nope
    gate_rank, gla_dk_all = w_a2.shape
    gla_dv = g_gla_out.shape[0]
    gla_heads = d // gla_dv
    gla_dk = gla_dk_all // gla_heads
    peer_heads, _, nkeys, half = sub_keys.shape
    assert nope == LANES and mla_v == LANES and rope <= LANES and rope % 2 == 0
    assert mla_heads * mla_v == d and gla_heads * gla_dv == d
    assert rope + gate_rank <= LANES and nkeys == LANES and half == LANES

    widths = (q_rank, kv_rank, rope, gla_dk_all, gla_dk_all, d, gate_rank, d, d, d)
    offs = [0]
    for wd in widths:
        offs.append(offs[-1] + wd)
    assert offs[-1] == w_in.shape[1]
    seg = lambda i: w_in[:, offs[i]:offs[i + 1]]
    w_main = jnp.concatenate([seg(5), seg(7), seg(8), seg(9), seg(3), seg(4), seg(0), seg(1)], axis=1).astype(BF16)
    w_small = _pad_cols(jnp.concatenate([seg(2), seg(6)], axis=1), LANES).astype(BF16)
    cq_off = 4 * d + 2 * gla_dk_all
    assert cq_off % q_rank == 0 and (cq_off + q_rank) % kv_rank == 0
    inv_freq = ROPE_THETA ** (-jnp.arange(0, rope, 2, dtype=F32) / rope)
    scale = (nope + rope) ** -0.5
    return dict(
        dims=dict(q_rank=q_rank, kv_rank=kv_rank, nope=nope, rope=rope, mla_heads=mla_heads, mla_v=mla_v,
                  gla_heads=gla_heads, gla_dk=gla_dk, gla_dv=gla_dv, peer_heads=peer_heads, nkeys=nkeys, half=half,
                  v_blk=0, og_blk=d // gla_dv, ga_blk=2, gb_blk=3, q_blk=4 * d // gla_dk,
                  k_blk=4 * d // gla_dk + gla_heads, cq_blk=cq_off // q_rank, ckv_blk=(cq_off + q_rank) // kv_rank),
        g_norm_mix=g_norm_mix[None, :], w_main=w_main, w_small=w_small,
        invf=_pad_cols(jnp.concatenate([inv_freq, inv_freq])[None, :], LANES),
        gq=_pad_cols(jnp.concatenate([g_qn, g_qr])[None, :] * scale, nope + LANES),
        gkr=_pad_cols(g_kr[None, :], LANES), g_cq=g_cq[None, :], g_ckv=g_ckv[None, :], g_kn=g_kn[None, :],
        w_uq=jnp.pad(w_uq.reshape(q_rank, mla_heads, nope + rope),
                     ((0, 0), (0, 0), (0, LANES - rope))).reshape(q_rank, -1).astype(BF16),
        w_ukv=w_ukv.astype(BF16),
        w2p=jnp.zeros((LANES, gla_dk_all), F32).at[rope:rope + gate_rank].set(w_a2).astype(BF16),
        b_a=b_a[None, :], g_on=g_gla_out[None, :], b_gate=b_gate, w_out=w_out.astype(BF16),
        g_norm_ffn=g_norm_ffn[None, :], w_pq=w_pq.astype(BF16),
        sk2d=sub_keys.reshape(peer_heads * 2 * nkeys, half).astype(BF16),
        uv=_pack_expert_table(u_emb, v_emb),
    )


def _mixers_and_route(x_all, row0, pos, w, *, batch, seq, tiles):
    t = batch * seq
    dm = w["dims"]
    p, small = _in_proj(x_all, w["g_norm_mix"], w["w_main"], w["w_small"], row0=row0, t=t,
                        tm=tiles["in_tm"], tn=tiles["in_tn"])
    q = _mla_q(p, dm["cq_blk"], w["g_cq"], w["w_uq"], w["gq"], pos, w["invf"], heads=dm["mla_heads"],
               rank=dm["q_rank"], nope=dm["nope"], rope=dm["rope"], tm=tiles["mla_tm"])
    k, v = _mla_kv(p, dm["ckv_blk"], w["g_ckv"], w["w_ukv"], w["g_kn"], small, w["gkr"], pos, w["invf"],
                   heads=dm["mla_heads"], rank=dm["kv_rank"], nope=dm["nope"], rope=dm["rope"], dv=dm["mla_v"],
                   tm=tiles["mla_tm"])
    o_mla = _mla_attn(q, k, v, batch=batch, seq=seq, heads=dm["mla_heads"], dk=dm["nope"] + LANES, dv=dm["mla_v"],
                      tq=tiles["attn_tq"], tk=tiles["attn_tk"])
    o_gla = _gla(p, small, w["w2p"], w["b_a"], w["g_on"], batch=batch, seq=seq, heads=dm["gla_heads"],
                 dk=dm["gla_dk"], dv=dm["gla_dv"], q_blk=dm["q_blk"], k_blk=dm["k_blk"], v_blk=dm["v_blk"],
                 og_blk=dm["og_blk"])
    x2 = _out_proj(x_all, p, o_mla, o_gla, w["b_gate"], w["w_out"], row0=row0, ga_blk=dm["ga_blk"],
                   gb_blk=dm["gb_blk"], tm=tiles["out_tm"], tn=tiles["out_tn"])
    h2, ids_t, gates_t = _peer_route(x2, w["g_norm_ffn"], w["w_pq"], w["sk2d"], heads=dm["peer_heads"],
                                     nkeys=dm["nkeys"], half=dm["half"], topk=PEER_TOPK, tm=tiles["route_tm"])
    picks = dm["peer_heads"] * PEER_TOPK
    gates3 = gates_t.reshape(picks, t // SUBLANES, SUBLANES).transpose(1, 0, 2)
    return x2, h2, ids_t.T, gates3


def _layer(x2, pos, w, *, batch, seq, tiles):
    t, d = x2.shape
    splits = tiles["mix_split"]
    groups = len(splits)
    tg, bg = t // groups, batch // groups
    routed = [_mixers_and_route(x2, g * tg, pos[g * tg:(g + 1) * tg], w, batch=bg, seq=seq, tiles=tiles)
              for g in range(groups)]
    out = []
    for (td, ts), (xg, hg, ids_tok, gates3) in zip(splits, routed):
        picks = ids_tok.shape[1]
        ids3 = ids_tok.reshape(tg // (td + ts), td + ts, picks)
        staged = _sc_gather_rows(w["uv"], ids3[:, td:].reshape(-1), chunk=tiles["sc_chunk"])
        out.append(_peer_mix(ids3[:, :td].reshape(-1, td * picks), w["uv"], staged, xg, hg, gates3, td=td, ts=ts))
    return jnp.concatenate(out, axis=0)


_TILES = dict(in_tm=2048, in_tn=512, mla_tm=2048, attn_tq=1024, attn_tk=1024,
              out_tm=1024, out_tn=512, route_tm=256, mix_split=((4, 12),) * 5 + ((20, 12),) * 2 + ((4, 12),), sc_chunk=16)


def kernel(x, positions, g_norm_mix, w_in, b_gate, g_cq, w_uq, g_ckv, w_ukv, g_qn, g_qr, g_kn, g_kr,
           w_a2, b_a, g_gla_out, w_out, g_norm_ffn, w_pq, sub_keys, u_emb, v_emb, tiles=None):
    tiles = _TILES if tiles is None else tiles
    batch, seq, d = x.shape
    x2 = x.reshape(batch * seq, d)
    pos = positions.reshape(batch * seq, 1)
    for l in range(g_norm_mix.shape[0]):
        w = _prepare_layer(g_norm_mix[l], w_in[l], b_gate[l], g_cq[l], w_uq[l], g_ckv[l], w_ukv[l], g_qn[l],
                           g_qr[l], g_kn[l], g_kr[l], w_a2[l], b_a[l], g_gla_out[l], w_out[l], g_norm_ffn[l],
                           w_pq[l], sub_keys[l], u_emb[l], v_emb[l])
        x2 = _layer(x2, pos, w, batch=batch, seq=seq, tiles=tiles)
    return x2.reshape(batch, seq, d)
```

```python
import functools

import jax
import jax.numpy as jnp
from jax import lax
from jax.experimental import pallas as pl
from jax.experimental.pallas import tpu as pltpu
from jax.experimental.pallas import tpu_sc as plsc

EPS = 1e-6
ROPE_THETA = 10000.0
GLA_TAU = 16.0
GLA_CHUNK = 64
PEER_TOPK = 16

LANES = 128
SUBLANES = 8
VMEM_LIMIT_BYTES = 56 * 1024 * 1024
MIX_SLOTS = 4
SC_CORES = 2
SC_SUBCORES = 16

F32 = jnp.float32
BF16 = jnp.bfloat16
NEG_INF = float("-inf")


def _params(semantics):
    return pltpu.CompilerParams(dimension_semantics=semantics, vmem_limit_bytes=VMEM_LIMIT_BYTES)


def _rms(x, gain, n=None):
    ss = jnp.sum(x * x, axis=-1, keepdims=True)
    n = x.shape[-1] if n is None else n
    return x * lax.rsqrt(ss * (1.0 / n) + EPS) * gain


def _gelu_exact(x):
    return 0.5 * x * (1.0 + lax.erf(x * (0.5 ** 0.5)))


def _dot(a, b):
    return jnp.dot(a, b, preferred_element_type=F32)


def _dot_nt(a, b):
    return lax.dot_general(a, b, (((1,), (1,)), ((), ())), preferred_element_type=F32)


def _dot_tn(a, b):
    return lax.dot_general(a, b, (((0,), (0,)), ((), ())), preferred_element_type=F32)


def _in_proj_kernel(x_ref, g_ref, w_ref, ws_ref, p_ref, ps_ref, h_scr):
    @pl.when(pl.program_id(1) == 0)
    def _():
        h = _rms(x_ref[...], g_ref[...]).astype(BF16)
        h_scr[...] = h
        ps_ref[...] = _dot(h, ws_ref[...])

    p_ref[...] = _dot(h_scr[...], w_ref[...]).astype(p_ref.dtype)


def _in_proj(x, g, w_main, w_small, *, row0, t, tm, tn):
    d = x.shape[1]
    n = w_main.shape[1]
    assert row0 % tm == 0
    b0 = row0 // tm
    x_mode = pl.Buffered(1) if t == tm else None
    return pl.pallas_call(
        _in_proj_kernel,
        grid=(t // tm, n // tn),
        in_specs=[
            pl.BlockSpec((tm, d), lambda i, j: (i + b0, 0), pipeline_mode=x_mode),
            pl.BlockSpec((1, d), lambda i, j: (0, 0)),
            pl.BlockSpec((d, tn), lambda i, j: (0, j)),
            pl.BlockSpec((d, LANES), lambda i, j: (0, 0)),
        ],
        out_specs=[
            pl.BlockSpec((tm, tn), lambda i, j: (i, j)),
            pl.BlockSpec((tm, LANES), lambda i, j: (i, 0)),
        ],
        out_shape=[
            jax.ShapeDtypeStruct((t, n), BF16),
            jax.ShapeDtypeStruct((t, LANES), F32),
        ],
        scratch_shapes=[pltpu.VMEM((tm, d), BF16)],
        compiler_params=_params(("parallel", "arbitrary")),
    )(x, g, w_main, w_small)


def _rope_tables(pos_ref, invf_ref, rope):
    ang = pos_ref[...].astype(F32) * invf_ref[...]
    cos, sin = jnp.cos(ang), jnp.sin(ang)
    lane = lax.broadcasted_iota(jnp.int32, ang.shape, 1)
    half = rope // 2
    c = jnp.where(lane < rope, cos, 0.0)
    s_lo = jnp.where(lane < half, -sin, 0.0)
    s_hi = jnp.where(lane < half, 0.0, jnp.where(lane < rope, sin, 0.0))
    return c, s_lo, s_hi


def _apply_rope(pe, c, s_lo, s_hi, rope):
    half = rope // 2
    from_hi = pltpu.roll(pe, LANES - half, 1)
    from_lo = pltpu.roll(pe, half, 1)
    return pe * c + from_hi * s_lo + from_lo * s_hi


def _mla_q_kernel(cq_ref, gcq_ref, w_ref, gq_ref, pos_ref, invf_ref, q_ref,
                  h_scr, c_scr, slo_scr, shi_scr, *, nope, rope):
    @pl.when(pl.program_id(1) == 0)
    def _():
        h_scr[...] = _rms(cq_ref[...].astype(F32), gcq_ref[...]).astype(BF16)
        c, s_lo, s_hi = _rope_tables(pos_ref, invf_ref, rope)
        c_scr[...] = c
        slo_scr[...] = s_lo
        shi_scr[...] = s_hi

    y = _dot(h_scr[...], w_ref[...])
    g = gq_ref[...]
    qn = _rms(y[:, :nope], g[:, :nope])
    pe = _rms(y[:, nope:], g[:, nope:], n=rope)
    pe = _apply_rope(pe, c_scr[...], slo_scr[...], shi_scr[...], rope)
    q_ref[:, :nope] = qn.astype(q_ref.dtype)
    q_ref[:, nope:] = pe.astype(q_ref.dtype)


def _mla_q(p, cq_blk, g_cq, w_uq_p, gq, pos, invf, *, heads, rank, nope, rope, tm):
    t = p.shape[0]
    hw = nope + LANES
    return pl.pallas_call(
        functools.partial(_mla_q_kernel, nope=nope, rope=rope),
        grid=(t // tm, heads),
        in_specs=[
            pl.BlockSpec((tm, rank), lambda i, j: (i, cq_blk)),
            pl.BlockSpec((1, rank), lambda i, j: (0, 0)),
            pl.BlockSpec((rank, hw), lambda i, j: (0, j)),
            pl.BlockSpec((1, hw), lambda i, j: (0, 0)),
            pl.BlockSpec((tm, 1), lambda i, j: (i, 0)),
            pl.BlockSpec((1, LANES), lambda i, j: (0, 0)),
        ],
        out_specs=pl.BlockSpec((tm, hw), lambda i, j: (i, j)),
        out_shape=jax.ShapeDtypeStruct((t, heads * hw), BF16),
        scratch_shapes=[
            pltpu.VMEM((tm, rank), BF16),
            pltpu.VMEM((tm, LANES), F32),
            pltpu.VMEM((tm, LANES), F32),
            pltpu.VMEM((tm, LANES), F32),
        ],
        compiler_params=_params(("parallel", "arbitrary")),
    )(p, g_cq, w_uq_p, gq, pos, invf)


def _mla_kv_kernel(ckv_ref, gckv_ref, w_ref, gkn_ref, small_ref, gkr_ref, pos_ref, invf_ref,
                   k_ref, v_ref, h_scr, kpe_scr, *, nope, rope):
    @pl.when(pl.program_id(1) == 0)
    def _():
        h_scr[...] = _rms(ckv_ref[...].astype(F32), gckv_ref[...]).astype(BF16)
        c, s_lo, s_hi = _rope_tables(pos_ref, invf_ref, rope)
        sm = small_ref[...]
        lane = lax.broadcasted_iota(jnp.int32, sm.shape, 1)
        pe = _rms(jnp.where(lane < rope, sm, 0.0), gkr_ref[...], n=rope)
        kpe_scr[...] = _apply_rope(pe, c, s_lo, s_hi, rope).astype(BF16)

    y = _dot(h_scr[...], w_ref[...])
    k_ref[:, :nope] = _rms(y[:, :nope], gkn_ref[...]).astype(k_ref.dtype)
    k_ref[:, nope:] = kpe_scr[...]
    v_ref[...] = y[:, nope:].astype(v_ref.dtype)


def _mla_kv(p, ckv_blk, g_ckv, w_ukv, g_kn, small, gkr, pos, invf, *, heads, rank, nope, rope, dv, tm):
    t = p.shape[0]
    kw = nope + LANES
    return pl.pallas_call(
        functools.partial(_mla_kv_kernel, nope=nope, rope=rope),
        grid=(t // tm, heads),
        in_specs=[
            pl.BlockSpec((tm, rank), lambda i, j: (i, ckv_blk)),
            pl.BlockSpec((1, rank), lambda i, j: (0, 0)),
            pl.BlockSpec((rank, nope + dv), lambda i, j: (0, j)),
            pl.BlockSpec((1, nope), lambda i, j: (0, 0)),
            pl.BlockSpec((tm, LANES), lambda i, j: (i, 0)),
            pl.BlockSpec((1, LANES), lambda i, j: (0, 0)),
            pl.BlockSpec((tm, 1), lambda i, j: (i, 0)),
            pl.BlockSpec((1, LANES), lambda i, j: (0, 0)),
        ],
        out_specs=[
            pl.BlockSpec((tm, kw), lambda i, j: (i, j)),
            pl.BlockSpec((tm, dv), lambda i, j: (i, j)),
        ],
        out_shape=[
            jax.ShapeDtypeStruct((t, heads * kw), BF16),
            jax.ShapeDtypeStruct((t, heads * dv), BF16),
        ],
        scratch_shapes=[pltpu.VMEM((tm, rank), BF16), pltpu.VMEM((tm, LANES), BF16)],
        compiler_params=_params(("parallel", "arbitrary")),
    )(p, g_ckv, w_ukv, g_kn, small, gkr, pos, invf)


def _attn_kernel(q_ref, k_ref, v_ref, o_ref, *, tq, tk):
    qi = pl.program_id(2)
    q = q_ref[...]
    dv = v_ref.shape[1]
    row = qi * tq + lax.broadcasted_iota(jnp.int32, (tq, tk), 0)
    col0 = lax.broadcasted_iota(jnp.int32, (tq, tk), 1)

    def body(kb, carry, masked):
        m, l, acc = carry
        k0 = pl.multiple_of(kb * tk, tk)
        s = _dot_nt(q, k_ref[pl.ds(k0, tk), :])
        if masked:
            s = jnp.where(col0 + k0 <= row, s, NEG_INF)
        m_new = jnp.maximum(m, jnp.max(s, axis=-1, keepdims=True))
        alpha = jnp.exp(m - m_new)
        pr = jnp.exp(s - m_new)
        l = alpha * l + jnp.sum(pr, axis=-1, keepdims=True)
        acc = alpha * acc + _dot(pr.astype(BF16), v_ref[pl.ds(k0, tk), :])
        return m_new, l, acc

    init = (jnp.full((tq, 1), NEG_INF, F32), jnp.zeros((tq, 1), F32), jnp.zeros((tq, dv), F32))
    n_below = (qi * tq) // tk
    nkb = ((qi + 1) * tq + tk - 1) // tk
    carry = lax.fori_loop(0, n_below, functools.partial(body, masked=False), init)
    _, l, acc = lax.fori_loop(n_below, nkb, functools.partial(body, masked=True), carry)
    o_ref[...] = (acc / l).astype(o_ref.dtype)


def _mla_attn(q, k, v, *, batch, seq, heads, dk, dv, tq, tk):
    t = q.shape[0]
    nq = seq // tq
    return pl.pallas_call(
        functools.partial(_attn_kernel, tq=tq, tk=tk),
        grid=(batch, heads, nq),
        in_specs=[
            pl.BlockSpec((tq, dk), lambda b, h, i: (b * nq + i, h)),
            pl.BlockSpec((seq, dk), lambda b, h, i: (b, h)),
            pl.BlockSpec((seq, dv), lambda b, h, i: (b, h)),
        ],
        out_specs=pl.BlockSpec((tq, dv), lambda b, h, i: (b * nq + i, h)),
        out_shape=jax.ShapeDtypeStruct((t, heads * dv), BF16),
        compiler_params=_params(("parallel", "parallel", "arbitrary")),
    )(q, k, v)


def _gla_kernel(q_ref, k_ref, v_ref, og_ref, small_ref, w2_ref, ba_ref, gon_ref, o_ref, st_scr,
                *, seq, dk, dv, chunk):
    c = chunk
    st_scr[...] = jnp.zeros_like(st_scr)
    r_i = lax.broadcasted_iota(jnp.int32, (c, c), 0)
    c_i = lax.broadcasted_iota(jnp.int32, (c, c), 1)
    tri = jnp.where(c_i <= r_i, 1.0, 0.0).astype(BF16)
    row_id = lax.broadcasted_iota(jnp.int32, (c, 1), 0)
    w2 = w2_ref[...]
    ba = ba_ref[...]
    gon = gon_ref[...]
    q_scale = dk ** -0.5

    def chunk_step(ci, carry):
        r0 = pl.multiple_of(ci * c, c)
        qc = q_ref[pl.ds(r0, c), :].astype(F32) * q_scale
        kc = k_ref[pl.ds(r0, c), :].astype(F32)
        vc = v_ref[pl.ds(r0, c), :]
        z = _dot(small_ref[pl.ds(r0, c), :].astype(BF16), w2) + ba
        la = jax.nn.log_sigmoid(z) * (1.0 / GLA_TAU)
        hi = la.astype(BF16)
        r1 = la - hi.astype(F32)
        mid = r1.astype(BF16)
        lo = (r1 - mid.astype(F32)).astype(BF16)
        b = _dot(tri, hi) + _dot(tri, mid) + _dot(tri, lo)

        st = st_scr[...]
        inter = _dot_nt((qc * jnp.exp(b)).astype(BF16), st.astype(BF16))

        att = jnp.zeros((c, c), F32)
        for j in range(c):
            lo_r = (j // SUBLANES) * SUBLANES
            d = b[lo_r:, :] - b[j:j + 1, :]
            head = jnp.where(row_id[lo_r:lo_r + SUBLANES, :] >= j, d[:SUBLANES, :], NEG_INF)
            e = jnp.exp(jnp.concatenate([head, d[SUBLANES:, :]], axis=0) if lo_r + SUBLANES < c else head)
            col = jnp.sum(qc[lo_r:, :] * kc[j:j + 1, :] * e, axis=-1, keepdims=True)
            if lo_r:
                col = jnp.concatenate([jnp.zeros((lo_r, 1), F32), col], axis=0)
            att = jnp.where(c_i == j, col, att)
        o = inter + _dot(att.astype(BF16), vc)

        b_last = b[c - 1:c, :]
        k_dec = (kc * jnp.exp(b_last - b)).astype(BF16)
        st_scr[...] = st * jnp.exp(b_last) + _dot_tn(vc, k_dec)

        og = og_ref[pl.ds(r0, c), :].astype(F32)
        out = _rms(o, gon) * (og * jax.nn.sigmoid(og))
        o_ref[pl.ds(r0, c), :] = out.astype(o_ref.dtype)
        return carry

    lax.fori_loop(0, seq // c, chunk_step, 0)


def _gla(p, small, w2p, b_a, g_on, *, batch, seq, heads, dk, dv, q_blk, k_blk, v_blk, og_blk):
    t = p.shape[0]
    return pl.pallas_call(
        functools.partial(_gla_kernel, seq=seq, dk=dk, dv=dv, chunk=GLA_CHUNK),
        grid=(batch, heads),
        in_specs=[
            pl.BlockSpec((seq, dk), lambda b, h: (b, q_blk + h)),
            pl.BlockSpec((seq, dk), lambda b, h: (b, k_blk + h)),
            pl.BlockSpec((seq, dv), lambda b, h: (b, v_blk + h)),
            pl.BlockSpec((seq, dv), lambda b, h: (b, og_blk + h)),
            pl.BlockSpec((seq, LANES), lambda b, h: (b, 0)),
            pl.BlockSpec((LANES, dk), lambda b, h: (0, h)),
            pl.BlockSpec((1, dk), lambda b, h: (0, h)),
            pl.BlockSpec((1, dv), lambda b, h: (0, 0)),
        ],
        out_specs=pl.BlockSpec((seq, dv), lambda b, h: (b, h)),
        out_shape=jax.ShapeDtypeStruct((t, heads * dv), BF16),
        scratch_shapes=[pltpu.VMEM((dv, dk), F32)],
        compiler_params=_params(("parallel", "parallel")),
    )(p, p, p, p, small, w2p, b_a, g_on)


def _out_proj_kernel(x_ref, ga_ref, gb_ref, oa_ref, ob_ref, bg_ref, w_ref, o_ref, m_scr):
    @pl.when(pl.program_id(1) == 0)
    def _():
        bg = bg_ref[...]
        sa = jax.nn.sigmoid(ga_ref[...].astype(F32) + bg[0:1, :])
        sb = jax.nn.sigmoid(gb_ref[...].astype(F32) + bg[1:2, :])
        m_scr[...] = (sa * oa_ref[...].astype(F32) + sb * ob_ref[...].astype(F32)).astype(BF16)

    o_ref[...] = x_ref[...] + _dot(m_scr[...], w_ref[...])


def _out_proj(x, p, o_mla, o_gla, b_gate, w_out, *, row0, ga_blk, gb_blk, tm, tn):
    t, d = p.shape[0], x.shape[1]
    assert row0 % tm == 0
    b0 = row0 // tm
    return pl.pallas_call(
        _out_proj_kernel,
        grid=(t // tm, d // tn),
        in_specs=[
            pl.BlockSpec((tm, tn), lambda i, j: (i + b0, j)),
            pl.BlockSpec((tm, d), lambda i, j: (i, ga_blk)),
            pl.BlockSpec((tm, d), lambda i, j: (i, gb_blk)),
            pl.BlockSpec((tm, d), lambda i, j: (i, 0)),
            pl.BlockSpec((tm, d), lambda i, j: (i, 0)),
            pl.BlockSpec((2, d), lambda i, j: (0, 0)),
            pl.BlockSpec((d, tn), lambda i, j: (0, j)),
        ],
        out_specs=pl.BlockSpec((tm, tn), lambda i, j: (i, j)),
        out_shape=jax.ShapeDtypeStruct((t, d), F32),
        scratch_shapes=[pltpu.VMEM((tm, d), BF16)],
        compiler_params=_params(("parallel", "arbitrary")),
    )(x, p, p, o_mla, o_gla, b_gate, w_out)


def _topk_rows(s, k, payload=None):
    n = s.shape[0]
    iota = lax.broadcasted_iota(jnp.int32, s.shape, 0).astype(F32)
    kiota = lax.broadcasted_iota(jnp.int32, (k, s.shape[1]), 0)
    vals = jnp.zeros((k, s.shape[1]), F32)
    picks = jnp.zeros((k, s.shape[1]), F32)
    for r in range(k):
        m = jnp.max(s, axis=0, keepdims=True)
        idx = jnp.min(jnp.where(s == m, iota, float(n)), axis=0, keepdims=True)
        hit = iota == idx
        if payload is None:
            pick = idx
        else:
            pick = jnp.sum(jnp.where(hit, payload, 0.0), axis=0, keepdims=True)
        vals = jnp.where(kiota == r, m, vals)
        picks = jnp.where(kiota == r, pick, picks)
        s = jnp.where(hit, NEG_INF, s)
    return vals, picks


def _peer_route_kernel(x_ref, g_ref, w_ref, sk_ref, h_ref, ids_ref, gates_ref, *, heads, nkeys, half, topk):
    h = _rms(x_ref[...], g_ref[...]).astype(BF16)
    h_ref[...] = h
    qf = _dot(h, w_ref[...])
    for hd in range(heads):
        tops = []
        for part in range(2):
            o = (hd * 2 + part) * half
            qh = qf[:, o:o + half].astype(BF16)
            keys = sk_ref[(hd * 2 + part) * nkeys:(hd * 2 + part + 1) * nkeys, :]
            tops.append(_topk_rows(_dot_nt(keys, qh), topk))
        (s1, i1), (s2, i2) = tops
        assert topk == 2 * SUBLANES
        hs = SUBLANES
        pair_s = [s1[0:1, :] + s2] + [s1[a:a + 1, :] + s2[:hs, :] for a in range(1, hs)] + [s1[hs:, :] + s2[0:1, :]]
        pair_id = ([i1[0:1, :] * float(nkeys) + i2]
                   + [i1[a:a + 1, :] * float(nkeys) + i2[:hs, :] for a in range(1, hs)]
                   + [i1[hs:, :] * float(nkeys) + i2[0:1, :]])
        best_s, best_id = _topk_rows(jnp.concatenate(pair_s, axis=0), topk,
                                     payload=jnp.concatenate(pair_id, axis=0))
        e = jnp.exp(best_s - best_s[0:1, :])
        gate = e / jnp.sum(e, axis=0, keepdims=True)
        ids_ref[hd * topk:(hd + 1) * topk, :] = best_id.astype(jnp.int32)
        gates_ref[hd * topk:(hd + 1) * topk, :] = gate


def _peer_route(x, g, w_pq, sk2d, *, heads, nkeys, half, topk, tm):
    t, d = x.shape
    dq = w_pq.shape[1]
    return pl.pallas_call(
        functools.partial(_peer_route_kernel, heads=heads, nkeys=nkeys, half=half, topk=topk),
        grid=(t // tm,),
        in_specs=[
            pl.BlockSpec((tm, d), lambda i: (i, 0)),
            pl.BlockSpec((1, d), lambda i: (0, 0)),
            pl.BlockSpec((d, dq), lambda i: (0, 0)),
            pl.BlockSpec((heads * 2 * nkeys, half), lambda i: (0, 0)),
        ],
        out_specs=[
            pl.BlockSpec((tm, d), lambda i: (i, 0)),
            pl.BlockSpec((heads * topk, tm), lambda i: (0, i)),
            pl.BlockSpec((heads * topk, tm), lambda i: (0, i)),
        ],
        out_shape=[
            jax.ShapeDtypeStruct((t, d), BF16),
            jax.ShapeDtypeStruct((heads * topk, t), jnp.int32),
            jax.ShapeDtypeStruct((heads * topk, t), F32),
        ],
        compiler_params=_params(("parallel",)),
    )(x, g, w_pq, sk2d)


def _pack_expert_table(u_emb, v_emb):
    n, d = u_emb.shape
    half = d // 2
    tm = 512
    assert n % tm == 0

    def pack(w):
        bits = lax.bitcast_convert_type(w.astype(BF16).astype(F32), jnp.int32)
        return ((bits[:, :half] >> 16) & jnp.int32(0xFFFF)) | (bits[:, half:] & jnp.int32(-65536))

    def pack_kernel(u_ref, v_ref, o_ref):
        o_ref[:, :half] = pack(u_ref[...])
        o_ref[:, half:] = pack(v_ref[...])

    return pl.pallas_call(
        pack_kernel,
        grid=(n // tm,),
        in_specs=[pl.BlockSpec((tm, d), lambda i: (i, 0)), pl.BlockSpec((tm, d), lambda i: (i, 0))],
        out_specs=pl.BlockSpec((tm, d), lambda i: (i, 0)),
        out_shape=jax.ShapeDtypeStruct((n, d), jnp.int32),
        compiler_params=_params(("parallel",)),
    )(u_emb, v_emb)


def _unpack_words(w):
    lo = lax.bitcast_convert_type(w << 16, F32)
    hi = lax.bitcast_convert_type(w & jnp.int32(-65536), F32)
    return lo, hi


def _expert_mix(words, h, gate, d):
    half = d // 2
    u_lo, u_hi = _unpack_words(words[:, :half])
    act = jnp.sum(u_lo * h[:, :half] + u_hi * h[:, half:], axis=-1, keepdims=True)
    w = gate * _gelu_exact(act)
    v_lo, v_hi = _unpack_words(words[:, half:])
    return jnp.concatenate([jnp.sum(v_lo * w, axis=0, keepdims=True),
                            jnp.sum(v_hi * w, axis=0, keepdims=True)], axis=1)


def _peer_mix_kernel(ids_hbm, uv_hbm, rows_ref, x_ref, h_ref, gates_ref, o_ref, ids_smem, buf, ids_sems, row_sems,
                     *, td, ts, picks, d, n_steps):
    step = pl.program_id(0)
    par = lax.rem(step, 2)
    ahead = MIX_SLOTS - 1
    assert td % MIX_SLOTS == 0 and td > ahead and (td + ts) % SUBLANES == 0

    def ids_copy(s, row):
        return pltpu.make_async_copy(ids_hbm.at[s], ids_smem.at[row], ids_sems.at[row])

    def issue(row, tok, slot):
        for j in range(picks):
            eid = ids_smem[row, tok * picks + j]
            pltpu.make_async_copy(uv_hbm.at[pl.ds(eid, 1), :], buf.at[slot, pl.ds(j, 1), :],
                                  row_sems.at[slot]).start()

    def wait_rows(slot):
        pltpu.make_async_copy(uv_hbm.at[pl.ds(0, picks), :], buf.at[slot], row_sems.at[slot]).wait()

    @pl.when(step == 0)
    def _():
        first = ids_copy(0, 0)
        first.start()
        first.wait()
        for u in range(ahead):
            issue(0, u, u)

    @pl.when(step + 1 < n_steps)
    def _():
        ids_copy(step + 1, 1 - par).start()

    loaded = {}

    def finish(pos, words):
        g, lane = divmod(pos, SUBLANES)
        if g not in loaded:
            loaded[g] = (gates_ref[g], h_ref[g * SUBLANES:(g + 1) * SUBLANES, :].astype(F32))
        gates, h8 = loaded[g]
        row = _expert_mix(words, h8[lane:lane + 1, :], gates[:, lane:lane + 1], d)
        o_ref[pos:pos + 1, :] = x_ref[pos:pos + 1, :] + row

    staged_done = 0
    for u in range(td):
        nxt = u + ahead
        if nxt < td:
            issue(par, nxt, nxt % MIX_SLOTS)
        else:
            @pl.when(step + 1 < n_steps)
            def _():
                if nxt == td:
                    ids_copy(step + 1, 1 - par).wait()
                issue(1 - par, nxt - td, nxt % MIX_SLOTS)

        staged_upto = (u + 1) * ts // td
        for k in range(staged_done, staged_upto):
            finish(td + k, rows_ref[k * picks:(k + 1) * picks, :])
        staged_done = staged_upto
        wait_rows(u % MIX_SLOTS)
        finish(u, buf[u % MIX_SLOTS])


def _peer_mix(ids_direct, uv, staged, x, h, gates3, *, td, ts):
    t, d = x.shape
    picks = gates3.shape[1]
    n_steps = t // (td + ts)
    assert staged.shape[0] == n_steps * ts * picks and ids_direct.shape == (n_steps, td * picks)
    return pl.pallas_call(
        functools.partial(_peer_mix_kernel, td=td, ts=ts, picks=picks, d=d, n_steps=n_steps),
        grid=(n_steps,),
        input_output_aliases={3: 0},
        in_specs=[
            pl.BlockSpec(memory_space=pl.ANY),
            pl.BlockSpec(memory_space=pl.ANY),
            pl.BlockSpec((ts * picks, d), lambda i: (i, 0)),
            pl.BlockSpec((td + ts, d), lambda i: (i, 0)),
            pl.BlockSpec((td + ts, d), lambda i: (i, 0)),
            pl.BlockSpec(((td + ts) // SUBLANES, picks, SUBLANES), lambda i: (i, 0, 0)),
        ],
        out_specs=pl.BlockSpec((td + ts, d), lambda i: (i, 0)),
        out_shape=jax.ShapeDtypeStruct((t, d), F32),
        scratch_shapes=[
            pltpu.SMEM((2, td * picks), jnp.int32),
            pltpu.VMEM((MIX_SLOTS, picks, d), jnp.int32),
            pltpu.SemaphoreType.DMA((2,)),
            pltpu.SemaphoreType.DMA((MIX_SLOTS,)),
        ],
        compiler_params=_params(("arbitrary",)),
    )(ids_direct, uv, staged, x, h, gates3)


def _sc_gather_rows(table, idx, *, chunk):
    n_rows = idx.shape[0]
    d = table.shape[1]
    workers = SC_CORES * SC_SUBCORES
    assert n_rows % (workers * 2 * chunk) == 0 and chunk % SUBLANES == 0 and chunk <= LANES
    rows_per_worker = n_rows // workers
    n_pairs = rows_per_worker // (2 * chunk)
    mesh = plsc.VectorSubcoreMesh(core_axis_name="c", subcore_axis_name="s")

    @functools.partial(
        pl.kernel, mesh=mesh,
        out_type=jax.ShapeDtypeStruct((n_rows, d), table.dtype),
        scratch_types=[
            pltpu.VMEM((chunk,), jnp.int32), pltpu.VMEM((chunk,), jnp.int32),
            pltpu.VMEM((chunk, d), table.dtype), pltpu.VMEM((chunk, d), table.dtype),
            pltpu.SemaphoreType.DMA, pltpu.SemaphoreType.DMA,
            pltpu.SemaphoreType.DMA, pltpu.SemaphoreType.DMA,
        ],
    )
    def gather_kernel(table_hbm, idx_hbm, out_hbm, idx0, idx1, rows0, rows1, gsem0, gsem1, wsem0, wsem1):
        idx_v, rows_v, gsem, wsem = (idx0, idx1), (rows0, rows1), (gsem0, gsem1), (wsem0, wsem1)
        worker = lax.axis_index("s") * SC_CORES + lax.axis_index("c")
        base = worker * rows_per_worker

        def out_rows(c):
            return pl.ds(pl.multiple_of(base + c * chunk, SUBLANES), chunk)

        def load_idx(slot, c):
            pltpu.sync_copy(idx_hbm.at[out_rows(c)], idx_v[slot])

        def gather(slot):
            return pltpu.make_async_copy(table_hbm.at[idx_v[slot]], rows_v[slot], gsem[slot])

        def writeout(slot, c):
            return pltpu.make_async_copy(rows_v[slot], out_hbm.at[out_rows(c)], wsem[slot])

        load_idx(0, 0)
        gather(0).start()

        @pl.loop(0, n_pairs)
        def _(p):
            c0 = 2 * p

            @pl.when(p > 0)
            def _():
                writeout(1, c0 - 1).wait()

            load_idx(1, c0 + 1)
            gather(1).start()
            gather(0).wait()
            writeout(0, c0).start()

            @pl.when(p + 1 < n_pairs)
            def _():
                load_idx(0, c0 + 2)
                writeout(0, c0).wait()
                gather(0).start()

            gather(1).wait()
            writeout(1, c0 + 1).start()

        writeout(0, 2 * n_pairs - 2).wait()
        writeout(1, 2 * n_pairs - 1).wait()

    return gather_kernel(table, idx)


def _pad_cols(w, width):
    return jnp.pad(w, ((0, 0), (0, width - w.shape[1])))


def _prepare_layer(g_norm_mix, w_in, b_gate, g_cq, w_uq, g_ckv, w_ukv, g_qn, g_qr, g_kn, g_kr,
                   w_a2, b_a, g_gla_out, w_out, g_norm_ffn, w_pq, sub_keys, u_emb, v_emb):
    d = w_in.shape[0]
    q_rank, kv_rank = g_cq.shape[0], g_ckv.shape[0]
    nope, rope = g_qn.shape[0], g_qr.shape[0]
    mla_heads = w_uq.shape[1] // (nope + rope)
    mla_v = w_ukv.shape[1] // mla_heads - nope
    gate_rank, gla_dk_all = w_a2.shape
    gla_dv = g_gla_out.shape[0]
    gla_heads = d // gla_dv
    gla_dk = gla_dk_all // gla_heads
    peer_heads, _, nkeys, half = sub_keys.shape
    assert nope == LANES and mla_v == LANES and rope <= LANES and rope % 2 == 0
    assert mla_heads * mla_v == d and gla_heads * gla_dv == d
    assert rope + gate_rank <= LANES and nkeys == LANES and half == LANES

    widths = (q_rank, kv_rank, rope, gla_dk_all, gla_dk_all, d, gate_rank, d, d, d)
    offs = [0]
    for wd in widths:
        offs.append(offs[-1] + wd)
    assert offs[-1] == w_in.shape[1]
    seg = lambda i: w_in[:, offs[i]:offs[i + 1]]
    w_main = jnp.concatenate([seg(5), seg(7), seg(8), seg(9), seg(3), seg(4), seg(0), seg(1)], axis=1).astype(BF16)
    w_small = _pad_cols(jnp.concatenate([seg(2), seg(6)], axis=1), LANES).astype(BF16)
    cq_off = 4 * d + 2 * gla_dk_all
    assert cq_off % q_rank == 0 and (cq_off + q_rank) % kv_rank == 0
    inv_freq = ROPE_THETA ** (-jnp.arange(0, rope, 2, dtype=F32) / rope)
    scale = (nope + rope) ** -0.5
    return dict(
        dims=dict(q_rank=q_rank, kv_rank=kv_rank, nope=nope, rope=rope, mla_heads=mla_heads, mla_v=mla_v,
                  gla_heads=gla_heads, gla_dk=gla_dk, gla_dv=gla_dv, peer_heads=peer_heads, nkeys=nkeys, half=half,
                  v_blk=0, og_blk=d // gla_dv, ga_blk=2, gb_blk=3, q_blk=4 * d // gla_dk,
                  k_blk=4 * d // gla_dk + gla_heads, cq_blk=cq_off // q_rank, ckv_blk=(cq_off + q_rank) // kv_rank),
        g_norm_mix=g_norm_mix[None, :], w_main=w_main, w_small=w_small,
        invf=_pad_cols(jnp.concatenate([inv_freq, inv_freq])[None, :], LANES),
        gq=_pad_cols(jnp.concatenate([g_qn, g_qr])[None, :] * scale, nope + LANES),
        gkr=_pad_cols(g_kr[None, :], LANES), g_cq=g_cq[None, :], g_ckv=g_ckv[None, :], g_kn=g_kn[None, :],
        w_uq=jnp.pad(w_uq.reshape(q_rank, mla_heads, nope + rope),
                     ((0, 0), (0, 0), (0, LANES - rope))).reshape(q_rank, -1).astype(BF16),
        w_ukv=w_ukv.astype(BF16),
        w2p=jnp.zeros((LANES, gla_dk_all), F32).at[rope:rope + gate_rank].set(w_a2).astype(BF16),
        b_a=b_a[None, :], g_on=g_gla_out[None, :], b_gate=b_gate, w_out=w_out.astype(BF16),
        g_norm_ffn=g_norm_ffn[None, :], w_pq=w_pq.astype(BF16),
        sk2d=sub_keys.reshape(peer_heads * 2 * nkeys, half).astype(BF16),
        uv=_pack_expert_table(u_emb, v_emb),
    )


def _mixers_and_route(x_all, row0, pos, w, *, batch, seq, tiles):
    t = batch * seq
    dm = w["dims"]
    p, small = _in_proj(x_all, w["g_norm_mix"], w["w_main"], w["w_small"], row0=row0, t=t,
                        tm=tiles["in_tm"], tn=tiles["in_tn"])
    q = _mla_q(p, dm["cq_blk"], w["g_cq"], w["w_uq"], w["gq"], pos, w["invf"], heads=dm["mla_heads"],
               rank=dm["q_rank"], nope=dm["nope"], rope=dm["rope"], tm=tiles["mla_tm"])
    k, v = _mla_kv(p, dm["ckv_blk"], w["g_ckv"], w["w_ukv"], w["g_kn"], small, w["gkr"], pos, w["invf"],
                   heads=dm["mla_heads"], rank=dm["kv_rank"], nope=dm["nope"], rope=dm["rope"], dv=dm["mla_v"],
                   tm=tiles["mla_tm"])
    o_mla = _mla_attn(q, k, v, batch=batch, seq=seq, heads=dm["mla_heads"], dk=dm["nope"] + LANES, dv=dm["mla_v"],
                      tq=tiles["attn_tq"], tk=tiles["attn_tk"])
    o_gla = _gla(p, small, w["w2p"], w["b_a"], w["g_on"], batch=batch, seq=seq, heads=dm["gla_heads"],
                 dk=dm["gla_dk"], dv=dm["gla_dv"], q_blk=dm["q_blk"], k_blk=dm["k_blk"], v_blk=dm["v_blk"],
                 og_blk=dm["og_blk"])
    x2 = _out_proj(x_all, p, o_mla, o_gla, w["b_gate"], w["w_out"], row0=row0, ga_blk=dm["ga_blk"],
                   gb_blk=dm["gb_blk"], tm=tiles["out_tm"], tn=tiles["out_tn"])
    h2, ids_t, gates_t = _peer_route(x2, w["g_norm_ffn"], w["w_pq"], w["sk2d"], heads=dm["peer_heads"],
                                     nkeys=dm["nkeys"], half=dm["half"], topk=PEER_TOPK, tm=tiles["route_tm"])
    picks = dm["peer_heads"] * PEER_TOPK
    gates3 = gates_t.reshape(picks, t // SUBLANES, SUBLANES).transpose(1, 0, 2)
    return x2, h2, ids_t.T, gates3


def _layer(x2, pos, w, *, batch, seq, tiles):
    t, d = x2.shape
    splits = tiles["mix_split"]
    groups = len(splits)
    tg, bg = t // groups, batch // groups
    routed = [_mixers_and_route(x2, g * tg, pos[g * tg:(g + 1) * tg], w, batch=bg, seq=seq, tiles=tiles)
              for g in range(groups)]
    out = []
    for (td, ts), (xg, hg, ids_tok, gates3) in zip(splits, routed):
        picks = ids_tok.shape[1]
        ids3 = ids_tok.reshape(tg // (td + ts), td + ts, picks)
        staged = _sc_gather_rows(w["uv"], ids3[:, td:].reshape(-1), chunk=tiles["sc_chunk"])
        out.append(_peer_mix(ids3[:, :td].reshape(-1, td * picks), w["uv"], staged, xg, hg, gates3, td=td, ts=ts))
    return jnp.concatenate(out, axis=0)


_TILES = dict(in_tm=2048, in_tn=512, mla_tm=2048, attn_tq=1024, attn_tk=1024,
              out_tm=1024, out_tn=512, route_tm=256, mix_split=((4, 12),) * 5 + ((24, 8),) * 2 + ((4, 12),), sc_chunk=16)


def kernel(x, positions, g_norm_mix, w_in, b_gate, g_cq, w_uq, g_ckv, w_ukv, g_qn, g_qr, g_kn, g_kr,
           w_a2, b_a, g_gla_out, w_out, g_norm_ffn, w_pq, sub_keys, u_emb, v_emb, tiles=None):
    tiles = _TILES if tiles is None else tiles
    batch, seq, d = x.shape
    x2 = x.reshape(batch * seq, d)
    pos = positions.reshape(batch * seq, 1)
    for l in range(g_norm_mix.shape[0]):
        w = _prepare_layer(g_norm_mix[l], w_in[l], b_gate[l], g_cq[l], w_uq[l], g_ckv[l], w_ukv[l], g_qn[l],
                           g_qr[l], g_kn[l], g_kr[l], w_a2[l], b_a[l], g_gla_out[l], w_out[l], g_norm_ffn[l],
                           w_pq[l], sub_keys[l], u_emb[l], v_emb[l])
        x2 = _layer(x2, pos, w, batch=batch, seq=seq, tiles=tiles)
    return x2.reshape(batch, seq, d)
```

```python
import functools

import jax
import jax.numpy as jnp
from jax import lax
from jax.experimental import pallas as pl
from jax.experimental.pallas import tpu as pltpu
from jax.experimental.pallas import tpu_sc as plsc

EPS = 1e-6
ROPE_THETA = 10000.0
GLA_TAU = 16.0
GLA_CHUNK = 64
PEER_TOPK = 16

LANES = 128
SUBLANES = 8
VMEM_LIMIT_BYTES = 56 * 1024 * 1024
MIX_SLOTS = (8, 4)
SC_CORES = 2
SC_SUBCORES = 16

F32 = jnp.float32
BF16 = jnp.bfloat16
NEG_INF = float("-inf")


def _params(semantics):
    return pltpu.CompilerParams(dimension_semantics=semantics, vmem_limit_bytes=VMEM_LIMIT_BYTES)


def _rms(x, gain, n=None):
    ss = jnp.sum(x * x, axis=-1, keepdims=True)
    n = x.shape[-1] if n is None else n
    return x * lax.rsqrt(ss * (1.0 / n) + EPS) * gain


def _gelu_exact(x):
    return 0.5 * x * (1.0 + lax.erf(x * (0.5 ** 0.5)))


def _dot(a, b):
    return jnp.dot(a, b, preferred_element_type=F32)


def _dot_nt(a, b):
    return lax.dot_general(a, b, (((1,), (1,)), ((), ())), preferred_element_type=F32)


def _dot_tn(a, b):
    return lax.dot_general(a, b, (((0,), (0,)), ((), ())), preferred_element_type=F32)


def _in_proj_kernel(x_ref, g_ref, w_ref, ws_ref, p_ref, ps_ref, h_scr):
    @pl.when(pl.program_id(1) == 0)
    def _():
        h = _rms(x_ref[...], g_ref[...]).astype(BF16)
        h_scr[...] = h
        ps_ref[...] = _dot(h, ws_ref[...])

    p_ref[...] = _dot(h_scr[...], w_ref[...]).astype(p_ref.dtype)


def _in_proj(x, g, w_main, w_small, *, row0, t, tm, tn):
    d = x.shape[1]
    n = w_main.shape[1]
    assert row0 % tm == 0
    b0 = row0 // tm
    x_mode = pl.Buffered(1) if t == tm else None
    return pl.pallas_call(
        _in_proj_kernel,
        grid=(t // tm, n // tn),
        in_specs=[
            pl.BlockSpec((tm, d), lambda i, j: (i + b0, 0), pipeline_mode=x_mode),
            pl.BlockSpec((1, d), lambda i, j: (0, 0)),
            pl.BlockSpec((d, tn), lambda i, j: (0, j)),
            pl.BlockSpec((d, LANES), lambda i, j: (0, 0)),
        ],
        out_specs=[
            pl.BlockSpec((tm, tn), lambda i, j: (i, j)),
            pl.BlockSpec((tm, LANES), lambda i, j: (i, 0)),
        ],
        out_shape=[
            jax.ShapeDtypeStruct((t, n), BF16),
            jax.ShapeDtypeStruct((t, LANES), F32),
        ],
        scratch_shapes=[pltpu.VMEM((tm, d), BF16)],
        compiler_params=_params(("parallel", "arbitrary")),
    )(x, g, w_main, w_small)


def _rope_tables(pos_ref, invf_ref, rope):
    ang = pos_ref[...].astype(F32) * invf_ref[...]
    cos, sin = jnp.cos(ang), jnp.sin(ang)
    lane = lax.broadcasted_iota(jnp.int32, ang.shape, 1)
    half = rope // 2
    c = jnp.where(lane < rope, cos, 0.0)
    s_lo = jnp.where(lane < half, -sin, 0.0)
    s_hi = jnp.where(lane < half, 0.0, jnp.where(lane < rope, sin, 0.0))
    return c, s_lo, s_hi


def _apply_rope(pe, c, s_lo, s_hi, rope):
    half = rope // 2
    from_hi = pltpu.roll(pe, LANES - half, 1)
    from_lo = pltpu.roll(pe, half, 1)
    return pe * c + from_hi * s_lo + from_lo * s_hi


def _mla_q_kernel(cq_ref, gcq_ref, w_ref, gq_ref, pos_ref, invf_ref, q_ref,
                  h_scr, c_scr, slo_scr, shi_scr, *, nope, rope):
    @pl.when(pl.program_id(1) == 0)
    def _():
        h_scr[...] = _rms(cq_ref[...].astype(F32), gcq_ref[...]).astype(BF16)
        c, s_lo, s_hi = _rope_tables(pos_ref, invf_ref, rope)
        c_scr[...] = c
        slo_scr[...] = s_lo
        shi_scr[...] = s_hi

    y = _dot(h_scr[...], w_ref[...])
    g = gq_ref[...]
    qn = _rms(y[:, :nope], g[:, :nope])
    pe = _rms(y[:, nope:], g[:, nope:], n=rope)
    pe = _apply_rope(pe, c_scr[...], slo_scr[...], shi_scr[...], rope)
    q_ref[:, :nope] = qn.astype(q_ref.dtype)
    q_ref[:, nope:] = pe.astype(q_ref.dtype)


def _mla_q(p, cq_blk, g_cq, w_uq_p, gq, pos, invf, *, heads, rank, nope, rope, tm):
    t = p.shape[0]
    hw = nope + LANES
    return pl.pallas_call(
        functools.partial(_mla_q_kernel, nope=nope, rope=rope),
        grid=(t // tm, heads),
        in_specs=[
            pl.BlockSpec((tm, rank), lambda i, j: (i, cq_blk)),
            pl.BlockSpec((1, rank), lambda i, j: (0, 0)),
            pl.BlockSpec((rank, hw), lambda i, j: (0, j)),
            pl.BlockSpec((1, hw), lambda i, j: (0, 0)),
            pl.BlockSpec((tm, 1), lambda i, j: (i, 0)),
            pl.BlockSpec((1, LANES), lambda i, j: (0, 0)),
        ],
        out_specs=pl.BlockSpec((tm, hw), lambda i, j: (i, j)),
        out_shape=jax.ShapeDtypeStruct((t, heads * hw), BF16),
        scratch_shapes=[
            pltpu.VMEM((tm, rank), BF16),
            pltpu.VMEM((tm, LANES), F32),
            pltpu.VMEM((tm, LANES), F32),
            pltpu.VMEM((tm, LANES), F32),
        ],
        compiler_params=_params(("parallel", "arbitrary")),
    )(p, g_cq, w_uq_p, gq, pos, invf)


def _mla_kv_kernel(ckv_ref, gckv_ref, w_ref, gkn_ref, small_ref, gkr_ref, pos_ref, invf_ref,
                   k_ref, v_ref, h_scr, kpe_scr, *, nope, rope):
    @pl.when(pl.program_id(1) == 0)
    def _():
        h_scr[...] = _rms(ckv_ref[...].astype(F32), gckv_ref[...]).astype(BF16)
        c, s_lo, s_hi = _rope_tables(pos_ref, invf_ref, rope)
        sm = small_ref[...]
        lane = lax.broadcasted_iota(jnp.int32, sm.shape, 1)
        pe = _rms(jnp.where(lane < rope, sm, 0.0), gkr_ref[...], n=rope)
        kpe_scr[...] = _apply_rope(pe, c, s_lo, s_hi, rope).astype(BF16)

    y = _dot(h_scr[...], w_ref[...])
    k_ref[:, :nope] = _rms(y[:, :nope], gkn_ref[...]).astype(k_ref.dtype)
    k_ref[:, nope:] = kpe_scr[...]
    v_ref[...] = y[:, nope:].astype(v_ref.dtype)


def _mla_kv(p, ckv_blk, g_ckv, w_ukv, g_kn, small, gkr, pos, invf, *, heads, rank, nope, rope, dv, tm):
    t = p.shape[0]
    kw = nope + LANES
    return pl.pallas_call(
        functools.partial(_mla_kv_kernel, nope=nope, rope=rope),
        grid=(t // tm, heads),
        in_specs=[
            pl.BlockSpec((tm, rank), lambda i, j: (i, ckv_blk)),
            pl.BlockSpec((1, rank), lambda i, j: (0, 0)),
            pl.BlockSpec((rank, nope + dv), lambda i, j: (0, j)),
            pl.BlockSpec((1, nope), lambda i, j: (0, 0)),
            pl.BlockSpec((tm, LANES), lambda i, j: (i, 0)),
            pl.BlockSpec((1, LANES), lambda i, j: (0, 0)),
            pl.BlockSpec((tm, 1), lambda i, j: (i, 0)),
            pl.BlockSpec((1, LANES), lambda i, j: (0, 0)),
        ],
        out_specs=[
            pl.BlockSpec((tm, kw), lambda i, j: (i, j)),
            pl.BlockSpec((tm, dv), lambda i, j: (i, j)),
        ],
        out_shape=[
            jax.ShapeDtypeStruct((t, heads * kw), BF16),
            jax.ShapeDtypeStruct((t, heads * dv), BF16),
        ],
        scratch_shapes=[pltpu.VMEM((tm, rank), BF16), pltpu.VMEM((tm, LANES), BF16)],
        compiler_params=_params(("parallel", "arbitrary")),
    )(p, g_ckv, w_ukv, g_kn, small, gkr, pos, invf)


def _attn_kernel(q_ref, k_ref, v_ref, o_ref, *, tq, tk):
    qi = pl.program_id(2)
    q = q_ref[...]
    dv = v_ref.shape[1]
    row = qi * tq + lax.broadcasted_iota(jnp.int32, (tq, tk), 0)
    col0 = lax.broadcasted_iota(jnp.int32, (tq, tk), 1)

    def body(kb, carry, masked):
        m, l, acc = carry
        k0 = pl.multiple_of(kb * tk, tk)
        s = _dot_nt(q, k_ref[pl.ds(k0, tk), :])
        if masked:
            s = jnp.where(col0 + k0 <= row, s, NEG_INF)
        m_new = jnp.maximum(m, jnp.max(s, axis=-1, keepdims=True))
        alpha = jnp.exp(m - m_new)
        pr = jnp.exp(s - m_new)
        l = alpha * l + jnp.sum(pr, axis=-1, keepdims=True)
        acc = alpha * acc + _dot(pr.astype(BF16), v_ref[pl.ds(k0, tk), :])
        return m_new, l, acc

    init = (jnp.full((tq, 1), NEG_INF, F32), jnp.zeros((tq, 1), F32), jnp.zeros((tq, dv), F32))
    n_below = (qi * tq) // tk
    nkb = ((qi + 1) * tq + tk - 1) // tk
    carry = lax.fori_loop(0, n_below, functools.partial(body, masked=False), init)
    _, l, acc = lax.fori_loop(n_below, nkb, functools.partial(body, masked=True), carry)
    o_ref[...] = (acc / l).astype(o_ref.dtype)


def _mla_attn(q, k, v, *, batch, seq, heads, dk, dv, tq, tk):
    t = q.shape[0]
    nq = seq // tq
    return pl.pallas_call(
        functools.partial(_attn_kernel, tq=tq, tk=tk),
        grid=(batch, heads, nq),
        in_specs=[
            pl.BlockSpec((tq, dk), lambda b, h, i: (b * nq + i, h)),
            pl.BlockSpec((seq, dk), lambda b, h, i: (b, h)),
            pl.BlockSpec((seq, dv), lambda b, h, i: (b, h)),
        ],
        out_specs=pl.BlockSpec((tq, dv), lambda b, h, i: (b * nq + i, h)),
        out_shape=jax.ShapeDtypeStruct((t, heads * dv), BF16),
        compiler_params=_params(("parallel", "parallel", "arbitrary")),
    )(q, k, v)


def _gla_kernel(q_ref, k_ref, v_ref, og_ref, small_ref, w2_ref, ba_ref, gon_ref, o_ref, st_scr,
                *, seq, dk, dv, chunk):
    c = chunk
    st_scr[...] = jnp.zeros_like(st_scr)
    r_i = lax.broadcasted_iota(jnp.int32, (c, c), 0)
    c_i = lax.broadcasted_iota(jnp.int32, (c, c), 1)
    tri = jnp.where(c_i <= r_i, 1.0, 0.0).astype(BF16)
    row_id = lax.broadcasted_iota(jnp.int32, (c, 1), 0)
    w2 = w2_ref[...]
    ba = ba_ref[...]
    gon = gon_ref[...]
    q_scale = dk ** -0.5

    def chunk_step(ci, carry):
        r0 = pl.multiple_of(ci * c, c)
        qc = q_ref[pl.ds(r0, c), :].astype(F32) * q_scale
        kc = k_ref[pl.ds(r0, c), :].astype(F32)
        vc = v_ref[pl.ds(r0, c), :]
        z = _dot(small_ref[pl.ds(r0, c), :].astype(BF16), w2) + ba
        la = jax.nn.log_sigmoid(z) * (1.0 / GLA_TAU)
        hi = la.astype(BF16)
        r1 = la - hi.astype(F32)
        mid = r1.astype(BF16)
        lo = (r1 - mid.astype(F32)).astype(BF16)
        b = _dot(tri, hi) + _dot(tri, mid) + _dot(tri, lo)

        st = st_scr[...]
        inter = _dot_nt((qc * jnp.exp(b)).astype(BF16), st.astype(BF16))

        att = jnp.zeros((c, c), F32)
        for j in range(c):
            lo_r = (j // SUBLANES) * SUBLANES
            d = b[lo_r:, :] - b[j:j + 1, :]
            head = jnp.where(row_id[lo_r:lo_r + SUBLANES, :] >= j, d[:SUBLANES, :], NEG_INF)
            e = jnp.exp(jnp.concatenate([head, d[SUBLANES:, :]], axis=0) if lo_r + SUBLANES < c else head)
            col = jnp.sum(qc[lo_r:, :] * kc[j:j + 1, :] * e, axis=-1, keepdims=True)
            if lo_r:
                col = jnp.concatenate([jnp.zeros((lo_r, 1), F32), col], axis=0)
            att = jnp.where(c_i == j, col, att)
        o = inter + _dot(att.astype(BF16), vc)

        b_last = b[c - 1:c, :]
        k_dec = (kc * jnp.exp(b_last - b)).astype(BF16)
        st_scr[...] = st * jnp.exp(b_last) + _dot_tn(vc, k_dec)

        og = og_ref[pl.ds(r0, c), :].astype(F32)
        out = _rms(o, gon) * (og * jax.nn.sigmoid(og))
        o_ref[pl.ds(r0, c), :] = out.astype(o_ref.dtype)
        return carry

    lax.fori_loop(0, seq // c, chunk_step, 0)


def _gla(p, small, w2p, b_a, g_on, *, batch, seq, heads, dk, dv, q_blk, k_blk, v_blk, og_blk):
    t = p.shape[0]
    return pl.pallas_call(
        functools.partial(_gla_kernel, seq=seq, dk=dk, dv=dv, chunk=GLA_CHUNK),
        grid=(batch, heads),
        in_specs=[
            pl.BlockSpec((seq, dk), lambda b, h: (b, q_blk + h)),
            pl.BlockSpec((seq, dk), lambda b, h: (b, k_blk + h)),
            pl.BlockSpec((seq, dv), lambda b, h: (b, v_blk + h)),
            pl.BlockSpec((seq, dv), lambda b, h: (b, og_blk + h)),
            pl.BlockSpec((seq, LANES), lambda b, h: (b, 0)),
            pl.BlockSpec((LANES, dk), lambda b, h: (0, h)),
            pl.BlockSpec((1, dk), lambda b, h: (0, h)),
            pl.BlockSpec((1, dv), lambda b, h: (0, 0)),
        ],
        out_specs=pl.BlockSpec((seq, dv), lambda b, h: (b, h)),
        out_shape=jax.ShapeDtypeStruct((t, heads * dv), BF16),
        scratch_shapes=[pltpu.VMEM((dv, dk), F32)],
        compiler_params=_params(("parallel", "parallel")),
    )(p, p, p, p, small, w2p, b_a, g_on)


def _out_proj_kernel(x_ref, ga_ref, gb_ref, oa_ref, ob_ref, bg_ref, w_ref, o_ref, m_scr):
    @pl.when(pl.program_id(1) == 0)
    def _():
        bg = bg_ref[...]
        sa = jax.nn.sigmoid(ga_ref[...].astype(F32) + bg[0:1, :])
        sb = jax.nn.sigmoid(gb_ref[...].astype(F32) + bg[1:2, :])
        m_scr[...] = (sa * oa_ref[...].astype(F32) + sb * ob_ref[...].astype(F32)).astype(BF16)

    o_ref[...] = x_ref[...] + _dot(m_scr[...], w_ref[...])


def _out_proj(x, p, o_mla, o_gla, b_gate, w_out, *, row0, ga_blk, gb_blk, tm, tn):
    t, d = p.shape[0], x.shape[1]
    assert row0 % tm == 0
    b0 = row0 // tm
    return pl.pallas_call(
        _out_proj_kernel,
        grid=(t // tm, d // tn),
        in_specs=[
            pl.BlockSpec((tm, tn), lambda i, j: (i + b0, j)),
            pl.BlockSpec((tm, d), lambda i, j: (i, ga_blk)),
            pl.BlockSpec((tm, d), lambda i, j: (i, gb_blk)),
            pl.BlockSpec((tm, d), lambda i, j: (i, 0)),
            pl.BlockSpec((tm, d), lambda i, j: (i, 0)),
            pl.BlockSpec((2, d), lambda i, j: (0, 0)),
            pl.BlockSpec((d, tn), lambda i, j: (0, j)),
        ],
        out_specs=pl.BlockSpec((tm, tn), lambda i, j: (i, j)),
        out_shape=jax.ShapeDtypeStruct((t, d), F32),
        scratch_shapes=[pltpu.VMEM((tm, d), BF16)],
        compiler_params=_params(("parallel", "arbitrary")),
    )(x, p, p, o_mla, o_gla, b_gate, w_out)


def _topk_rows(s, k, payload=None):
    n = s.shape[0]
    iota = lax.broadcasted_iota(jnp.int32, s.shape, 0).astype(F32)
    kiota = lax.broadcasted_iota(jnp.int32, (k, s.shape[1]), 0)
    vals = jnp.zeros((k, s.shape[1]), F32)
    picks = jnp.zeros((k, s.shape[1]), F32)
    for r in range(k):
        m = jnp.max(s, axis=0, keepdims=True)
        idx = jnp.min(jnp.where(s == m, iota, float(n)), axis=0, keepdims=True)
        hit = iota == idx
        if payload is None:
            pick = idx
        else:
            pick = jnp.sum(jnp.where(hit, payload, 0.0), axis=0, keepdims=True)
        vals = jnp.where(kiota == r, m, vals)
        picks = jnp.where(kiota == r, pick, picks)
        s = jnp.where(hit, NEG_INF, s)
    return vals, picks


def _peer_route_kernel(x_ref, g_ref, w_ref, sk_ref, h_ref, ids_ref, gates_ref, *, heads, nkeys, half, topk):
    h = _rms(x_ref[...], g_ref[...]).astype(BF16)
    h_ref[...] = h
    qf = _dot(h, w_ref[...])
    for hd in range(heads):
        tops = []
        for part in range(2):
            o = (hd * 2 + part) * half
            qh = qf[:, o:o + half].astype(BF16)
            keys = sk_ref[(hd * 2 + part) * nkeys:(hd * 2 + part + 1) * nkeys, :]
            tops.append(_topk_rows(_dot_nt(keys, qh), topk))
        (s1, i1), (s2, i2) = tops
        assert topk == 2 * SUBLANES
        hs = SUBLANES
        pair_s = [s1[0:1, :] + s2] + [s1[a:a + 1, :] + s2[:hs, :] for a in range(1, hs)] + [s1[hs:, :] + s2[0:1, :]]
        pair_id = ([i1[0:1, :] * float(nkeys) + i2]
                   + [i1[a:a + 1, :] * float(nkeys) + i2[:hs, :] for a in range(1, hs)]
                   + [i1[hs:, :] * float(nkeys) + i2[0:1, :]])
        best_s, best_id = _topk_rows(jnp.concatenate(pair_s, axis=0), topk,
                                     payload=jnp.concatenate(pair_id, axis=0))
        e = jnp.exp(best_s - best_s[0:1, :])
        gate = e / jnp.sum(e, axis=0, keepdims=True)
        ids_ref[hd * topk:(hd + 1) * topk, :] = best_id.astype(jnp.int32)
        gates_ref[hd * topk:(hd + 1) * topk, :] = gate


def _peer_route(x, g, w_pq, sk2d, *, heads, nkeys, half, topk, tm):
    t, d = x.shape
    dq = w_pq.shape[1]
    return pl.pallas_call(
        functools.partial(_peer_route_kernel, heads=heads, nkeys=nkeys, half=half, topk=topk),
        grid=(t // tm,),
        in_specs=[
            pl.BlockSpec((tm, d), lambda i: (i, 0)),
            pl.BlockSpec((1, d), lambda i: (0, 0)),
            pl.BlockSpec((d, dq), lambda i: (0, 0)),
            pl.BlockSpec((heads * 2 * nkeys, half), lambda i: (0, 0)),
        ],
        out_specs=[
            pl.BlockSpec((tm, d), lambda i: (i, 0)),
            pl.BlockSpec((heads * topk, tm), lambda i: (0, i)),
            pl.BlockSpec((heads * topk, tm), lambda i: (0, i)),
        ],
        out_shape=[
            jax.ShapeDtypeStruct((t, d), BF16),
            jax.ShapeDtypeStruct((heads * topk, t), jnp.int32),
            jax.ShapeDtypeStruct((heads * topk, t), F32),
        ],
        compiler_params=_params(("parallel",)),
    )(x, g, w_pq, sk2d)


def _pack_expert_table(u_emb, v_emb):
    n, d = u_emb.shape
    half = d // 2
    tm = 512
    assert n % tm == 0

    def pack(w):
        bits = lax.bitcast_convert_type(w.astype(BF16).astype(F32), jnp.int32)
        return ((bits[:, :half] >> 16) & jnp.int32(0xFFFF)) | (bits[:, half:] & jnp.int32(-65536))

    def pack_kernel(u_ref, v_ref, o_ref):
        o_ref[:, :half] = pack(u_ref[...])
        o_ref[:, half:] = pack(v_ref[...])

    return pl.pallas_call(
        pack_kernel,
        grid=(n // tm,),
        in_specs=[pl.BlockSpec((tm, d), lambda i: (i, 0)), pl.BlockSpec((tm, d), lambda i: (i, 0))],
        out_specs=pl.BlockSpec((tm, d), lambda i: (i, 0)),
        out_shape=jax.ShapeDtypeStruct((n, d), jnp.int32),
        compiler_params=_params(("parallel",)),
    )(u_emb, v_emb)


def _unpack_words(w):
    lo = lax.bitcast_convert_type(w << 16, F32)
    hi = lax.bitcast_convert_type(w & jnp.int32(-65536), F32)
    return lo, hi


def _expert_mix(words, h, gate, d):
    half = d // 2
    u_lo, u_hi = _unpack_words(words[:, :half])
    act = jnp.sum(u_lo * h[:, :half] + u_hi * h[:, half:], axis=-1, keepdims=True)
    w = gate * _gelu_exact(act)
    v_lo, v_hi = _unpack_words(words[:, half:])
    return jnp.concatenate([jnp.sum(v_lo * w, axis=0, keepdims=True),
                            jnp.sum(v_hi * w, axis=0, keepdims=True)], axis=1)


def _peer_mix_kernel(ids_hbm, uv_hbm, rows_ref, x_ref, h_ref, gates_ref, o_ref, ids_smem, buf, ids_sems, row_sems,
                     *, td, ts, picks, d, n_steps, slots):
    step = pl.program_id(0)
    par = lax.rem(step, 2)
    ahead = slots - 1
    assert td % slots == 0 and td > ahead and (td + ts) % SUBLANES == 0

    def ids_copy(s, row):
        return pltpu.make_async_copy(ids_hbm.at[s], ids_smem.at[row], ids_sems.at[row])

    def issue(row, tok, slot):
        for j in range(picks):
            eid = ids_smem[row, tok * picks + j]
            pltpu.make_async_copy(uv_hbm.at[pl.ds(eid, 1), :], buf.at[slot, pl.ds(j, 1), :],
                                  row_sems.at[slot]).start()

    def wait_rows(slot):
        pltpu.make_async_copy(uv_hbm.at[pl.ds(0, picks), :], buf.at[slot], row_sems.at[slot]).wait()

    @pl.when(step == 0)
    def _():
        first = ids_copy(0, 0)
        first.start()
        first.wait()
        for u in range(ahead):
            issue(0, u, u)

    @pl.when(step + 1 < n_steps)
    def _():
        ids_copy(step + 1, 1 - par).start()

    loaded = {}

    def finish(pos, words):
        g, lane = divmod(pos, SUBLANES)
        if g not in loaded:
            loaded[g] = (gates_ref[g], h_ref[g * SUBLANES:(g + 1) * SUBLANES, :].astype(F32))
        gates, h8 = loaded[g]
        row = _expert_mix(words, h8[lane:lane + 1, :], gates[:, lane:lane + 1], d)
        o_ref[pos:pos + 1, :] = x_ref[pos:pos + 1, :] + row

    staged_done = 0
    for u in range(td):
        nxt = u + ahead
        if nxt < td:
            issue(par, nxt, nxt % slots)
        else:
            @pl.when(step + 1 < n_steps)
            def _():
                if nxt == td:
                    ids_copy(step + 1, 1 - par).wait()
                issue(1 - par, nxt - td, nxt % slots)

        staged_upto = (u + 1) * ts // td
        for k in range(staged_done, staged_upto):
            finish(td + k, rows_ref[k * picks:(k + 1) * picks, :])
        staged_done = staged_upto
        wait_rows(u % slots)
        finish(u, buf[u % slots])


def _peer_mix(ids_direct, uv, staged, x, h, gates3, *, td, ts):
    t, d = x.shape
    picks = gates3.shape[1]
    n_steps = t // (td + ts)
    assert staged.shape[0] == n_steps * ts * picks and ids_direct.shape == (n_steps, td * picks)
    slots = next(s for s in MIX_SLOTS if td % s == 0 and td > s - 1)
    return pl.pallas_call(
        functools.partial(_peer_mix_kernel, td=td, ts=ts, picks=picks, d=d, n_steps=n_steps, slots=slots),
        grid=(n_steps,),
        input_output_aliases={3: 0},
        in_specs=[
            pl.BlockSpec(memory_space=pl.ANY),
            pl.BlockSpec(memory_space=pl.ANY),
            pl.BlockSpec((ts * picks, d), lambda i: (i, 0)),
            pl.BlockSpec((td + ts, d), lambda i: (i, 0)),
            pl.BlockSpec((td + ts, d), lambda i: (i, 0)),
            pl.BlockSpec(((td + ts) // SUBLANES, picks, SUBLANES), lambda i: (i, 0, 0)),
        ],
        out_specs=pl.BlockSpec((td + ts, d), lambda i: (i, 0)),
        out_shape=jax.ShapeDtypeStruct((t, d), F32),
        scratch_shapes=[
            pltpu.SMEM((2, td * picks), jnp.int32),
            pltpu.VMEM((slots, picks, d), jnp.int32),
            pltpu.SemaphoreType.DMA((2,)),
            pltpu.SemaphoreType.DMA((slots,)),
        ],
        compiler_params=_params(("arbitrary",)),
    )(ids_direct, uv, staged, x, h, gates3)


def _sc_gather_rows(table, idx, *, chunk):
    n_rows = idx.shape[0]
    d = table.shape[1]
    workers = SC_CORES * SC_SUBCORES
    assert n_rows % (workers * 2 * chunk) == 0 and chunk % SUBLANES == 0 and chunk <= LANES
    rows_per_worker = n_rows // workers
    n_pairs = rows_per_worker // (2 * chunk)
    mesh = plsc.VectorSubcoreMesh(core_axis_name="c", subcore_axis_name="s")

    @functools.partial(
        pl.kernel, mesh=mesh,
        out_type=jax.ShapeDtypeStruct((n_rows, d), table.dtype),
        scratch_types=[
            pltpu.VMEM((chunk,), jnp.int32), pltpu.VMEM((chunk,), jnp.int32),
            pltpu.VMEM((chunk, d), table.dtype), pltpu.VMEM((chunk, d), table.dtype),
            pltpu.SemaphoreType.DMA, pltpu.SemaphoreType.DMA,
            pltpu.SemaphoreType.DMA, pltpu.SemaphoreType.DMA,
        ],
    )
    def gather_kernel(table_hbm, idx_hbm, out_hbm, idx0, idx1, rows0, rows1, gsem0, gsem1, wsem0, wsem1):
        idx_v, rows_v, gsem, wsem = (idx0, idx1), (rows0, rows1), (gsem0, gsem1), (wsem0, wsem1)
        worker = lax.axis_index("s") * SC_CORES + lax.axis_index("c")
        base = worker * rows_per_worker

        def out_rows(c):
            return pl.ds(pl.multiple_of(base + c * chunk, SUBLANES), chunk)

        def load_idx(slot, c):
            pltpu.sync_copy(idx_hbm.at[out_rows(c)], idx_v[slot])

        def gather(slot):
            return pltpu.make_async_copy(table_hbm.at[idx_v[slot]], rows_v[slot], gsem[slot])

        def writeout(slot, c):
            return pltpu.make_async_copy(rows_v[slot], out_hbm.at[out_rows(c)], wsem[slot])

        load_idx(0, 0)
        gather(0).start()

        @pl.loop(0, n_pairs)
        def _(p):
            c0 = 2 * p

            @pl.when(p > 0)
            def _():
                writeout(1, c0 - 1).wait()

            load_idx(1, c0 + 1)
            gather(1).start()
            gather(0).wait()
            writeout(0, c0).start()

            @pl.when(p + 1 < n_pairs)
            def _():
                load_idx(0, c0 + 2)
                writeout(0, c0).wait()
                gather(0).start()

            gather(1).wait()
            writeout(1, c0 + 1).start()

        writeout(0, 2 * n_pairs - 2).wait()
        writeout(1, 2 * n_pairs - 1).wait()

    return gather_kernel(table, idx)


def _pad_cols(w, width):
    return jnp.pad(w, ((0, 0), (0, width - w.shape[1])))


def _prepare_layer(g_norm_mix, w_in, b_gate, g_cq, w_uq, g_ckv, w_ukv, g_qn, g_qr, g_kn, g_kr,
                   w_a2, b_a, g_gla_out, w_out, g_norm_ffn, w_pq, sub_keys, u_emb, v_emb):
    d = w_in.shape[0]
    q_rank, kv_rank = g_cq.shape[0], g_ckv.shape[0]
    nope, rope = g_qn.shape[0], g_qr.shape[0]
    mla_heads = w_uq.shape[1] // (nope + rope)
    mla_v = w_ukv.shape[1] // mla_heads - nope
    gate_rank, gla_dk_all = w_a2.shape
    gla_dv = g_gla_out.shape[0]
    gla_heads = d // gla_dv
    gla_dk = gla_dk_all // gla_heads
    peer_heads, _, nkeys, half = sub_keys.shape
    assert nope == LANES and mla_v == LANES and rope <= LANES and rope % 2 == 0
    assert mla_heads * mla_v == d and gla_heads * gla_dv == d
    assert rope + gate_rank <= LANES and nkeys == LANES and half == LANES

    widths = (q_rank, kv_rank, rope, gla_dk_all, gla_dk_all, d, gate_rank, d, d, d)
    offs = [0]
    for wd in widths:
        offs.append(offs[-1] + wd)
    assert offs[-1] == w_in.shape[1]
    seg = lambda i: w_in[:, offs[i]:offs[i + 1]]
    w_main = jnp.concatenate([seg(5), seg(7), seg(8), seg(9), seg(3), seg(4), seg(0), seg(1)], axis=1).astype(BF16)
    w_small = _pad_cols(jnp.concatenate([seg(2), seg(6)], axis=1), LANES).astype(BF16)
    cq_off = 4 * d + 2 * gla_dk_all
    assert cq_off % q_rank == 0 and (cq_off + q_rank) % kv_rank == 0
    inv_freq = ROPE_THETA ** (-jnp.arange(0, rope, 2, dtype=F32) / rope)
    scale = (nope + rope) ** -0.5
    return dict(
        dims=dict(q_rank=q_rank, kv_rank=kv_rank, nope=nope, rope=rope, mla_heads=mla_heads, mla_v=mla_v,
                  gla_heads=gla_heads, gla_dk=gla_dk, gla_dv=gla_dv, peer_heads=peer_heads, nkeys=nkeys, half=half,
                  v_blk=0, og_blk=d // gla_dv, ga_blk=2, gb_blk=3, q_blk=4 * d // gla_dk,
                  k_blk=4 * d // gla_dk + gla_heads, cq_blk=cq_off // q_rank, ckv_blk=(cq_off + q_rank) // kv_rank),
        g_norm_mix=g_norm_mix[None, :], w_main=w_main, w_small=w_small,
        invf=_pad_cols(jnp.concatenate([inv_freq, inv_freq])[None, :], LANES),
        gq=_pad_cols(jnp.concatenate([g_qn, g_qr])[None, :] * scale, nope + LANES),
        gkr=_pad_cols(g_kr[None, :], LANES), g_cq=g_cq[None, :], g_ckv=g_ckv[None, :], g_kn=g_kn[None, :],
        w_uq=jnp.pad(w_uq.reshape(q_rank, mla_heads, nope + rope),
                     ((0, 0), (0, 0), (0, LANES - rope))).reshape(q_rank, -1).astype(BF16),
        w_ukv=w_ukv.astype(BF16),
        w2p=jnp.zeros((LANES, gla_dk_all), F32).at[rope:rope + gate_rank].set(w_a2).astype(BF16),
        b_a=b_a[None, :], g_on=g_gla_out[None, :], b_gate=b_gate, w_out=w_out.astype(BF16),
        g_norm_ffn=g_norm_ffn[None, :], w_pq=w_pq.astype(BF16),
        sk2d=sub_keys.reshape(peer_heads * 2 * nkeys, half).astype(BF16),
        uv=_pack_expert_table(u_emb, v_emb),
    )


def _mixers_and_route(x_all, row0, pos, w, *, batch, seq, tiles):
    t = batch * seq
    dm = w["dims"]
    p, small = _in_proj(x_all, w["g_norm_mix"], w["w_main"], w["w_small"], row0=row0, t=t,
                        tm=tiles["in_tm"], tn=tiles["in_tn"])
    q = _mla_q(p, dm["cq_blk"], w["g_cq"], w["w_uq"], w["gq"], pos, w["invf"], heads=dm["mla_heads"],
               rank=dm["q_rank"], nope=dm["nope"], rope=dm["rope"], tm=tiles["mla_tm"])
    k, v = _mla_kv(p, dm["ckv_blk"], w["g_ckv"], w["w_ukv"], w["g_kn"], small, w["gkr"], pos, w["invf"],
                   heads=dm["mla_heads"], rank=dm["kv_rank"], nope=dm["nope"], rope=dm["rope"], dv=dm["mla_v"],
                   tm=tiles["mla_tm"])
    o_mla = _mla_attn(q, k, v, batch=batch, seq=seq, heads=dm["mla_heads"], dk=dm["nope"] + LANES, dv=dm["mla_v"],
                      tq=tiles["attn_tq"], tk=tiles["attn_tk"])
    o_gla = _gla(p, small, w["w2p"], w["b_a"], w["g_on"], batch=batch, seq=seq, heads=dm["gla_heads"],
                 dk=dm["gla_dk"], dv=dm["gla_dv"], q_blk=dm["q_blk"], k_blk=dm["k_blk"], v_blk=dm["v_blk"],
                 og_blk=dm["og_blk"])
    x2 = _out_proj(x_all, p, o_mla, o_gla, w["b_gate"], w["w_out"], row0=row0, ga_blk=dm["ga_blk"],
                   gb_blk=dm["gb_blk"], tm=tiles["out_tm"], tn=tiles["out_tn"])
    h2, ids_t, gates_t = _peer_route(x2, w["g_norm_ffn"], w["w_pq"], w["sk2d"], heads=dm["peer_heads"],
                                     nkeys=dm["nkeys"], half=dm["half"], topk=PEER_TOPK, tm=tiles["route_tm"])
    picks = dm["peer_heads"] * PEER_TOPK
    gates3 = gates_t.reshape(picks, t // SUBLANES, SUBLANES).transpose(1, 0, 2)
    return x2, h2, ids_t.T, gates3


def _layer(x2, pos, w, *, batch, seq, tiles):
    t, d = x2.shape
    splits = tiles["mix_split"]
    groups = len(splits)
    tg, bg = t // groups, batch // groups
    routed = [_mixers_and_route(x2, g * tg, pos[g * tg:(g + 1) * tg], w, batch=bg, seq=seq, tiles=tiles)
              for g in range(groups)]
    out = []
    for (td, ts), (xg, hg, ids_tok, gates3) in zip(splits, routed):
        picks = ids_tok.shape[1]
        ids3 = ids_tok.reshape(tg // (td + ts), td + ts, picks)
        staged = _sc_gather_rows(w["uv"], ids3[:, td:].reshape(-1), chunk=tiles["sc_chunk"])
        out.append(_peer_mix(ids3[:, :td].reshape(-1, td * picks), w["uv"], staged, xg, hg, gates3, td=td, ts=ts))
    return jnp.concatenate(out, axis=0)


_TILES = dict(in_tm=2048, in_tn=512, mla_tm=2048, attn_tq=1024, attn_tk=1024,
              out_tm=1024, out_tn=512, route_tm=256, mix_split=((4, 12),) * 5 + ((24, 8),) * 2 + ((4, 12),), sc_chunk=16)


def kernel(x, positions, g_norm_mix, w_in, b_gate, g_cq, w_uq, g_ckv, w_ukv, g_qn, g_qr, g_kn, g_kr,
           w_a2, b_a, g_gla_out, w_out, g_norm_ffn, w_pq, sub_keys, u_emb, v_emb, tiles=None):
    tiles = _TILES if tiles is None else tiles
    batch, seq, d = x.shape
    x2 = x.reshape(batch * seq, d)
    pos = positions.reshape(batch * seq, 1)
    for l in range(g_norm_mix.shape[0]):
        w = _prepare_layer(g_norm_mix[l], w_in[l], b_gate[l], g_cq[l], w_uq[l], g_ckv[l], w_ukv[l], g_qn[l],
                           g_qr[l], g_kn[l], g_kr[l], w_a2[l], b_a[l], g_gla_out[l], w_out[l], g_norm_ffn[l],
                           w_pq[l], sub_keys[l], u_emb[l], v_emb[l])
        x2 = _layer(x2, pos, w, batch=batch, seq=seq, tiles=tiles)
    return x2.reshape(batch, seq, d)
```

```python
import functools

import jax
import jax.numpy as jnp
from jax import lax
from jax.experimental import pallas as pl
from jax.experimental.pallas import tpu as pltpu
from jax.experimental.pallas import tpu_sc as plsc

EPS = 1e-6
ROPE_THETA = 10000.0
GLA_TAU = 16.0
GLA_CHUNK = 64
GLA_UNROLL = 2
PEER_TOPK = 16

LANES = 128
SUBLANES = 8
VMEM_LIMIT_BYTES = 56 * 1024 * 1024
MIX_SLOTS = 4
SC_CORES = 2
SC_SUBCORES = 16

F32 = jnp.float32
BF16 = jnp.bfloat16
NEG_INF = float("-inf")


def _params(semantics):
    return pltpu.CompilerParams(dimension_semantics=semantics, vmem_limit_bytes=VMEM_LIMIT_BYTES)


def _rms(x, gain, n=None):
    ss = jnp.sum(x * x, axis=-1, keepdims=True)
    n = x.shape[-1] if n is None else n
    return x * lax.rsqrt(ss * (1.0 / n) + EPS) * gain


def _gelu_exact(x):
    return 0.5 * x * (1.0 + lax.erf(x * (0.5 ** 0.5)))


def _dot(a, b):
    return jnp.dot(a, b, preferred_element_type=F32)


def _dot_nt(a, b):
    return lax.dot_general(a, b, (((1,), (1,)), ((), ())), preferred_element_type=F32)


def _dot_tn(a, b):
    return lax.dot_general(a, b, (((0,), (0,)), ((), ())), preferred_element_type=F32)


def _in_proj_kernel(x_ref, g_ref, w_ref, ws_ref, p_ref, ps_ref, h_scr):
    @pl.when(pl.program_id(1) == 0)
    def _():
        h = _rms(x_ref[...], g_ref[...]).astype(BF16)
        h_scr[...] = h
        ps_ref[...] = _dot(h, ws_ref[...])

    p_ref[...] = _dot(h_scr[...], w_ref[...]).astype(p_ref.dtype)


def _in_proj(x, g, w_main, w_small, *, row0, t, tm, tn):
    d = x.shape[1]
    n = w_main.shape[1]
    assert row0 % tm == 0
    b0 = row0 // tm
    x_mode = pl.Buffered(1) if t == tm else None
    return pl.pallas_call(
        _in_proj_kernel,
        grid=(t // tm, n // tn),
        in_specs=[
            pl.BlockSpec((tm, d), lambda i, j: (i + b0, 0), pipeline_mode=x_mode),
            pl.BlockSpec((1, d), lambda i, j: (0, 0)),
            pl.BlockSpec((d, tn), lambda i, j: (0, j)),
            pl.BlockSpec((d, LANES), lambda i, j: (0, 0)),
        ],
        out_specs=[
            pl.BlockSpec((tm, tn), lambda i, j: (i, j)),
            pl.BlockSpec((tm, LANES), lambda i, j: (i, 0)),
        ],
        out_shape=[
            jax.ShapeDtypeStruct((t, n), BF16),
            jax.ShapeDtypeStruct((t, LANES), F32),
        ],
        scratch_shapes=[pltpu.VMEM((tm, d), BF16)],
        compiler_params=_params(("parallel", "arbitrary")),
    )(x, g, w_main, w_small)


def _rope_tables(pos_ref, invf_ref, rope):
    ang = pos_ref[...].astype(F32) * invf_ref[...]
    cos, sin = jnp.cos(ang), jnp.sin(ang)
    lane = lax.broadcasted_iota(jnp.int32, ang.shape, 1)
    half = rope // 2
    c = jnp.where(lane < rope, cos, 0.0)
    s_lo = jnp.where(lane < half, -sin, 0.0)
    s_hi = jnp.where(lane < half, 0.0, jnp.where(lane < rope, sin, 0.0))
    return c, s_lo, s_hi


def _apply_rope(pe, c, s_lo, s_hi, rope):
    half = rope // 2
    from_hi = pltpu.roll(pe, LANES - half, 1)
    from_lo = pltpu.roll(pe, half, 1)
    return pe * c + from_hi * s_lo + from_lo * s_hi


def _mla_q_kernel(cq_ref, gcq_ref, w_ref, gq_ref, pos_ref, invf_ref, q_ref,
                  h_scr, c_scr, slo_scr, shi_scr, *, nope, rope):
    @pl.when(pl.program_id(1) == 0)
    def _():
        h_scr[...] = _rms(cq_ref[...].astype(F32), gcq_ref[...]).astype(BF16)
        c, s_lo, s_hi = _rope_tables(pos_ref, invf_ref, rope)
        c_scr[...] = c
        slo_scr[...] = s_lo
        shi_scr[...] = s_hi

    y = _dot(h_scr[...], w_ref[...])
    g = gq_ref[...]
    qn = _rms(y[:, :nope], g[:, :nope])
    pe = _rms(y[:, nope:], g[:, nope:], n=rope)
    pe = _apply_rope(pe, c_scr[...], slo_scr[...], shi_scr[...], rope)
    q_ref[:, :nope] = qn.astype(q_ref.dtype)
    q_ref[:, nope:] = pe.astype(q_ref.dtype)


def _mla_q(p, cq_blk, g_cq, w_uq_p, gq, pos, invf, *, heads, rank, nope, rope, tm):
    t = p.shape[0]
    hw = nope + LANES
    return pl.pallas_call(
        functools.partial(_mla_q_kernel, nope=nope, rope=rope),
        grid=(t // tm, heads),
        in_specs=[
            pl.BlockSpec((tm, rank), lambda i, j: (i, cq_blk)),
            pl.BlockSpec((1, rank), lambda i, j: (0, 0)),
            pl.BlockSpec((rank, hw), lambda i, j: (0, j)),
            pl.BlockSpec((1, hw), lambda i, j: (0, 0)),
            pl.BlockSpec((tm, 1), lambda i, j: (i, 0)),
            pl.BlockSpec((1, LANES), lambda i, j: (0, 0)),
        ],
        out_specs=pl.BlockSpec((tm, hw), lambda i, j: (i, j)),
        out_shape=jax.ShapeDtypeStruct((t, heads * hw), BF16),
        scratch_shapes=[
            pltpu.VMEM((tm, rank), BF16),
            pltpu.VMEM((tm, LANES), F32),
            pltpu.VMEM((tm, LANES), F32),
            pltpu.VMEM((tm, LANES), F32),
        ],
        compiler_params=_params(("parallel", "arbitrary")),
    )(p, g_cq, w_uq_p, gq, pos, invf)


def _mla_kv_kernel(ckv_ref, gckv_ref, w_ref, gkn_ref, small_ref, gkr_ref, pos_ref, invf_ref,
                   k_ref, v_ref, h_scr, kpe_scr, *, nope, rope):
    @pl.when(pl.program_id(1) == 0)
    def _():
        h_scr[...] = _rms(ckv_ref[...].astype(F32), gckv_ref[...]).astype(BF16)
        c, s_lo, s_hi = _rope_tables(pos_ref, invf_ref, rope)
        sm = small_ref[...]
        lane = lax.broadcasted_iota(jnp.int32, sm.shape, 1)
        pe = _rms(jnp.where(lane < rope, sm, 0.0), gkr_ref[...], n=rope)
        kpe_scr[...] = _apply_rope(pe, c, s_lo, s_hi, rope).astype(BF16)

    y = _dot(h_scr[...], w_ref[...])
    k_ref[:, :nope] = _rms(y[:, :nope], gkn_ref[...]).astype(k_ref.dtype)
    k_ref[:, nope:] = kpe_scr[...]
    v_ref[...] = y[:, nope:].astype(v_ref.dtype)


def _mla_kv(p, ckv_blk, g_ckv, w_ukv, g_kn, small, gkr, pos, invf, *, heads, rank, nope, rope, dv, tm):
    t = p.shape[0]
    kw = nope + LANES
    return pl.pallas_call(
        functools.partial(_mla_kv_kernel, nope=nope, rope=rope),
        grid=(t // tm, heads),
        in_specs=[
            pl.BlockSpec((tm, rank), lambda i, j: (i, ckv_blk)),
            pl.BlockSpec((1, rank), lambda i, j: (0, 0)),
            pl.BlockSpec((rank, nope + dv), lambda i, j: (0, j)),
            pl.BlockSpec((1, nope), lambda i, j: (0, 0)),
            pl.BlockSpec((tm, LANES), lambda i, j: (i, 0)),
            pl.BlockSpec((1, LANES), lambda i, j: (0, 0)),
            pl.BlockSpec((tm, 1), lambda i, j: (i, 0)),
            pl.BlockSpec((1, LANES), lambda i, j: (0, 0)),
        ],
        out_specs=[
            pl.BlockSpec((tm, kw), lambda i, j: (i, j)),
            pl.BlockSpec((tm, dv), lambda i, j: (i, j)),
        ],
        out_shape=[
            jax.ShapeDtypeStruct((t, heads * kw), BF16),
            jax.ShapeDtypeStruct((t, heads * dv), BF16),
        ],
        scratch_shapes=[pltpu.VMEM((tm, rank), BF16), pltpu.VMEM((tm, LANES), BF16)],
        compiler_params=_params(("parallel", "arbitrary")),
    )(p, g_ckv, w_ukv, g_kn, small, gkr, pos, invf)


def _attn_kernel(q_ref, k_ref, v_ref, o_ref, *, tq, tk):
    qi = pl.program_id(2)
    q = q_ref[...]
    dv = v_ref.shape[1]
    row = qi * tq + lax.broadcasted_iota(jnp.int32, (tq, tk), 0)
    col0 = lax.broadcasted_iota(jnp.int32, (tq, tk), 1)

    def body(kb, carry, masked):
        m, l, acc = carry
        k0 = pl.multiple_of(kb * tk, tk)
        s = _dot_nt(q, k_ref[pl.ds(k0, tk), :])
        if masked:
            s = jnp.where(col0 + k0 <= row, s, NEG_INF)
        m_new = jnp.maximum(m, jnp.max(s, axis=-1, keepdims=True))
        alpha = jnp.exp(m - m_new)
        pr = jnp.exp(s - m_new)
        l = alpha * l + jnp.sum(pr, axis=-1, keepdims=True)
        acc = alpha * acc + _dot(pr.astype(BF16), v_ref[pl.ds(k0, tk), :])
        return m_new, l, acc

    init = (jnp.full((tq, 1), NEG_INF, F32), jnp.zeros((tq, 1), F32), jnp.zeros((tq, dv), F32))
    n_below = (qi * tq) // tk
    nkb = ((qi + 1) * tq + tk - 1) // tk
    carry = lax.fori_loop(0, n_below, functools.partial(body, masked=False), init)
    _, l, acc = lax.fori_loop(n_below, nkb, functools.partial(body, masked=True), carry)
    o_ref[...] = (acc / l).astype(o_ref.dtype)


def _mla_attn(q, k, v, *, batch, seq, heads, dk, dv, tq, tk):
    t = q.shape[0]
    nq = seq // tq
    return pl.pallas_call(
        functools.partial(_attn_kernel, tq=tq, tk=tk),
        grid=(batch, heads, nq),
        in_specs=[
            pl.BlockSpec((tq, dk), lambda b, h, i: (b * nq + i, h)),
            pl.BlockSpec((seq, dk), lambda b, h, i: (b, h)),
            pl.BlockSpec((seq, dv), lambda b, h, i: (b, h)),
        ],
        out_specs=pl.BlockSpec((tq, dv), lambda b, h, i: (b * nq + i, h)),
        out_shape=jax.ShapeDtypeStruct((t, heads * dv), BF16),
        compiler_params=_params(("parallel", "parallel", "arbitrary")),
    )(q, k, v)


def _gla_kernel(q_ref, k_ref, v_ref, og_ref, small_ref, w2_ref, ba_ref, gon_ref, o_ref, st_scr,
                *, seq, dk, dv, chunk):
    c = chunk
    st_scr[...] = jnp.zeros_like(st_scr)
    r_i = lax.broadcasted_iota(jnp.int32, (c, c), 0)
    c_i = lax.broadcasted_iota(jnp.int32, (c, c), 1)
    tri = jnp.where(c_i <= r_i, 1.0, 0.0).astype(BF16)
    row_id = lax.broadcasted_iota(jnp.int32, (c, 1), 0)
    w2 = w2_ref[...]
    ba = ba_ref[...]
    gon = gon_ref[...]
    q_scale = dk ** -0.5

    def chunk_step(ci, carry):
        r0 = pl.multiple_of(ci * c, c)
        qc = q_ref[pl.ds(r0, c), :].astype(F32) * q_scale
        kc = k_ref[pl.ds(r0, c), :].astype(F32)
        vc = v_ref[pl.ds(r0, c), :]
        z = _dot(small_ref[pl.ds(r0, c), :].astype(BF16), w2) + ba
        la = jax.nn.log_sigmoid(z) * (1.0 / GLA_TAU)
        hi = la.astype(BF16)
        r1 = la - hi.astype(F32)
        mid = r1.astype(BF16)
        lo = (r1 - mid.astype(F32)).astype(BF16)
        b = _dot(tri, hi) + _dot(tri, mid) + _dot(tri, lo)

        st = st_scr[...]
        inter = _dot_nt((qc * jnp.exp(b)).astype(BF16), st.astype(BF16))

        att = jnp.zeros((c, c), F32)
        for j in range(c):
            lo_r = (j // SUBLANES) * SUBLANES
            d = b[lo_r:, :] - b[j:j + 1, :]
            head = jnp.where(row_id[lo_r:lo_r + SUBLANES, :] >= j, d[:SUBLANES, :], NEG_INF)
            e = jnp.exp(jnp.concatenate([head, d[SUBLANES:, :]], axis=0) if lo_r + SUBLANES < c else head)
            col = jnp.sum(qc[lo_r:, :] * kc[j:j + 1, :] * e, axis=-1, keepdims=True)
            if lo_r:
                col = jnp.concatenate([jnp.zeros((lo_r, 1), F32), col], axis=0)
            att = jnp.where(c_i == j, col, att)
        o = inter + _dot(att.astype(BF16), vc)

        b_last = b[c - 1:c, :]
        k_dec = (kc * jnp.exp(b_last - b)).astype(BF16)
        st_scr[...] = st * jnp.exp(b_last) + _dot_tn(vc, k_dec)

        og = og_ref[pl.ds(r0, c), :].astype(F32)
        out = _rms(o, gon) * (og * jax.nn.sigmoid(og))
        o_ref[pl.ds(r0, c), :] = out.astype(o_ref.dtype)
        return carry

    def chunk_group(gi, carry):
        for k in range(GLA_UNROLL):
            carry = chunk_step(GLA_UNROLL * gi + k, carry)
        return carry

    assert (seq // c) % GLA_UNROLL == 0
    lax.fori_loop(0, seq // (GLA_UNROLL * c), chunk_group, 0)


def _gla(p, small, w2p, b_a, g_on, *, batch, seq, heads, dk, dv, q_blk, k_blk, v_blk, og_blk):
    t = p.shape[0]
    return pl.pallas_call(
        functools.partial(_gla_kernel, seq=seq, dk=dk, dv=dv, chunk=GLA_CHUNK),
        grid=(batch, heads),
        in_specs=[
            pl.BlockSpec((seq, dk), lambda b, h: (b, q_blk + h)),
            pl.BlockSpec((seq, dk), lambda b, h: (b, k_blk + h)),
            pl.BlockSpec((seq, dv), lambda b, h: (b, v_blk + h)),
            pl.BlockSpec((seq, dv), lambda b, h: (b, og_blk + h)),
            pl.BlockSpec((seq, LANES), lambda b, h: (b, 0)),
            pl.BlockSpec((LANES, dk), lambda b, h: (0, h)),
            pl.BlockSpec((1, dk), lambda b, h: (0, h)),
            pl.BlockSpec((1, dv), lambda b, h: (0, 0)),
        ],
        out_specs=pl.BlockSpec((seq, dv), lambda b, h: (b, h)),
        out_shape=jax.ShapeDtypeStruct((t, heads * dv), BF16),
        scratch_shapes=[pltpu.VMEM((dv, dk), F32)],
        compiler_params=_params(("parallel", "parallel")),
    )(p, p, p, p, small, w2p, b_a, g_on)


def _out_proj_kernel(x_ref, ga_ref, gb_ref, oa_ref, ob_ref, bg_ref, w_ref, o_ref, m_scr):
    @pl.when(pl.program_id(1) == 0)
    def _():
        bg = bg_ref[...]
        sa = jax.nn.sigmoid(ga_ref[...].astype(F32) + bg[0:1, :])
        sb = jax.nn.sigmoid(gb_ref[...].astype(F32) + bg[1:2, :])
        m_scr[...] = (sa * oa_ref[...].astype(F32) + sb * ob_ref[...].astype(F32)).astype(BF16)

    o_ref[...] = x_ref[...] + _dot(m_scr[...], w_ref[...])


def _out_proj(x, p, o_mla, o_gla, b_gate, w_out, *, row0, ga_blk, gb_blk, tm, tn):
    t, d = p.shape[0], x.shape[1]
    assert row0 % tm == 0
    b0 = row0 // tm
    return pl.pallas_call(
        _out_proj_kernel,
        grid=(t // tm, d // tn),
        in_specs=[
            pl.BlockSpec((tm, tn), lambda i, j: (i + b0, j)),
            pl.BlockSpec((tm, d), lambda i, j: (i, ga_blk)),
            pl.BlockSpec((tm, d), lambda i, j: (i, gb_blk)),
            pl.BlockSpec((tm, d), lambda i, j: (i, 0)),
            pl.BlockSpec((tm, d), lambda i, j: (i, 0)),
            pl.BlockSpec((2, d), lambda i, j: (0, 0)),
            pl.BlockSpec((d, tn), lambda i, j: (0, j)),
        ],
        out_specs=pl.BlockSpec((tm, tn), lambda i, j: (i, j)),
        out_shape=jax.ShapeDtypeStruct((t, d), F32),
        scratch_shapes=[pltpu.VMEM((tm, d), BF16)],
        compiler_params=_params(("parallel", "arbitrary")),
    )(x, p, p, o_mla, o_gla, b_gate, w_out)


def _topk_rows(s, k, payload=None):
    n = s.shape[0]
    iota = lax.broadcasted_iota(jnp.int32, s.shape, 0).astype(F32)
    kiota = lax.broadcasted_iota(jnp.int32, (k, s.shape[1]), 0)
    vals = jnp.zeros((k, s.shape[1]), F32)
    picks = jnp.zeros((k, s.shape[1]), F32)
    for r in range(k):
        m = jnp.max(s, axis=0, keepdims=True)
        idx = jnp.min(jnp.where(s == m, iota, float(n)), axis=0, keepdims=True)
        hit = iota == idx
        if payload is None:
            pick = idx
        else:
            pick = jnp.sum(jnp.where(hit, payload, 0.0), axis=0, keepdims=True)
        vals = jnp.where(kiota == r, m, vals)
        picks = jnp.where(kiota == r, pick, picks)
        s = jnp.where(hit, NEG_INF, s)
    return vals, picks


def _peer_route_kernel(x_ref, g_ref, w_ref, sk_ref, h_ref, ids_ref, gates_ref, *, heads, nkeys, half, topk):
    h = _rms(x_ref[...], g_ref[...]).astype(BF16)
    h_ref[...] = h
    qf = _dot(h, w_ref[...])
    for hd in range(heads):
        tops = []
        for part in range(2):
            o = (hd * 2 + part) * half
            qh = qf[:, o:o + half].astype(BF16)
            keys = sk_ref[(hd * 2 + part) * nkeys:(hd * 2 + part + 1) * nkeys, :]
            tops.append(_topk_rows(_dot_nt(keys, qh), topk))
        (s1, i1), (s2, i2) = tops
        assert topk == 2 * SUBLANES
        hs = SUBLANES
        pair_s = [s1[0:1, :] + s2] + [s1[a:a + 1, :] + s2[:hs, :] for a in range(1, hs)] + [s1[hs:, :] + s2[0:1, :]]
        pair_id = ([i1[0:1, :] * float(nkeys) + i2]
                   + [i1[a:a + 1, :] * float(nkeys) + i2[:hs, :] for a in range(1, hs)]
                   + [i1[hs:, :] * float(nkeys) + i2[0:1, :]])
        best_s, best_id = _topk_rows(jnp.concatenate(pair_s, axis=0), topk,
                                     payload=jnp.concatenate(pair_id, axis=0))
        e = jnp.exp(best_s - best_s[0:1, :])
        gate = e / jnp.sum(e, axis=0, keepdims=True)
        ids_ref[hd * topk:(hd + 1) * topk, :] = best_id.astype(jnp.int32)
        gates_ref[hd * topk:(hd + 1) * topk, :] = gate


def _peer_route(x, g, w_pq, sk2d, *, heads, nkeys, half, topk, tm):
    t, d = x.shape
    dq = w_pq.shape[1]
    return pl.pallas_call(
        functools.partial(_peer_route_kernel, heads=heads, nkeys=nkeys, half=half, topk=topk),
        grid=(t // tm,),
        in_specs=[
            pl.BlockSpec((tm, d), lambda i: (i, 0)),
            pl.BlockSpec((1, d), lambda i: (0, 0)),
            pl.BlockSpec((d, dq), lambda i: (0, 0)),
            pl.BlockSpec((heads * 2 * nkeys, half), lambda i: (0, 0)),
        ],
        out_specs=[
            pl.BlockSpec((tm, d), lambda i: (i, 0)),
            pl.BlockSpec((heads * topk, tm), lambda i: (0, i)),
            pl.BlockSpec((heads * topk, tm), lambda i: (0, i)),
        ],
        out_shape=[
            jax.ShapeDtypeStruct((t, d), BF16),
            jax.ShapeDtypeStruct((heads * topk, t), jnp.int32),
            jax.ShapeDtypeStruct((heads * topk, t), F32),
        ],
        compiler_params=_params(("parallel",)),
    )(x, g, w_pq, sk2d)


def _pack_expert_table(u_emb, v_emb):
    n, d = u_emb.shape
    half = d // 2
    tm = 512
    assert n % tm == 0

    def pack(w):
        bits = lax.bitcast_convert_type(w.astype(BF16).astype(F32), jnp.int32)
        return ((bits[:, :half] >> 16) & jnp.int32(0xFFFF)) | (bits[:, half:] & jnp.int32(-65536))

    def pack_kernel(u_ref, v_ref, o_ref):
        o_ref[:, :half] = pack(u_ref[...])
        o_ref[:, half:] = pack(v_ref[...])

    return pl.pallas_call(
        pack_kernel,
        grid=(n // tm,),
        in_specs=[pl.BlockSpec((tm, d), lambda i: (i, 0)), pl.BlockSpec((tm, d), lambda i: (i, 0))],
        out_specs=pl.BlockSpec((tm, d), lambda i: (i, 0)),
        out_shape=jax.ShapeDtypeStruct((n, d), jnp.int32),
        compiler_params=_params(("parallel",)),
    )(u_emb, v_emb)


def _unpack_words(w):
    lo = lax.bitcast_convert_type(w << 16, F32)
    hi = lax.bitcast_convert_type(w & jnp.int32(-65536), F32)
    return lo, hi


def _expert_mix(words, h, gate, d):
    half = d // 2
    u_lo, u_hi = _unpack_words(words[:, :half])
    act = jnp.sum(u_lo * h[:, :half] + u_hi * h[:, half:], axis=-1, keepdims=True)
    w = gate * _gelu_exact(act)
    v_lo, v_hi = _unpack_words(words[:, half:])
    return jnp.concatenate([jnp.sum(v_lo * w, axis=0, keepdims=True),
                            jnp.sum(v_hi * w, axis=0, keepdims=True)], axis=1)


def _peer_mix_kernel(ids_hbm, uv_hbm, rows_ref, x_ref, h_ref, gates_ref, o_ref, ids_smem, buf, ids_sems, row_sems,
                     *, td, ts, picks, d, n_steps):
    step = pl.program_id(0)
    par = lax.rem(step, 2)
    ahead = MIX_SLOTS - 1
    assert td % MIX_SLOTS == 0 and td > ahead and (td + ts) % SUBLANES == 0

    def ids_copy(s, row):
        return pltpu.make_async_copy(ids_hbm.at[s], ids_smem.at[row], ids_sems.at[row])

    def issue(row, tok, slot):
        for j in range(picks):
            eid = ids_smem[row, tok * picks + j]
            pltpu.make_async_copy(uv_hbm.at[pl.ds(eid, 1), :], buf.at[slot, pl.ds(j, 1), :],
                                  row_sems.at[slot]).start()

    def wait_rows(slot):
        pltpu.make_async_copy(uv_hbm.at[pl.ds(0, picks), :], buf.at[slot], row_sems.at[slot]).wait()

    @pl.when(step == 0)
    def _():
        first = ids_copy(0, 0)
        first.start()
        first.wait()
        for u in range(ahead):
            issue(0, u, u)

    @pl.when(step + 1 < n_steps)
    def _():
        ids_copy(step + 1, 1 - par).start()

    loaded = {}

    def finish(pos, words):
        g, lane = divmod(pos, SUBLANES)
        if g not in loaded:
            loaded[g] = (gates_ref[g], h_ref[g * SUBLANES:(g + 1) * SUBLANES, :].astype(F32))
        gates, h8 = loaded[g]
        row = _expert_mix(words, h8[lane:lane + 1, :], gates[:, lane:lane + 1], d)
        o_ref[pos:pos + 1, :] = x_ref[pos:pos + 1, :] + row

    staged_done = 0
    for u in range(td):
        nxt = u + ahead
        if nxt < td:
            issue(par, nxt, nxt % MIX_SLOTS)
        else:
            @pl.when(step + 1 < n_steps)
            def _():
                if nxt == td:
                    ids_copy(step + 1, 1 - par).wait()
                issue(1 - par, nxt - td, nxt % MIX_SLOTS)

        staged_upto = (u + 1) * ts // td
        for k in range(staged_done, staged_upto):
            finish(td + k, rows_ref[k * picks:(k + 1) * picks, :])
        staged_done = staged_upto
        wait_rows(u % MIX_SLOTS)
        finish(u, buf[u % MIX_SLOTS])


def _peer_mix(ids_direct, uv, staged, x, h, gates3, *, td, ts):
    t, d = x.shape
    picks = gates3.shape[1]
    n_steps = t // (td + ts)
    assert staged.shape[0] == n_steps * ts * picks and ids_direct.shape == (n_steps, td * picks)
    return pl.pallas_call(
        functools.partial(_peer_mix_kernel, td=td, ts=ts, picks=picks, d=d, n_steps=n_steps),
        grid=(n_steps,),
        input_output_aliases={3: 0},
        in_specs=[
            pl.BlockSpec(memory_space=pl.ANY),
            pl.BlockSpec(memory_space=pl.ANY),
            pl.BlockSpec((ts * picks, d), lambda i: (i, 0)),
            pl.BlockSpec((td + ts, d), lambda i: (i, 0)),
            pl.BlockSpec((td + ts, d), lambda i: (i, 0)),
            pl.BlockSpec(((td + ts) // SUBLANES, picks, SUBLANES), lambda i: (i, 0, 0)),
        ],
        out_specs=pl.BlockSpec((td + ts, d), lambda i: (i, 0)),
        out_shape=jax.ShapeDtypeStruct((t, d), F32),
        scratch_shapes=[
            pltpu.SMEM((2, td * picks), jnp.int32),
            pltpu.VMEM((MIX_SLOTS, picks, d), jnp.int32),
            pltpu.SemaphoreType.DMA((2,)),
            pltpu.SemaphoreType.DMA((MIX_SLOTS,)),
        ],
        compiler_params=_params(("arbitrary",)),
    )(ids_direct, uv, staged, x, h, gates3)


def _sc_gather_rows(table, idx, *, chunk):
    n_rows = idx.shape[0]
    d = table.shape[1]
    workers = SC_CORES * SC_SUBCORES
    assert n_rows % (workers * 2 * chunk) == 0 and chunk % SUBLANES == 0 and chunk <= LANES
    rows_per_worker = n_rows // workers
    n_pairs = rows_per_worker // (2 * chunk)
    mesh = plsc.VectorSubcoreMesh(core_axis_name="c", subcore_axis_name="s")

    @functools.partial(
        pl.kernel, mesh=mesh,
        out_type=jax.ShapeDtypeStruct((n_rows, d), table.dtype),
        scratch_types=[
            pltpu.VMEM((chunk,), jnp.int32), pltpu.VMEM((chunk,), jnp.int32),
            pltpu.VMEM((chunk, d), table.dtype), pltpu.VMEM((chunk, d), table.dtype),
            pltpu.SemaphoreType.DMA, pltpu.SemaphoreType.DMA,
            pltpu.SemaphoreType.DMA, pltpu.SemaphoreType.DMA,
        ],
    )
    def gather_kernel(table_hbm, idx_hbm, out_hbm, idx0, idx1, rows0, rows1, gsem0, gsem1, wsem0, wsem1):
        idx_v, rows_v, gsem, wsem = (idx0, idx1), (rows0, rows1), (gsem0, gsem1), (wsem0, wsem1)
        worker = lax.axis_index("s") * SC_CORES + lax.axis_index("c")
        base = worker * rows_per_worker

        def out_rows(c):
            return pl.ds(pl.multiple_of(base + c * chunk, SUBLANES), chunk)

        def load_idx(slot, c):
            pltpu.sync_copy(idx_hbm.at[out_rows(c)], idx_v[slot])

        def gather(slot):
            return pltpu.make_async_copy(table_hbm.at[idx_v[slot]], rows_v[slot], gsem[slot])

        def writeout(slot, c):
            return pltpu.make_async_copy(rows_v[slot], out_hbm.at[out_rows(c)], wsem[slot])

        load_idx(0, 0)
        gather(0).start()

        @pl.loop(0, n_pairs)
        def _(p):
            c0 = 2 * p

            @pl.when(p > 0)
            def _():
                writeout(1, c0 - 1).wait()

            load_idx(1, c0 + 1)
            gather(1).start()
            gather(0).wait()
            writeout(0, c0).start()

            @pl.when(p + 1 < n_pairs)
            def _():
                load_idx(0, c0 + 2)
                writeout(0, c0).wait()
                gather(0).start()

            gather(1).wait()
            writeout(1, c0 + 1).start()

        writeout(0, 2 * n_pairs - 2).wait()
        writeout(1, 2 * n_pairs - 1).wait()

    return gather_kernel(table, idx)


def _pad_cols(w, width):
    return jnp.pad(w, ((0, 0), (0, width - w.shape[1])))


def _prepare_layer(g_norm_mix, w_in, b_gate, g_cq, w_uq, g_ckv, w_ukv, g_qn, g_qr, g_kn, g_kr,
                   w_a2, b_a, g_gla_out, w_out, g_norm_ffn, w_pq, sub_keys, u_emb, v_emb):
    d = w_in.shape[0]
    q_rank, kv_rank = g_cq.shape[0], g_ckv.shape[0]
    nope, rope = g_qn.shape[0], g_qr.shape[0]
    mla_heads = w_uq.shape[1] // (nope + rope)
    mla_v = w_ukv.shape[1] // mla_heads - nope
    gate_rank, gla_dk_all = w_a2.shape
    gla_dv = g_gla_out.shape[0]
    gla_heads = d // gla_dv
    gla_dk = gla_dk_all // gla_heads
    peer_heads, _, nkeys, half = sub_keys.shape
    assert nope == LANES and mla_v == LANES and rope <= LANES and rope % 2 == 0
    assert mla_heads * mla_v == d and gla_heads * gla_dv == d
    assert rope + gate_rank <= LANES and nkeys == LANES and half == LANES

    widths = (q_rank, kv_rank, rope, gla_dk_all, gla_dk_all, d, gate_rank, d, d, d)
    offs = [0]
    for wd in widths:
        offs.append(offs[-1] + wd)
    assert offs[-1] == w_in.shape[1]
    seg = lambda i: w_in[:, offs[i]:offs[i + 1]]
    w_main = jnp.concatenate([seg(5), seg(7), seg(8), seg(9), seg(3), seg(4), seg(0), seg(1)], axis=1).astype(BF16)
    w_small = _pad_cols(jnp.concatenate([seg(2), seg(6)], axis=1), LANES).astype(BF16)
    cq_off = 4 * d + 2 * gla_dk_all
    assert cq_off % q_rank == 0 and (cq_off + q_rank) % kv_rank == 0
    inv_freq = ROPE_THETA ** (-jnp.arange(0, rope, 2, dtype=F32) / rope)
    scale = (nope + rope) ** -0.5
    return dict(
        dims=dict(q_rank=q_rank, kv_rank=kv_rank, nope=nope, rope=rope, mla_heads=mla_heads, mla_v=mla_v,
                  gla_heads=gla_heads, gla_dk=gla_dk, gla_dv=gla_dv, peer_heads=peer_heads, nkeys=nkeys, half=half,
                  v_blk=0, og_blk=d // gla_dv, ga_blk=2, gb_blk=3, q_blk=4 * d // gla_dk,
                  k_blk=4 * d // gla_dk + gla_heads, cq_blk=cq_off // q_rank, ckv_blk=(cq_off + q_rank) // kv_rank),
        g_norm_mix=g_norm_mix[None, :], w_main=w_main, w_small=w_small,
        invf=_pad_cols(jnp.concatenate([inv_freq, inv_freq])[None, :], LANES),
        gq=_pad_cols(jnp.concatenate([g_qn, g_qr])[None, :] * scale, nope + LANES),
        gkr=_pad_cols(g_kr[None, :], LANES), g_cq=g_cq[None, :], g_ckv=g_ckv[None, :], g_kn=g_kn[None, :],
        w_uq=jnp.pad(w_uq.reshape(q_rank, mla_heads, nope + rope),
                     ((0, 0), (0, 0), (0, LANES - rope))).reshape(q_rank, -1).astype(BF16),
        w_ukv=w_ukv.astype(BF16),
        w2p=jnp.zeros((LANES, gla_dk_all), F32).at[rope:rope + gate_rank].set(w_a2).astype(BF16),
        b_a=b_a[None, :], g_on=g_gla_out[None, :], b_gate=b_gate, w_out=w_out.astype(BF16),
        g_norm_ffn=g_norm_ffn[None, :], w_pq=w_pq.astype(BF16),
        sk2d=sub_keys.reshape(peer_heads * 2 * nkeys, half).astype(BF16),
        uv=_pack_expert_table(u_emb, v_emb),
    )


def _mixers_and_route(x_all, row0, pos, w, *, batch, seq, tiles):
    t = batch * seq
    dm = w["dims"]
    p, small = _in_proj(x_all, w["g_norm_mix"], w["w_main"], w["w_small"], row0=row0, t=t,
                        tm=tiles["in_tm"], tn=tiles["in_tn"])
    q = _mla_q(p, dm["cq_blk"], w["g_cq"], w["w_uq"], w["gq"], pos, w["invf"], heads=dm["mla_heads"],
               rank=dm["q_rank"], nope=dm["nope"], rope=dm["rope"], tm=tiles["mla_tm"])
    k, v = _mla_kv(p, dm["ckv_blk"], w["g_ckv"], w["w_ukv"], w["g_kn"], small, w["gkr"], pos, w["invf"],
                   heads=dm["mla_heads"], rank=dm["kv_rank"], nope=dm["nope"], rope=dm["rope"], dv=dm["mla_v"],
                   tm=tiles["mla_tm"])
    o_mla = _mla_attn(q, k, v, batch=batch, seq=seq, heads=dm["mla_heads"], dk=dm["nope"] + LANES, dv=dm["mla_v"],
                      tq=tiles["attn_tq"], tk=tiles["attn_tk"])
    o_gla = _gla(p, small, w["w2p"], w["b_a"], w["g_on"], batch=batch, seq=seq, heads=dm["gla_heads"],
                 dk=dm["gla_dk"], dv=dm["gla_dv"], q_blk=dm["q_blk"], k_blk=dm["k_blk"], v_blk=dm["v_blk"],
                 og_blk=dm["og_blk"])
    x2 = _out_proj(x_all, p, o_mla, o_gla, w["b_gate"], w["w_out"], row0=row0, ga_blk=dm["ga_blk"],
                   gb_blk=dm["gb_blk"], tm=tiles["out_tm"], tn=tiles["out_tn"])
    h2, ids_t, gates_t = _peer_route(x2, w["g_norm_ffn"], w["w_pq"], w["sk2d"], heads=dm["peer_heads"],
                                     nkeys=dm["nkeys"], half=dm["half"], topk=PEER_TOPK, tm=tiles["route_tm"])
    picks = dm["peer_heads"] * PEER_TOPK
    gates3 = gates_t.reshape(picks, t // SUBLANES, SUBLANES).transpose(1, 0, 2)
    return x2, h2, ids_t.T, gates3


def _layer(x2, pos, w, *, batch, seq, tiles):
    t, d = x2.shape
    splits = tiles["mix_split"]
    groups = len(splits)
    tg, bg = t // groups, batch // groups
    routed = [_mixers_and_route(x2, g * tg, pos[g * tg:(g + 1) * tg], w, batch=bg, seq=seq, tiles=tiles)
              for g in range(groups)]
    out = []
    for (td, ts), (xg, hg, ids_tok, gates3) in zip(splits, routed):
        picks = ids_tok.shape[1]
        ids3 = ids_tok.reshape(tg // (td + ts), td + ts, picks)
        staged = _sc_gather_rows(w["uv"], ids3[:, td:].reshape(-1), chunk=tiles["sc_chunk"])
        out.append(_peer_mix(ids3[:, :td].reshape(-1, td * picks), w["uv"], staged, xg, hg, gates3, td=td, ts=ts))
    return jnp.concatenate(out, axis=0)


_TILES = dict(in_tm=2048, in_tn=512, mla_tm=2048, attn_tq=1024, attn_tk=1024,
              out_tm=1024, out_tn=512, route_tm=256, mix_split=((4, 12),) * 5 + ((24, 8),) * 2 + ((4, 12),), sc_chunk=16)


def kernel(x, positions, g_norm_mix, w_in, b_gate, g_cq, w_uq, g_ckv, w_ukv, g_qn, g_qr, g_kn, g_kr,
           w_a2, b_a, g_gla_out, w_out, g_norm_ffn, w_pq, sub_keys, u_emb, v_emb, tiles=None):
    tiles = _TILES if tiles is None else tiles
    batch, seq, d = x.shape
    x2 = x.reshape(batch * seq, d)
    pos = positions.reshape(batch * seq, 1)
    for l in range(g_norm_mix.shape[0]):
        w = _prepare_layer(g_norm_mix[l], w_in[l], b_gate[l], g_cq[l], w_uq[l], g_ckv[l], w_ukv[l], g_qn[l],
                           g_qr[l], g_kn[l], g_kr[l], w_a2[l], b_a[l], g_gla_out[l], w_out[l], g_norm_ffn[l],
                           w_pq[l], sub_keys[l], u_emb[l], v_emb[l])
        x2 = _layer(x2, pos, w, batch=batch, seq=seq, tiles=tiles)
    return x2.reshape(batch, seq, d)
```

```python
import functools

import jax
import jax.numpy as jnp
from jax import lax
from jax.experimental import pallas as pl
from jax.experimental.pallas import tpu as pltpu
from jax.experimental.pallas import tpu_sc as plsc

EPS = 1e-6
ROPE_THETA = 10000.0
GLA_TAU = 16.0
GLA_CHUNK = 64
PEER_TOPK = 16

LANES = 128
SUBLANES = 8
VMEM_LIMIT_BYTES = 56 * 1024 * 1024
MIX_SLOTS = 4
SC_CORES = 2
SC_SUBCORES = 16

F32 = jnp.float32
BF16 = jnp.bfloat16
NEG_INF = float("-inf")


def _params(semantics):
    return pltpu.CompilerParams(dimension_semantics=semantics, vmem_limit_bytes=VMEM_LIMIT_BYTES)


def _rms(x, gain, n=None):
    ss = jnp.sum(x * x, axis=-1, keepdims=True)
    n = x.shape[-1] if n is None else n
    return x * lax.rsqrt(ss * (1.0 / n) + EPS) * gain


def _gelu_exact(x):
    return 0.5 * x * (1.0 + lax.erf(x * (0.5 ** 0.5)))


def _dot(a, b):
    return jnp.dot(a, b, preferred_element_type=F32)


def _dot_nt(a, b):
    return lax.dot_general(a, b, (((1,), (1,)), ((), ())), preferred_element_type=F32)


def _dot_tn(a, b):
    return lax.dot_general(a, b, (((0,), (0,)), ((), ())), preferred_element_type=F32)


def _in_proj_kernel(x_ref, g_ref, w_ref, ws_ref, p_ref, ps_ref, h_scr):
    @pl.when(pl.program_id(1) == 0)
    def _():
        h = _rms(x_ref[...], g_ref[...]).astype(BF16)
        h_scr[...] = h
        ps_ref[...] = _dot(h, ws_ref[...])

    p_ref[...] = _dot(h_scr[...], w_ref[...]).astype(p_ref.dtype)


def _in_proj(x, g, w_main, w_small, *, row0, t, tm, tn):
    d = x.shape[1]
    n = w_main.shape[1]
    assert row0 % tm == 0
    b0 = row0 // tm
    x_mode = pl.Buffered(1) if t == tm else None
    return pl.pallas_call(
        _in_proj_kernel,
        grid=(t // tm, n // tn),
        in_specs=[
            pl.BlockSpec((tm, d), lambda i, j: (i + b0, 0), pipeline_mode=x_mode),
            pl.BlockSpec((1, d), lambda i, j: (0, 0)),
            pl.BlockSpec((d, tn), lambda i, j: (0, j)),
            pl.BlockSpec((d, LANES), lambda i, j: (0, 0)),
        ],
        out_specs=[
            pl.BlockSpec((tm, tn), lambda i, j: (i, j)),
            pl.BlockSpec((tm, LANES), lambda i, j: (i, 0)),
        ],
        out_shape=[
            jax.ShapeDtypeStruct((t, n), BF16),
            jax.ShapeDtypeStruct((t, LANES), F32),
        ],
        scratch_shapes=[pltpu.VMEM((tm, d), BF16)],
        compiler_params=_params(("parallel", "arbitrary")),
    )(x, g, w_main, w_small)


def _rope_tables(pos_ref, invf_ref, rope):
    ang = pos_ref[...].astype(F32) * invf_ref[...]
    cos, sin = jnp.cos(ang), jnp.sin(ang)
    lane = lax.broadcasted_iota(jnp.int32, ang.shape, 1)
    half = rope // 2
    c = jnp.where(lane < rope, cos, 0.0)
    s_lo = jnp.where(lane < half, -sin, 0.0)
    s_hi = jnp.where(lane < half, 0.0, jnp.where(lane < rope, sin, 0.0))
    return c, s_lo, s_hi


def _apply_rope(pe, c, s_lo, s_hi, rope):
    half = rope // 2
    from_hi = pltpu.roll(pe, LANES - half, 1)
    from_lo = pltpu.roll(pe, half, 1)
    return pe * c + from_hi * s_lo + from_lo * s_hi


def _mla_q_kernel(cq_ref, gcq_ref, w_ref, gq_ref, pos_ref, invf_ref, q_ref,
                  h_scr, c_scr, slo_scr, shi_scr, *, nope, rope):
    @pl.when(pl.program_id(1) == 0)
    def _():
        h_scr[...] = _rms(cq_ref[...].astype(F32), gcq_ref[...]).astype(BF16)
        c, s_lo, s_hi = _rope_tables(pos_ref, invf_ref, rope)
        c_scr[...] = c
        slo_scr[...] = s_lo
        shi_scr[...] = s_hi

    y = _dot(h_scr[...], w_ref[...])
    g = gq_ref[...]
    qn = _rms(y[:, :nope], g[:, :nope])
    pe = _rms(y[:, nope:], g[:, nope:], n=rope)
    pe = _apply_rope(pe, c_scr[...], slo_scr[...], shi_scr[...], rope)
    q_ref[:, :nope] = qn.astype(q_ref.dtype)
    q_ref[:, nope:] = pe.astype(q_ref.dtype)


def _mla_q(p, cq_blk, g_cq, w_uq_p, gq, pos, invf, *, heads, rank, nope, rope, tm):
    t = p.shape[0]
    hw = nope + LANES
    return pl.pallas_call(
        functools.partial(_mla_q_kernel, nope=nope, rope=rope),
        grid=(t // tm, heads),
        in_specs=[
            pl.BlockSpec((tm, rank), lambda i, j: (i, cq_blk)),
            pl.BlockSpec((1, rank), lambda i, j: (0, 0)),
            pl.BlockSpec((rank, hw), lambda i, j: (0, j)),
            pl.BlockSpec((1, hw), lambda i, j: (0, 0)),
            pl.BlockSpec((tm, 1), lambda i, j: (i, 0)),
            pl.BlockSpec((1, LANES), lambda i, j: (0, 0)),
        ],
        out_specs=pl.BlockSpec((tm, hw), lambda i, j: (i, j)),
        out_shape=jax.ShapeDtypeStruct((t, heads * hw), BF16),
        scratch_shapes=[
            pltpu.VMEM((tm, rank), BF16),
            pltpu.VMEM((tm, LANES), F32),
            pltpu.VMEM((tm, LANES), F32),
            pltpu.VMEM((tm, LANES), F32),
        ],
        compiler_params=_params(("parallel", "arbitrary")),
    )(p, g_cq, w_uq_p, gq, pos, invf)


def _mla_kv_kernel(ckv_ref, gckv_ref, w_ref, gkn_ref, small_ref, gkr_ref, pos_ref, invf_ref,
                   k_ref, v_ref, h_scr, kpe_scr, *, nope, rope):
    @pl.when(pl.program_id(1) == 0)
    def _():
        h_scr[...] = _rms(ckv_ref[...].astype(F32), gckv_ref[...]).astype(BF16)
        c, s_lo, s_hi = _rope_tables(pos_ref, invf_ref, rope)
        sm = small_ref[...]
        lane = lax.broadcasted_iota(jnp.int32, sm.shape, 1)
        pe = _rms(jnp.where(lane < rope, sm, 0.0), gkr_ref[...], n=rope)
        kpe_scr[...] = _apply_rope(pe, c, s_lo, s_hi, rope).astype(BF16)

    y = _dot(h_scr[...], w_ref[...])
    k_ref[:, :nope] = _rms(y[:, :nope], gkn_ref[...]).astype(k_ref.dtype)
    k_ref[:, nope:] = kpe_scr[...]
    v_ref[...] = y[:, nope:].astype(v_ref.dtype)


def _mla_kv(p, ckv_blk, g_ckv, w_ukv, g_kn, small, gkr, pos, invf, *, heads, rank, nope, rope, dv, tm):
    t = p.shape[0]
    kw = nope + LANES
    return pl.pallas_call(
        functools.partial(_mla_kv_kernel, nope=nope, rope=rope),
        grid=(t // tm, heads),
        in_specs=[
            pl.BlockSpec((tm, rank), lambda i, j: (i, ckv_blk)),
            pl.BlockSpec((1, rank), lambda i, j: (0, 0)),
            pl.BlockSpec((rank, nope + dv), lambda i, j: (0, j)),
            pl.BlockSpec((1, nope), lambda i, j: (0, 0)),
            pl.BlockSpec((tm, LANES), lambda i, j: (i, 0)),
            pl.BlockSpec((1, LANES), lambda i, j: (0, 0)),
            pl.BlockSpec((tm, 1), lambda i, j: (i, 0)),
            pl.BlockSpec((1, LANES), lambda i, j: (0, 0)),
        ],
        out_specs=[
            pl.BlockSpec((tm, kw), lambda i, j: (i, j)),
            pl.BlockSpec((tm, dv), lambda i, j: (i, j)),
        ],
        out_shape=[
            jax.ShapeDtypeStruct((t, heads * kw), BF16),
            jax.ShapeDtypeStruct((t, heads * dv), BF16),
        ],
        scratch_shapes=[pltpu.VMEM((tm, rank), BF16), pltpu.VMEM((tm, LANES), BF16)],
        compiler_params=_params(("parallel", "arbitrary")),
    )(p, g_ckv, w_ukv, g_kn, small, gkr, pos, invf)


def _attn_kernel(q_ref, k_ref, v_ref, o_ref, *, tq, tk):
    qi = pl.program_id(2)
    q = q_ref[...]
    dv = v_ref.shape[1]
    row = qi * tq + lax.broadcasted_iota(jnp.int32, (tq, tk), 0)
    col0 = lax.broadcasted_iota(jnp.int32, (tq, tk), 1)

    def body(kb, carry, masked):
        m, l, acc = carry
        k0 = pl.multiple_of(kb * tk, tk)
        s = _dot_nt(q, k_ref[pl.ds(k0, tk), :])
        if masked:
            s = jnp.where(col0 + k0 <= row, s, NEG_INF)
        m_new = jnp.maximum(m, jnp.max(s, axis=-1, keepdims=True))
        alpha = jnp.exp(m - m_new)
        pr = jnp.exp(s - m_new)
        l = alpha * l + jnp.sum(pr, axis=-1, keepdims=True)
        acc = alpha * acc + _dot(pr.astype(BF16), v_ref[pl.ds(k0, tk), :])
        return m_new, l, acc

    init = (jnp.full((tq, 1), NEG_INF, F32), jnp.zeros((tq, 1), F32), jnp.zeros((tq, dv), F32))
    n_below = (qi * tq) // tk
    nkb = ((qi + 1) * tq + tk - 1) // tk
    carry = lax.fori_loop(0, n_below, functools.partial(body, masked=False), init)
    _, l, acc = lax.fori_loop(n_below, nkb, functools.partial(body, masked=True), carry)
    o_ref[...] = (acc / l).astype(o_ref.dtype)


def _mla_attn(q, k, v, *, batch, seq, heads, dk, dv, tq, tk):
    t = q.shape[0]
    nq = seq // tq
    return pl.pallas_call(
        functools.partial(_attn_kernel, tq=tq, tk=tk),
        grid=(batch, heads, nq),
        in_specs=[
            pl.BlockSpec((tq, dk), lambda b, h, i: (b * nq + i, h)),
            pl.BlockSpec((seq, dk), lambda b, h, i: (b, h)),
            pl.BlockSpec((seq, dv), lambda b, h, i: (b, h)),
        ],
        out_specs=pl.BlockSpec((tq, dv), lambda b, h, i: (b * nq + i, h)),
        out_shape=jax.ShapeDtypeStruct((t, heads * dv), BF16),
        compiler_params=_params(("parallel", "parallel", "arbitrary")),
    )(q, k, v)


def _gla_kernel(q_ref, k_ref, v_ref, og_ref, small_ref, w2_ref, ba_ref, gon_ref, o_ref, st_scr,
                *, seq, dk, dv, chunk):
    c = chunk
    st_scr[...] = jnp.zeros_like(st_scr)
    r_i = lax.broadcasted_iota(jnp.int32, (c, c), 0)
    c_i = lax.broadcasted_iota(jnp.int32, (c, c), 1)
    tri = jnp.where(c_i <= r_i, 1.0, 0.0).astype(BF16)
    row_id = lax.broadcasted_iota(jnp.int32, (c, 1), 0)
    w2 = w2_ref[...]
    ba = ba_ref[...]
    gon = gon_ref[...]
    q_scale = dk ** -0.5

    def chunk_step(ci, carry):
        r0 = pl.multiple_of(ci * c, c)
        qc = q_ref[pl.ds(r0, c), :].astype(F32) * q_scale
        kc = k_ref[pl.ds(r0, c), :].astype(F32)
        vc = v_ref[pl.ds(r0, c), :]
        z = _dot(small_ref[pl.ds(r0, c), :].astype(BF16), w2) + ba
        la = jax.nn.log_sigmoid(z) * (1.0 / GLA_TAU)
        hi = la.astype(BF16)
        r1 = la - hi.astype(F32)
        mid = r1.astype(BF16)
        lo = (r1 - mid.astype(F32)).astype(BF16)
        b = _dot(tri, hi) + _dot(tri, mid) + _dot(tri, lo)

        st = st_scr[...]
        inter = _dot_nt((qc * jnp.exp(b)).astype(BF16), st.astype(BF16))

        att = jnp.zeros((c, c), F32)
        for j in range(c):
            lo_r = (j // SUBLANES) * SUBLANES
            d = b[lo_r:, :] - b[j:j + 1, :]
            head = jnp.where(row_id[lo_r:lo_r + SUBLANES, :] >= j, d[:SUBLANES, :], NEG_INF)
            e = jnp.exp(jnp.concatenate([head, d[SUBLANES:, :]], axis=0) if lo_r + SUBLANES < c else head)
            col = jnp.sum(qc[lo_r:, :] * kc[j:j + 1, :] * e, axis=-1, keepdims=True)
            if lo_r:
                col = jnp.concatenate([jnp.zeros((lo_r, 1), F32), col], axis=0)
            att = jnp.where(c_i == j, col, att)
        o = inter + _dot(att.astype(BF16), vc)

        b_last = b[c - 1:c, :]
        k_dec = (kc * jnp.exp(b_last - b)).astype(BF16)
        st_scr[...] = st * jnp.exp(b_last) + _dot_tn(vc, k_dec)

        og = og_ref[pl.ds(r0, c), :].astype(F32)
        out = _rms(o, gon) * (og * jax.nn.sigmoid(og))
        o_ref[pl.ds(r0, c), :] = out.astype(o_ref.dtype)
        return carry

    lax.fori_loop(0, seq // c, chunk_step, 0)


def _gla(p, small, w2p, b_a, g_on, *, batch, seq, heads, dk, dv, q_blk, k_blk, v_blk, og_blk):
    t = p.shape[0]
    return pl.pallas_call(
        functools.partial(_gla_kernel, seq=seq, dk=dk, dv=dv, chunk=GLA_CHUNK),
        grid=(batch, heads),
        in_specs=[
            pl.BlockSpec((seq, dk), lambda b, h: (b, q_blk + h)),
            pl.BlockSpec((seq, dk), lambda b, h: (b, k_blk + h)),
            pl.BlockSpec((seq, dv), lambda b, h: (b, v_blk + h)),
            pl.BlockSpec((seq, dv), lambda b, h: (b, og_blk + h)),
            pl.BlockSpec((seq, LANES), lambda b, h: (b, 0)),
            pl.BlockSpec((LANES, dk), lambda b, h: (0, h)),
            pl.BlockSpec((1, dk), lambda b, h: (0, h)),
            pl.BlockSpec((1, dv), lambda b, h: (0, 0)),
        ],
        out_specs=pl.BlockSpec((seq, dv), lambda b, h: (b, h)),
        out_shape=jax.ShapeDtypeStruct((t, heads * dv), BF16),
        scratch_shapes=[pltpu.VMEM((dv, dk), F32)],
        compiler_params=_params(("parallel", "parallel")),
    )(p, p, p, p, small, w2p, b_a, g_on)


def _out_proj_kernel(x_ref, ga_ref, gb_ref, oa_ref, ob_ref, bg_ref, w_ref, o_ref, m_scr):
    @pl.when(pl.program_id(1) == 0)
    def _():
        bg = bg_ref[...]
        sa = jax.nn.sigmoid(ga_ref[...].astype(F32) + bg[0:1, :])
        sb = jax.nn.sigmoid(gb_ref[...].astype(F32) + bg[1:2, :])
        m_scr[...] = (sa * oa_ref[...].astype(F32) + sb * ob_ref[...].astype(F32)).astype(BF16)

    o_ref[...] = x_ref[...] + _dot(m_scr[...], w_ref[...])


def _out_proj(x, p, o_mla, o_gla, b_gate, w_out, *, row0, ga_blk, gb_blk, tm, tn):
    t, d = p.shape[0], x.shape[1]
    assert row0 % tm == 0
    b0 = row0 // tm
    return pl.pallas_call(
        _out_proj_kernel,
        grid=(t // tm, d // tn),
        in_specs=[
            pl.BlockSpec((tm, tn), lambda i, j: (i + b0, j)),
            pl.BlockSpec((tm, d), lambda i, j: (i, ga_blk)),
            pl.BlockSpec((tm, d), lambda i, j: (i, gb_blk)),
            pl.BlockSpec((tm, d), lambda i, j: (i, 0)),
            pl.BlockSpec((tm, d), lambda i, j: (i, 0)),
            pl.BlockSpec((2, d), lambda i, j: (0, 0)),
            pl.BlockSpec((d, tn), lambda i, j: (0, j)),
        ],
        out_specs=pl.BlockSpec((tm, tn), lambda i, j: (i, j)),
        out_shape=jax.ShapeDtypeStruct((t, d), F32),
        scratch_shapes=[pltpu.VMEM((tm, d), BF16)],
        compiler_params=_params(("parallel", "arbitrary")),
    )(x, p, p, o_mla, o_gla, b_gate, w_out)


def _topk_rows(s, k, payload=None):
    n = s.shape[0]
    iota = lax.broadcasted_iota(jnp.int32, s.shape, 0).astype(F32)
    kiota = lax.broadcasted_iota(jnp.int32, (k, s.shape[1]), 0)
    vals = jnp.zeros((k, s.shape[1]), F32)
    picks = jnp.zeros((k, s.shape[1]), F32)
    for r in range(k):
        m = jnp.max(s, axis=0, keepdims=True)
        idx = jnp.min(jnp.where(s == m, iota, float(n)), axis=0, keepdims=True)
        hit = iota == idx
        if payload is None:
            pick = idx
        else:
            pick = jnp.sum(jnp.where(hit, payload, 0.0), axis=0, keepdims=True)
        vals = jnp.where(kiota == r, m, vals)
        picks = jnp.where(kiota == r, pick, picks)
        s = jnp.where(hit, NEG_INF, s)
    return vals, picks


def _peer_route_kernel(x_ref, g_ref, w_ref, sk_ref, h_ref, ids_ref, gates_ref, *, heads, nkeys, half, topk):
    h = _rms(x_ref[...], g_ref[...]).astype(BF16)
    h_ref[...] = h
    qf = _dot(h, w_ref[...])
    for hd in range(heads):
        tops = []
        for part in range(2):
            o = (hd * 2 + part) * half
            qh = qf[:, o:o + half].astype(BF16)
            keys = sk_ref[(hd * 2 + part) * nkeys:(hd * 2 + part + 1) * nkeys, :]
            tops.append(_topk_rows(_dot_nt(keys, qh), topk))
        (s1, i1), (s2, i2) = tops
        assert topk == 2 * SUBLANES
        hs = SUBLANES
        pair_s = [s1[0:1, :] + s2] + [s1[a:a + 1, :] + s2[:hs, :] for a in range(1, hs)] + [s1[hs:, :] + s2[0:1, :]]
        pair_id = ([i1[0:1, :] * float(nkeys) + i2]
                   + [i1[a:a + 1, :] * float(nkeys) + i2[:hs, :] for a in range(1, hs)]
                   + [i1[hs:, :] * float(nkeys) + i2[0:1, :]])
        best_s, best_id = _topk_rows(jnp.concatenate(pair_s, axis=0), topk,
                                     payload=jnp.concatenate(pair_id, axis=0))
        e = jnp.exp(best_s - best_s[0:1, :])
        gate = e / jnp.sum(e, axis=0, keepdims=True)
        ids_ref[hd * topk:(hd + 1) * topk, :] = best_id.astype(jnp.int32)
        gates_ref[hd * topk:(hd + 1) * topk, :] = gate


def _peer_route(x, g, w_pq, sk2d, *, heads, nkeys, half, topk, tm):
    t, d = x.shape
    dq = w_pq.shape[1]
    return pl.pallas_call(
        functools.partial(_peer_route_kernel, heads=heads, nkeys=nkeys, half=half, topk=topk),
        grid=(t // tm,),
        in_specs=[
            pl.BlockSpec((tm, d), lambda i: (i, 0)),
            pl.BlockSpec((1, d), lambda i: (0, 0)),
            pl.BlockSpec((d, dq), lambda i: (0, 0)),
            pl.BlockSpec((heads * 2 * nkeys, half), lambda i: (0, 0)),
        ],
        out_specs=[
            pl.BlockSpec((tm, d), lambda i: (i, 0)),
            pl.BlockSpec((heads * topk, tm), lambda i: (0, i)),
            pl.BlockSpec((heads * topk, tm), lambda i: (0, i)),
        ],
        out_shape=[
            jax.ShapeDtypeStruct((t, d), BF16),
            jax.ShapeDtypeStruct((heads * topk, t), jnp.int32),
            jax.ShapeDtypeStruct((heads * topk, t), F32),
        ],
        compiler_params=_params(("parallel",)),
    )(x, g, w_pq, sk2d)


def _pack_expert_table(u_emb, v_emb):
    n, d = u_emb.shape
    half = d // 2
    tm = 512
    assert n % tm == 0

    def pack(w):
        bits = lax.bitcast_convert_type(w.astype(BF16).astype(F32), jnp.int32)
        return ((bits[:, :half] >> 16) & jnp.int32(0xFFFF)) | (bits[:, half:] & jnp.int32(-65536))

    def pack_kernel(u_ref, v_ref, o_ref):
        o_ref[:, :half] = pack(u_ref[...])
        o_ref[:, half:] = pack(v_ref[...])

    return pl.pallas_call(
        pack_kernel,
        grid=(n // tm,),
        in_specs=[pl.BlockSpec((tm, d), lambda i: (i, 0)), pl.BlockSpec((tm, d), lambda i: (i, 0))],
        out_specs=pl.BlockSpec((tm, d), lambda i: (i, 0)),
        out_shape=jax.ShapeDtypeStruct((n, d), jnp.int32),
        compiler_params=_params(("parallel",)),
    )(u_emb, v_emb)


def _unpack_words(w):
    lo = lax.bitcast_convert_type(w << 16, F32)
    hi = lax.bitcast_convert_type(w & jnp.int32(-65536), F32)
    return lo, hi


def _expert_mix(words, h, gate, d):
    half = d // 2
    u_lo, u_hi = _unpack_words(words[:, :half])
    act = jnp.sum(u_lo * h[:, :half] + u_hi * h[:, half:], axis=-1, keepdims=True)
    w = gate * _gelu_exact(act)
    v_lo, v_hi = _unpack_words(words[:, half:])
    return jnp.concatenate([jnp.sum(v_lo * w, axis=0, keepdims=True),
                            jnp.sum(v_hi * w, axis=0, keepdims=True)], axis=1)


def _peer_mix_kernel(ids_hbm, uv_hbm, rows_ref, x_ref, h_ref, gates_ref, o_ref, ids_smem, buf, ids_sems, row_sems,
                     *, td, ts, picks, d, n_steps):
    step = pl.program_id(0)
    par = lax.rem(step, 2)
    ahead = MIX_SLOTS - 1
    assert td % MIX_SLOTS == 0 and td > ahead and (td + ts) % SUBLANES == 0

    def ids_copy(s, row):
        return pltpu.make_async_copy(ids_hbm.at[s], ids_smem.at[row], ids_sems.at[row])

    def issue(row, tok, slot):
        for j in range(picks):
            eid = ids_smem[row, tok * picks + j]
            pltpu.make_async_copy(uv_hbm.at[pl.ds(eid, 1), :], buf.at[slot, pl.ds(j, 1), :],
                                  row_sems.at[slot]).start(priority=j % 2)

    def wait_rows(slot):
        pltpu.make_async_copy(uv_hbm.at[pl.ds(0, picks), :], buf.at[slot], row_sems.at[slot]).wait()

    @pl.when(step == 0)
    def _():
        first = ids_copy(0, 0)
        first.start()
        first.wait()
        for u in range(ahead):
            issue(0, u, u)

    @pl.when(step + 1 < n_steps)
    def _():
        ids_copy(step + 1, 1 - par).start()

    loaded = {}

    def finish(pos, words):
        g, lane = divmod(pos, SUBLANES)
        if g not in loaded:
            loaded[g] = (gates_ref[g], h_ref[g * SUBLANES:(g + 1) * SUBLANES, :].astype(F32))
        gates, h8 = loaded[g]
        row = _expert_mix(words, h8[lane:lane + 1, :], gates[:, lane:lane + 1], d)
        o_ref[pos:pos + 1, :] = x_ref[pos:pos + 1, :] + row

    staged_done = 0
    for u in range(td):
        nxt = u + ahead
        if nxt < td:
            issue(par, nxt, nxt % MIX_SLOTS)
        else:
            @pl.when(step + 1 < n_steps)
            def _():
                if nxt == td:
                    ids_copy(step + 1, 1 - par).wait()
                issue(1 - par, nxt - td, nxt % MIX_SLOTS)

        staged_upto = (u + 1) * ts // td
        for k in range(staged_done, staged_upto):
            finish(td + k, rows_ref[k * picks:(k + 1) * picks, :])
        staged_done = staged_upto
        wait_rows(u % MIX_SLOTS)
        finish(u, buf[u % MIX_SLOTS])


def _peer_mix(ids_direct, uv, staged, x, h, gates3, *, td, ts):
    t, d = x.shape
    picks = gates3.shape[1]
    n_steps = t // (td + ts)
    assert staged.shape[0] == n_steps * ts * picks and ids_direct.shape == (n_steps, td * picks)
    return pl.pallas_call(
        functools.partial(_peer_mix_kernel, td=td, ts=ts, picks=picks, d=d, n_steps=n_steps),
        grid=(n_steps,),
        input_output_aliases={3: 0},
        in_specs=[
            pl.BlockSpec(memory_space=pl.ANY),
            pl.BlockSpec(memory_space=pl.ANY),
            pl.BlockSpec((ts * picks, d), lambda i: (i, 0)),
            pl.BlockSpec((td + ts, d), lambda i: (i, 0)),
            pl.BlockSpec((td + ts, d), lambda i: (i, 0)),
            pl.BlockSpec(((td + ts) // SUBLANES, picks, SUBLANES), lambda i: (i, 0, 0)),
        ],
        out_specs=pl.BlockSpec((td + ts, d), lambda i: (i, 0)),
        out_shape=jax.ShapeDtypeStruct((t, d), F32),
        scratch_shapes=[
            pltpu.SMEM((2, td * picks), jnp.int32),
            pltpu.VMEM((MIX_SLOTS, picks, d), jnp.int32),
            pltpu.SemaphoreType.DMA((2,)),
            pltpu.SemaphoreType.DMA((MIX_SLOTS,)),
        ],
        compiler_params=_params(("arbitrary",)),
    )(ids_direct, uv, staged, x, h, gates3)


def _sc_gather_rows(table, idx, *, chunk):
    n_rows = idx.shape[0]
    d = table.shape[1]
    workers = SC_CORES * SC_SUBCORES
    assert n_rows % (workers * 2 * chunk) == 0 and chunk % SUBLANES == 0 and chunk <= LANES
    rows_per_worker = n_rows // workers
    n_pairs = rows_per_worker // (2 * chunk)
    mesh = plsc.VectorSubcoreMesh(core_axis_name="c", subcore_axis_name="s")

    @functools.partial(
        pl.kernel, mesh=mesh,
        out_type=jax.ShapeDtypeStruct((n_rows, d), table.dtype),
        scratch_types=[
            pltpu.VMEM((chunk,), jnp.int32), pltpu.VMEM((chunk,), jnp.int32),
            pltpu.VMEM((chunk, d), table.dtype), pltpu.VMEM((chunk, d), table.dtype),
            pltpu.SemaphoreType.DMA, pltpu.SemaphoreType.DMA,
            pltpu.SemaphoreType.DMA, pltpu.SemaphoreType.DMA,
        ],
    )
    def gather_kernel(table_hbm, idx_hbm, out_hbm, idx0, idx1, rows0, rows1, gsem0, gsem1, wsem0, wsem1):
        idx_v, rows_v, gsem, wsem = (idx0, idx1), (rows0, rows1), (gsem0, gsem1), (wsem0, wsem1)
        worker = lax.axis_index("s") * SC_CORES + lax.axis_index("c")
        base = worker * rows_per_worker

        def out_rows(c):
            return pl.ds(pl.multiple_of(base + c * chunk, SUBLANES), chunk)

        def load_idx(slot, c):
            pltpu.sync_copy(idx_hbm.at[out_rows(c)], idx_v[slot])

        def gather(slot):
            return pltpu.make_async_copy(table_hbm.at[idx_v[slot]], rows_v[slot], gsem[slot])

        def writeout(slot, c):
            return pltpu.make_async_copy(rows_v[slot], out_hbm.at[out_rows(c)], wsem[slot])

        load_idx(0, 0)
        gather(0).start()

        @pl.loop(0, n_pairs)
        def _(p):
            c0 = 2 * p

            @pl.when(p > 0)
            def _():
                writeout(1, c0 - 1).wait()

            load_idx(1, c0 + 1)
            gather(1).start()
            gather(0).wait()
            writeout(0, c0).start()

            @pl.when(p + 1 < n_pairs)
            def _():
                load_idx(0, c0 + 2)
                writeout(0, c0).wait()
                gather(0).start()

            gather(1).wait()
            writeout(1, c0 + 1).start()

        writeout(0, 2 * n_pairs - 2).wait()
        writeout(1, 2 * n_pairs - 1).wait()

    return gather_kernel(table, idx)


def _pad_cols(w, width):
    return jnp.pad(w, ((0, 0), (0, width - w.shape[1])))


def _prepare_layer(g_norm_mix, w_in, b_gate, g_cq, w_uq, g_ckv, w_ukv, g_qn, g_qr, g_kn, g_kr,
                   w_a2, b_a, g_gla_out, w_out, g_norm_ffn, w_pq, sub_keys, u_emb, v_emb):
    d = w_in.shape[0]
    q_rank, kv_rank = g_cq.shape[0], g_ckv.shape[0]
    nope, rope = g_qn.shape[0], g_qr.shape[0]
    mla_heads = w_uq.shape[1] // (nope + rope)
    mla_v = w_ukv.shape[1] // mla_heads - nope
    gate_rank, gla_dk_all = w_a2.shape
    gla_dv = g_gla_out.shape[0]
    gla_heads = d // gla_dv
    gla_dk = gla_dk_all // gla_heads
    peer_heads, _, nkeys, half = sub_keys.shape
    assert nope == LANES and mla_v == LANES and rope <= LANES and rope % 2 == 0
    assert mla_heads * mla_v == d and gla_heads * gla_dv == d
    assert rope + gate_rank <= LANES and nkeys == LANES and half == LANES

    widths = (q_rank, kv_rank, rope, gla_dk_all, gla_dk_all, d, gate_rank, d, d, d)
    offs = [0]
    for wd in widths:
        offs.append(offs[-1] + wd)
    assert offs[-1] == w_in.shape[1]
    seg = lambda i: w_in[:, offs[i]:offs[i + 1]]
    w_main = jnp.concatenate([seg(5), seg(7), seg(8), seg(9), seg(3), seg(4), seg(0), seg(1)], axis=1).astype(BF16)
    w_small = _pad_cols(jnp.concatenate([seg(2), seg(6)], axis=1), LANES).astype(BF16)
    cq_off = 4 * d + 2 * gla_dk_all
    assert cq_off % q_rank == 0 and (cq_off + q_rank) % kv_rank == 0
    inv_freq = ROPE_THETA ** (-jnp.arange(0, rope, 2, dtype=F32) / rope)
    scale = (nope + rope) ** -0.5
    return dict(
        dims=dict(q_rank=q_rank, kv_rank=kv_rank, nope=nope, rope=rope, mla_heads=mla_heads, mla_v=mla_v,
                  gla_heads=gla_heads, gla_dk=gla_dk, gla_dv=gla_dv, peer_heads=peer_heads, nkeys=nkeys, half=half,
                  v_blk=0, og_blk=d // gla_dv, ga_blk=2, gb_blk=3, q_blk=4 * d // gla_dk,
                  k_blk=4 * d // gla_dk + gla_heads, cq_blk=cq_off // q_rank, ckv_blk=(cq_off + q_rank) // kv_rank),
        g_norm_mix=g_norm_mix[None, :], w_main=w_main, w_small=w_small,
        invf=_pad_cols(jnp.concatenate([inv_freq, inv_freq])[None, :], LANES),
        gq=_pad_cols(jnp.concatenate([g_qn, g_qr])[None, :] * scale, nope + LANES),
        gkr=_pad_cols(g_kr[None, :], LANES), g_cq=g_cq[None, :], g_ckv=g_ckv[None, :], g_kn=g_kn[None, :],
        w_uq=jnp.pad(w_uq.reshape(q_rank, mla_heads, nope + rope),
                     ((0, 0), (0, 0), (0, LANES - rope))).reshape(q_rank, -1).astype(BF16),
        w_ukv=w_ukv.astype(BF16),
        w2p=jnp.zeros((LANES, gla_dk_all), F32).at[rope:rope + gate_rank].set(w_a2).astype(BF16),
        b_a=b_a[None, :], g_on=g_gla_out[None, :], b_gate=b_gate, w_out=w_out.astype(BF16),
        g_norm_ffn=g_norm_ffn[None, :], w_pq=w_pq.astype(BF16),
        sk2d=sub_keys.reshape(peer_heads * 2 * nkeys, half).astype(BF16),
        uv=_pack_expert_table(u_emb, v_emb),
    )


def _mixers_and_route(x_all, row0, pos, w, *, batch, seq, tiles):
    t = batch * seq
    dm = w["dims"]
    p, small = _in_proj(x_all, w["g_norm_mix"], w["w_main"], w["w_small"], row0=row0, t=t,
                        tm=tiles["in_tm"], tn=tiles["in_tn"])
    q = _mla_q(p, dm["cq_blk"], w["g_cq"], w["w_uq"], w["gq"], pos, w["invf"], heads=dm["mla_heads"],
               rank=dm["q_rank"], nope=dm["nope"], rope=dm["rope"], tm=tiles["mla_tm"])
    k, v = _mla_kv(p, dm["ckv_blk"], w["g_ckv"], w["w_ukv"], w["g_kn"], small, w["gkr"], pos, w["invf"],
                   heads=dm["mla_heads"], rank=dm["kv_rank"], nope=dm["nope"], rope=dm["rope"], dv=dm["mla_v"],
                   tm=tiles["mla_tm"])
    o_mla = _mla_attn(q, k, v, batch=batch, seq=seq, heads=dm["mla_heads"], dk=dm["nope"] + LANES, dv=dm["mla_v"],
                      tq=tiles["attn_tq"], tk=tiles["attn_tk"])
    o_gla = _gla(p, small, w["w2p"], w["b_a"], w["g_on"], batch=batch, seq=seq, heads=dm["gla_heads"],
                 dk=dm["gla_dk"], dv=dm["gla_dv"], q_blk=dm["q_blk"], k_blk=dm["k_blk"], v_blk=dm["v_blk"],
                 og_blk=dm["og_blk"])
    x2 = _out_proj(x_all, p, o_mla, o_gla, w["b_gate"], w["w_out"], row0=row0, ga_blk=dm["ga_blk"],
                   gb_blk=dm["gb_blk"], tm=tiles["out_tm"], tn=tiles["out_tn"])
    h2, ids_t, gates_t = _peer_route(x2, w["g_norm_ffn"], w["w_pq"], w["sk2d"], heads=dm["peer_heads"],
                                     nkeys=dm["nkeys"], half=dm["half"], topk=PEER_TOPK, tm=tiles["route_tm"])
    picks = dm["peer_heads"] * PEER_TOPK
    gates3 = gates_t.reshape(picks, t // SUBLANES, SUBLANES).transpose(1, 0, 2)
    return x2, h2, ids_t.T, gates3


def _layer(x2, pos, w, *, batch, seq, tiles):
    t, d = x2.shape
    splits = tiles["mix_split"]
    groups = len(splits)
    tg, bg = t // groups, batch // groups
    routed = [_mixers_and_route(x2, g * tg, pos[g * tg:(g + 1) * tg], w, batch=bg, seq=seq, tiles=tiles)
              for g in range(groups)]
    out = []
    for (td, ts), (xg, hg, ids_tok, gates3) in zip(splits, routed):
        picks = ids_tok.shape[1]
        ids3 = ids_tok.reshape(tg // (td + ts), td + ts, picks)
        staged = _sc_gather_rows(w["uv"], ids3[:, td:].reshape(-1), chunk=tiles["sc_chunk"])
        out.append(_peer_mix(ids3[:, :td].reshape(-1, td * picks), w["uv"], staged, xg, hg, gates3, td=td, ts=ts))
    return jnp.concatenate(out, axis=0)


_TILES = dict(in_tm=2048, in_tn=512, mla_tm=2048, attn_tq=1024, attn_tk=1024,
              out_tm=1024, out_tn=512, route_tm=256, mix_split=((4, 12),) * 5 + ((24, 8),) * 2 + ((4, 12),), sc_chunk=16)


def kernel(x, positions, g_norm_mix, w_in, b_gate, g_cq, w_uq, g_ckv, w_ukv, g_qn, g_qr, g_kn, g_kr,
           w_a2, b_a, g_gla_out, w_out, g_norm_ffn, w_pq, sub_keys, u_emb, v_emb, tiles=None):
    tiles = _TILES if tiles is None else tiles
    batch, seq, d = x.shape
    x2 = x.reshape(batch * seq, d)
    pos = positions.reshape(batch * seq, 1)
    for l in range(g_norm_mix.shape[0]):
        w = _prepare_layer(g_norm_mix[l], w_in[l], b_gate[l], g_cq[l], w_uq[l], g_ckv[l], w_ukv[l], g_qn[l],
                           g_qr[l], g_kn[l], g_kr[l], w_a2[l], b_a[l], g_gla_out[l], w_out[l], g_norm_ffn[l],
                           w_pq[l], sub_keys[l], u_emb[l], v_emb[l])
        x2 = _layer(x2, pos, w, batch=batch, seq=seq, tiles=tiles)
    return x2.reshape(batch, seq, d)
```
